```python
import math
import jax, jax.numpy as jnp
from jax import lax
import numpy as np

D_MODEL = 1024
BATCH = 16
SEQ = 4096
DEPTH = 1

D_RNN = 1024
RNN_HEADS = 16
RNN_HEAD_DIM = D_RNN // RNN_HEADS
CONV_WIDTH = 4
LRU_C = 8.0
D_SG = 1024
SG_GROUPS = 8
SG_GROUP_DIM = D_SG // SG_GROUPS
SG_CHUNK = 128
N_EXPERTS = 32
TOP_K = 4
D_EXPERT = 1024
SWIGLU_LIMIT = 7.0
SWIGLU_ALPHA = 1.702
MOE_BLOCK = 256
EPS = 1e-6
N_MOD = 6
IN_SPLITS = (D_RNN, 2 * D_RNN, 2 * D_RNN + D_SG, 2 * D_RNN + 2 * D_SG, 2 * D_RNN + 2 * D_SG + D_MODEL)
D_IN = 2 * D_RNN + 2 * D_SG + 2 * D_MODEL

kernel_name = "hybrid_rglru_gmlp_moe_adaln"


def rmsnorm(x, g):
    x32 = x.astype(jnp.float32)
    y = x32 * lax.rsqrt(jnp.mean(x32 * x32, axis=-1, keepdims=True) + EPS)
    return (y * g.astype(jnp.float32)).astype(x.dtype)


def modulate(h, shift, scale):
    return h * (1.0 + scale[:, None, :]) + shift[:, None, :]


def causal_depthwise_conv(x, w, b):
    C = x.shape[-1]
    y = lax.conv_general_dilated(x, w[:, None, :].astype(x.dtype), window_strides=(1,),
                                 padding=[(CONV_WIDTH - 1, 0)],
                                 dimension_numbers=('NWC', 'WIO', 'NWC'),
                                 feature_group_count=C)
    return y + b


def rg_lru(x, w_a, b_a, w_x, b_x, lam):
    B, S, _ = x.shape
    x32 = x.astype(jnp.float32)
    xh = x32.reshape(B, S, RNN_HEADS, RNN_HEAD_DIM)
    r = jax.nn.sigmoid(jnp.einsum('bshi,hij->bshj', xh, w_a.astype(jnp.float32)).reshape(B, S, D_RNN)
                       + b_a.astype(jnp.float32))
    i = jax.nn.sigmoid(jnp.einsum('bshi,hij->bshj', xh, w_x.astype(jnp.float32)).reshape(B, S, D_RNN)
                       + b_x.astype(jnp.float32))
    log_a = -LRU_C * r * jax.nn.softplus(-lam.astype(jnp.float32))
    a = jnp.exp(log_a)
    mult = jnp.sqrt(-jnp.expm1(2.0 * log_a))
    u = mult * (i * x32)

    def combine(left, right):
        a_l, h_l = left
        a_r, h_r = right
        return a_l * a_r, a_r * h_l + h_r

    _, h = lax.associative_scan(combine, (a, u), axis=1)
    return h.astype(x.dtype)


def spatial_gating(u, v, ln_g, ln_b, w_s, b_s):
    B, S, _ = v.shape
    v32 = v.astype(jnp.float32)
    mu = jnp.mean(v32, axis=-1, keepdims=True)
    var = jnp.mean(jnp.square(v32 - mu), axis=-1, keepdims=True)
    vn = ((v32 - mu) * lax.rsqrt(var + EPS) * ln_g.astype(jnp.float32) + ln_b.astype(jnp.float32)).astype(v.dtype)
    vc = vn.reshape(B, S // SG_CHUNK, SG_CHUNK, SG_GROUPS, SG_GROUP_DIM)
    causal = jnp.tril(jnp.ones((SG_CHUNK, SG_CHUNK), dtype=bool))
    w = jnp.where(causal[None], w_s, jnp.zeros((), w_s.dtype))
    sv = jnp.einsum('gts,bnsgc->bntgc', w, vc) + b_s.T[None, None, :, :, None]
    return u * sv.reshape(B, S, D_SG)


def hybrid_mixer(h, w_in, conv_w, conv_b, lru_wa, lru_ba, lru_wx, lru_bx, lru_lam,
                 sg_ln_g, sg_ln_b, sg_ws, sg_bs, w_br_rnn, w_br_sg, w_out):
    z = h @ w_in
    rnn_x, rnn_gate, sg_u, sg_v, g_rnn, g_sg = jnp.split(z, IN_SPLITS, axis=-1)
    y_rnn = rg_lru(causal_depthwise_conv(rnn_x, conv_w, conv_b), lru_wa, lru_ba, lru_wx, lru_bx, lru_lam)
    y_rnn = y_rnn * jax.nn.gelu(rnn_gate)
    y_sg = spatial_gating(jax.nn.gelu(sg_u), jax.nn.gelu(sg_v), sg_ln_g, sg_ln_b, sg_ws, sg_bs)
    m = jax.nn.sigmoid(g_rnn) * (y_rnn @ w_br_rnn) + jax.nn.sigmoid(g_sg) * (y_sg @ w_br_sg)
    return m @ w_out


def moe_ffn(h, w_router, b_router, w_gu, b_gu, w_down, b_down):
    B, S, D = h.shape
    N = B * S
    t = h.reshape(N, D)
    logits = (t @ w_router + b_router).astype(jnp.float32)
    top_val, top_idx = lax.top_k(logits, TOP_K)
    top_w = jax.nn.softmax(top_val, axis=-1)
    A = N * TOP_K
    cap = A + N_EXPERTS * MOE_BLOCK
    n_blocks = cap // MOE_BLOCK
    flat_e = top_idx.reshape(A).astype(jnp.int32)
    flat_tok = (jnp.arange(A, dtype=jnp.int32) // TOP_K)
    flat_w = top_w.reshape(A)
    order = jnp.argsort(flat_e)
    sorted_e = flat_e[order]
    counts = jnp.bincount(flat_e, length=N_EXPERTS).astype(jnp.int32)
    padded = ((counts + MOE_BLOCK - 1) // MOE_BLOCK) * MOE_BLOCK
    start = jnp.cumsum(counts) - counts
    padded_end = jnp.cumsum(padded)
    padded_start = padded_end - padded
    dest = padded_start[sorted_e] + jnp.arange(A, dtype=jnp.int32) - start[sorted_e]
    slot_tok = jnp.zeros((cap,), jnp.int32).at[dest].set(flat_tok[order])
    slot_w = jnp.zeros((cap,), t.dtype).at[dest].set(flat_w[order].astype(t.dtype))
    block_e = jnp.searchsorted(padded_end, jnp.arange(n_blocks, dtype=jnp.int32) * MOE_BLOCK, side='right')
    block_e = jnp.minimum(block_e, N_EXPERTS - 1).astype(jnp.int32)

    def expert_block(args):
        tok, w, e = args
        xb = t[tok]
        gu = xb @ w_gu[e] + b_gu[e]
        gate, up = gu[:, :D_EXPERT], gu[:, D_EXPERT:]
        gate = jnp.minimum(gate, SWIGLU_LIMIT)
        up = jnp.clip(up, -SWIGLU_LIMIT, SWIGLU_LIMIT)
        act = (up + 1.0) * (gate * jax.nn.sigmoid(SWIGLU_ALPHA * gate))
        return (act @ w_down[e] + b_down[e]) * w[:, None]

    ys = lax.map(expert_block, (slot_tok.reshape(n_blocks, MOE_BLOCK),
                                slot_w.reshape(n_blocks, MOE_BLOCK), block_e))
    out = jax.ops.segment_sum(ys.reshape(cap, D), slot_tok, num_segments=N)
    return out.reshape(B, S, D)


def setup_inputs(seed: int = 0) -> dict:
    key = jax.random.key(seed)
    ks = jax.random.split(key, 32)
    f32 = jnp.float32
    L = DEPTH
    nrm = lambda k, shape, s: jax.random.normal(k, shape, f32) * s
    a0 = jax.random.uniform(ks[10], (L, D_RNN), f32, 0.9, 0.999)
    s0 = a0 ** (1.0 / LRU_C)
    lam = jnp.log(s0) - jnp.log1p(-s0)
    return {
        "x": nrm(ks[0], (BATCH, SEQ, D_MODEL), 1.0),
        "c": nrm(ks[1], (BATCH, D_MODEL), 1.0),
        "ada_w": nrm(ks[2], (L, D_MODEL, N_MOD * D_MODEL), 0.5 * D_MODEL ** -0.5),
        "ada_b": nrm(ks[3], (L, N_MOD * D_MODEL), 0.02),
        "norm1_g": 1.0 + nrm(ks[4], (L, D_MODEL), 0.02),
        "w_in": nrm(ks[5], (L, D_MODEL, D_IN), D_MODEL ** -0.5),
        "conv_w": nrm(ks[6], (L, CONV_WIDTH, D_RNN), CONV_WIDTH ** -0.5),
        "conv_b": nrm(ks[7], (L, D_RNN), 0.02),
        "lru_wa": nrm(ks[8], (L, RNN_HEADS, RNN_HEAD_DIM, RNN_HEAD_DIM), RNN_HEAD_DIM ** -0.5),
        "lru_ba": nrm(ks[9], (L, D_RNN), 0.1),
        "lru_wx": nrm(ks[11], (L, RNN_HEADS, RNN_HEAD_DIM, RNN_HEAD_DIM), RNN_HEAD_DIM ** -0.5),
        "lru_bx": nrm(ks[12], (L, D_RNN), 0.1),
        "lru_lam": lam,
        "sg_ln_g": 1.0 + nrm(ks[13], (L, D_SG), 0.02),
        "sg_ln_b": nrm(ks[14], (L, D_SG), 0.02),
        "sg_ws": nrm(ks[15], (L, SG_GROUPS, SG_CHUNK, SG_CHUNK), SG_CHUNK ** -0.5),
        "sg_bs": 1.0 + nrm(ks[16], (L, SG_GROUPS, SG_CHUNK), 0.02),
        "w_br_rnn": nrm(ks[17], (L, D_RNN, D_MODEL), D_RNN ** -0.5),
        "w_br_sg": nrm(ks[18], (L, D_SG, D_MODEL), D_SG ** -0.5),
        "w_out": nrm(ks[19], (L, D_MODEL, D_MODEL), D_MODEL ** -0.5),
        "norm2_g": 1.0 + nrm(ks[20], (L, D_MODEL), 0.02),
        "w_router": nrm(ks[21], (L, D_MODEL, N_EXPERTS), D_MODEL ** -0.5),
        "b_router": nrm(ks[22], (L, N_EXPERTS), 0.01),
        "w_gu": nrm(ks[23], (L, N_EXPERTS, D_MODEL, 2 * D_EXPERT), D_MODEL ** -0.5),
        "b_gu": nrm(ks[24], (L, N_EXPERTS, 2 * D_EXPERT), 0.02),
        "w_down": nrm(ks[25], (L, N_EXPERTS, D_EXPERT, D_MODEL), D_EXPERT ** -0.5),
        "b_down": nrm(ks[26], (L, N_EXPERTS, D_MODEL), 0.02),
        "final_g": 1.0 + nrm(ks[27], (D_MODEL,), 0.02),
    }


def reference(x, c, ada_w, ada_b, norm1_g, w_in, conv_w, conv_b, lru_wa, lru_ba, lru_wx, lru_bx,
              lru_lam, sg_ln_g, sg_ln_b, sg_ws, sg_bs, w_br_rnn, w_br_sg, w_out, norm2_g,
              w_router, b_router, w_gu, b_gu, w_down, b_down, final_g):
    for l in range(DEPTH):
        mod = jax.nn.silu(c) @ ada_w[l] + ada_b[l]
        sh1, sc1, gt1, sh2, sc2, gt2 = jnp.split(mod, N_MOD, axis=-1)
        h = modulate(rmsnorm(x, norm1_g[l]), sh1, sc1)
        x = x + gt1[:, None, :] * hybrid_mixer(
            h, w_in[l], conv_w[l], conv_b[l], lru_wa[l], lru_ba[l], lru_wx[l], lru_bx[l], lru_lam[l],
            sg_ln_g[l], sg_ln_b[l], sg_ws[l], sg_bs[l], w_br_rnn[l], w_br_sg[l], w_out[l])
        h = modulate(rmsnorm(x, norm2_g[l]), sh2, sc2)
        x = x + gt2[:, None, :] * moe_ffn(h, w_router[l], b_router[l], w_gu[l], b_gu[l], w_down[l], b_down[l])
    return rmsnorm(x, final_g)
```

```python
import functools

import jax
import jax.numpy as jnp
from jax import lax
from jax.experimental import pallas as pl
from jax.experimental.pallas import tpu as pltpu

F32 = jnp.float32
BF16 = jnp.bfloat16
I32 = jnp.int32

D_MODEL = 1024
D_RNN = 1024
RNN_HEADS = 16
RNN_HEAD_DIM = D_RNN // RNN_HEADS
CONV_WIDTH = 4
LRU_C = 8.0
D_SG = 1024
SG_GROUPS = 8
SG_GROUP_DIM = D_SG // SG_GROUPS
SG_CHUNK = 128
N_EXPERTS = 32
TOP_K = 4
D_EXPERT = 1024
SWIGLU_LIMIT = 7.0
SWIGLU_ALPHA = 1.702
EPS = 1e-6
N_MOD = 6
D_IN = 2 * D_RNN + 2 * D_SG + 2 * D_MODEL

SUBLANES = 8
GATE_BLOCK = 256
N_GATE_BLOCKS = D_RNN // GATE_BLOCK

ADA_TN = 1536
INPROJ_TM = 512
INPROJ_TN = 1024
MIXER_TS = 256
MOE_BM = 256
GATHER_ROWS = 256
FINAL_TM = 256
VMEM_LIMIT = 56 * 1024 * 1024


def _sigmoid(x):
    return 0.5 * jnp.tanh(0.5 * x) + 0.5


def _gelu_tanh(x):
    return 0.5 * x * (1.0 + jnp.tanh(0.7978845608028654 * (x + 0.044715 * (x * x * x))))


def _bdot(a, b):
    return jnp.dot(a, b, preferred_element_type=F32)


def _ada_kernel(c_ref, w_ref, b_ref, o_ref):
    c = c_ref[...]
    s = c * _sigmoid(c)
    o_ref[...] = jnp.dot(s, w_ref[...], preferred_element_type=F32,
                         precision=lax.Precision.HIGHEST) + b_ref[...]


def _ada(c, ada_w, ada_b):
    b, d = c.shape
    n = ada_w.shape[1]
    return pl.pallas_call(
        _ada_kernel,
        grid=(n // ADA_TN,),
        in_specs=[
            pl.BlockSpec((b, d), lambda j: (0, 0)),
            pl.BlockSpec((d, ADA_TN), lambda j: (0, j)),
            pl.BlockSpec((1, ADA_TN), lambda j: (0, j)),
        ],
        out_specs=pl.BlockSpec((b, ADA_TN), lambda j: (0, j)),
        out_shape=jax.ShapeDtypeStruct((b, n), F32),
        compiler_params=pltpu.CompilerParams(
            dimension_semantics=("arbitrary",), vmem_limit_bytes=VMEM_LIMIT),
        name="ada",
    )(c, ada_w, ada_b.reshape(1, n))


def _norm_mod(x, g, sc, sh):
    ms = jnp.mean(x * x, axis=-1, keepdims=True)
    y = x * lax.rsqrt(ms + EPS)
    return (y * g) * (1.0 + sc) + sh


def _inproj_kernel(x_ref, g_ref, sc_ref, sh_ref, w_ref, z_ref):
    h = _norm_mod(x_ref[...], g_ref[...], sc_ref[0], sh_ref[0]).astype(BF16)
    for j in range(D_IN // INPROJ_TN):
        cols = slice(j * INPROJ_TN, (j + 1) * INPROJ_TN)
        z_ref[:, cols] = _bdot(h, w_ref[:, cols]).astype(BF16)


def _inproj(x2d, g, sc, sh, w_in_bf16, seq):
    n, d = x2d.shape
    tiles_per_seq = seq // INPROJ_TM
    bvec = lambda i: (i // tiles_per_seq, 0, 0)
    return pl.pallas_call(
        _inproj_kernel,
        grid=(n // INPROJ_TM,),
        in_specs=[
            pl.BlockSpec((INPROJ_TM, d), lambda i: (i, 0)),
            pl.BlockSpec((1, d), lambda i: (0, 0)),
            pl.BlockSpec((1, 1, d), bvec),
            pl.BlockSpec((1, 1, d), bvec),
            pl.BlockSpec((d, D_IN), lambda i: (0, 0)),
        ],
        out_specs=pl.BlockSpec((INPROJ_TM, D_IN), lambda i: (i, 0)),
        out_shape=jax.ShapeDtypeStruct((n, D_IN), BF16),
        compiler_params=pltpu.CompilerParams(
            dimension_semantics=("arbitrary",), vmem_limit_bytes=VMEM_LIMIT),
        name="inproj",
    )(x2d, g, sc, sh, w_in_bf16)


def _mixer_kernel(z_ref, x_ref, gt1_ref, sc2_ref, sh2_ref,
                  convw_ref, convb_ref, wa_ref, ba_ref, wx_ref, bx_ref, lam_ref,
                  lng_ref, lnb_ref, ws_ref, bs_ref, wbr_ref, wbs_ref, wout_ref,
                  n2g_ref, wr_ref, br_ref,
                  x2_ref, h2_ref, idx_ref, tw_ref, rank_ref, cnt_ref,
                  xp_ref, a_ref, hh_ref, sv_ref, hstate_ref):
    ts = MIXER_TS
    b = pl.program_id(0)
    j = pl.program_id(1)

    @pl.when(j == 0)
    def _():
        xp_ref[0:SUBLANES, :] = jnp.zeros((SUBLANES, D_RNN), F32)
        hstate_ref[...] = jnp.zeros_like(hstate_ref)

    @pl.when((b == 0) & (j == 0))
    def _():
        cnt_ref[...] = jnp.zeros_like(cnt_ref)

    rnn_x = z_ref[:, 0:D_RNN].astype(F32)
    xp_ref[SUBLANES:ts + SUBLANES, :] = rnn_x
    cw = convw_ref[...]
    xc = (cw[3:4] * rnn_x
          + cw[2:3] * xp_ref[SUBLANES - 1:ts + SUBLANES - 1, :]
          + cw[1:2] * xp_ref[SUBLANES - 2:ts + SUBLANES - 2, :]
          + cw[0:1] * xp_ref[SUBLANES - 3:ts + SUBLANES - 3, :]) + convb_ref[...]
    xp_ref[0:SUBLANES, :] = xp_ref[ts:ts + SUBLANES, :]

    xcb = xc.astype(BF16)
    r_parts, i_parts = [], []
    for g in range(N_GATE_BLOCKS):
        blk = xcb[:, g * GATE_BLOCK:(g + 1) * GATE_BLOCK]
        r_parts.append(_bdot(blk, wa_ref[g]))
        i_parts.append(_bdot(blk, wx_ref[g]))
    r = _sigmoid(jnp.concatenate(r_parts, axis=1) + ba_ref[...])
    ig = _sigmoid(jnp.concatenate(i_parts, axis=1) + bx_ref[...])

    nl = -lam_ref[...]
    softplus = jnp.maximum(nl, 0.0) + jnp.log(1.0 + jnp.exp(-jnp.abs(nl)))
    log_a = (-LRU_C) * r * softplus
    a = jnp.exp(log_a)
    u = jnp.sqrt(1.0 - a * a) * (ig * xc)

    row = lax.broadcasted_iota(I32, (ts, D_RNN), 0) & (SUBLANES - 1)
    hloc = u
    for k in (1, 2, 4):
        keep = row >= k
        a_sh = pltpu.roll(a, k, 0)
        h_sh = pltpu.roll(hloc, k, 0)
        hloc = jnp.where(keep, hloc + a * h_sh, hloc)
        a = jnp.where(keep, a * a_sh, a)
    a_ref[...] = a
    hh_ref[...] = hloc

    def carry_body(gi, hc):
        rows = pl.ds(pl.multiple_of(gi * SUBLANES, SUBLANES), SUBLANES)
        hg = hh_ref[rows, :] + a_ref[rows, :] * hc
        hh_ref[rows, :] = hg
        return jnp.broadcast_to(hg[SUBLANES - 1:SUBLANES, :], (SUBLANES, D_RNN))

    hc_last = lax.fori_loop(0, ts // SUBLANES, carry_body, hstate_ref[...])
    hstate_ref[...] = hc_last

    y_rnn = (hh_ref[...] * _gelu_tanh(z_ref[:, D_RNN:2 * D_RNN].astype(F32))).astype(BF16)

    gv = _gelu_tanh(z_ref[:, 2 * D_RNN + D_SG:2 * D_RNN + 2 * D_SG].astype(F32))
    mu = jnp.mean(gv, axis=-1, keepdims=True)
    dv = gv - mu
    var = jnp.mean(dv * dv, axis=-1, keepdims=True)
    vn = (dv * lax.rsqrt(var + EPS) * lng_ref[...] + lnb_ref[...]).astype(BF16)
    tr = lax.broadcasted_iota(I32, (SG_CHUNK, SG_CHUNK), 0)
    tc = lax.broadcasted_iota(I32, (SG_CHUNK, SG_CHUNK), 1)
    causal = tc <= tr
    for g in range(SG_GROUPS):
        wg = jnp.where(causal, ws_ref[g], 0.0).astype(BF16)
        cols = slice(g * SG_GROUP_DIM, (g + 1) * SG_GROUP_DIM)
        for n in range(ts // SG_CHUNK):
            rows = slice(n * SG_CHUNK, (n + 1) * SG_CHUNK)
            sv_ref[rows, cols] = _bdot(wg, vn[rows, cols]) + bs_ref[:, cols]
    gu = _gelu_tanh(z_ref[:, 2 * D_RNN:2 * D_RNN + D_SG].astype(F32))
    y_sg = (gu * sv_ref[...]).astype(BF16)

    g_rnn = z_ref[:, 2 * D_RNN + 2 * D_SG:2 * D_RNN + 2 * D_SG + D_MODEL].astype(F32)
    g_sg = z_ref[:, 2 * D_RNN + 2 * D_SG + D_MODEL:D_IN].astype(F32)
    m = (_sigmoid(g_rnn) * _bdot(y_rnn, wbr_ref[...])
         + _sigmoid(g_sg) * _bdot(y_sg, wbs_ref[...])).astype(BF16)
    x2 = x_ref[...] + gt1_ref[0] * _bdot(m, wout_ref[...])
    x2_ref[...] = x2

    h2 = _norm_mod(x2, n2g_ref[...], sc2_ref[0], sh2_ref[0])
    h2_ref[...] = h2
    logits = lax.dot_general(wr_ref[...], h2, (((1,), (1,)), ((), ())),
                             preferred_element_type=F32,
                             precision=lax.Precision.HIGHEST) + br_ref[...]
    e_iota = lax.broadcasted_iota(I32, (N_EXPERTS, ts), 0)
    v = logits
    vals, idxs, sels = [], [], []
    for _ in range(TOP_K):
        mx = jnp.max(v, axis=0, keepdims=True)
        ik = jnp.min(jnp.where(v == mx, e_iota, N_EXPERTS), axis=0, keepdims=True)
        sel = e_iota == ik
        v = jnp.where(sel, -jnp.inf, v)
        vals.append(mx)
        idxs.append(ik)
        sels.append(sel)
    exps = [jnp.exp(val - vals[0]) for val in vals]
    denom = exps[0] + exps[1] + exps[2] + exps[3]
    idx_ref[...] = jnp.concatenate(idxs, axis=0)
    tw_ref[...] = jnp.concatenate([e / denom for e in exps], axis=0)

    onehot = jnp.zeros((N_EXPERTS, ts), F32)
    for sel in sels:
        onehot = jnp.where(sel, 1.0, onehot)
    sr = lax.broadcasted_iota(I32, (ts, ts), 0)
    st = lax.broadcasted_iota(I32, (ts, ts), 1)
    before = jnp.where(sr < st, 1.0, 0.0).astype(BF16)
    total = cnt_ref[...] + _bdot(onehot.astype(BF16), before)
    ranks = [jnp.sum(jnp.where(sel, total, 0.0), axis=0, keepdims=True) for sel in sels]
    rank_ref[...] = jnp.concatenate(ranks, axis=0).astype(I32)
    cnt_ref[...] = cnt_ref[...] + jnp.sum(onehot, axis=1, keepdims=True)


def _mixer(z, x2d, gt1, sc2, sh2, conv_w, conv_b, wa_bd, ba, wx_bd, bx, lam,
           ln_g, ln_b, ws, bs_tile, wbr, wbs, wout, n2g, wr_t, br, batch, seq):
    n, d = x2d.shape
    ts = MIXER_TS
    nt = seq // ts
    tok = lambda b, j: (b * nt + j, 0)
    tokt = lambda b, j: (0, b * nt + j)
    bvec = lambda b, j: (b, 0, 0)
    c2 = lambda b, j: (0, 0)
    c3 = lambda b, j: (0, 0, 0)
    in_specs = [
        pl.BlockSpec((ts, D_IN), tok),
        pl.BlockSpec((ts, d), tok),
        pl.BlockSpec((1, 1, d), bvec),
        pl.BlockSpec((1, 1, d), bvec),
        pl.BlockSpec((1, 1, d), bvec),
        pl.BlockSpec((CONV_WIDTH, D_RNN), c2),
        pl.BlockSpec((1, D_RNN), c2),
        pl.BlockSpec((N_GATE_BLOCKS, GATE_BLOCK, GATE_BLOCK), c3),
        pl.BlockSpec((1, D_RNN), c2),
        pl.BlockSpec((N_GATE_BLOCKS, GATE_BLOCK, GATE_BLOCK), c3),
        pl.BlockSpec((1, D_RNN), c2),
        pl.BlockSpec((1, D_RNN), c2),
        pl.BlockSpec((1, D_SG), c2),
        pl.BlockSpec((1, D_SG), c2),
        pl.BlockSpec((SG_GROUPS, SG_CHUNK, SG_CHUNK), c3),
        pl.BlockSpec((SG_CHUNK, D_SG), c2),
        pl.BlockSpec((D_RNN, d), c2),
        pl.BlockSpec((D_SG, d), c2),
        pl.BlockSpec((d, d), c2),
        pl.BlockSpec((1, d), c2),
        pl.BlockSpec((N_EXPERTS, d), c2),
        pl.BlockSpec((N_EXPERTS, 1), c2),
    ]
    out_specs = [
        pl.BlockSpec((ts, d), tok),
        pl.BlockSpec((ts, d), tok),
        pl.BlockSpec((TOP_K, ts), tokt),
        pl.BlockSpec((TOP_K, ts), tokt),
        pl.BlockSpec((TOP_K, ts), tokt),
        pl.BlockSpec((N_EXPERTS, 1), c2),
    ]
    out_shape = [
        jax.ShapeDtypeStruct((n, d), F32),
        jax.ShapeDtypeStruct((n, d), F32),
        jax.ShapeDtypeStruct((TOP_K, n), I32),
        jax.ShapeDtypeStruct((TOP_K, n), F32),
        jax.ShapeDtypeStruct((TOP_K, n), I32),
        jax.ShapeDtypeStruct((N_EXPERTS, 1), F32),
    ]
    scratch = [
        pltpu.VMEM((ts + 2 * SUBLANES, D_RNN), F32),
        pltpu.VMEM((ts, D_RNN), F32),
        pltpu.VMEM((ts, D_RNN), F32),
        pltpu.VMEM((ts, D_SG), F32),
        pltpu.VMEM((SUBLANES, D_RNN), F32),
    ]
    return pl.pallas_call(
        _mixer_kernel,
        grid=(batch, nt),
        in_specs=in_specs,
        out_specs=out_specs,
        out_shape=out_shape,
        scratch_shapes=scratch,
        compiler_params=pltpu.CompilerParams(
            dimension_semantics=("arbitrary", "arbitrary"), vmem_limit_bytes=VMEM_LIMIT),
        name="mixer",
    )(z, x2d, gt1, sc2, sh2, conv_w, conv_b, wa_bd, ba, wx_bd, bx, lam,
      ln_g, ln_b, ws, bs_tile, wbr, wbs, wout, n2g, wr_t, br)


def _gather_kernel(idx_ref, table_ref, out_ref, sem):
    def row_copy(i):
        return pltpu.make_async_copy(
            table_ref.at[pl.ds(idx_ref[0, 0, i], 1), :], out_ref.at[pl.ds(i, 1), :], sem)

    def start(i, carry):
        row_copy(i).start()
        return carry

    lax.fori_loop(0, GATHER_ROWS, start, 0)

    def wait(i, carry):
        row_copy(i).wait()
        return carry

    lax.fori_loop(0, GATHER_ROWS, wait, 0)


def _gather_rows(table, idx):
    m = idx.shape[0]
    d = table.shape[1]
    nb = m // GATHER_ROWS
    return pl.pallas_call(
        _gather_kernel,
        grid=(nb,),
        in_specs=[
            pl.BlockSpec((1, 1, GATHER_ROWS), lambda i: (i, 0, 0), memory_space=pltpu.SMEM),
            pl.BlockSpec(memory_space=pl.ANY),
        ],
        out_specs=pl.BlockSpec((GATHER_ROWS, d), lambda i: (i, 0)),
        out_shape=jax.ShapeDtypeStruct((m, d), table.dtype),
        scratch_shapes=[pltpu.SemaphoreType.DMA],
        compiler_params=pltpu.CompilerParams(
            dimension_semantics=("arbitrary",), vmem_limit_bytes=VMEM_LIMIT),
        name="gather_rows",
    )(idx.reshape(nb, 1, GATHER_ROWS), table)


def _expert_kernel(be_ref, nu_ref, xs_ref, wgu_ref, bgu_ref, wd_ref, bd_ref, ys_ref):
    i = pl.program_id(0)

    @pl.when(i < nu_ref[0])
    def _():
        xb = xs_ref[...].astype(BF16)
        gu = _bdot(xb, wgu_ref[0]) + bgu_ref[0]
        gate = jnp.minimum(gu[:, :D_EXPERT], SWIGLU_LIMIT)
        up = jnp.clip(gu[:, D_EXPERT:], -SWIGLU_LIMIT, SWIGLU_LIMIT)
        act = (up + 1.0) * (gate * _sigmoid(SWIGLU_ALPHA * gate))
        ys_ref[...] = _bdot(act.astype(BF16), wd_ref[0]) + bd_ref[0]

    @pl.when(i >= nu_ref[0])
    def _():
        ys_ref[...] = jnp.zeros_like(ys_ref)


def _experts(block_e, n_used, xs, w_gu, b_gu, w_down, b_down):
    cap, d = xs.shape
    nb = cap // MOE_BM
    grid_spec = pltpu.PrefetchScalarGridSpec(
        num_scalar_prefetch=2,
        grid=(nb,),
        in_specs=[
            pl.BlockSpec((MOE_BM, d), lambda i, be, nu: (i, 0)),
            pl.BlockSpec((1, d, 2 * D_EXPERT), lambda i, be, nu: (be[i], 0, 0)),
            pl.BlockSpec((1, 1, 2 * D_EXPERT), lambda i, be, nu: (be[i], 0, 0)),
            pl.BlockSpec((1, D_EXPERT, d), lambda i, be, nu: (be[i], 0, 0)),
            pl.BlockSpec((1, 1, d), lambda i, be, nu: (be[i], 0, 0)),
        ],
        out_specs=pl.BlockSpec((MOE_BM, d), lambda i, be, nu: (i, 0)),
    )
    return pl.pallas_call(
        _expert_kernel,
        grid_spec=grid_spec,
        out_shape=jax.ShapeDtypeStruct((cap, d), F32),
        compiler_params=pltpu.CompilerParams(
            dimension_semantics=("arbitrary",), vmem_limit_bytes=VMEM_LIMIT),
        name="experts",
    )(block_e, n_used, xs, w_gu, b_gu, w_down, b_down)


def _final_kernel(x2_ref, yg_ref, tw_ref, gt2_ref, fg_ref, o_ref):
    tw = tw_ref[...]
    moe = tw[:, 0:1] * yg_ref[0]
    for k in range(1, TOP_K):
        moe = moe + tw[:, k:k + 1] * yg_ref[k]
    x3 = x2_ref[...] + gt2_ref[0] * moe
    ms = jnp.mean(x3 * x3, axis=-1, keepdims=True)
    o_ref[...] = (x3 * lax.rsqrt(ms + EPS)) * fg_ref[...]


def _final(x2, yg, tw_tok, gt2, final_g, seq):
    n, d = x2.shape
    tiles_per_seq = seq // FINAL_TM
    return pl.pallas_call(
        _final_kernel,
        grid=(n // FINAL_TM,),
        in_specs=[
            pl.BlockSpec((FINAL_TM, d), lambda i: (i, 0)),
            pl.BlockSpec((TOP_K, FINAL_TM, d), lambda i: (0, i, 0)),
            pl.BlockSpec((FINAL_TM, TOP_K), lambda i: (i, 0)),
            pl.BlockSpec((1, 1, d), lambda i: (i // tiles_per_seq, 0, 0)),
            pl.BlockSpec((1, d), lambda i: (0, 0)),
        ],
        out_specs=pl.BlockSpec((FINAL_TM, d), lambda i: (i, 0)),
        out_shape=jax.ShapeDtypeStruct((n, d), F32),
        compiler_params=pltpu.CompilerParams(
            dimension_semantics=("arbitrary",), vmem_limit_bytes=VMEM_LIMIT),
        name="final",
    )(x2, yg, tw_tok, gt2, final_g)


def _block_diag(w):
    per = GATE_BLOCK // RNN_HEAD_DIM
    w4 = w.reshape(N_GATE_BLOCKS, per, RNN_HEAD_DIM, RNN_HEAD_DIM)
    eye = jnp.eye(per, dtype=w.dtype)
    bd = jnp.einsum("gpij,pq->gpiqj", w4, eye)
    return bd.reshape(N_GATE_BLOCKS, GATE_BLOCK, GATE_BLOCK)


def _layer(x2d, c, batch, seq, ada_w, ada_b, norm1_g, w_in, conv_w, conv_b, lru_wa, lru_ba,
           lru_wx, lru_bx, lru_lam, sg_ln_g, sg_ln_b, sg_ws, sg_bs, w_br_rnn, w_br_sg, w_out,
           norm2_g, w_router, b_router, w_gu, b_gu, w_down, b_down):
    n, d = x2d.shape
    mod = _ada(c, ada_w, ada_b)
    sh1, sc1, gt1, sh2, sc2, gt2 = [
        mod[:, i * d:(i + 1) * d].reshape(batch, 1, d) for i in range(N_MOD)]
    row = lambda v: v.reshape(1, -1)

    z = _inproj(x2d, row(norm1_g), sc1, sh1, w_in.astype(BF16), seq)

    bs_tile = jnp.repeat(sg_bs.T, SG_GROUP_DIM, axis=1)
    x2, h2, top_idx, top_w, rank, counts = _mixer(
        z, x2d, gt1, sc2, sh2, conv_w, row(conv_b),
        _block_diag(lru_wa).astype(BF16), row(lru_ba), _block_diag(lru_wx).astype(BF16), row(lru_bx),
        row(lru_lam), row(sg_ln_g), row(sg_ln_b), sg_ws, bs_tile,
        w_br_rnn.astype(BF16), w_br_sg.astype(BF16), w_out.astype(BF16), row(norm2_g),
        w_router.T, b_router.reshape(N_EXPERTS, 1), batch, seq)

    a_total = n * TOP_K
    cap = a_total + N_EXPERTS * MOE_BM
    nb = cap // MOE_BM
    cnt = counts[:, 0].astype(I32)
    padded = ((cnt + MOE_BM - 1) // MOE_BM) * MOE_BM
    padded_end = jnp.cumsum(padded)
    padded_start = padded_end - padded
    pos = padded_start[top_idx] + rank
    tok_ids = jnp.broadcast_to(jnp.arange(n, dtype=I32)[None, :], (TOP_K, n))
    slot_tok = jnp.zeros((cap,), I32).at[pos.reshape(-1)].set(tok_ids.reshape(-1))
    block_e = jnp.searchsorted(padded_end, jnp.arange(nb, dtype=I32) * MOE_BM, side="right")
    block_e = jnp.minimum(block_e, N_EXPERTS - 1).astype(I32)
    n_used = (padded_end[-1] // MOE_BM).astype(I32).reshape(1)

    xs = _gather_rows(h2, slot_tok)
    ys = _experts(block_e, n_used, xs, w_gu.astype(BF16), b_gu.reshape(N_EXPERTS, 1, -1),
                  w_down.astype(BF16), b_down.reshape(N_EXPERTS, 1, -1))
    yg = _gather_rows(ys, pos.reshape(-1)).reshape(TOP_K, n, d)
    return x2, yg, top_w.T, gt2


def kernel(x, c, ada_w, ada_b, norm1_g, w_in, conv_w, conv_b, lru_wa, lru_ba, lru_wx, lru_bx,
           lru_lam, sg_ln_g, sg_ln_b, sg_ws, sg_bs, w_br_rnn, w_br_sg, w_out, norm2_g,
           w_router, b_router, w_gu, b_gu, w_down, b_down, final_g):
    batch, seq, d = x.shape
    depth = ada_w.shape[0]
    assert depth == 1, "the combine is fused with the final norm, which follows the only layer"
    x2d = x.reshape(batch * seq, d)
    l = 0
    x2, yg, tw_tok, gt2 = _layer(
        x2d, c, batch, seq, ada_w[l], ada_b[l], norm1_g[l], w_in[l], conv_w[l], conv_b[l],
        lru_wa[l], lru_ba[l], lru_wx[l], lru_bx[l], lru_lam[l], sg_ln_g[l], sg_ln_b[l],
        sg_ws[l], sg_bs[l], w_br_rnn[l], w_br_sg[l], w_out[l], norm2_g[l], w_router[l],
        b_router[l], w_gu[l], b_gu[l], w_down[l], b_down[l])
    out = _final(x2, yg, tw_tok, gt2, final_g.reshape(1, d), seq)
    return out.reshape(batch, seq, d)
```

```python
import functools

import jax
import jax.numpy as jnp
from jax import lax
from jax.experimental import pallas as pl
from jax.experimental.pallas import tpu as pltpu
from jax.experimental.pallas import tpu_sc as plsc

F32 = jnp.float32
BF16 = jnp.bfloat16
I32 = jnp.int32

D_MODEL = 1024
D_RNN = 1024
RNN_HEADS = 16
RNN_HEAD_DIM = D_RNN // RNN_HEADS
CONV_WIDTH = 4
LRU_C = 8.0
D_SG = 1024
SG_GROUPS = 8
SG_GROUP_DIM = D_SG // SG_GROUPS
SG_CHUNK = 128
N_EXPERTS = 32
TOP_K = 4
D_EXPERT = 1024
SWIGLU_LIMIT = 7.0
SWIGLU_ALPHA = 1.702
EPS = 1e-6
N_MOD = 6
D_IN = 2 * D_RNN + 2 * D_SG + 2 * D_MODEL

SUBLANES = 8
GATE_BLOCK = 256
N_GATE_BLOCKS = D_RNN // GATE_BLOCK

ADA_TN = 1536
INPROJ_TM = 512
INPROJ_TN = 1024
MIXER_TS = 256
MOE_BM = 256
SC_WINDOW = 128
PACK_W = D_MODEL // 4
FINAL_TM = 256
VMEM_LIMIT = 56 * 1024 * 1024


def _sigmoid(x):
    return 0.5 * jnp.tanh(0.5 * x) + 0.5


def _gelu_tanh(x):
    return 0.5 * x * (1.0 + jnp.tanh(0.7978845608028654 * (x + 0.044715 * (x * x * x))))


def _bdot(a, b):
    return jnp.dot(a, b, preferred_element_type=F32)


def _pack_halves(v):
    bits = lax.bitcast_convert_type(v.astype(BF16).astype(F32), jnp.uint32)
    hi = bits[:, :2 * PACK_W]
    lo = lax.shift_right_logical(bits[:, 2 * PACK_W:], jnp.uint32(16))
    word = hi | lo
    return word[:, :PACK_W], word[:, PACK_W:]


def _unpack_halves(wa, wb):
    mask = jnp.uint32(0xFFFF0000)
    parts = [wa & mask, wb & mask, lax.shift_left(wa, jnp.uint32(16)), lax.shift_left(wb, jnp.uint32(16))]
    return jnp.concatenate([lax.bitcast_convert_type(p, F32) for p in parts], axis=1)


def _ada_kernel(c_ref, w_ref, b_ref, o_ref):
    c = c_ref[...]
    s = c * _sigmoid(c)
    o_ref[...] = jnp.dot(s, w_ref[...], preferred_element_type=F32,
                         precision=lax.Precision.HIGHEST) + b_ref[...]


def _ada(c, ada_w, ada_b):
    b, d = c.shape
    n = ada_w.shape[1]
    return pl.pallas_call(
        _ada_kernel,
        grid=(n // ADA_TN,),
        in_specs=[
            pl.BlockSpec((b, d), lambda j: (0, 0)),
            pl.BlockSpec((d, ADA_TN), lambda j: (0, j)),
            pl.BlockSpec((1, ADA_TN), lambda j: (0, j)),
        ],
        out_specs=pl.BlockSpec((b, ADA_TN), lambda j: (0, j)),
        out_shape=jax.ShapeDtypeStruct((b, n), F32),
        compiler_params=pltpu.CompilerParams(
            dimension_semantics=("arbitrary",), vmem_limit_bytes=VMEM_LIMIT),
        name="ada",
    )(c, ada_w, ada_b.reshape(1, n))


def _norm_mod(x, g, sc, sh):
    ms = jnp.mean(x * x, axis=-1, keepdims=True)
    y = x * lax.rsqrt(ms + EPS)
    return (y * g) * (1.0 + sc) + sh


def _inproj_kernel(x_ref, g_ref, sc_ref, sh_ref, w_ref, z_ref):
    h = _norm_mod(x_ref[...], g_ref[...], sc_ref[0], sh_ref[0]).astype(BF16)
    for j in range(D_IN // INPROJ_TN):
        cols = slice(j * INPROJ_TN, (j + 1) * INPROJ_TN)
        z_ref[:, cols] = _bdot(h, w_ref[:, cols]).astype(BF16)


def _inproj(x2d, g, sc, sh, w_in_bf16, seq):
    n, d = x2d.shape
    tiles_per_seq = seq // INPROJ_TM
    bvec = lambda i: (i // tiles_per_seq, 0, 0)
    return pl.pallas_call(
        _inproj_kernel,
        grid=(n // INPROJ_TM,),
        in_specs=[
            pl.BlockSpec((INPROJ_TM, d), lambda i: (i, 0)),
            pl.BlockSpec((1, d), lambda i: (0, 0)),
            pl.BlockSpec((1, 1, d), bvec),
            pl.BlockSpec((1, 1, d), bvec),
            pl.BlockSpec((d, D_IN), lambda i: (0, 0)),
        ],
        out_specs=pl.BlockSpec((INPROJ_TM, D_IN), lambda i: (i, 0)),
        out_shape=jax.ShapeDtypeStruct((n, D_IN), BF16),
        compiler_params=pltpu.CompilerParams(
            dimension_semantics=("arbitrary",), vmem_limit_bytes=VMEM_LIMIT),
        name="inproj",
    )(x2d, g, sc, sh, w_in_bf16)


def _mixer_kernel(z_ref, x_ref, gt1_ref, sc2_ref, sh2_ref,
                  convw_ref, convb_ref, wa_ref, ba_ref, wx_ref, bx_ref, lam_ref,
                  lng_ref, lnb_ref, ws_ref, bs_ref, wbr_ref, wbs_ref, wout_ref,
                  n2g_ref, wr_ref, br_ref,
                  x2_ref, h2a_ref, h2b_ref, idx_ref, tw_ref, rank_ref, cnt_ref,
                  xp_ref, a_ref, hh_ref, sv_ref, hstate_ref):
    ts = MIXER_TS
    b = pl.program_id(0)
    j = pl.program_id(1)

    @pl.when(j == 0)
    def _():
        xp_ref[0:SUBLANES, :] = jnp.zeros((SUBLANES, D_RNN), F32)
        hstate_ref[...] = jnp.zeros_like(hstate_ref)

    @pl.when((b == 0) & (j == 0))
    def _():
        cnt_ref[...] = jnp.zeros_like(cnt_ref)

    rnn_x = z_ref[:, 0:D_RNN].astype(F32)
    xp_ref[SUBLANES:ts + SUBLANES, :] = rnn_x
    cw = convw_ref[...]
    xc = (cw[3:4] * rnn_x
          + cw[2:3] * xp_ref[SUBLANES - 1:ts + SUBLANES - 1, :]
          + cw[1:2] * xp_ref[SUBLANES - 2:ts + SUBLANES - 2, :]
          + cw[0:1] * xp_ref[SUBLANES - 3:ts + SUBLANES - 3, :]) + convb_ref[...]
    xp_ref[0:SUBLANES, :] = xp_ref[ts:ts + SUBLANES, :]

    xcb = xc.astype(BF16)
    r_parts, i_parts = [], []
    for g in range(N_GATE_BLOCKS):
        blk = xcb[:, g * GATE_BLOCK:(g + 1) * GATE_BLOCK]
        r_parts.append(_bdot(blk, wa_ref[g]))
        i_parts.append(_bdot(blk, wx_ref[g]))
    r = _sigmoid(jnp.concatenate(r_parts, axis=1) + ba_ref[...])
    ig = _sigmoid(jnp.concatenate(i_parts, axis=1) + bx_ref[...])

    nl = -lam_ref[...]
    softplus = jnp.maximum(nl, 0.0) + jnp.log(1.0 + jnp.exp(-jnp.abs(nl)))
    log_a = (-LRU_C) * r * softplus
    a = jnp.exp(log_a)
    u = jnp.sqrt(1.0 - a * a) * (ig * xc)

    row = lax.broadcasted_iota(I32, (ts, D_RNN), 0) & (SUBLANES - 1)
    hloc = u
    for k in (1, 2, 4):
        keep = row >= k
        a_sh = pltpu.roll(a, k, 0)
        h_sh = pltpu.roll(hloc, k, 0)
        hloc = jnp.where(keep, hloc + a * h_sh, hloc)
        a = jnp.where(keep, a * a_sh, a)
    a_ref[...] = a
    hh_ref[...] = hloc

    def carry_body(gi, hc):
        rows = pl.ds(pl.multiple_of(gi * SUBLANES, SUBLANES), SUBLANES)
        hg = hh_ref[rows, :] + a_ref[rows, :] * hc
        hh_ref[rows, :] = hg
        return jnp.broadcast_to(hg[SUBLANES - 1:SUBLANES, :], (SUBLANES, D_RNN))

    hc_last = lax.fori_loop(0, ts // SUBLANES, carry_body, hstate_ref[...])
    hstate_ref[...] = hc_last

    y_rnn = (hh_ref[...] * _gelu_tanh(z_ref[:, D_RNN:2 * D_RNN].astype(F32))).astype(BF16)

    gv = _gelu_tanh(z_ref[:, 2 * D_RNN + D_SG:2 * D_RNN + 2 * D_SG].astype(F32))
    mu = jnp.mean(gv, axis=-1, keepdims=True)
    dv = gv - mu
    var = jnp.mean(dv * dv, axis=-1, keepdims=True)
    vn = (dv * lax.rsqrt(var + EPS) * lng_ref[...] + lnb_ref[...]).astype(BF16)
    tr = lax.broadcasted_iota(I32, (SG_CHUNK, SG_CHUNK), 0)
    tc = lax.broadcasted_iota(I32, (SG_CHUNK, SG_CHUNK), 1)
    causal = tc <= tr
    for g in range(SG_GROUPS):
        wg = jnp.where(causal, ws_ref[g], 0.0).astype(BF16)
        cols = slice(g * SG_GROUP_DIM, (g + 1) * SG_GROUP_DIM)
        for n in range(ts // SG_CHUNK):
            rows = slice(n * SG_CHUNK, (n + 1) * SG_CHUNK)
            sv_ref[rows, cols] = _bdot(wg, vn[rows, cols]) + bs_ref[:, cols]
    gu = _gelu_tanh(z_ref[:, 2 * D_RNN:2 * D_RNN + D_SG].astype(F32))
    y_sg = (gu * sv_ref[...]).astype(BF16)

    g_rnn = z_ref[:, 2 * D_RNN + 2 * D_SG:2 * D_RNN + 2 * D_SG + D_MODEL].astype(F32)
    g_sg = z_ref[:, 2 * D_RNN + 2 * D_SG + D_MODEL:D_IN].astype(F32)
    m = (_sigmoid(g_rnn) * _bdot(y_rnn, wbr_ref[...])
         + _sigmoid(g_sg) * _bdot(y_sg, wbs_ref[...])).astype(BF16)
    x2 = x_ref[...] + gt1_ref[0] * _bdot(m, wout_ref[...])
    x2_ref[...] = x2

    h2 = _norm_mod(x2, n2g_ref[...], sc2_ref[0], sh2_ref[0])
    h2a_ref[...], h2b_ref[...] = _pack_halves(h2)
    logits = lax.dot_general(wr_ref[...], h2, (((1,), (1,)), ((), ())),
                             preferred_element_type=F32,
                             precision=lax.Precision.HIGHEST) + br_ref[...]
    e_iota = lax.broadcasted_iota(I32, (N_EXPERTS, ts), 0)
    v = logits
    vals, idxs, sels = [], [], []
    for _ in range(TOP_K):
        mx = jnp.max(v, axis=0, keepdims=True)
        ik = jnp.min(jnp.where(v == mx, e_iota, N_EXPERTS), axis=0, keepdims=True)
        sel = e_iota == ik
        v = jnp.where(sel, -jnp.inf, v)
        vals.append(mx)
        idxs.append(ik)
        sels.append(sel)
    exps = [jnp.exp(val - vals[0]) for val in vals]
    denom = exps[0] + exps[1] + exps[2] + exps[3]
    idx_ref[...] = jnp.concatenate(idxs, axis=0)
    tw_ref[...] = jnp.concatenate([e / denom for e in exps], axis=0)

    onehot = jnp.zeros((N_EXPERTS, ts), F32)
    for sel in sels:
        onehot = jnp.where(sel, 1.0, onehot)
    sr = lax.broadcasted_iota(I32, (ts, ts), 0)
    st = lax.broadcasted_iota(I32, (ts, ts), 1)
    before = jnp.where(sr < st, 1.0, 0.0).astype(BF16)
    total = cnt_ref[...] + _bdot(onehot.astype(BF16), before)
    ranks = [jnp.sum(jnp.where(sel, total, 0.0), axis=0, keepdims=True) for sel in sels]
    rank_ref[...] = jnp.concatenate(ranks, axis=0).astype(I32)
    cnt_ref[...] = cnt_ref[...] + jnp.sum(onehot, axis=1, keepdims=True)


def _mixer(z, x2d, gt1, sc2, sh2, conv_w, conv_b, wa_bd, ba, wx_bd, bx, lam,
           ln_g, ln_b, ws, bs_tile, wbr, wbs, wout, n2g, wr_t, br, batch, seq):
    n, d = x2d.shape
    ts = MIXER_TS
    nt = seq // ts
    tok = lambda b, j: (b * nt + j, 0)
    tokt = lambda b, j: (0, b * nt + j)
    bvec = lambda b, j: (b, 0, 0)
    c2 = lambda b, j: (0, 0)
    c3 = lambda b, j: (0, 0, 0)
    in_specs = [
        pl.BlockSpec((ts, D_IN), tok),
        pl.BlockSpec((ts, d), tok),
        pl.BlockSpec((1, 1, d), bvec),
        pl.BlockSpec((1, 1, d), bvec),
        pl.BlockSpec((1, 1, d), bvec),
        pl.BlockSpec((CONV_WIDTH, D_RNN), c2),
        pl.BlockSpec((1, D_RNN), c2),
        pl.BlockSpec((N_GATE_BLOCKS, GATE_BLOCK, GATE_BLOCK), c3),
        pl.BlockSpec((1, D_RNN), c2),
        pl.BlockSpec((N_GATE_BLOCKS, GATE_BLOCK, GATE_BLOCK), c3),
        pl.BlockSpec((1, D_RNN), c2),
        pl.BlockSpec((1, D_RNN), c2),
        pl.BlockSpec((1, D_SG), c2),
        pl.BlockSpec((1, D_SG), c2),
        pl.BlockSpec((SG_GROUPS, SG_CHUNK, SG_CHUNK), c3),
        pl.BlockSpec((SG_CHUNK, D_SG), c2),
        pl.BlockSpec((D_RNN, d), c2),
        pl.BlockSpec((D_SG, d), c2),
        pl.BlockSpec((d, d), c2),
        pl.BlockSpec((1, d), c2),
        pl.BlockSpec((N_EXPERTS, d), c2),
        pl.BlockSpec((N_EXPERTS, 1), c2),
    ]
    out_specs = [
        pl.BlockSpec((ts, d), tok),
        pl.BlockSpec((ts, PACK_W), tok),
        pl.BlockSpec((ts, PACK_W), tok),
        pl.BlockSpec((TOP_K, ts), tokt),
        pl.BlockSpec((TOP_K, ts), tokt),
        pl.BlockSpec((TOP_K, ts), tokt),
        pl.BlockSpec((N_EXPERTS, 1), c2),
    ]
    out_shape = [
        jax.ShapeDtypeStruct((n, d), F32),
        jax.ShapeDtypeStruct((n, PACK_W), jnp.uint32),
        jax.ShapeDtypeStruct((n, PACK_W), jnp.uint32),
        jax.ShapeDtypeStruct((TOP_K, n), I32),
        jax.ShapeDtypeStruct((TOP_K, n), F32),
        jax.ShapeDtypeStruct((TOP_K, n), I32),
        jax.ShapeDtypeStruct((N_EXPERTS, 1), F32),
    ]
    scratch = [
        pltpu.VMEM((ts + 2 * SUBLANES, D_RNN), F32),
        pltpu.VMEM((ts, D_RNN), F32),
        pltpu.VMEM((ts, D_RNN), F32),
        pltpu.VMEM((ts, D_SG), F32),
        pltpu.VMEM((SUBLANES, D_RNN), F32),
    ]
    return pl.pallas_call(
        _mixer_kernel,
        grid=(batch, nt),
        in_specs=in_specs,
        out_specs=out_specs,
        out_shape=out_shape,
        scratch_shapes=scratch,
        compiler_params=pltpu.CompilerParams(
            dimension_semantics=("arbitrary", "arbitrary"), vmem_limit_bytes=VMEM_LIMIT),
        name="mixer",
    )(z, x2d, gt1, sc2, sh2, conv_w, conv_b, wa_bd, ba, wx_bd, bx, lam,
      ln_g, ln_b, ws, bs_tile, wbr, wbs, wout, n2g, wr_t, br)


def _sc_mesh():
    return plsc.VectorSubcoreMesh(core_axis_name="core", subcore_axis_name="subcore")


def _sc_scatter_rows(rows, pos, cap):
    n, d = rows.shape
    kk = pos.shape[0]

    @functools.partial(
        pl.kernel, out_type=jax.ShapeDtypeStruct((cap, d), rows.dtype), mesh=_sc_mesh(),
        scratch_types=[], name="sc_scatter_rows")
    def scatter(x_hbm, i_hbm, o_hbm):
        def body(x_vmem, i_vmem):
            pltpu.sync_copy(x_vmem, o_hbm.at[i_vmem.at[0]])

        pltpu.emit_pipeline(
            body,
            grid=(n // SC_WINDOW, kk),
            in_specs=[pl.BlockSpec((SC_WINDOW, d), lambda i, k: (i, 0)),
                      pl.BlockSpec((1, SC_WINDOW), lambda i, k: (k, i))],
            out_specs=[],
            core_axis_name=("core", "subcore"),
            dimension_semantics=(pltpu.PARALLEL, pltpu.ARBITRARY),
        )(x_hbm, i_hbm)

    return scatter(rows, pos)


def _sc_gather_rows(table, idx):
    m = idx.shape[1]
    d = table.shape[1]

    @functools.partial(
        pl.kernel, out_type=jax.ShapeDtypeStruct((m, d), table.dtype), mesh=_sc_mesh(),
        scratch_types=[], name="sc_gather_rows")
    def gather(x_hbm, i_hbm, o_hbm):
        def body(i_vmem, o_vmem):
            pltpu.sync_copy(x_hbm.at[i_vmem.at[0]], o_vmem)

        pltpu.emit_pipeline(
            body,
            grid=(m // SC_WINDOW,),
            in_specs=[pl.BlockSpec((1, SC_WINDOW), lambda i: (0, i))],
            out_specs=[pl.BlockSpec((SC_WINDOW, d), lambda i: (i, 0))],
            core_axis_name=("core", "subcore"),
            dimension_semantics=(pltpu.PARALLEL,),
        )(i_hbm, o_hbm)

    return gather(table, idx)


def _expert_kernel(be_ref, nv_ref, xa_ref, xb_ref, wgu_ref, bgu_ref, wd_ref, bd_ref,
                   ya_ref, yb_ref):
    nvalid = nv_ref[pl.program_id(0)]

    @pl.when(nvalid > 0)
    def _():
        live = lax.broadcasted_iota(I32, (MOE_BM, 1), 0) < nvalid
        xb = jnp.where(live, _unpack_halves(xa_ref[...], xb_ref[...]), 0.0).astype(BF16)
        gu = _bdot(xb, wgu_ref[0]) + bgu_ref[0]
        gate = jnp.minimum(gu[:, :D_EXPERT], SWIGLU_LIMIT)
        up = jnp.clip(gu[:, D_EXPERT:], -SWIGLU_LIMIT, SWIGLU_LIMIT)
        act = (up + 1.0) * (gate * _sigmoid(SWIGLU_ALPHA * gate))
        y = _bdot(act.astype(BF16), wd_ref[0]) + bd_ref[0]
        ya_ref[...], yb_ref[...] = _pack_halves(y)

    @pl.when(nvalid <= 0)
    def _():
        ya_ref[...] = jnp.zeros_like(ya_ref)
        yb_ref[...] = jnp.zeros_like(yb_ref)


def _experts(block_e, n_valid, xa, xb, w_gu, b_gu, w_down, b_down):
    cap = xa.shape[0]
    d = D_MODEL
    nb = cap // MOE_BM
    half = pl.BlockSpec((MOE_BM, PACK_W), lambda i, be, nv: (i, 0))
    grid_spec = pltpu.PrefetchScalarGridSpec(
        num_scalar_prefetch=2,
        grid=(nb,),
        in_specs=[
            half,
            half,
            pl.BlockSpec((1, d, 2 * D_EXPERT), lambda i, be, nv: (be[i], 0, 0)),
            pl.BlockSpec((1, 1, 2 * D_EXPERT), lambda i, be, nv: (be[i], 0, 0)),
            pl.BlockSpec((1, D_EXPERT, d), lambda i, be, nv: (be[i], 0, 0)),
            pl.BlockSpec((1, 1, d), lambda i, be, nv: (be[i], 0, 0)),
        ],
        out_specs=[half, half],
    )
    return pl.pallas_call(
        _expert_kernel,
        grid_spec=grid_spec,
        out_shape=[jax.ShapeDtypeStruct((cap, PACK_W), jnp.uint32)] * 2,
        compiler_params=pltpu.CompilerParams(
            dimension_semantics=("arbitrary",), vmem_limit_bytes=VMEM_LIMIT),
        name="experts",
    )(block_e, n_valid, xa, xb, w_gu, b_gu, w_down, b_down)


def _final_kernel(x2_ref, ya_ref, yb_ref, tw_ref, gt2_ref, fg_ref, o_ref):
    tw = tw_ref[...]
    moe = tw[:, 0:1] * _unpack_halves(ya_ref[0], yb_ref[0])
    for k in range(1, TOP_K):
        moe = moe + tw[:, k:k + 1] * _unpack_halves(ya_ref[k], yb_ref[k])
    x3 = x2_ref[...] + gt2_ref[0] * moe
    ms = jnp.mean(x3 * x3, axis=-1, keepdims=True)
    o_ref[...] = (x3 * lax.rsqrt(ms + EPS)) * fg_ref[...]


def _final(x2, yga, ygb, tw_tok, gt2, final_g, seq):
    n, d = x2.shape
    tiles_per_seq = seq // FINAL_TM
    half = pl.BlockSpec((TOP_K, FINAL_TM, PACK_W), lambda i: (0, i, 0))
    return pl.pallas_call(
        _final_kernel,
        grid=(n // FINAL_TM,),
        in_specs=[
            pl.BlockSpec((FINAL_TM, d), lambda i: (i, 0)),
            half,
            half,
            pl.BlockSpec((FINAL_TM, TOP_K), lambda i: (i, 0)),
            pl.BlockSpec((1, 1, d), lambda i: (i // tiles_per_seq, 0, 0)),
            pl.BlockSpec((1, d), lambda i: (0, 0)),
        ],
        out_specs=pl.BlockSpec((FINAL_TM, d), lambda i: (i, 0)),
        out_shape=jax.ShapeDtypeStruct((n, d), F32),
        compiler_params=pltpu.CompilerParams(
            dimension_semantics=("arbitrary",), vmem_limit_bytes=VMEM_LIMIT),
        name="final",
    )(x2, yga, ygb, tw_tok, gt2, final_g)


def _block_diag(w):
    per = GATE_BLOCK // RNN_HEAD_DIM
    w4 = w.reshape(N_GATE_BLOCKS, per, RNN_HEAD_DIM, RNN_HEAD_DIM)
    eye = jnp.eye(per, dtype=w.dtype)
    bd = jnp.einsum("gpij,pq->gpiqj", w4, eye)
    return bd.reshape(N_GATE_BLOCKS, GATE_BLOCK, GATE_BLOCK)


def _layer(x2d, c, batch, seq, ada_w, ada_b, norm1_g, w_in, conv_w, conv_b, lru_wa, lru_ba,
           lru_wx, lru_bx, lru_lam, sg_ln_g, sg_ln_b, sg_ws, sg_bs, w_br_rnn, w_br_sg, w_out,
           norm2_g, w_router, b_router, w_gu, b_gu, w_down, b_down):
    n, d = x2d.shape
    mod = _ada(c, ada_w, ada_b)
    sh1, sc1, gt1, sh2, sc2, gt2 = [
        mod[:, i * d:(i + 1) * d].reshape(batch, 1, d) for i in range(N_MOD)]
    row = lambda v: v.reshape(1, -1)

    z = _inproj(x2d, row(norm1_g), sc1, sh1, w_in.astype(BF16), seq)

    bs_tile = jnp.repeat(sg_bs.T, SG_GROUP_DIM, axis=1)
    x2, h2a, h2b, top_idx, top_w, rank, counts = _mixer(
        z, x2d, gt1, sc2, sh2, conv_w, row(conv_b),
        _block_diag(lru_wa).astype(BF16), row(lru_ba), _block_diag(lru_wx).astype(BF16), row(lru_bx),
        row(lru_lam), row(sg_ln_g), row(sg_ln_b), sg_ws, bs_tile,
        w_br_rnn.astype(BF16), w_br_sg.astype(BF16), w_out.astype(BF16), row(norm2_g),
        w_router.T, b_router.reshape(N_EXPERTS, 1), batch, seq)

    a_total = n * TOP_K
    cap = a_total + N_EXPERTS * MOE_BM
    nb = cap // MOE_BM
    cnt = counts[:, 0].astype(I32)
    padded = ((cnt + MOE_BM - 1) // MOE_BM) * MOE_BM
    padded_end = jnp.cumsum(padded)
    padded_start = padded_end - padded
    e_ids = jnp.arange(N_EXPERTS, dtype=I32)
    start_of = jnp.sum(jnp.where(top_idx[None] == e_ids[:, None, None],
                                 padded_start[:, None, None], 0), axis=0)
    pos = start_of + rank
    blk_row0 = jnp.arange(nb, dtype=I32) * MOE_BM
    block_e = jnp.minimum(jnp.sum((padded_end[None, :] <= blk_row0[:, None]).astype(I32), axis=1),
                          N_EXPERTS - 1)
    n_valid = jnp.clip(cnt[block_e] - (blk_row0 - padded_start[block_e]), 0, MOE_BM).astype(I32)

    xa = _sc_scatter_rows(h2a, pos, cap)
    xb = _sc_scatter_rows(h2b, pos, cap)
    ya, yb = _experts(block_e, n_valid, xa, xb, w_gu.astype(BF16), b_gu.reshape(N_EXPERTS, 1, -1),
                      w_down.astype(BF16), b_down.reshape(N_EXPERTS, 1, -1))
    flat_pos = pos.reshape(1, -1)
    yga = _sc_gather_rows(ya, flat_pos).reshape(TOP_K, n, PACK_W)
    ygb = _sc_gather_rows(yb, flat_pos).reshape(TOP_K, n, PACK_W)
    return x2, yga, ygb, top_w.T, gt2


def kernel(x, c, ada_w, ada_b, norm1_g, w_in, conv_w, conv_b, lru_wa, lru_ba, lru_wx, lru_bx,
           lru_lam, sg_ln_g, sg_ln_b, sg_ws, sg_bs, w_br_rnn, w_br_sg, w_out, norm2_g,
           w_router, b_router, w_gu, b_gu, w_down, b_down, final_g):
    batch, seq, d = x.shape
    depth = ada_w.shape[0]
    assert depth == 1, "the combine is fused with the final norm, which follows the only layer"
    x2d = x.reshape(batch * seq, d)
    l = 0
    x2, yga, ygb, tw_tok, gt2 = _layer(
        x2d, c, batch, seq, ada_w[l], ada_b[l], norm1_g[l], w_in[l], conv_w[l], conv_b[l],
        lru_wa[l], lru_ba[l], lru_wx[l], lru_bx[l], lru_lam[l], sg_ln_g[l], sg_ln_b[l],
        sg_ws[l], sg_bs[l], w_br_rnn[l], w_br_sg[l], w_out[l], norm2_g[l], w_router[l],
        b_router[l], w_gu[l], b_gu[l], w_down[l], b_down[l])
    out = _final(x2, yga, ygb, tw_tok, gt2, final_g.reshape(1, d), seq)
    return out.reshape(batch, seq, d)
```

```python
import functools

import jax
import jax.numpy as jnp
from jax import lax
from jax.experimental import pallas as pl
from jax.experimental.pallas import tpu as pltpu
from jax.experimental.pallas import tpu_sc as plsc

F32 = jnp.float32
BF16 = jnp.bfloat16
I32 = jnp.int32

D_MODEL = 1024
D_RNN = 1024
RNN_HEADS = 16
RNN_HEAD_DIM = D_RNN // RNN_HEADS
CONV_WIDTH = 4
LRU_C = 8.0
D_SG = 1024
SG_GROUPS = 8
SG_GROUP_DIM = D_SG // SG_GROUPS
SG_CHUNK = 128
N_EXPERTS = 32
TOP_K = 4
D_EXPERT = 1024
SWIGLU_LIMIT = 7.0
SWIGLU_ALPHA = 1.702
EPS = 1e-6
N_MOD = 6
D_IN = 2 * D_RNN + 2 * D_SG + 2 * D_MODEL

SUBLANES = 8
GATE_BLOCK = 256
N_GATE_BLOCKS = D_RNN // GATE_BLOCK

ADA_TN = 1536
INPROJ_TM = 512
INPROJ_TN = 1024
MIXER_TS = 256
MOE_BM = 512
MOE_HALF = MOE_BM // 2
SC_WINDOW = 128
PACK_W = D_MODEL // 4
FINAL_TM = 256
VMEM_LIMIT = 56 * 1024 * 1024


def _sigmoid(x):
    return 0.5 * jnp.tanh(0.5 * x) + 0.5


def _gelu_tanh(x):
    k = 0.7978845608028654
    hx = 0.5 * x
    return hx + hx * jnp.tanh(x * (k + (k * 0.044715) * (x * x)))


def _bdot(a, b):
    return jnp.dot(a, b, preferred_element_type=F32)


def _pack_halves(v):
    word = pltpu.pack_elementwise([v[:, 2 * PACK_W:], v[:, :2 * PACK_W]], packed_dtype=BF16)
    return word[:, :PACK_W], word[:, PACK_W:]


def _unpack_halves(wa, wb):
    part = lambda w, i: pltpu.unpack_elementwise(w, index=i, packed_dtype=BF16, unpacked_dtype=F32)
    return jnp.concatenate([part(wa, 1), part(wb, 1), part(wa, 0), part(wb, 0)], axis=1)


def _ada_kernel(c_ref, w_ref, b_ref, o_ref):
    c = c_ref[...]
    s = c * _sigmoid(c)
    o_ref[...] = jnp.dot(s, w_ref[...], preferred_element_type=F32,
                         precision=lax.Precision.HIGHEST) + b_ref[...]


def _ada(c, ada_w, ada_b):
    b, d = c.shape
    n = ada_w.shape[1]
    return pl.pallas_call(
        _ada_kernel,
        grid=(n // ADA_TN,),
        in_specs=[
            pl.BlockSpec((b, d), lambda j: (0, 0)),
            pl.BlockSpec((d, ADA_TN), lambda j: (0, j)),
            pl.BlockSpec((1, ADA_TN), lambda j: (0, j)),
        ],
        out_specs=pl.BlockSpec((b, ADA_TN), lambda j: (0, j)),
        out_shape=jax.ShapeDtypeStruct((b, n), F32),
        compiler_params=pltpu.CompilerParams(
            dimension_semantics=("arbitrary",), vmem_limit_bytes=VMEM_LIMIT),
        name="ada",
    )(c, ada_w, ada_b.reshape(1, n))


def _norm_mod(x, g, sc, sh):
    ms = jnp.mean(x * x, axis=-1, keepdims=True)
    y = x * lax.rsqrt(ms + EPS)
    return (y * g) * (1.0 + sc) + sh


def _inproj_kernel(x_ref, g_ref, sc_ref, sh_ref, w_ref, z_ref):
    h = _norm_mod(x_ref[...], g_ref[...], sc_ref[0], sh_ref[0]).astype(BF16)
    for j in range(D_IN // INPROJ_TN):
        cols = slice(j * INPROJ_TN, (j + 1) * INPROJ_TN)
        z_ref[:, cols] = _bdot(h, w_ref[:, cols]).astype(BF16)


def _inproj(x2d, g, sc, sh, w_in_bf16, seq):
    n, d = x2d.shape
    tiles_per_seq = seq // INPROJ_TM
    bvec = lambda i: (i // tiles_per_seq, 0, 0)
    return pl.pallas_call(
        _inproj_kernel,
        grid=(n // INPROJ_TM,),
        in_specs=[
            pl.BlockSpec((INPROJ_TM, d), lambda i: (i, 0)),
            pl.BlockSpec((1, d), lambda i: (0, 0)),
            pl.BlockSpec((1, 1, d), bvec),
            pl.BlockSpec((1, 1, d), bvec),
            pl.BlockSpec((d, D_IN), lambda i: (0, 0)),
        ],
        out_specs=pl.BlockSpec((INPROJ_TM, D_IN), lambda i: (i, 0)),
        out_shape=jax.ShapeDtypeStruct((n, D_IN), BF16),
        compiler_params=pltpu.CompilerParams(
            dimension_semantics=("arbitrary",), vmem_limit_bytes=VMEM_LIMIT),
        name="inproj",
    )(x2d, g, sc, sh, w_in_bf16)


def _mixer_kernel(z_ref, x_ref, gt1_ref, sc2_ref, sh2_ref,
                  convw_ref, convb_ref, wa_ref, ba_ref, wx_ref, bx_ref, lam_ref,
                  lng_ref, lnb_ref, ws_ref, bs_ref, wbr_ref, wbs_ref, wout_ref,
                  n2g_ref, wr_ref, br_ref, shift_ref,
                  x2_ref, h2a_ref, h2b_ref, idx_ref, tw_ref, rank_ref, cnt_ref,
                  xp_ref, xc_ref, a_ref, hh_ref, sv_ref, hstate_ref):
    ts = MIXER_TS
    b = pl.program_id(0)
    j = pl.program_id(1)

    @pl.when(j == 0)
    def _():
        xp_ref[0:SUBLANES, :] = jnp.zeros((SUBLANES, D_RNN), F32)
        hstate_ref[...] = jnp.zeros_like(hstate_ref)

    @pl.when((b == 0) & (j == 0))
    def _():
        cnt_ref[...] = jnp.zeros_like(cnt_ref)

    x16 = z_ref[:, 0:D_RNN]
    rnn_x = x16.astype(F32)
    cw = convw_ref[...]
    xc = cw[3:4] * rnn_x + convb_ref[...]
    for s in range(1, CONV_WIDTH):
        xc = xc + cw[3 - s:4 - s] * _bdot(shift_ref[s - 1], x16)
    xc_ref[...] = xc
    xp_ref[SUBLANES:2 * SUBLANES, :] = rnn_x[0:SUBLANES, :]
    xc_ref[0:SUBLANES, :] = (
        cw[3:4] * xp_ref[SUBLANES:2 * SUBLANES, :]
        + cw[2:3] * xp_ref[SUBLANES - 1:2 * SUBLANES - 1, :]
        + cw[1:2] * xp_ref[SUBLANES - 2:2 * SUBLANES - 2, :]
        + cw[0:1] * xp_ref[SUBLANES - 3:2 * SUBLANES - 3, :]) + convb_ref[...]
    xp_ref[0:SUBLANES, :] = rnn_x[ts - SUBLANES:ts, :]
    xc = xc_ref[...]

    xcb = xc.astype(BF16)
    r_parts, i_parts = [], []
    for g in range(N_GATE_BLOCKS):
        blk = xcb[:, g * GATE_BLOCK:(g + 1) * GATE_BLOCK]
        r_parts.append(_bdot(blk, wa_ref[g]))
        i_parts.append(_bdot(blk, wx_ref[g]))
    r = _sigmoid(jnp.concatenate(r_parts, axis=1) + ba_ref[...])
    ig = _sigmoid(jnp.concatenate(i_parts, axis=1) + bx_ref[...])

    nl = -lam_ref[...]
    softplus = jnp.maximum(nl, 0.0) + jnp.log(1.0 + jnp.exp(-jnp.abs(nl)))
    a = jnp.exp(r * ((-LRU_C) * softplus))
    t = 1.0 - a * a
    u = jnp.where(t > 0.0, t * lax.rsqrt(t), 0.0) * (ig * xc)

    groups = ts // SUBLANES
    a3 = a.reshape(groups, SUBLANES, D_RNN)
    h3 = u.reshape(groups, SUBLANES, D_RNN)
    sub = lax.broadcasted_iota(I32, (groups, SUBLANES, D_RNN), 1)
    for k in (1, 2, 4):
        keep = sub >= k
        a_sh = jnp.where(keep, pltpu.roll(a3, k, 1), 1.0)
        h_sh = jnp.where(keep, pltpu.roll(h3, k, 1), 0.0)
        h3 = h3 + a3 * h_sh
        a3 = a3 * a_sh
    a_ref[...] = a3.reshape(ts, D_RNN)
    hh_ref[...] = h3.reshape(ts, D_RNN)

    def carry_body(gi, hc):
        rows = pl.ds(pl.multiple_of(gi * SUBLANES, SUBLANES), SUBLANES)
        hg = hh_ref[rows, :] + a_ref[rows, :] * hc
        hh_ref[rows, :] = hg
        return jnp.broadcast_to(hg[SUBLANES - 1:SUBLANES, :], (SUBLANES, D_RNN))

    hc_last = lax.fori_loop(0, ts // SUBLANES, carry_body, hstate_ref[...])
    hstate_ref[...] = hc_last

    y_rnn = (hh_ref[...] * _gelu_tanh(z_ref[:, D_RNN:2 * D_RNN].astype(F32))).astype(BF16)

    gv = _gelu_tanh(z_ref[:, 2 * D_RNN + D_SG:2 * D_RNN + 2 * D_SG].astype(F32))
    mu = jnp.mean(gv, axis=-1, keepdims=True)
    dv = gv - mu
    var = jnp.mean(dv * dv, axis=-1, keepdims=True)
    vn = (dv * lax.rsqrt(var + EPS) * lng_ref[...] + lnb_ref[...]).astype(BF16)
    tr = lax.broadcasted_iota(I32, (SG_CHUNK, SG_CHUNK), 0)
    tc = lax.broadcasted_iota(I32, (SG_CHUNK, SG_CHUNK), 1)
    causal = tc <= tr
    for g in range(SG_GROUPS):
        wg = jnp.where(causal, ws_ref[g], 0.0).astype(BF16)
        cols = slice(g * SG_GROUP_DIM, (g + 1) * SG_GROUP_DIM)
        for n in range(ts // SG_CHUNK):
            rows = slice(n * SG_CHUNK, (n + 1) * SG_CHUNK)
            sv_ref[rows, cols] = _bdot(wg, vn[rows, cols]) + bs_ref[:, cols]
    gu = _gelu_tanh(z_ref[:, 2 * D_RNN:2 * D_RNN + D_SG].astype(F32))
    y_sg = (gu * sv_ref[...]).astype(BF16)

    g_rnn = z_ref[:, 2 * D_RNN + 2 * D_SG:2 * D_RNN + 2 * D_SG + D_MODEL].astype(F32)
    g_sg = z_ref[:, 2 * D_RNN + 2 * D_SG + D_MODEL:D_IN].astype(F32)
    m = (_sigmoid(g_rnn) * _bdot(y_rnn, wbr_ref[...])
         + _sigmoid(g_sg) * _bdot(y_sg, wbs_ref[...])).astype(BF16)
    x2 = x_ref[...] + gt1_ref[0] * _bdot(m, wout_ref[...])
    x2_ref[...] = x2

    h2 = _norm_mod(x2, n2g_ref[...], sc2_ref[0], sh2_ref[0])
    h2a_ref[...], h2b_ref[...] = _pack_halves(h2)
    h_hi = h2.astype(BF16)
    h_lo = (h2 - h_hi.astype(F32)).astype(BF16)
    nt_dims = (((1,), (1,)), ((), ()))
    by_hi = lax.dot_general(wr_ref[...], h_hi, nt_dims, preferred_element_type=F32)
    logits = (by_hi[:N_EXPERTS] + by_hi[N_EXPERTS:]
              + lax.dot_general(wr_ref[0:N_EXPERTS, :], h_lo, nt_dims, preferred_element_type=F32)
              + br_ref[...])
    e_iota = lax.broadcasted_iota(I32, (N_EXPERTS, ts), 0)
    v = logits
    vals, idxs, sels = [], [], []
    for _ in range(TOP_K):
        mx = jnp.max(v, axis=0, keepdims=True)
        ik = jnp.min(jnp.where(v == mx, e_iota, N_EXPERTS), axis=0, keepdims=True)
        sel = e_iota == ik
        v = jnp.where(sel, -jnp.inf, v)
        vals.append(mx)
        idxs.append(ik)
        sels.append(sel)
    exps = [jnp.exp(val - vals[0]) for val in vals]
    denom = exps[0] + exps[1] + exps[2] + exps[3]
    idx_ref[...] = jnp.concatenate(idxs, axis=0)
    tw_ref[...] = jnp.concatenate([e / denom for e in exps], axis=0)

    onehot = jnp.zeros((N_EXPERTS, ts), F32)
    for sel in sels:
        onehot = jnp.where(sel, 1.0, onehot)
    sr = lax.broadcasted_iota(I32, (ts, ts), 0)
    st = lax.broadcasted_iota(I32, (ts, ts), 1)
    before = jnp.where(sr < st, 1.0, 0.0).astype(BF16)
    total = cnt_ref[...] + _bdot(onehot.astype(BF16), before)
    ranks = [jnp.sum(jnp.where(sel, total, 0.0), axis=0, keepdims=True) for sel in sels]
    rank_ref[...] = jnp.concatenate(ranks, axis=0).astype(I32)
    cnt_ref[...] = cnt_ref[...] + jnp.sum(onehot, axis=1, keepdims=True)


def _mixer(z, x2d, gt1, sc2, sh2, conv_w, conv_b, wa_bd, ba, wx_bd, bx, lam,
           ln_g, ln_b, ws, bs_tile, wbr, wbs, wout, n2g, wr_split, br, shifts, batch, seq):
    n, d = x2d.shape
    ts = MIXER_TS
    nt = seq // ts
    tok = lambda b, j: (b * nt + j, 0)
    tokt = lambda b, j: (0, b * nt + j)
    bvec = lambda b, j: (b, 0, 0)
    c2 = lambda b, j: (0, 0)
    c3 = lambda b, j: (0, 0, 0)
    in_specs = [
        pl.BlockSpec((ts, D_IN), tok),
        pl.BlockSpec((ts, d), tok),
        pl.BlockSpec((1, 1, d), bvec),
        pl.BlockSpec((1, 1, d), bvec),
        pl.BlockSpec((1, 1, d), bvec),
        pl.BlockSpec((CONV_WIDTH, D_RNN), c2),
        pl.BlockSpec((1, D_RNN), c2),
        pl.BlockSpec((N_GATE_BLOCKS, GATE_BLOCK, GATE_BLOCK), c3),
        pl.BlockSpec((1, D_RNN), c2),
        pl.BlockSpec((N_GATE_BLOCKS, GATE_BLOCK, GATE_BLOCK), c3),
        pl.BlockSpec((1, D_RNN), c2),
        pl.BlockSpec((1, D_RNN), c2),
        pl.BlockSpec((1, D_SG), c2),
        pl.BlockSpec((1, D_SG), c2),
        pl.BlockSpec((SG_GROUPS, SG_CHUNK, SG_CHUNK), c3),
        pl.BlockSpec((SG_CHUNK, D_SG), c2),
        pl.BlockSpec((D_RNN, d), c2),
        pl.BlockSpec((D_SG, d), c2),
        pl.BlockSpec((d, d), c2),
        pl.BlockSpec((1, d), c2),
        pl.BlockSpec((2 * N_EXPERTS, d), c2),
        pl.BlockSpec((N_EXPERTS, 1), c2),
        pl.BlockSpec((CONV_WIDTH - 1, ts, ts), c3),
    ]
    out_specs = [
        pl.BlockSpec((ts, d), tok),
        pl.BlockSpec((ts, PACK_W), tok),
        pl.BlockSpec((ts, PACK_W), tok),
        pl.BlockSpec((TOP_K, ts), tokt),
        pl.BlockSpec((TOP_K, ts), tokt),
        pl.BlockSpec((TOP_K, ts), tokt),
        pl.BlockSpec((N_EXPERTS, 1), c2),
    ]
    out_shape = [
        jax.ShapeDtypeStruct((n, d), F32),
        jax.ShapeDtypeStruct((n, PACK_W), jnp.uint32),
        jax.ShapeDtypeStruct((n, PACK_W), jnp.uint32),
        jax.ShapeDtypeStruct((TOP_K, n), I32),
        jax.ShapeDtypeStruct((TOP_K, n), F32),
        jax.ShapeDtypeStruct((TOP_K, n), I32),
        jax.ShapeDtypeStruct((N_EXPERTS, 1), F32),
    ]
    scratch = [
        pltpu.VMEM((2 * SUBLANES, D_RNN), F32),
        pltpu.VMEM((ts, D_RNN), F32),
        pltpu.VMEM((ts, D_RNN), F32),
        pltpu.VMEM((ts, D_RNN), F32),
        pltpu.VMEM((ts, D_SG), F32),
        pltpu.VMEM((SUBLANES, D_RNN), F32),
    ]
    return pl.pallas_call(
        _mixer_kernel,
        grid=(batch, nt),
        in_specs=in_specs,
        out_specs=out_specs,
        out_shape=out_shape,
        scratch_shapes=scratch,
        compiler_params=pltpu.CompilerParams(
            dimension_semantics=("arbitrary", "arbitrary"), vmem_limit_bytes=VMEM_LIMIT),
        name="mixer",
    )(z, x2d, gt1, sc2, sh2, conv_w, conv_b, wa_bd, ba, wx_bd, bx, lam,
      ln_g, ln_b, ws, bs_tile, wbr, wbs, wout, n2g, wr_split, br, shifts)


def _sc_mesh():
    return plsc.VectorSubcoreMesh(core_axis_name="core", subcore_axis_name="subcore")


def _sc_scatter_rows(rows, pos, cap):
    n, d = rows.shape
    kk = pos.shape[0]

    @functools.partial(
        pl.kernel, out_type=jax.ShapeDtypeStruct((cap, d), rows.dtype), mesh=_sc_mesh(),
        scratch_types=[], name="sc_scatter_rows")
    def scatter(x_hbm, i_hbm, o_hbm):
        def body(x_vmem, i_vmem):
            pltpu.sync_copy(x_vmem, o_hbm.at[i_vmem.at[0]])

        pltpu.emit_pipeline(
            body,
            grid=(n // SC_WINDOW, kk),
            in_specs=[pl.BlockSpec((SC_WINDOW, d), lambda i, k: (i, 0)),
                      pl.BlockSpec((1, SC_WINDOW), lambda i, k: (k, i))],
            out_specs=[],
            core_axis_name=("core", "subcore"),
            dimension_semantics=(pltpu.PARALLEL, pltpu.ARBITRARY),
        )(x_hbm, i_hbm)

    return scatter(rows, pos)


def _sc_gather_rows(table, idx):
    m = idx.shape[1]
    d = table.shape[1]

    @functools.partial(
        pl.kernel, out_type=jax.ShapeDtypeStruct((m, d), table.dtype), mesh=_sc_mesh(),
        scratch_types=[], name="sc_gather_rows")
    def gather(x_hbm, i_hbm, o_hbm):
        def body(i_vmem, o_vmem):
            pltpu.sync_copy(x_hbm.at[i_vmem.at[0]], o_vmem)

        pltpu.emit_pipeline(
            body,
            grid=(m // SC_WINDOW,),
            in_specs=[pl.BlockSpec((1, SC_WINDOW), lambda i: (0, i))],
            out_specs=[pl.BlockSpec((SC_WINDOW, d), lambda i: (i, 0))],
            core_axis_name=("core", "subcore"),
            dimension_semantics=(pltpu.PARALLEL,),
        )(i_hbm, o_hbm)

    return gather(table, idx)


def _expert_kernel(be_ref, nv_ref, xa_ref, xb_ref, wgu_ref, bgu_ref, wd_ref, bd_ref,
                   ya_ref, yb_ref, wgu_bf, wd_bf):
    i = pl.program_id(0)
    nvalid = nv_ref[i]

    @pl.when((i == 0) | (be_ref[i] != be_ref[jnp.maximum(i - 1, 0)]))
    def _():
        wgu_bf[...] = wgu_ref[0].astype(BF16)
        wd_bf[...] = wd_ref[0].astype(BF16)

    def mlp_rows(h):
        rows = slice(h * MOE_HALF, (h + 1) * MOE_HALF)
        live = lax.broadcasted_iota(I32, (MOE_HALF, 1), 0) < nvalid - h * MOE_HALF
        xb = jnp.where(live, _unpack_halves(xa_ref[rows, :], xb_ref[rows, :]), 0.0).astype(BF16)
        gu = _bdot(xb, wgu_bf[...]) + bgu_ref[0]
        gate = jnp.minimum(gu[:, :D_EXPERT], SWIGLU_LIMIT)
        up = jnp.clip(gu[:, D_EXPERT:], -SWIGLU_LIMIT, SWIGLU_LIMIT)
        act = (up + 1.0) * (gate * _sigmoid(SWIGLU_ALPHA * gate))
        y = _bdot(act.astype(BF16), wd_bf[...]) + bd_ref[0]
        ya_ref[rows, :], yb_ref[rows, :] = _pack_halves(y)

    def zero_rows(h):
        rows = slice(h * MOE_HALF, (h + 1) * MOE_HALF)
        ya_ref[rows, :] = jnp.zeros((MOE_HALF, PACK_W), jnp.uint32)
        yb_ref[rows, :] = jnp.zeros((MOE_HALF, PACK_W), jnp.uint32)

    @pl.when(nvalid > MOE_HALF)
    def _():
        mlp_rows(0)
        mlp_rows(1)

    @pl.when((nvalid > 0) & (nvalid <= MOE_HALF))
    def _():
        mlp_rows(0)
        zero_rows(1)

    @pl.when(nvalid <= 0)
    def _():
        zero_rows(0)
        zero_rows(1)


def _experts(block_e, n_valid, xa, xb, w_gu, b_gu, w_down, b_down):
    cap = xa.shape[0]
    d = D_MODEL
    nb = cap // MOE_BM
    half = pl.BlockSpec((MOE_BM, PACK_W), lambda i, be, nv: (i, 0))
    grid_spec = pltpu.PrefetchScalarGridSpec(
        num_scalar_prefetch=2,
        grid=(nb,),
        in_specs=[
            half,
            half,
            pl.BlockSpec((1, d, 2 * D_EXPERT), lambda i, be, nv: (be[i], 0, 0)),
            pl.BlockSpec((1, 1, 2 * D_EXPERT), lambda i, be, nv: (be[i], 0, 0)),
            pl.BlockSpec((1, D_EXPERT, d), lambda i, be, nv: (be[i], 0, 0)),
            pl.BlockSpec((1, 1, d), lambda i, be, nv: (be[i], 0, 0)),
        ],
        out_specs=[half, half],
        scratch_shapes=[pltpu.VMEM((d, 2 * D_EXPERT), BF16), pltpu.VMEM((D_EXPERT, d), BF16)],
    )
    return pl.pallas_call(
        _expert_kernel,
        grid_spec=grid_spec,
        out_shape=[jax.ShapeDtypeStruct((cap, PACK_W), jnp.uint32)] * 2,
        compiler_params=pltpu.CompilerParams(
            dimension_semantics=("arbitrary",), vmem_limit_bytes=VMEM_LIMIT),
        name="experts",
    )(block_e, n_valid, xa, xb, w_gu, b_gu, w_down, b_down)


def _final_kernel(x2_ref, ya_ref, yb_ref, tw_ref, gt2_ref, fg_ref, o_ref):
    tw = tw_ref[...]
    moe = tw[:, 0:1] * _unpack_halves(ya_ref[0], yb_ref[0])
    for k in range(1, TOP_K):
        moe = moe + tw[:, k:k + 1] * _unpack_halves(ya_ref[k], yb_ref[k])
    x3 = x2_ref[...] + gt2_ref[0] * moe
    ms = jnp.mean(x3 * x3, axis=-1, keepdims=True)
    o_ref[...] = (x3 * lax.rsqrt(ms + EPS)) * fg_ref[...]


def _final(x2, yga, ygb, tw_tok, gt2, final_g, seq):
    n, d = x2.shape
    tiles_per_seq = seq // FINAL_TM
    half = pl.BlockSpec((TOP_K, FINAL_TM, PACK_W), lambda i: (0, i, 0))
    return pl.pallas_call(
        _final_kernel,
        grid=(n // FINAL_TM,),
        in_specs=[
            pl.BlockSpec((FINAL_TM, d), lambda i: (i, 0)),
            half,
            half,
            pl.BlockSpec((FINAL_TM, TOP_K), lambda i: (i, 0)),
            pl.BlockSpec((1, 1, d), lambda i: (i // tiles_per_seq, 0, 0)),
            pl.BlockSpec((1, d), lambda i: (0, 0)),
        ],
        out_specs=pl.BlockSpec((FINAL_TM, d), lambda i: (i, 0)),
        out_shape=jax.ShapeDtypeStruct((n, d), F32),
        compiler_params=pltpu.CompilerParams(
            dimension_semantics=("arbitrary",), vmem_limit_bytes=VMEM_LIMIT),
        name="final",
    )(x2, yga, ygb, tw_tok, gt2, final_g)


def _block_diag(w):
    per = GATE_BLOCK // RNN_HEAD_DIM
    w4 = w.reshape(N_GATE_BLOCKS, per, RNN_HEAD_DIM, RNN_HEAD_DIM)
    eye = jnp.eye(per, dtype=w.dtype)
    bd = jnp.einsum("gpij,pq->gpiqj", w4, eye)
    return bd.reshape(N_GATE_BLOCKS, GATE_BLOCK, GATE_BLOCK)


def _layer(x2d, c, batch, seq, ada_w, ada_b, norm1_g, w_in, conv_w, conv_b, lru_wa, lru_ba,
           lru_wx, lru_bx, lru_lam, sg_ln_g, sg_ln_b, sg_ws, sg_bs, w_br_rnn, w_br_sg, w_out,
           norm2_g, w_router, b_router, w_gu, b_gu, w_down, b_down):
    n, d = x2d.shape
    mod = _ada(c, ada_w, ada_b)
    sh1, sc1, gt1, sh2, sc2, gt2 = [
        mod[:, i * d:(i + 1) * d].reshape(batch, 1, d) for i in range(N_MOD)]
    row = lambda v: v.reshape(1, -1)

    z = _inproj(x2d, row(norm1_g), sc1, sh1, w_in.astype(BF16), seq)

    bs_tile = jnp.repeat(sg_bs.T, SG_GROUP_DIM, axis=1)
    wr_t = w_router.T
    wr_hi = wr_t.astype(BF16)
    wr_lo = (wr_t - wr_hi.astype(F32)).astype(BF16)
    t_out = lax.broadcasted_iota(I32, (MIXER_TS, MIXER_TS), 0)
    t_in = lax.broadcasted_iota(I32, (MIXER_TS, MIXER_TS), 1)
    shifts = jnp.stack([(t_out - t_in == s) for s in range(1, CONV_WIDTH)]).astype(BF16)
    x2, h2a, h2b, top_idx, top_w, rank, counts = _mixer(
        z, x2d, gt1, sc2, sh2, conv_w, row(conv_b),
        _block_diag(lru_wa).astype(BF16), row(lru_ba), _block_diag(lru_wx).astype(BF16), row(lru_bx),
        row(lru_lam), row(sg_ln_g), row(sg_ln_b), sg_ws, bs_tile,
        w_br_rnn.astype(BF16), w_br_sg.astype(BF16), w_out.astype(BF16), row(norm2_g),
        jnp.concatenate([wr_hi, wr_lo], axis=0), b_router.reshape(N_EXPERTS, 1), shifts, batch, seq)

    a_total = n * TOP_K
    cap = a_total + N_EXPERTS * MOE_BM
    nb = cap // MOE_BM
    cnt = counts[:, 0].astype(I32)
    padded = ((cnt + MOE_BM - 1) // MOE_BM) * MOE_BM
    padded_end = jnp.cumsum(padded)
    padded_start = padded_end - padded
    e_ids = jnp.arange(N_EXPERTS, dtype=I32)
    start_of = jnp.sum(jnp.where(top_idx[None] == e_ids[:, None, None],
                                 padded_start[:, None, None], 0), axis=0)
    pos = start_of + rank
    blk_row0 = jnp.arange(nb, dtype=I32) * MOE_BM
    block_e = jnp.minimum(jnp.sum((padded_end[None, :] <= blk_row0[:, None]).astype(I32), axis=1),
                          N_EXPERTS - 1)
    n_valid = jnp.clip(cnt[block_e] - (blk_row0 - padded_start[block_e]), 0, MOE_BM).astype(I32)

    xa = _sc_scatter_rows(h2a, pos, cap)
    xb = _sc_scatter_rows(h2b, pos, cap)
    ya, yb = _experts(block_e, n_valid, xa, xb, w_gu, b_gu.reshape(N_EXPERTS, 1, -1),
                      w_down, b_down.reshape(N_EXPERTS, 1, -1))
    flat_pos = pos.reshape(1, -1)
    yga = _sc_gather_rows(ya, flat_pos).reshape(TOP_K, n, PACK_W)
    ygb = _sc_gather_rows(yb, flat_pos).reshape(TOP_K, n, PACK_W)
    return x2, yga, ygb, top_w.T, gt2


def kernel(x, c, ada_w, ada_b, norm1_g, w_in, conv_w, conv_b, lru_wa, lru_ba, lru_wx, lru_bx,
           lru_lam, sg_ln_g, sg_ln_b, sg_ws, sg_bs, w_br_rnn, w_br_sg, w_out, norm2_g,
           w_router, b_router, w_gu, b_gu, w_down, b_down, final_g):
    batch, seq, d = x.shape
    depth = ada_w.shape[0]
    assert depth == 1, "the combine is fused with the final norm, which follows the only layer"
    x2d = x.reshape(batch * seq, d)
    l = 0
    x2, yga, ygb, tw_tok, gt2 = _layer(
        x2d, c, batch, seq, ada_w[l], ada_b[l], norm1_g[l], w_in[l], conv_w[l], conv_b[l],
        lru_wa[l], lru_ba[l], lru_wx[l], lru_bx[l], lru_lam[l], sg_ln_g[l], sg_ln_b[l],
        sg_ws[l], sg_bs[l], w_br_rnn[l], w_br_sg[l], w_out[l], norm2_g[l], w_router[l],
        b_router[l], w_gu[l], b_gu[l], w_down[l], b_down[l])
    out = _final(x2, yga, ygb, tw_tok, gt2, final_g.reshape(1, d), seq)
    return out.reshape(batch, seq, d)
```

```python
import functools

import jax
import jax.numpy as jnp
from jax import lax
from jax.experimental import pallas as pl
from jax.experimental.pallas import tpu as pltpu
from jax.experimental.pallas import tpu_sc as plsc

F32 = jnp.float32
BF16 = jnp.bfloat16
I32 = jnp.int32

D_MODEL = 1024
D_RNN = 1024
RNN_HEADS = 16
RNN_HEAD_DIM = D_RNN // RNN_HEADS
CONV_WIDTH = 4
LRU_C = 8.0
D_SG = 1024
SG_GROUPS = 8
SG_GROUP_DIM = D_SG // SG_GROUPS
SG_CHUNK = 128
N_EXPERTS = 32
TOP_K = 4
D_EXPERT = 1024
SWIGLU_LIMIT = 7.0
SWIGLU_ALPHA = 1.702
EPS = 1e-6
N_MOD = 6
D_IN = 2 * D_RNN + 2 * D_SG + 2 * D_MODEL

SUBLANES = 8
GATE_BLOCK = 256
N_GATE_BLOCKS = D_RNN // GATE_BLOCK

ADA_TN = 1536
INPROJ_TN = 1024
MIXER_TS = 256
MOE_BM = 512
MOE_HALF = MOE_BM // 2
SC_WINDOW = 128
PACK_W = D_MODEL // 4
FINAL_TM = 256
VMEM_LIMIT = 56 * 1024 * 1024


def _sigmoid(x):
    return 0.5 * jnp.tanh(0.5 * x) + 0.5


def _gelu_tanh(x):
    k = 0.7978845608028654
    hx = 0.5 * x
    return hx + hx * jnp.tanh(x * (k + (k * 0.044715) * (x * x)))


def _bdot(a, b):
    return jnp.dot(a, b, preferred_element_type=F32)


def _pack_halves(v):
    word = pltpu.pack_elementwise([v[:, 2 * PACK_W:], v[:, :2 * PACK_W]], packed_dtype=BF16)
    return word[:, :PACK_W], word[:, PACK_W:]


def _unpack_halves(wa, wb):
    part = lambda w, i: pltpu.unpack_elementwise(w, index=i, packed_dtype=BF16, unpacked_dtype=F32)
    return jnp.concatenate([part(wa, 1), part(wb, 1), part(wa, 0), part(wb, 0)], axis=1)


def _ada_kernel(c_ref, w_ref, b_ref, o_ref):
    c = c_ref[...]
    s = c * _sigmoid(c)
    o_ref[...] = jnp.dot(s, w_ref[...], preferred_element_type=F32,
                         precision=lax.Precision.HIGHEST) + b_ref[...]


def _ada(c, ada_w, ada_b):
    b, d = c.shape
    n = ada_w.shape[1]
    return pl.pallas_call(
        _ada_kernel,
        grid=(n // ADA_TN,),
        in_specs=[
            pl.BlockSpec((b, d), lambda j: (0, 0)),
            pl.BlockSpec((d, ADA_TN), lambda j: (0, j)),
            pl.BlockSpec((1, ADA_TN), lambda j: (0, j)),
        ],
        out_specs=pl.BlockSpec((b, ADA_TN), lambda j: (0, j)),
        out_shape=jax.ShapeDtypeStruct((b, n), F32),
        compiler_params=pltpu.CompilerParams(
            dimension_semantics=("arbitrary",), vmem_limit_bytes=VMEM_LIMIT),
        name="ada",
    )(c, ada_w, ada_b.reshape(1, n))


def _norm_mod(x, g, sc, sh):
    ms = jnp.mean(x * x, axis=-1, keepdims=True)
    y = x * lax.rsqrt(ms + EPS)
    return (y * g) * (1.0 + sc) + sh


def _mixer_kernel(x_ref, xn_ref, n1g_ref, sc1_ref, sh1_ref, sc1n_ref, sh1n_ref, win_ref,
                  gt1_ref, sc2_ref, sh2_ref,
                  convw_ref, convb_ref, wa_ref, ba_ref, wx_ref, bx_ref, lam_ref,
                  lng_ref, lnb_ref, ws_ref, bs_ref, wbr_ref, wbs_ref, wout_ref,
                  n2g_ref, wr_ref, br_ref, shift_ref,
                  x2_ref, h2a_ref, h2b_ref, idx_ref, tw_ref, rank_ref, cnt_ref,
                  z0_ref, z1_ref, hn0_ref, hn1_ref, xp_ref, xc_ref, a_ref, hh_ref, sv_ref, hstate_ref,
                  *, pairs_per_seq):
    ts = MIXER_TS
    s = pl.program_id(0)

    def inproj_norm(hn_ref, x_rows, sc, sh):
        hn_ref[...] = _norm_mod(x_rows, n1g_ref[...], sc, sh).astype(BF16)

    def inproj_chunks(z_dst, hn_ref, first, last):
        for c in range(first * INPROJ_TN, last * INPROJ_TN, INPROJ_TN):
            z_dst[:, c:c + INPROJ_TN] = _bdot(hn_ref[...], win_ref[:, c:c + INPROJ_TN]).astype(BF16)

    @pl.when(s == 0)
    def _():
        cnt_ref[...] = jnp.zeros_like(cnt_ref)
        inproj_norm(hn0_ref, x_ref[0:ts, :], sc1_ref[0], sh1_ref[0])
        inproj_chunks(z0_ref, hn0_ref, 0, D_IN // INPROJ_TN)
        inproj_norm(hn1_ref, x_ref[ts:2 * ts, :], sc1_ref[0], sh1_ref[0])

    @pl.when(s % pairs_per_seq == 0)
    def _():
        xp_ref[0:SUBLANES, :] = jnp.zeros((SUBLANES, D_RNN), F32)
        hstate_ref[...] = jnp.zeros_like(hstate_ref)

    def tile(k, z_ref, z_next_ref, hn_ref, hn_after_ref):
        rows = slice(k * ts, (k + 1) * ts)

        x16 = z_ref[:, 0:D_RNN]
        rnn_x = x16.astype(F32)
        cw = convw_ref[...]
        xc = cw[3:4] * rnn_x + convb_ref[...]
        for sft in range(1, CONV_WIDTH):
            xc = xc + cw[3 - sft:4 - sft] * _bdot(shift_ref[sft - 1], x16)
        xc_ref[...] = xc
        xp_ref[SUBLANES:2 * SUBLANES, :] = rnn_x[0:SUBLANES, :]
        xc_ref[0:SUBLANES, :] = (
            cw[3:4] * xp_ref[SUBLANES:2 * SUBLANES, :]
            + cw[2:3] * xp_ref[SUBLANES - 1:2 * SUBLANES - 1, :]
            + cw[1:2] * xp_ref[SUBLANES - 2:2 * SUBLANES - 2, :]
            + cw[0:1] * xp_ref[SUBLANES - 3:2 * SUBLANES - 3, :]) + convb_ref[...]
        xp_ref[0:SUBLANES, :] = rnn_x[ts - SUBLANES:ts, :]
        xc = xc_ref[...]

        xcb = xc.astype(BF16)
        r_parts, i_parts = [], []
        for g in range(N_GATE_BLOCKS):
            blk = xcb[:, g * GATE_BLOCK:(g + 1) * GATE_BLOCK]
            r_parts.append(_bdot(blk, wa_ref[g]))
            i_parts.append(_bdot(blk, wx_ref[g]))
        inproj_chunks(z_next_ref, hn_ref, 0, 3)
        r = _sigmoid(jnp.concatenate(r_parts, axis=1) + ba_ref[...])
        ig = _sigmoid(jnp.concatenate(i_parts, axis=1) + bx_ref[...])

        nl = -lam_ref[...]
        softplus = jnp.maximum(nl, 0.0) + jnp.log(1.0 + jnp.exp(-jnp.abs(nl)))
        a = jnp.exp(r * ((-LRU_C) * softplus))
        t = 1.0 - a * a
        u = jnp.where(t > 0.0, t * lax.rsqrt(t), 0.0) * (ig * xc)

        groups = ts // SUBLANES
        a3 = a.reshape(groups, SUBLANES, D_RNN)
        h3 = u.reshape(groups, SUBLANES, D_RNN)
        sub = lax.broadcasted_iota(I32, (groups, SUBLANES, D_RNN), 1)
        for step in (1, 2, 4):
            keep = sub >= step
            a_sh = jnp.where(keep, pltpu.roll(a3, step, 1), 1.0)
            h_sh = jnp.where(keep, pltpu.roll(h3, step, 1), 0.0)
            h3 = h3 + a3 * h_sh
            a3 = a3 * a_sh
        a_ref[...] = a3.reshape(ts, D_RNN)
        hh_ref[...] = h3.reshape(ts, D_RNN)

        def carry_body(gi, hc):
            grp = pl.ds(pl.multiple_of(gi * SUBLANES, SUBLANES), SUBLANES)
            hg = hh_ref[grp, :] + a_ref[grp, :] * hc
            hh_ref[grp, :] = hg
            return jnp.broadcast_to(hg[SUBLANES - 1:SUBLANES, :], (SUBLANES, D_RNN))

        hstate_ref[...] = lax.fori_loop(0, groups, carry_body, hstate_ref[...])

        inproj_chunks(z_next_ref, hn_ref, 3, 4)
        y_rnn = (hh_ref[...] * _gelu_tanh(z_ref[:, D_RNN:2 * D_RNN].astype(F32))).astype(BF16)

        gv = _gelu_tanh(z_ref[:, 2 * D_RNN + D_SG:2 * D_RNN + 2 * D_SG].astype(F32))
        mu = jnp.mean(gv, axis=-1, keepdims=True)
        dv = gv - mu
        var = jnp.mean(dv * dv, axis=-1, keepdims=True)
        vn = (dv * lax.rsqrt(var + EPS) * lng_ref[...] + lnb_ref[...]).astype(BF16)
        tr = lax.broadcasted_iota(I32, (SG_CHUNK, SG_CHUNK), 0)
        tc = lax.broadcasted_iota(I32, (SG_CHUNK, SG_CHUNK), 1)
        causal = tc <= tr
        for g in range(SG_GROUPS):
            wg = jnp.where(causal, ws_ref[g], 0.0).astype(BF16)
            cols = slice(g * SG_GROUP_DIM, (g + 1) * SG_GROUP_DIM)
            for n in range(ts // SG_CHUNK):
                chunk = slice(n * SG_CHUNK, (n + 1) * SG_CHUNK)
                sv_ref[chunk, cols] = _bdot(wg, vn[chunk, cols]) + bs_ref[:, cols]
        inproj_chunks(z_next_ref, hn_ref, 4, 5)
        gu = _gelu_tanh(z_ref[:, 2 * D_RNN:2 * D_RNN + D_SG].astype(F32))
        y_sg = (gu * sv_ref[...]).astype(BF16)

        g_rnn = z_ref[:, 2 * D_RNN + 2 * D_SG:2 * D_RNN + 2 * D_SG + D_MODEL].astype(F32)
        g_sg = z_ref[:, 2 * D_RNN + 2 * D_SG + D_MODEL:D_IN].astype(F32)
        m = (_sigmoid(g_rnn) * _bdot(y_rnn, wbr_ref[...])
             + _sigmoid(g_sg) * _bdot(y_sg, wbs_ref[...])).astype(BF16)
        x2 = x_ref[rows, :] + gt1_ref[0] * _bdot(m, wout_ref[...])
        x2_ref[rows, :] = x2
        inproj_chunks(z_next_ref, hn_ref, 5, 6)

        h2 = _norm_mod(x2, n2g_ref[...], sc2_ref[0], sh2_ref[0])
        h2a_ref[rows, :], h2b_ref[rows, :] = _pack_halves(h2)
        h_hi = h2.astype(BF16)
        h_lo = (h2 - h_hi.astype(F32)).astype(BF16)
        nt_dims = (((1,), (1,)), ((), ()))
        by_hi = lax.dot_general(wr_ref[...], h_hi, nt_dims, preferred_element_type=F32)
        logits = (by_hi[:N_EXPERTS] + by_hi[N_EXPERTS:]
                  + lax.dot_general(wr_ref[0:N_EXPERTS, :], h_lo, nt_dims,
                                    preferred_element_type=F32)
                  + br_ref[...])
        e_iota = lax.broadcasted_iota(I32, (N_EXPERTS, ts), 0)
        v = logits
        vals, idxs, sels = [], [], []
        for _ in range(TOP_K):
            mx = jnp.max(v, axis=0, keepdims=True)
            ik = jnp.min(jnp.where(v == mx, e_iota, N_EXPERTS), axis=0, keepdims=True)
            sel = e_iota == ik
            v = jnp.where(sel, -jnp.inf, v)
            vals.append(mx)
            idxs.append(ik)
            sels.append(sel)
        exps = [jnp.exp(val - vals[0]) for val in vals]
        denom = exps[0] + exps[1] + exps[2] + exps[3]
        idx_ref[:, rows] = jnp.concatenate(idxs, axis=0)
        tw_ref[:, rows] = jnp.concatenate([e / denom for e in exps], axis=0)

        onehot = jnp.zeros((N_EXPERTS, ts), F32)
        for sel in sels:
            onehot = jnp.where(sel, 1.0, onehot)
        sr = lax.broadcasted_iota(I32, (ts, ts), 0)
        st = lax.broadcasted_iota(I32, (ts, ts), 1)
        before = jnp.where(sr < st, 1.0, 0.0).astype(BF16)
        total = cnt_ref[...] + _bdot(onehot.astype(BF16), before)
        ranks = [jnp.sum(jnp.where(sel, total, 0.0), axis=0, keepdims=True) for sel in sels]
        rank_ref[:, rows] = jnp.concatenate(ranks, axis=0).astype(I32)
        cnt_ref[...] = cnt_ref[...] + jnp.sum(onehot, axis=1, keepdims=True)

        inproj_norm(hn_after_ref, xn_ref[rows, :], sc1n_ref[0], sh1n_ref[0])

    tile(0, z0_ref, z1_ref, hn1_ref, hn0_ref)
    tile(1, z1_ref, z0_ref, hn0_ref, hn1_ref)


def _mixer(x2d, n1g, sc1, sh1, w_in, gt1, sc2, sh2, conv_w, conv_b, wa_bd, ba, wx_bd, bx, lam,
           ln_g, ln_b, ws, bs_tile, wbr, wbs, wout, n2g, wr_split, br, shifts, batch, seq):
    n, d = x2d.shape
    ts = MIXER_TS
    tiles = n // ts
    tiles_per_seq = seq // ts
    pairs_per_seq = tiles_per_seq // 2
    assert tiles_per_seq % 2 == 0
    next_pair = lambda s: jnp.minimum(s + 1, tiles // 2 - 1)
    pair = lambda s: (s, 0)
    pairt = lambda s: (0, s)
    bvec = lambda s: (s // pairs_per_seq, 0, 0)
    bvec_next = lambda s: (next_pair(s) // pairs_per_seq, 0, 0)
    c2 = lambda s: (0, 0)
    c3 = lambda s: (0, 0, 0)
    in_specs = [
        pl.BlockSpec((2 * ts, d), pair),
        pl.BlockSpec((2 * ts, d), lambda s: (next_pair(s), 0)),
        pl.BlockSpec((1, d), c2),
        pl.BlockSpec((1, 1, d), bvec),
        pl.BlockSpec((1, 1, d), bvec),
        pl.BlockSpec((1, 1, d), bvec_next),
        pl.BlockSpec((1, 1, d), bvec_next),
        pl.BlockSpec((d, D_IN), c2),
        pl.BlockSpec((1, 1, d), bvec),
        pl.BlockSpec((1, 1, d), bvec),
        pl.BlockSpec((1, 1, d), bvec),
        pl.BlockSpec((CONV_WIDTH, D_RNN), c2),
        pl.BlockSpec((1, D_RNN), c2),
        pl.BlockSpec((N_GATE_BLOCKS, GATE_BLOCK, GATE_BLOCK), c3),
        pl.BlockSpec((1, D_RNN), c2),
        pl.BlockSpec((N_GATE_BLOCKS, GATE_BLOCK, GATE_BLOCK), c3),
        pl.BlockSpec((1, D_RNN), c2),
        pl.BlockSpec((1, D_RNN), c2),
        pl.BlockSpec((1, D_SG), c2),
        pl.BlockSpec((1, D_SG), c2),
        pl.BlockSpec((SG_GROUPS, SG_CHUNK, SG_CHUNK), c3),
        pl.BlockSpec((SG_CHUNK, D_SG), c2),
        pl.BlockSpec((D_RNN, d), c2),
        pl.BlockSpec((D_SG, d), c2),
        pl.BlockSpec((d, d), c2),
        pl.BlockSpec((1, d), c2),
        pl.BlockSpec((2 * N_EXPERTS, d), c2),
        pl.BlockSpec((N_EXPERTS, 1), c2),
        pl.BlockSpec((CONV_WIDTH - 1, ts, ts), c3),
    ]
    out_specs = [
        pl.BlockSpec((2 * ts, d), pair),
        pl.BlockSpec((2 * ts, PACK_W), pair),
        pl.BlockSpec((2 * ts, PACK_W), pair),
        pl.BlockSpec((TOP_K, 2 * ts), pairt),
        pl.BlockSpec((TOP_K, 2 * ts), pairt),
        pl.BlockSpec((TOP_K, 2 * ts), pairt),
        pl.BlockSpec((N_EXPERTS, 1), c2),
    ]
    out_shape = [
        jax.ShapeDtypeStruct((n, d), F32),
        jax.ShapeDtypeStruct((n, PACK_W), jnp.uint32),
        jax.ShapeDtypeStruct((n, PACK_W), jnp.uint32),
        jax.ShapeDtypeStruct((TOP_K, n), I32),
        jax.ShapeDtypeStruct((TOP_K, n), F32),
        jax.ShapeDtypeStruct((TOP_K, n), I32),
        jax.ShapeDtypeStruct((N_EXPERTS, 1), F32),
    ]
    scratch = [
        pltpu.VMEM((ts, D_IN), BF16),
        pltpu.VMEM((ts, D_IN), BF16),
        pltpu.VMEM((ts, d), BF16),
        pltpu.VMEM((ts, d), BF16),
        pltpu.VMEM((2 * SUBLANES, D_RNN), F32),
        pltpu.VMEM((ts, D_RNN), F32),
        pltpu.VMEM((ts, D_RNN), F32),
        pltpu.VMEM((ts, D_RNN), F32),
        pltpu.VMEM((ts, D_SG), F32),
        pltpu.VMEM((SUBLANES, D_RNN), F32),
    ]
    return pl.pallas_call(
        functools.partial(_mixer_kernel, pairs_per_seq=pairs_per_seq),
        grid=(tiles // 2,),
        in_specs=in_specs,
        out_specs=out_specs,
        out_shape=out_shape,
        scratch_shapes=scratch,
        compiler_params=pltpu.CompilerParams(
            dimension_semantics=("arbitrary",), vmem_limit_bytes=VMEM_LIMIT),
        name="mixer",
    )(x2d, x2d, n1g, sc1, sh1, sc1, sh1, w_in, gt1, sc2, sh2, conv_w, conv_b, wa_bd, ba, wx_bd, bx,
      lam, ln_g, ln_b, ws, bs_tile, wbr, wbs, wout, n2g, wr_split, br, shifts)


def _sc_mesh():
    return plsc.VectorSubcoreMesh(core_axis_name="core", subcore_axis_name="subcore")


def _sc_scatter_rows(rows, pos, cap):
    n, d = rows.shape
    kk = pos.shape[0]

    @functools.partial(
        pl.kernel, out_type=jax.ShapeDtypeStruct((cap, d), rows.dtype), mesh=_sc_mesh(),
        scratch_types=[], name="sc_scatter_rows")
    def scatter(x_hbm, i_hbm, o_hbm):
        def body(x_vmem, i_vmem):
            pltpu.sync_copy(x_vmem, o_hbm.at[i_vmem.at[0]])

        pltpu.emit_pipeline(
            body,
            grid=(n // SC_WINDOW, kk),
            in_specs=[pl.BlockSpec((SC_WINDOW, d), lambda i, k: (i, 0)),
                      pl.BlockSpec((1, SC_WINDOW), lambda i, k: (k, i))],
            out_specs=[],
            core_axis_name=("core", "subcore"),
            dimension_semantics=(pltpu.PARALLEL, pltpu.ARBITRARY),
        )(x_hbm, i_hbm)

    return scatter(rows, pos)


def _sc_gather_rows(table, idx):
    m = idx.shape[1]
    d = table.shape[1]

    @functools.partial(
        pl.kernel, out_type=jax.ShapeDtypeStruct((m, d), table.dtype), mesh=_sc_mesh(),
        scratch_types=[], name="sc_gather_rows")
    def gather(x_hbm, i_hbm, o_hbm):
        def body(i_vmem, o_vmem):
            pltpu.sync_copy(x_hbm.at[i_vmem.at[0]], o_vmem)

        pltpu.emit_pipeline(
            body,
            grid=(m // SC_WINDOW,),
            in_specs=[pl.BlockSpec((1, SC_WINDOW), lambda i: (0, i))],
            out_specs=[pl.BlockSpec((SC_WINDOW, d), lambda i: (i, 0))],
            core_axis_name=("core", "subcore"),
            dimension_semantics=(pltpu.PARALLEL,),
        )(i_hbm, o_hbm)

    return gather(table, idx)


def _expert_kernel(be_ref, nv_ref, xa_ref, xb_ref, wgu_ref, bgu_ref, wd_ref, bd_ref,
                   ya_ref, yb_ref, wgu_bf, wd_bf):
    i = pl.program_id(0)
    nvalid = nv_ref[i]

    @pl.when((i == 0) | (be_ref[i] != be_ref[jnp.maximum(i - 1, 0)]))
    def _():
        wgu_bf[...] = wgu_ref[0].astype(BF16)
        wd_bf[...] = wd_ref[0].astype(BF16)

    def mlp_rows(h):
        rows = slice(h * MOE_HALF, (h + 1) * MOE_HALF)
        live = lax.broadcasted_iota(I32, (MOE_HALF, 1), 0) < nvalid - h * MOE_HALF
        xb = jnp.where(live, _unpack_halves(xa_ref[rows, :], xb_ref[rows, :]), 0.0).astype(BF16)
        gu = _bdot(xb, wgu_bf[...]) + bgu_ref[0]
        gate = jnp.minimum(gu[:, :D_EXPERT], SWIGLU_LIMIT)
        up = jnp.clip(gu[:, D_EXPERT:], -SWIGLU_LIMIT, SWIGLU_LIMIT)
        act = (up + 1.0) * (gate * _sigmoid(SWIGLU_ALPHA * gate))
        y = _bdot(act.astype(BF16), wd_bf[...]) + bd_ref[0]
        ya_ref[rows, :], yb_ref[rows, :] = _pack_halves(y)

    def zero_rows(h):
        rows = slice(h * MOE_HALF, (h + 1) * MOE_HALF)
        ya_ref[rows, :] = jnp.zeros((MOE_HALF, PACK_W), jnp.uint32)
        yb_ref[rows, :] = jnp.zeros((MOE_HALF, PACK_W), jnp.uint32)

    @pl.when(nvalid > MOE_HALF)
    def _():
        mlp_rows(0)
        mlp_rows(1)

    @pl.when((nvalid > 0) & (nvalid <= MOE_HALF))
    def _():
        mlp_rows(0)
        zero_rows(1)

    @pl.when(nvalid <= 0)
    def _():
        zero_rows(0)
        zero_rows(1)


def _experts(block_e, n_valid, xa, xb, w_gu, b_gu, w_down, b_down):
    cap = xa.shape[0]
    d = D_MODEL
    nb = cap // MOE_BM
    half = pl.BlockSpec((MOE_BM, PACK_W), lambda i, be, nv: (i, 0))
    grid_spec = pltpu.PrefetchScalarGridSpec(
        num_scalar_prefetch=2,
        grid=(nb,),
        in_specs=[
            half,
            half,
            pl.BlockSpec((1, d, 2 * D_EXPERT), lambda i, be, nv: (be[i], 0, 0)),
            pl.BlockSpec((1, 1, 2 * D_EXPERT), lambda i, be, nv: (be[i], 0, 0)),
            pl.BlockSpec((1, D_EXPERT, d), lambda i, be, nv: (be[i], 0, 0)),
            pl.BlockSpec((1, 1, d), lambda i, be, nv: (be[i], 0, 0)),
        ],
        out_specs=[half, half],
        scratch_shapes=[pltpu.VMEM((d, 2 * D_EXPERT), BF16), pltpu.VMEM((D_EXPERT, d), BF16)],
    )
    return pl.pallas_call(
        _expert_kernel,
        grid_spec=grid_spec,
        out_shape=[jax.ShapeDtypeStruct((cap, PACK_W), jnp.uint32)] * 2,
        compiler_params=pltpu.CompilerParams(
            dimension_semantics=("arbitrary",), vmem_limit_bytes=VMEM_LIMIT),
        name="experts",
    )(block_e, n_valid, xa, xb, w_gu, b_gu, w_down, b_down)


def _final_kernel(x2_ref, ya_ref, yb_ref, tw_ref, gt2_ref, fg_ref, o_ref):
    tw = tw_ref[...]
    moe = tw[:, 0:1] * _unpack_halves(ya_ref[0], yb_ref[0])
    for k in range(1, TOP_K):
        moe = moe + tw[:, k:k + 1] * _unpack_halves(ya_ref[k], yb_ref[k])
    x3 = x2_ref[...] + gt2_ref[0] * moe
    ms = jnp.mean(x3 * x3, axis=-1, keepdims=True)
    o_ref[...] = (x3 * lax.rsqrt(ms + EPS)) * fg_ref[...]


def _final(x2, yga, ygb, tw_tok, gt2, final_g, seq):
    n, d = x2.shape
    tiles_per_seq = seq // FINAL_TM
    half = pl.BlockSpec((TOP_K, FINAL_TM, PACK_W), lambda i: (0, i, 0))
    return pl.pallas_call(
        _final_kernel,
        grid=(n // FINAL_TM,),
        in_specs=[
            pl.BlockSpec((FINAL_TM, d), lambda i: (i, 0)),
            half,
            half,
            pl.BlockSpec((FINAL_TM, TOP_K), lambda i: (i, 0)),
            pl.BlockSpec((1, 1, d), lambda i: (i // tiles_per_seq, 0, 0)),
            pl.BlockSpec((1, d), lambda i: (0, 0)),
        ],
        out_specs=pl.BlockSpec((FINAL_TM, d), lambda i: (i, 0)),
        out_shape=jax.ShapeDtypeStruct((n, d), F32),
        compiler_params=pltpu.CompilerParams(
            dimension_semantics=("arbitrary",), vmem_limit_bytes=VMEM_LIMIT),
        name="final",
    )(x2, yga, ygb, tw_tok, gt2, final_g)


def _block_diag(w):
    per = GATE_BLOCK // RNN_HEAD_DIM
    w4 = w.reshape(N_GATE_BLOCKS, per, RNN_HEAD_DIM, RNN_HEAD_DIM)
    eye = jnp.eye(per, dtype=w.dtype)
    bd = jnp.einsum("gpij,pq->gpiqj", w4, eye)
    return bd.reshape(N_GATE_BLOCKS, GATE_BLOCK, GATE_BLOCK)


def _layer(x2d, c, batch, seq, ada_w, ada_b, norm1_g, w_in, conv_w, conv_b, lru_wa, lru_ba,
           lru_wx, lru_bx, lru_lam, sg_ln_g, sg_ln_b, sg_ws, sg_bs, w_br_rnn, w_br_sg, w_out,
           norm2_g, w_router, b_router, w_gu, b_gu, w_down, b_down):
    n, d = x2d.shape
    mod = _ada(c, ada_w, ada_b)
    sh1, sc1, gt1, sh2, sc2, gt2 = [
        mod[:, i * d:(i + 1) * d].reshape(batch, 1, d) for i in range(N_MOD)]
    row = lambda v: v.reshape(1, -1)

    bs_tile = jnp.repeat(sg_bs.T, SG_GROUP_DIM, axis=1)
    wr_t = w_router.T
    wr_hi = wr_t.astype(BF16)
    wr_lo = (wr_t - wr_hi.astype(F32)).astype(BF16)
    t_out = lax.broadcasted_iota(I32, (MIXER_TS, MIXER_TS), 0)
    t_in = lax.broadcasted_iota(I32, (MIXER_TS, MIXER_TS), 1)
    shifts = jnp.stack([(t_out - t_in == s) for s in range(1, CONV_WIDTH)]).astype(BF16)
    x2, h2a, h2b, top_idx, top_w, rank, counts = _mixer(
        x2d, row(norm1_g), sc1, sh1, w_in.astype(BF16), gt1, sc2, sh2, conv_w, row(conv_b),
        _block_diag(lru_wa).astype(BF16), row(lru_ba), _block_diag(lru_wx).astype(BF16), row(lru_bx),
        row(lru_lam), row(sg_ln_g), row(sg_ln_b), sg_ws, bs_tile,
        w_br_rnn.astype(BF16), w_br_sg.astype(BF16), w_out.astype(BF16), row(norm2_g),
        jnp.concatenate([wr_hi, wr_lo], axis=0), b_router.reshape(N_EXPERTS, 1), shifts, batch, seq)

    a_total = n * TOP_K
    cap = a_total + N_EXPERTS * MOE_BM
    nb = cap // MOE_BM
    cnt = counts[:, 0].astype(I32)
    padded = ((cnt + MOE_BM - 1) // MOE_BM) * MOE_BM
    padded_end = jnp.cumsum(padded)
    padded_start = padded_end - padded
    e_ids = jnp.arange(N_EXPERTS, dtype=I32)
    start_of = jnp.sum(jnp.where(top_idx[None] == e_ids[:, None, None],
                                 padded_start[:, None, None], 0), axis=0)
    pos = start_of + rank
    blk_row0 = jnp.arange(nb, dtype=I32) * MOE_BM
    block_e = jnp.minimum(jnp.sum((padded_end[None, :] <= blk_row0[:, None]).astype(I32), axis=1),
                          N_EXPERTS - 1)
    n_valid = jnp.clip(cnt[block_e] - (blk_row0 - padded_start[block_e]), 0, MOE_BM).astype(I32)

    xa = _sc_scatter_rows(h2a, pos, cap)
    xb = _sc_scatter_rows(h2b, pos, cap)
    ya, yb = _experts(block_e, n_valid, xa, xb, w_gu, b_gu.reshape(N_EXPERTS, 1, -1),
                      w_down, b_down.reshape(N_EXPERTS, 1, -1))
    flat_pos = pos.reshape(1, -1)
    yga = _sc_gather_rows(ya, flat_pos).reshape(TOP_K, n, PACK_W)
    ygb = _sc_gather_rows(yb, flat_pos).reshape(TOP_K, n, PACK_W)
    return x2, yga, ygb, top_w.T, gt2


def kernel(x, c, ada_w, ada_b, norm1_g, w_in, conv_w, conv_b, lru_wa, lru_ba, lru_wx, lru_bx,
           lru_lam, sg_ln_g, sg_ln_b, sg_ws, sg_bs, w_br_rnn, w_br_sg, w_out, norm2_g,
           w_router, b_router, w_gu, b_gu, w_down, b_down, final_g):
    batch, seq, d = x.shape
    depth = ada_w.shape[0]
    assert depth == 1, "the combine is fused with the final norm, which follows the only layer"
    x2d = x.reshape(batch * seq, d)
    l = 0
    x2, yga, ygb, tw_tok, gt2 = _layer(
        x2d, c, batch, seq, ada_w[l], ada_b[l], norm1_g[l], w_in[l], conv_w[l], conv_b[l],
        lru_wa[l], lru_ba[l], lru_wx[l], lru_bx[l], lru_lam[l], sg_ln_g[l], sg_ln_b[l],
        sg_ws[l], sg_bs[l], w_br_rnn[l], w_br_sg[l], w_out[l], norm2_g[l], w_router[l],
        b_router[l], w_gu[l], b_gu[l], w_down[l], b_down[l])
    out = _final(x2, yga, ygb, tw_tok, gt2, final_g.reshape(1, d), seq)
    return out.reshape(batch, seq, d)
```

```python
import functools

import jax
import jax.numpy as jnp
from jax import lax
from jax.experimental import pallas as pl
from jax.experimental.pallas import tpu as pltpu
from jax.experimental.pallas import tpu_sc as plsc

F32 = jnp.float32
BF16 = jnp.bfloat16
I32 = jnp.int32

D_MODEL = 1024
D_RNN = 1024
RNN_HEADS = 16
RNN_HEAD_DIM = D_RNN // RNN_HEADS
CONV_WIDTH = 4
LRU_C = 8.0
D_SG = 1024
SG_GROUPS = 8
SG_GROUP_DIM = D_SG // SG_GROUPS
SG_CHUNK = 128
N_EXPERTS = 32
TOP_K = 4
D_EXPERT = 1024
SWIGLU_LIMIT = 7.0
SWIGLU_ALPHA = 1.702
EPS = 1e-6
N_MOD = 6
D_IN = 2 * D_RNN + 2 * D_SG + 2 * D_MODEL

SUBLANES = 8
GATE_BLOCK = 256
N_GATE_BLOCKS = D_RNN // GATE_BLOCK

ADA_TN = 1536
INPROJ_TN = 1024
MIXER_TS = 256
MOE_BM = 512
MOE_HALF = MOE_BM // 2
TOKEN_GROUPS = 2
SC_WINDOW = 128
PACK_W = D_MODEL // 4
FINAL_TM = 256
VMEM_LIMIT = 56 * 1024 * 1024


def _sigmoid(x):
    return 0.5 * jnp.tanh(0.5 * x) + 0.5


def _gelu_tanh(x):
    k = 0.7978845608028654
    hx = 0.5 * x
    return hx + hx * jnp.tanh(x * (k + (k * 0.044715) * (x * x)))


def _bdot(a, b):
    return jnp.dot(a, b, preferred_element_type=F32)


def _pack_halves(v):
    word = pltpu.pack_elementwise([v[:, 2 * PACK_W:], v[:, :2 * PACK_W]], packed_dtype=BF16)
    return word[:, :PACK_W], word[:, PACK_W:]


def _unpack_halves(wa, wb):
    part = lambda w, i: pltpu.unpack_elementwise(w, index=i, packed_dtype=BF16, unpacked_dtype=F32)
    return jnp.concatenate([part(wa, 1), part(wb, 1), part(wa, 0), part(wb, 0)], axis=1)


def _ada_kernel(c_ref, w_ref, b_ref, o_ref):
    c = c_ref[...]
    s = c * _sigmoid(c)
    o_ref[...] = jnp.dot(s, w_ref[...], preferred_element_type=F32,
                         precision=lax.Precision.HIGHEST) + b_ref[...]


def _ada(c, ada_w, ada_b):
    b, d = c.shape
    n = ada_w.shape[1]
    return pl.pallas_call(
        _ada_kernel,
        grid=(n // ADA_TN,),
        in_specs=[
            pl.BlockSpec((b, d), lambda j: (0, 0)),
            pl.BlockSpec((d, ADA_TN), lambda j: (0, j)),
            pl.BlockSpec((1, ADA_TN), lambda j: (0, j)),
        ],
        out_specs=pl.BlockSpec((b, ADA_TN), lambda j: (0, j)),
        out_shape=jax.ShapeDtypeStruct((b, n), F32),
        compiler_params=pltpu.CompilerParams(
            dimension_semantics=("arbitrary",), vmem_limit_bytes=VMEM_LIMIT),
        name="ada",
    )(c, ada_w, ada_b.reshape(1, n))


def _norm_mod(x, g, sc, sh):
    ms = jnp.mean(x * x, axis=-1, keepdims=True)
    y = x * lax.rsqrt(ms + EPS)
    return (y * g) * (1.0 + sc) + sh


def _mixer_kernel(x_ref, xn_ref, n1g_ref, sc1_ref, sh1_ref, sc1n_ref, sh1n_ref, win_ref,
                  gt1_ref, sc2_ref, sh2_ref,
                  convw_ref, convb_ref, wa_ref, ba_ref, wx_ref, bx_ref, lam_ref,
                  lng_ref, lnb_ref, ws_ref, bs_ref, wbr_ref, wbs_ref, wout_ref,
                  n2g_ref, wr_ref, br_ref, shift_ref,
                  x2_ref, h2a_ref, h2b_ref, idx_ref, tw_ref, rank_ref, cnt_ref,
                  z0_ref, z1_ref, hn0_ref, hn1_ref, xp_ref, xc_ref, a_ref, hh_ref, sv_ref, hstate_ref,
                  *, pairs_per_seq):
    ts = MIXER_TS
    s = pl.program_id(0)

    def inproj_norm(hn_ref, x_rows, sc, sh):
        hn_ref[...] = _norm_mod(x_rows, n1g_ref[...], sc, sh).astype(BF16)

    def inproj_chunks(z_dst, hn_ref, first, last):
        for c in range(first * INPROJ_TN, last * INPROJ_TN, INPROJ_TN):
            z_dst[:, c:c + INPROJ_TN] = _bdot(hn_ref[...], win_ref[:, c:c + INPROJ_TN]).astype(BF16)

    @pl.when(s == 0)
    def _():
        cnt_ref[...] = jnp.zeros_like(cnt_ref)
        inproj_norm(hn0_ref, x_ref[0:ts, :], sc1_ref[0], sh1_ref[0])
        inproj_chunks(z0_ref, hn0_ref, 0, D_IN // INPROJ_TN)
        inproj_norm(hn1_ref, x_ref[ts:2 * ts, :], sc1_ref[0], sh1_ref[0])

    @pl.when(s % pairs_per_seq == 0)
    def _():
        xp_ref[0:SUBLANES, :] = jnp.zeros((SUBLANES, D_RNN), F32)
        hstate_ref[...] = jnp.zeros_like(hstate_ref)

    def tile(k, z_ref, z_next_ref, hn_ref, hn_after_ref):
        rows = slice(k * ts, (k + 1) * ts)

        x16 = z_ref[:, 0:D_RNN]
        rnn_x = x16.astype(F32)
        cw = convw_ref[...]
        xc = cw[3:4] * rnn_x + convb_ref[...]
        for sft in range(1, CONV_WIDTH):
            xc = xc + cw[3 - sft:4 - sft] * _bdot(shift_ref[sft - 1], x16)
        xc_ref[...] = xc
        xp_ref[SUBLANES:2 * SUBLANES, :] = rnn_x[0:SUBLANES, :]
        xc_ref[0:SUBLANES, :] = (
            cw[3:4] * xp_ref[SUBLANES:2 * SUBLANES, :]
            + cw[2:3] * xp_ref[SUBLANES - 1:2 * SUBLANES - 1, :]
            + cw[1:2] * xp_ref[SUBLANES - 2:2 * SUBLANES - 2, :]
            + cw[0:1] * xp_ref[SUBLANES - 3:2 * SUBLANES - 3, :]) + convb_ref[...]
        xp_ref[0:SUBLANES, :] = rnn_x[ts - SUBLANES:ts, :]
        xc = xc_ref[...]

        xcb = xc.astype(BF16)
        r_parts, i_parts = [], []
        for g in range(N_GATE_BLOCKS):
            blk = xcb[:, g * GATE_BLOCK:(g + 1) * GATE_BLOCK]
            r_parts.append(_bdot(blk, wa_ref[g]))
            i_parts.append(_bdot(blk, wx_ref[g]))
        inproj_chunks(z_next_ref, hn_ref, 0, 3)
        r = _sigmoid(jnp.concatenate(r_parts, axis=1) + ba_ref[...])
        ig = _sigmoid(jnp.concatenate(i_parts, axis=1) + bx_ref[...])

        nl = -lam_ref[...]
        softplus = jnp.maximum(nl, 0.0) + jnp.log(1.0 + jnp.exp(-jnp.abs(nl)))
        a = jnp.exp(r * ((-LRU_C) * softplus))
        t = 1.0 - a * a
        u = jnp.where(t > 0.0, t * lax.rsqrt(t), 0.0) * (ig * xc)

        groups = ts // SUBLANES
        a3 = a.reshape(groups, SUBLANES, D_RNN)
        h3 = u.reshape(groups, SUBLANES, D_RNN)
        sub = lax.broadcasted_iota(I32, (groups, SUBLANES, D_RNN), 1)
        for step in (1, 2, 4):
            keep = sub >= step
            a_sh = jnp.where(keep, pltpu.roll(a3, step, 1), 1.0)
            h_sh = jnp.where(keep, pltpu.roll(h3, step, 1), 0.0)
            h3 = h3 + a3 * h_sh
            a3 = a3 * a_sh
        a_ref[...] = a3.reshape(ts, D_RNN)
        hh_ref[...] = h3.reshape(ts, D_RNN)

        def carry_body(gi, hc):
            grp = pl.ds(pl.multiple_of(gi * SUBLANES, SUBLANES), SUBLANES)
            hg = hh_ref[grp, :] + a_ref[grp, :] * hc
            hh_ref[grp, :] = hg
            return jnp.broadcast_to(hg[SUBLANES - 1:SUBLANES, :], (SUBLANES, D_RNN))

        hstate_ref[...] = lax.fori_loop(0, groups, carry_body, hstate_ref[...])

        inproj_chunks(z_next_ref, hn_ref, 3, 4)
        y_rnn = (hh_ref[...] * _gelu_tanh(z_ref[:, D_RNN:2 * D_RNN].astype(F32))).astype(BF16)

        gv = _gelu_tanh(z_ref[:, 2 * D_RNN + D_SG:2 * D_RNN + 2 * D_SG].astype(F32))
        mu = jnp.mean(gv, axis=-1, keepdims=True)
        dv = gv - mu
        var = jnp.mean(dv * dv, axis=-1, keepdims=True)
        vn = (dv * lax.rsqrt(var + EPS) * lng_ref[...] + lnb_ref[...]).astype(BF16)
        tr = lax.broadcasted_iota(I32, (SG_CHUNK, SG_CHUNK), 0)
        tc = lax.broadcasted_iota(I32, (SG_CHUNK, SG_CHUNK), 1)
        causal = tc <= tr
        for g in range(SG_GROUPS):
            wg = jnp.where(causal, ws_ref[g], 0.0).astype(BF16)
            cols = slice(g * SG_GROUP_DIM, (g + 1) * SG_GROUP_DIM)
            for n in range(ts // SG_CHUNK):
                chunk = slice(n * SG_CHUNK, (n + 1) * SG_CHUNK)
                sv_ref[chunk, cols] = _bdot(wg, vn[chunk, cols]) + bs_ref[:, cols]
        inproj_chunks(z_next_ref, hn_ref, 4, 5)
        gu = _gelu_tanh(z_ref[:, 2 * D_RNN:2 * D_RNN + D_SG].astype(F32))
        y_sg = (gu * sv_ref[...]).astype(BF16)

        g_rnn = z_ref[:, 2 * D_RNN + 2 * D_SG:2 * D_RNN + 2 * D_SG + D_MODEL].astype(F32)
        g_sg = z_ref[:, 2 * D_RNN + 2 * D_SG + D_MODEL:D_IN].astype(F32)
        m = (_sigmoid(g_rnn) * _bdot(y_rnn, wbr_ref[...])
             + _sigmoid(g_sg) * _bdot(y_sg, wbs_ref[...])).astype(BF16)
        x2 = x_ref[rows, :] + gt1_ref[0] * _bdot(m, wout_ref[...])
        x2_ref[rows, :] = x2
        inproj_chunks(z_next_ref, hn_ref, 5, 6)

        h2 = _norm_mod(x2, n2g_ref[...], sc2_ref[0], sh2_ref[0])
        h2a_ref[rows, :], h2b_ref[rows, :] = _pack_halves(h2)
        h_hi = h2.astype(BF16)
        h_lo = (h2 - h_hi.astype(F32)).astype(BF16)
        nt_dims = (((1,), (1,)), ((), ()))
        by_hi = lax.dot_general(wr_ref[...], h_hi, nt_dims, preferred_element_type=F32)
        logits = (by_hi[:N_EXPERTS] + by_hi[N_EXPERTS:]
                  + lax.dot_general(wr_ref[0:N_EXPERTS, :], h_lo, nt_dims,
                                    preferred_element_type=F32)
                  + br_ref[...])
        e_iota = lax.broadcasted_iota(I32, (N_EXPERTS, ts), 0)
        v = logits
        vals, idxs, sels = [], [], []
        for _ in range(TOP_K):
            mx = jnp.max(v, axis=0, keepdims=True)
            ik = jnp.min(jnp.where(v == mx, e_iota, N_EXPERTS), axis=0, keepdims=True)
            sel = e_iota == ik
            v = jnp.where(sel, -jnp.inf, v)
            vals.append(mx)
            idxs.append(ik)
            sels.append(sel)
        exps = [jnp.exp(val - vals[0]) for val in vals]
        denom = exps[0] + exps[1] + exps[2] + exps[3]
        idx_ref[:, rows] = jnp.concatenate(idxs, axis=0)
        tw_ref[:, rows] = jnp.concatenate([e / denom for e in exps], axis=0)

        onehot = jnp.zeros((N_EXPERTS, ts), F32)
        for sel in sels:
            onehot = jnp.where(sel, 1.0, onehot)
        sr = lax.broadcasted_iota(I32, (ts, ts), 0)
        st = lax.broadcasted_iota(I32, (ts, ts), 1)
        before = jnp.where(sr < st, 1.0, 0.0).astype(BF16)
        total = cnt_ref[...] + _bdot(onehot.astype(BF16), before)
        ranks = [jnp.sum(jnp.where(sel, total, 0.0), axis=0, keepdims=True) for sel in sels]
        rank_ref[:, rows] = jnp.concatenate(ranks, axis=0).astype(I32)
        cnt_ref[...] = cnt_ref[...] + jnp.sum(onehot, axis=1, keepdims=True)

        inproj_norm(hn_after_ref, xn_ref[rows, :], sc1n_ref[0], sh1n_ref[0])

    tile(0, z0_ref, z1_ref, hn1_ref, hn0_ref)
    tile(1, z1_ref, z0_ref, hn0_ref, hn1_ref)


def _mixer(x2d, n1g, sc1, sh1, w_in, gt1, sc2, sh2, conv_w, conv_b, wa_bd, ba, wx_bd, bx, lam,
           ln_g, ln_b, ws, bs_tile, wbr, wbs, wout, n2g, wr_split, br, shifts, batch0, batches, seq):
    d = x2d.shape[1]
    ts = MIXER_TS
    n = batches * seq
    tiles_per_seq = seq // ts
    pairs_per_seq = tiles_per_seq // 2
    assert tiles_per_seq % 2 == 0
    steps = batches * pairs_per_seq
    pair0 = batch0 * pairs_per_seq
    next_step = lambda s: jnp.minimum(s + 1, steps - 1)
    pair = lambda s: (s, 0)
    pairt = lambda s: (0, s)
    bvec = lambda s: (batch0 + s // pairs_per_seq, 0, 0)
    bvec_next = lambda s: (batch0 + next_step(s) // pairs_per_seq, 0, 0)
    c2 = lambda s: (0, 0)
    c3 = lambda s: (0, 0, 0)
    in_specs = [
        pl.BlockSpec((2 * ts, d), lambda s: (pair0 + s, 0)),
        pl.BlockSpec((2 * ts, d), lambda s: (pair0 + next_step(s), 0)),
        pl.BlockSpec((1, d), c2),
        pl.BlockSpec((1, 1, d), bvec),
        pl.BlockSpec((1, 1, d), bvec),
        pl.BlockSpec((1, 1, d), bvec_next),
        pl.BlockSpec((1, 1, d), bvec_next),
        pl.BlockSpec((d, D_IN), c2),
        pl.BlockSpec((1, 1, d), bvec),
        pl.BlockSpec((1, 1, d), bvec),
        pl.BlockSpec((1, 1, d), bvec),
        pl.BlockSpec((CONV_WIDTH, D_RNN), c2),
        pl.BlockSpec((1, D_RNN), c2),
        pl.BlockSpec((N_GATE_BLOCKS, GATE_BLOCK, GATE_BLOCK), c3),
        pl.BlockSpec((1, D_RNN), c2),
        pl.BlockSpec((N_GATE_BLOCKS, GATE_BLOCK, GATE_BLOCK), c3),
        pl.BlockSpec((1, D_RNN), c2),
        pl.BlockSpec((1, D_RNN), c2),
        pl.BlockSpec((1, D_SG), c2),
        pl.BlockSpec((1, D_SG), c2),
        pl.BlockSpec((SG_GROUPS, SG_CHUNK, SG_CHUNK), c3),
        pl.BlockSpec((SG_CHUNK, D_SG), c2),
        pl.BlockSpec((D_RNN, d), c2),
        pl.BlockSpec((D_SG, d), c2),
        pl.BlockSpec((d, d), c2),
        pl.BlockSpec((1, d), c2),
        pl.BlockSpec((2 * N_EXPERTS, d), c2),
        pl.BlockSpec((N_EXPERTS, 1), c2),
        pl.BlockSpec((CONV_WIDTH - 1, ts, ts), c3),
    ]
    out_specs = [
        pl.BlockSpec((2 * ts, d), pair),
        pl.BlockSpec((2 * ts, PACK_W), pair),
        pl.BlockSpec((2 * ts, PACK_W), pair),
        pl.BlockSpec((TOP_K, 2 * ts), pairt),
        pl.BlockSpec((TOP_K, 2 * ts), pairt),
        pl.BlockSpec((TOP_K, 2 * ts), pairt),
        pl.BlockSpec((N_EXPERTS, 1), c2),
    ]
    out_shape = [
        jax.ShapeDtypeStruct((n, d), F32),
        jax.ShapeDtypeStruct((n, PACK_W), jnp.uint32),
        jax.ShapeDtypeStruct((n, PACK_W), jnp.uint32),
        jax.ShapeDtypeStruct((TOP_K, n), I32),
        jax.ShapeDtypeStruct((TOP_K, n), F32),
        jax.ShapeDtypeStruct((TOP_K, n), I32),
        jax.ShapeDtypeStruct((N_EXPERTS, 1), F32),
    ]
    scratch = [
        pltpu.VMEM((ts, D_IN), BF16),
        pltpu.VMEM((ts, D_IN), BF16),
        pltpu.VMEM((ts, d), BF16),
        pltpu.VMEM((ts, d), BF16),
        pltpu.VMEM((2 * SUBLANES, D_RNN), F32),
        pltpu.VMEM((ts, D_RNN), F32),
        pltpu.VMEM((ts, D_RNN), F32),
        pltpu.VMEM((ts, D_RNN), F32),
        pltpu.VMEM((ts, D_SG), F32),
        pltpu.VMEM((SUBLANES, D_RNN), F32),
    ]
    return pl.pallas_call(
        functools.partial(_mixer_kernel, pairs_per_seq=pairs_per_seq),
        grid=(steps,),
        in_specs=in_specs,
        out_specs=out_specs,
        out_shape=out_shape,
        scratch_shapes=scratch,
        compiler_params=pltpu.CompilerParams(
            dimension_semantics=("arbitrary",), vmem_limit_bytes=VMEM_LIMIT),
        name="mixer",
    )(x2d, x2d, n1g, sc1, sh1, sc1, sh1, w_in, gt1, sc2, sh2, conv_w, conv_b, wa_bd, ba, wx_bd, bx,
      lam, ln_g, ln_b, ws, bs_tile, wbr, wbs, wout, n2g, wr_split, br, shifts)


def _sc_mesh():
    return plsc.VectorSubcoreMesh(core_axis_name="core", subcore_axis_name="subcore")


def _sc_scatter_rows(rows, pos, cap):
    n, d = rows.shape
    kk = pos.shape[0]

    @functools.partial(
        pl.kernel, out_type=jax.ShapeDtypeStruct((cap, d), rows.dtype), mesh=_sc_mesh(),
        scratch_types=[], name="sc_scatter_rows")
    def scatter(x_hbm, i_hbm, o_hbm):
        def body(x_vmem, i_vmem):
            pltpu.sync_copy(x_vmem, o_hbm.at[i_vmem.at[0]])

        pltpu.emit_pipeline(
            body,
            grid=(n // SC_WINDOW, kk),
            in_specs=[pl.BlockSpec((SC_WINDOW, d), lambda i, k: (i, 0)),
                      pl.BlockSpec((1, SC_WINDOW), lambda i, k: (k, i))],
            out_specs=[],
            core_axis_name=("core", "subcore"),
            dimension_semantics=(pltpu.PARALLEL, pltpu.ARBITRARY),
        )(x_hbm, i_hbm)

    return scatter(rows, pos)


def _sc_gather_rows(table, idx):
    m = idx.shape[1]
    d = table.shape[1]

    @functools.partial(
        pl.kernel, out_type=jax.ShapeDtypeStruct((m, d), table.dtype), mesh=_sc_mesh(),
        scratch_types=[], name="sc_gather_rows")
    def gather(x_hbm, i_hbm, o_hbm):
        def body(i_vmem, o_vmem):
            pltpu.sync_copy(x_hbm.at[i_vmem.at[0]], o_vmem)

        pltpu.emit_pipeline(
            body,
            grid=(m // SC_WINDOW,),
            in_specs=[pl.BlockSpec((1, SC_WINDOW), lambda i: (0, i))],
            out_specs=[pl.BlockSpec((SC_WINDOW, d), lambda i: (i, 0))],
            core_axis_name=("core", "subcore"),
            dimension_semantics=(pltpu.PARALLEL,),
        )(i_hbm, o_hbm)

    return gather(table, idx)


def _expert_kernel(be_ref, nv_ref, xa_ref, xb_ref, wgu_ref, bgu_ref, wd_ref, bd_ref,
                   ya_ref, yb_ref, wgu_bf, wd_bf):
    i = pl.program_id(0)
    nvalid = nv_ref[i]

    @pl.when((i == 0) | (be_ref[i] != be_ref[jnp.maximum(i - 1, 0)]))
    def _():
        wgu_bf[...] = wgu_ref[0].astype(BF16)
        wd_bf[...] = wd_ref[0].astype(BF16)

    def mlp_rows(h):
        rows = slice(h * MOE_HALF, (h + 1) * MOE_HALF)
        live = lax.broadcasted_iota(I32, (MOE_HALF, 1), 0) < nvalid - h * MOE_HALF
        xb = jnp.where(live, _unpack_halves(xa_ref[rows, :], xb_ref[rows, :]), 0.0).astype(BF16)
        gu = _bdot(xb, wgu_bf[...]) + bgu_ref[0]
        gate = jnp.minimum(gu[:, :D_EXPERT], SWIGLU_LIMIT)
        up = jnp.clip(gu[:, D_EXPERT:], -SWIGLU_LIMIT, SWIGLU_LIMIT)
        act = (up + 1.0) * (gate * _sigmoid(SWIGLU_ALPHA * gate))
        y = _bdot(act.astype(BF16), wd_bf[...]) + bd_ref[0]
        ya_ref[rows, :], yb_ref[rows, :] = _pack_halves(y)

    def zero_rows(h):
        rows = slice(h * MOE_HALF, (h + 1) * MOE_HALF)
        ya_ref[rows, :] = jnp.zeros((MOE_HALF, PACK_W), jnp.uint32)
        yb_ref[rows, :] = jnp.zeros((MOE_HALF, PACK_W), jnp.uint32)

    @pl.when(nvalid > MOE_HALF)
    def _():
        mlp_rows(0)
        mlp_rows(1)

    @pl.when((nvalid > 0) & (nvalid <= MOE_HALF))
    def _():
        mlp_rows(0)
        zero_rows(1)

    @pl.when(nvalid <= 0)
    def _():
        zero_rows(0)
        zero_rows(1)


def _experts(block_e, n_valid, xa, xb, w_gu, b_gu, w_down, b_down):
    cap = xa.shape[0]
    d = D_MODEL
    nb = cap // MOE_BM
    half = pl.BlockSpec((MOE_BM, PACK_W), lambda i, be, nv: (i, 0))
    grid_spec = pltpu.PrefetchScalarGridSpec(
        num_scalar_prefetch=2,
        grid=(nb,),
        in_specs=[
            half,
            half,
            pl.BlockSpec((1, d, 2 * D_EXPERT), lambda i, be, nv: (be[i], 0, 0)),
            pl.BlockSpec((1, 1, 2 * D_EXPERT), lambda i, be, nv: (be[i], 0, 0)),
            pl.BlockSpec((1, D_EXPERT, d), lambda i, be, nv: (be[i], 0, 0)),
            pl.BlockSpec((1, 1, d), lambda i, be, nv: (be[i], 0, 0)),
        ],
        out_specs=[half, half],
        scratch_shapes=[pltpu.VMEM((d, 2 * D_EXPERT), BF16), pltpu.VMEM((D_EXPERT, d), BF16)],
    )
    return pl.pallas_call(
        _expert_kernel,
        grid_spec=grid_spec,
        out_shape=[jax.ShapeDtypeStruct((cap, PACK_W), jnp.uint32)] * 2,
        compiler_params=pltpu.CompilerParams(
            dimension_semantics=("arbitrary",), vmem_limit_bytes=VMEM_LIMIT),
        name="experts",
    )(block_e, n_valid, xa, xb, w_gu, b_gu, w_down, b_down)


def _final_kernel(x2_ref, ya_ref, yb_ref, tw_ref, gt2_ref, fg_ref, *rest):
    o_ref = rest[-1]
    tw = tw_ref[...]
    moe = tw[:, 0:1] * _unpack_halves(ya_ref[0], yb_ref[0])
    for k in range(1, TOP_K):
        moe = moe + tw[:, k:k + 1] * _unpack_halves(ya_ref[k], yb_ref[k])
    x3 = x2_ref[...] + gt2_ref[0] * moe
    ms = jnp.mean(x3 * x3, axis=-1, keepdims=True)
    o_ref[...] = (x3 * lax.rsqrt(ms + EPS)) * fg_ref[...]


def _final(x2, yga, ygb, tw_tok, gt2, final_g, seq, batch0, n_total, out_prev):
    n, d = x2.shape
    tiles_per_seq = seq // FINAL_TM
    tile0 = batch0 * tiles_per_seq
    half = pl.BlockSpec((TOP_K, FINAL_TM, PACK_W), lambda i: (0, i, 0))
    in_specs = [
        pl.BlockSpec((FINAL_TM, d), lambda i: (i, 0)),
        half,
        half,
        pl.BlockSpec((FINAL_TM, TOP_K), lambda i: (i, 0)),
        pl.BlockSpec((1, 1, d), lambda i: (batch0 + i // tiles_per_seq, 0, 0)),
        pl.BlockSpec((1, d), lambda i: (0, 0)),
    ]
    args = [x2, yga, ygb, tw_tok, gt2, final_g]
    aliases = {}
    if out_prev is not None:
        in_specs.append(pl.BlockSpec(memory_space=pl.ANY))
        aliases = {len(args): 0}
        args.append(out_prev)
    return pl.pallas_call(
        _final_kernel,
        grid=(n // FINAL_TM,),
        in_specs=in_specs,
        out_specs=pl.BlockSpec((FINAL_TM, d), lambda i: (tile0 + i, 0)),
        out_shape=jax.ShapeDtypeStruct((n_total, d), F32),
        input_output_aliases=aliases,
        compiler_params=pltpu.CompilerParams(
            dimension_semantics=("arbitrary",), vmem_limit_bytes=VMEM_LIMIT),
        name="final",
    )(*args)


def _block_diag(w):
    per = GATE_BLOCK // RNN_HEAD_DIM
    w4 = w.reshape(N_GATE_BLOCKS, per, RNN_HEAD_DIM, RNN_HEAD_DIM)
    eye = jnp.eye(per, dtype=w.dtype)
    bd = jnp.einsum("gpij,pq->gpiqj", w4, eye)
    return bd.reshape(N_GATE_BLOCKS, GATE_BLOCK, GATE_BLOCK)


def _layer(x2d, c, batch, seq, ada_w, ada_b, norm1_g, w_in, conv_w, conv_b, lru_wa, lru_ba,
           lru_wx, lru_bx, lru_lam, sg_ln_g, sg_ln_b, sg_ws, sg_bs, w_br_rnn, w_br_sg, w_out,
           norm2_g, w_router, b_router, w_gu, b_gu, w_down, b_down, final_g):
    n_total, d = x2d.shape
    mod = _ada(c, ada_w, ada_b)
    sh1, sc1, gt1, sh2, sc2, gt2 = [
        mod[:, i * d:(i + 1) * d].reshape(batch, 1, d) for i in range(N_MOD)]
    row = lambda v: v.reshape(1, -1)

    bs_tile = jnp.repeat(sg_bs.T, SG_GROUP_DIM, axis=1)
    wr_t = w_router.T
    wr_hi = wr_t.astype(BF16)
    wr_lo = (wr_t - wr_hi.astype(F32)).astype(BF16)
    t_out = lax.broadcasted_iota(I32, (MIXER_TS, MIXER_TS), 0)
    t_in = lax.broadcasted_iota(I32, (MIXER_TS, MIXER_TS), 1)
    shifts = jnp.stack([(t_out - t_in == s) for s in range(1, CONV_WIDTH)]).astype(BF16)
    mixer_weights = (
        conv_w, row(conv_b), _block_diag(lru_wa).astype(BF16), row(lru_ba),
        _block_diag(lru_wx).astype(BF16), row(lru_bx), row(lru_lam), row(sg_ln_g), row(sg_ln_b),
        sg_ws, bs_tile, w_br_rnn.astype(BF16), w_br_sg.astype(BF16), w_out.astype(BF16),
        row(norm2_g), jnp.concatenate([wr_hi, wr_lo], axis=0), b_router.reshape(N_EXPERTS, 1), shifts)
    w_in_bf = w_in.astype(BF16)

    groups = TOKEN_GROUPS if batch % TOKEN_GROUPS == 0 else 1
    batches = batch // groups
    n = batches * seq
    out = None
    for grp in range(groups):
        batch0 = grp * batches
        x2, h2a, h2b, top_idx, top_w, rank, counts = _mixer(
            x2d, row(norm1_g), sc1, sh1, w_in_bf, gt1, sc2, sh2, *mixer_weights, batch0, batches, seq)

        cap = n * TOP_K + N_EXPERTS * MOE_BM
        nb = cap // MOE_BM
        cnt = counts[:, 0].astype(I32)
        padded = ((cnt + MOE_BM - 1) // MOE_BM) * MOE_BM
        padded_end = jnp.cumsum(padded)
        padded_start = padded_end - padded
        e_ids = jnp.arange(N_EXPERTS, dtype=I32)
        start_of = jnp.sum(jnp.where(top_idx[None] == e_ids[:, None, None],
                                     padded_start[:, None, None], 0), axis=0)
        pos = start_of + rank
        blk_row0 = jnp.arange(nb, dtype=I32) * MOE_BM
        block_e = jnp.minimum(
            jnp.sum((padded_end[None, :] <= blk_row0[:, None]).astype(I32), axis=1), N_EXPERTS - 1)
        n_valid = jnp.clip(cnt[block_e] - (blk_row0 - padded_start[block_e]), 0, MOE_BM).astype(I32)

        xa = _sc_scatter_rows(h2a, pos, cap)
        xb = _sc_scatter_rows(h2b, pos, cap)
        ya, yb = _experts(block_e, n_valid, xa, xb, w_gu, b_gu.reshape(N_EXPERTS, 1, -1),
                          w_down, b_down.reshape(N_EXPERTS, 1, -1))
        flat_pos = pos.reshape(1, -1)
        yga = _sc_gather_rows(ya, flat_pos).reshape(TOP_K, n, PACK_W)
        ygb = _sc_gather_rows(yb, flat_pos).reshape(TOP_K, n, PACK_W)
        out = _final(x2, yga, ygb, top_w.T, gt2, final_g.reshape(1, d), seq, batch0, n_total, out)
    return out


def kernel(x, c, ada_w, ada_b, norm1_g, w_in, conv_w, conv_b, lru_wa, lru_ba, lru_wx, lru_bx,
           lru_lam, sg_ln_g, sg_ln_b, sg_ws, sg_bs, w_br_rnn, w_br_sg, w_out, norm2_g,
           w_router, b_router, w_gu, b_gu, w_down, b_down, final_g):
    batch, seq, d = x.shape
    depth = ada_w.shape[0]
    assert depth == 1, "the combine is fused with the final norm, which follows the only layer"
    x2d = x.reshape(batch * seq, d)
    l = 0
    out = _layer(
        x2d, c, batch, seq, ada_w[l], ada_b[l], norm1_g[l], w_in[l], conv_w[l], conv_b[l],
        lru_wa[l], lru_ba[l], lru_wx[l], lru_bx[l], lru_lam[l], sg_ln_g[l], sg_ln_b[l],
        sg_ws[l], sg_bs[l], w_br_rnn[l], w_br_sg[l], w_out[l], norm2_g[l], w_router[l],
        b_router[l], w_gu[l], b_gu[l], w_down[l], b_down[l], final_g)
    return out.reshape(batch, seq, d)
```

```python
import functools

import jax
import jax.numpy as jnp
from jax import lax
from jax.experimental import pallas as pl
from jax.experimental.pallas import tpu as pltpu
from jax.experimental.pallas import tpu_sc as plsc

F32 = jnp.float32
BF16 = jnp.bfloat16
I32 = jnp.int32

D_MODEL = 1024
D_RNN = 1024
RNN_HEADS = 16
RNN_HEAD_DIM = D_RNN // RNN_HEADS
CONV_WIDTH = 4
LRU_C = 8.0
D_SG = 1024
SG_GROUPS = 8
SG_GROUP_DIM = D_SG // SG_GROUPS
SG_CHUNK = 128
N_EXPERTS = 32
TOP_K = 4
D_EXPERT = 1024
SWIGLU_LIMIT = 7.0
SWIGLU_ALPHA = 1.702
EPS = 1e-6
N_MOD = 6
D_IN = 2 * D_RNN + 2 * D_SG + 2 * D_MODEL

SUBLANES = 8
GATE_BLOCK = 256
N_GATE_BLOCKS = D_RNN // GATE_BLOCK

ADA_TN = 1536
INPROJ_TN = 1024
MIXER_TS = 256
MOE_BM = 512
MOE_HALF = MOE_BM // 2
TOKEN_GROUPS = 2
SC_WINDOW = 128
PACK_W = D_MODEL // 4
FINAL_TM = 256
CAST_ROWS = 512
VMEM_LIMIT = 56 * 1024 * 1024


def _sigmoid(x):
    return 0.5 * jnp.tanh(0.5 * x) + 0.5


def _gelu_tanh(x):
    k = 0.7978845608028654
    hx = 0.5 * x
    return hx + hx * jnp.tanh(x * (k + (k * 0.044715) * (x * x)))


def _bdot(a, b):
    return jnp.dot(a, b, preferred_element_type=F32)


def _pack_halves(v):
    word = pltpu.pack_elementwise([v[:, 2 * PACK_W:], v[:, :2 * PACK_W]], packed_dtype=BF16)
    return word[:, :PACK_W], word[:, PACK_W:]


def _unpack_halves(wa, wb):
    part = lambda w, i: pltpu.unpack_elementwise(w, index=i, packed_dtype=BF16, unpacked_dtype=F32)
    return jnp.concatenate([part(wa, 1), part(wb, 1), part(wa, 0), part(wb, 0)], axis=1)


def _ada_kernel(c_ref, w_ref, b_ref, o_ref):
    c = c_ref[...]
    s = c * _sigmoid(c)
    o_ref[...] = jnp.dot(s, w_ref[...], preferred_element_type=F32,
                         precision=lax.Precision.HIGHEST) + b_ref[...]


def _ada(c, ada_w, ada_b):
    b, d = c.shape
    n = ada_w.shape[1]
    return pl.pallas_call(
        _ada_kernel,
        grid=(n // ADA_TN,),
        in_specs=[
            pl.BlockSpec((b, d), lambda j: (0, 0)),
            pl.BlockSpec((d, ADA_TN), lambda j: (0, j)),
            pl.BlockSpec((1, ADA_TN), lambda j: (0, j)),
        ],
        out_specs=pl.BlockSpec((b, ADA_TN), lambda j: (0, j)),
        out_shape=jax.ShapeDtypeStruct((b, n), F32),
        compiler_params=pltpu.CompilerParams(
            dimension_semantics=("arbitrary",), vmem_limit_bytes=VMEM_LIMIT),
        name="ada",
    )(c, ada_w, ada_b.reshape(1, n))


def _norm_mod(x, g, sc, sh):
    ms = jnp.mean(x * x, axis=-1, keepdims=True)
    y = x * lax.rsqrt(ms + EPS)
    return (y * g) * (1.0 + sc) + sh


def _mixer_kernel(x_ref, xn_ref, n1g_ref, sc1_ref, sh1_ref, sc1n_ref, sh1n_ref, win_ref,
                  gt1_ref, sc2_ref, sh2_ref,
                  convw_ref, convb_ref, wa_ref, ba_ref, wx_ref, bx_ref, lam_ref,
                  lng_ref, lnb_ref, ws_ref, bs_ref, wbr_ref, wbs_ref, wout_ref,
                  n2g_ref, wr_ref, br_ref, shift_ref,
                  x2_ref, h2a_ref, h2b_ref, idx_ref, tw_ref, rank_ref, cnt_ref,
                  z0_ref, z1_ref, hn0_ref, hn1_ref, xp_ref, xc_ref, a_ref, hh_ref, sv_ref, hstate_ref,
                  *, pairs_per_seq):
    ts = MIXER_TS
    s = pl.program_id(0)

    def inproj_norm(hn_ref, x_rows, sc, sh):
        hn_ref[...] = _norm_mod(x_rows, n1g_ref[...], sc, sh).astype(BF16)

    def inproj_chunks(z_dst, hn_ref, first, last):
        for c in range(first * INPROJ_TN, last * INPROJ_TN, INPROJ_TN):
            z_dst[:, c:c + INPROJ_TN] = _bdot(hn_ref[...], win_ref[:, c:c + INPROJ_TN])

    @pl.when(s == 0)
    def _():
        cnt_ref[...] = jnp.zeros_like(cnt_ref)
        inproj_norm(hn0_ref, x_ref[0:ts, :], sc1_ref[0], sh1_ref[0])
        inproj_chunks(z0_ref, hn0_ref, 0, D_IN // INPROJ_TN)
        inproj_norm(hn1_ref, x_ref[ts:2 * ts, :], sc1_ref[0], sh1_ref[0])

    @pl.when(s % pairs_per_seq == 0)
    def _():
        xp_ref[0:SUBLANES, :] = jnp.zeros((SUBLANES, D_RNN), F32)
        hstate_ref[...] = jnp.zeros_like(hstate_ref)

    def tile(k, z_ref, z_next_ref, hn_ref, hn_after_ref):
        rows = slice(k * ts, (k + 1) * ts)

        x16 = z_ref[:, 0:D_RNN].astype(BF16)
        rnn_x = x16.astype(F32)
        cw = convw_ref[...]
        xc = cw[3:4] * rnn_x + convb_ref[...]
        for sft in range(1, CONV_WIDTH):
            xc = xc + cw[3 - sft:4 - sft] * _bdot(shift_ref[sft - 1], x16)
        xc_ref[...] = xc
        xp_ref[SUBLANES:2 * SUBLANES, :] = rnn_x[0:SUBLANES, :]
        xc_ref[0:SUBLANES, :] = (
            cw[3:4] * xp_ref[SUBLANES:2 * SUBLANES, :]
            + cw[2:3] * xp_ref[SUBLANES - 1:2 * SUBLANES - 1, :]
            + cw[1:2] * xp_ref[SUBLANES - 2:2 * SUBLANES - 2, :]
            + cw[0:1] * xp_ref[SUBLANES - 3:2 * SUBLANES - 3, :]) + convb_ref[...]
        xp_ref[0:SUBLANES, :] = rnn_x[ts - SUBLANES:ts, :]
        xc = xc_ref[...]

        xcb = xc.astype(BF16)
        r_parts, i_parts = [], []
        for g in range(N_GATE_BLOCKS):
            blk = xcb[:, g * GATE_BLOCK:(g + 1) * GATE_BLOCK]
            r_parts.append(_bdot(blk, wa_ref[g]))
            i_parts.append(_bdot(blk, wx_ref[g]))
        inproj_chunks(z_next_ref, hn_ref, 0, 3)
        r = _sigmoid(jnp.concatenate(r_parts, axis=1) + ba_ref[...])
        ig = _sigmoid(jnp.concatenate(i_parts, axis=1) + bx_ref[...])

        nl = -lam_ref[...]
        softplus = jnp.maximum(nl, 0.0) + jnp.log(1.0 + jnp.exp(-jnp.abs(nl)))
        a = jnp.exp(r * ((-LRU_C) * softplus))
        t = 1.0 - a * a
        u = jnp.where(t > 0.0, t * lax.rsqrt(t), 0.0) * (ig * xc)

        groups = ts // SUBLANES
        a3 = a.reshape(groups, SUBLANES, D_RNN)
        h3 = u.reshape(groups, SUBLANES, D_RNN)
        sub = lax.broadcasted_iota(I32, (groups, SUBLANES, D_RNN), 1)
        for step in (1, 2, 4):
            keep = sub >= step
            a_sh = jnp.where(keep, pltpu.roll(a3, step, 1), 1.0)
            h_sh = jnp.where(keep, pltpu.roll(h3, step, 1), 0.0)
            h3 = h3 + a3 * h_sh
            a3 = a3 * a_sh
        a_ref[...] = a3.reshape(ts, D_RNN)
        hh_ref[...] = h3.reshape(ts, D_RNN)

        def carry_body(gi, hc):
            grp = pl.ds(pl.multiple_of(gi * SUBLANES, SUBLANES), SUBLANES)
            hg = hh_ref[grp, :] + a_ref[grp, :] * hc
            hh_ref[grp, :] = hg
            return jnp.broadcast_to(hg[SUBLANES - 1:SUBLANES, :], (SUBLANES, D_RNN))

        hstate_ref[...] = lax.fori_loop(0, groups, carry_body, hstate_ref[...])

        inproj_chunks(z_next_ref, hn_ref, 3, 4)
        y_rnn = (hh_ref[...] * _gelu_tanh(z_ref[:, D_RNN:2 * D_RNN].astype(F32))).astype(BF16)

        gv = _gelu_tanh(z_ref[:, 2 * D_RNN + D_SG:2 * D_RNN + 2 * D_SG].astype(F32))
        mu = jnp.mean(gv, axis=-1, keepdims=True)
        dv = gv - mu
        var = jnp.mean(dv * dv, axis=-1, keepdims=True)
        vn = (dv * lax.rsqrt(var + EPS) * lng_ref[...] + lnb_ref[...]).astype(BF16)
        tr = lax.broadcasted_iota(I32, (SG_CHUNK, SG_CHUNK), 0)
        tc = lax.broadcasted_iota(I32, (SG_CHUNK, SG_CHUNK), 1)
        causal = tc <= tr
        for g in range(SG_GROUPS):
            wg = jnp.where(causal, ws_ref[g], 0.0).astype(BF16)
            cols = slice(g * SG_GROUP_DIM, (g + 1) * SG_GROUP_DIM)
            for n in range(ts // SG_CHUNK):
                chunk = slice(n * SG_CHUNK, (n + 1) * SG_CHUNK)
                sv_ref[chunk, cols] = _bdot(wg, vn[chunk, cols]) + bs_ref[:, cols]
        inproj_chunks(z_next_ref, hn_ref, 4, 5)
        gu = _gelu_tanh(z_ref[:, 2 * D_RNN:2 * D_RNN + D_SG].astype(F32))
        y_sg = (gu * sv_ref[...]).astype(BF16)

        g_rnn = z_ref[:, 2 * D_RNN + 2 * D_SG:2 * D_RNN + 2 * D_SG + D_MODEL].astype(F32)
        g_sg = z_ref[:, 2 * D_RNN + 2 * D_SG + D_MODEL:D_IN].astype(F32)
        m = (_sigmoid(g_rnn) * _bdot(y_rnn, wbr_ref[...])
             + _sigmoid(g_sg) * _bdot(y_sg, wbs_ref[...])).astype(BF16)
        x2 = x_ref[rows, :] + gt1_ref[0] * _bdot(m, wout_ref[...])
        x2_ref[rows, :] = x2
        inproj_chunks(z_next_ref, hn_ref, 5, 6)

        h2 = _norm_mod(x2, n2g_ref[...], sc2_ref[0], sh2_ref[0])
        h2a_ref[rows, :], h2b_ref[rows, :] = _pack_halves(h2)
        h_hi = h2.astype(BF16)
        h_lo = (h2 - h_hi.astype(F32)).astype(BF16)
        nt_dims = (((1,), (1,)), ((), ()))
        by_hi = lax.dot_general(wr_ref[...], h_hi, nt_dims, preferred_element_type=F32)
        logits = (by_hi[:N_EXPERTS] + by_hi[N_EXPERTS:]
                  + lax.dot_general(wr_ref[0:N_EXPERTS, :], h_lo, nt_dims,
                                    preferred_element_type=F32)
                  + br_ref[...])
        e_iota = lax.broadcasted_iota(I32, (N_EXPERTS, ts), 0)
        v = logits
        vals, idxs, sels = [], [], []
        for _ in range(TOP_K):
            mx = jnp.max(v, axis=0, keepdims=True)
            ik = jnp.min(jnp.where(v == mx, e_iota, N_EXPERTS), axis=0, keepdims=True)
            sel = e_iota == ik
            v = jnp.where(sel, -jnp.inf, v)
            vals.append(mx)
            idxs.append(ik)
            sels.append(sel)
        exps = [jnp.exp(val - vals[0]) for val in vals]
        denom = exps[0] + exps[1] + exps[2] + exps[3]
        idx_ref[:, rows] = jnp.concatenate(idxs, axis=0)
        tw_ref[:, rows] = jnp.concatenate([e / denom for e in exps], axis=0)

        onehot = jnp.zeros((N_EXPERTS, ts), F32)
        for sel in sels:
            onehot = jnp.where(sel, 1.0, onehot)
        sr = lax.broadcasted_iota(I32, (ts, ts), 0)
        st = lax.broadcasted_iota(I32, (ts, ts), 1)
        before = jnp.where(sr < st, 1.0, 0.0).astype(BF16)
        total = cnt_ref[...] + _bdot(onehot.astype(BF16), before)
        ranks = [jnp.sum(jnp.where(sel, total, 0.0), axis=0, keepdims=True) for sel in sels]
        rank_ref[:, rows] = jnp.concatenate(ranks, axis=0).astype(I32)
        cnt_ref[...] = cnt_ref[...] + jnp.sum(onehot, axis=1, keepdims=True)

        inproj_norm(hn_after_ref, xn_ref[rows, :], sc1n_ref[0], sh1n_ref[0])

    tile(0, z0_ref, z1_ref, hn1_ref, hn0_ref)
    tile(1, z1_ref, z0_ref, hn0_ref, hn1_ref)


def _mixer(x2d, n1g, sc1, sh1, w_in, gt1, sc2, sh2, conv_w, conv_b, wa_bd, ba, wx_bd, bx, lam,
           ln_g, ln_b, ws, bs_tile, wbr, wbs, wout, n2g, wr_split, br, shifts, batch0, batches, seq):
    d = x2d.shape[1]
    ts = MIXER_TS
    n = batches * seq
    tiles_per_seq = seq // ts
    pairs_per_seq = tiles_per_seq // 2
    assert tiles_per_seq % 2 == 0
    steps = batches * pairs_per_seq
    pair0 = batch0 * pairs_per_seq
    next_step = lambda s: jnp.minimum(s + 1, steps - 1)
    pair = lambda s: (s, 0)
    pairt = lambda s: (0, s)
    bvec = lambda s: (batch0 + s // pairs_per_seq, 0, 0)
    bvec_next = lambda s: (batch0 + next_step(s) // pairs_per_seq, 0, 0)
    c2 = lambda s: (0, 0)
    c3 = lambda s: (0, 0, 0)
    in_specs = [
        pl.BlockSpec((2 * ts, d), lambda s: (pair0 + s, 0)),
        pl.BlockSpec((2 * ts, d), lambda s: (pair0 + next_step(s), 0)),
        pl.BlockSpec((1, d), c2),
        pl.BlockSpec((1, 1, d), bvec),
        pl.BlockSpec((1, 1, d), bvec),
        pl.BlockSpec((1, 1, d), bvec_next),
        pl.BlockSpec((1, 1, d), bvec_next),
        pl.BlockSpec((d, D_IN), c2),
        pl.BlockSpec((1, 1, d), bvec),
        pl.BlockSpec((1, 1, d), bvec),
        pl.BlockSpec((1, 1, d), bvec),
        pl.BlockSpec((CONV_WIDTH, D_RNN), c2),
        pl.BlockSpec((1, D_RNN), c2),
        pl.BlockSpec((N_GATE_BLOCKS, GATE_BLOCK, GATE_BLOCK), c3),
        pl.BlockSpec((1, D_RNN), c2),
        pl.BlockSpec((N_GATE_BLOCKS, GATE_BLOCK, GATE_BLOCK), c3),
        pl.BlockSpec((1, D_RNN), c2),
        pl.BlockSpec((1, D_RNN), c2),
        pl.BlockSpec((1, D_SG), c2),
        pl.BlockSpec((1, D_SG), c2),
        pl.BlockSpec((SG_GROUPS, SG_CHUNK, SG_CHUNK), c3),
        pl.BlockSpec((SG_CHUNK, D_SG), c2),
        pl.BlockSpec((D_RNN, d), c2),
        pl.BlockSpec((D_SG, d), c2),
        pl.BlockSpec((d, d), c2),
        pl.BlockSpec((1, d), c2),
        pl.BlockSpec((2 * N_EXPERTS, d), c2),
        pl.BlockSpec((N_EXPERTS, 1), c2),
        pl.BlockSpec((CONV_WIDTH - 1, ts, ts), c3),
    ]
    out_specs = [
        pl.BlockSpec((2 * ts, d), pair),
        pl.BlockSpec((2 * ts, PACK_W), pair),
        pl.BlockSpec((2 * ts, PACK_W), pair),
        pl.BlockSpec((TOP_K, 2 * ts), pairt),
        pl.BlockSpec((TOP_K, 2 * ts), pairt),
        pl.BlockSpec((TOP_K, 2 * ts), pairt),
        pl.BlockSpec((N_EXPERTS, 1), c2),
    ]
    out_shape = [
        jax.ShapeDtypeStruct((n, d), F32),
        jax.ShapeDtypeStruct((n, PACK_W), jnp.uint32),
        jax.ShapeDtypeStruct((n, PACK_W), jnp.uint32),
        jax.ShapeDtypeStruct((TOP_K, n), I32),
        jax.ShapeDtypeStruct((TOP_K, n), F32),
        jax.ShapeDtypeStruct((TOP_K, n), I32),
        jax.ShapeDtypeStruct((N_EXPERTS, 1), F32),
    ]
    scratch = [
        pltpu.VMEM((ts, D_IN), F32),
        pltpu.VMEM((ts, D_IN), F32),
        pltpu.VMEM((ts, d), BF16),
        pltpu.VMEM((ts, d), BF16),
        pltpu.VMEM((2 * SUBLANES, D_RNN), F32),
        pltpu.VMEM((ts, D_RNN), F32),
        pltpu.VMEM((ts, D_RNN), F32),
        pltpu.VMEM((ts, D_RNN), F32),
        pltpu.VMEM((ts, D_SG), F32),
        pltpu.VMEM((SUBLANES, D_RNN), F32),
    ]
    return pl.pallas_call(
        functools.partial(_mixer_kernel, pairs_per_seq=pairs_per_seq),
        grid=(steps,),
        in_specs=in_specs,
        out_specs=out_specs,
        out_shape=out_shape,
        scratch_shapes=scratch,
        compiler_params=pltpu.CompilerParams(
            dimension_semantics=("arbitrary",), vmem_limit_bytes=VMEM_LIMIT),
        name="mixer",
    )(x2d, x2d, n1g, sc1, sh1, sc1, sh1, w_in, gt1, sc2, sh2, conv_w, conv_b, wa_bd, ba, wx_bd, bx,
      lam, ln_g, ln_b, ws, bs_tile, wbr, wbs, wout, n2g, wr_split, br, shifts)


def _sc_mesh():
    return plsc.VectorSubcoreMesh(core_axis_name="core", subcore_axis_name="subcore")


def _sc_scatter_rows(rows, pos, cap):
    n, d = rows.shape
    kk = pos.shape[0]

    @functools.partial(
        pl.kernel, out_type=jax.ShapeDtypeStruct((cap, d), rows.dtype), mesh=_sc_mesh(),
        scratch_types=[], name="sc_scatter_rows")
    def scatter(x_hbm, i_hbm, o_hbm):
        def body(x_vmem, i_vmem):
            pltpu.sync_copy(x_vmem, o_hbm.at[i_vmem.at[0]])

        pltpu.emit_pipeline(
            body,
            grid=(n // SC_WINDOW, kk),
            in_specs=[pl.BlockSpec((SC_WINDOW, d), lambda i, k: (i, 0)),
                      pl.BlockSpec((1, SC_WINDOW), lambda i, k: (k, i))],
            out_specs=[],
            core_axis_name=("core", "subcore"),
            dimension_semantics=(pltpu.PARALLEL, pltpu.ARBITRARY),
        )(x_hbm, i_hbm)

    return scatter(rows, pos)


def _sc_gather_rows(table, idx):
    m = idx.shape[1]
    d = table.shape[1]

    @functools.partial(
        pl.kernel, out_type=jax.ShapeDtypeStruct((m, d), table.dtype), mesh=_sc_mesh(),
        scratch_types=[], name="sc_gather_rows")
    def gather(x_hbm, i_hbm, o_hbm):
        def body(i_vmem, o_vmem):
            pltpu.sync_copy(x_hbm.at[i_vmem.at[0]], o_vmem)

        pltpu.emit_pipeline(
            body,
            grid=(m // SC_WINDOW,),
            in_specs=[pl.BlockSpec((1, SC_WINDOW), lambda i: (0, i))],
            out_specs=[pl.BlockSpec((SC_WINDOW, d), lambda i: (i, 0))],
            core_axis_name=("core", "subcore"),
            dimension_semantics=(pltpu.PARALLEL,),
        )(i_hbm, o_hbm)

    return gather(table, idx)


def _cast_kernel(w_ref, o_ref):
    o_ref[...] = w_ref[...].astype(BF16)


def _to_bf16(w):
    e, k, n = w.shape
    kb = CAST_ROWS
    return pl.pallas_call(
        _cast_kernel,
        grid=(e, k // kb),
        in_specs=[pl.BlockSpec((1, kb, n), lambda i, j: (i, j, 0))],
        out_specs=pl.BlockSpec((1, kb, n), lambda i, j: (i, j, 0)),
        out_shape=jax.ShapeDtypeStruct(w.shape, BF16),
        compiler_params=pltpu.CompilerParams(
            dimension_semantics=("arbitrary", "arbitrary"), vmem_limit_bytes=VMEM_LIMIT),
        name="cast_bf16",
    )(w)


def _expert_kernel(be_ref, nv_ref, xa_ref, xb_ref, wgu_ref, bgu_ref, wd_ref, bd_ref,
                   ya_ref, yb_ref):
    nvalid = nv_ref[pl.program_id(0)]

    def mlp_rows(h):
        rows = slice(h * MOE_HALF, (h + 1) * MOE_HALF)
        live = lax.broadcasted_iota(I32, (MOE_HALF, 1), 0) < nvalid - h * MOE_HALF
        xb = jnp.where(live, _unpack_halves(xa_ref[rows, :], xb_ref[rows, :]), 0.0).astype(BF16)
        gu = _bdot(xb, wgu_ref[0]) + bgu_ref[0]
        gate = jnp.minimum(gu[:, :D_EXPERT], SWIGLU_LIMIT)
        up = jnp.clip(gu[:, D_EXPERT:], -SWIGLU_LIMIT, SWIGLU_LIMIT)
        act = (up + 1.0) * (gate * _sigmoid(SWIGLU_ALPHA * gate))
        y = _bdot(act.astype(BF16), wd_ref[0]) + bd_ref[0]
        ya_ref[rows, :], yb_ref[rows, :] = _pack_halves(y)

    def zero_rows(h):
        rows = slice(h * MOE_HALF, (h + 1) * MOE_HALF)
        ya_ref[rows, :] = jnp.zeros((MOE_HALF, PACK_W), jnp.uint32)
        yb_ref[rows, :] = jnp.zeros((MOE_HALF, PACK_W), jnp.uint32)

    @pl.when(nvalid > MOE_HALF)
    def _():
        mlp_rows(0)
        mlp_rows(1)

    @pl.when((nvalid > 0) & (nvalid <= MOE_HALF))
    def _():
        mlp_rows(0)
        zero_rows(1)

    @pl.when(nvalid <= 0)
    def _():
        zero_rows(0)
        zero_rows(1)


def _experts(block_e, n_valid, xa, xb, w_gu, b_gu, w_down, b_down):
    cap = xa.shape[0]
    d = D_MODEL
    nb = cap // MOE_BM
    half = pl.BlockSpec((MOE_BM, PACK_W), lambda i, be, nv: (i, 0))
    grid_spec = pltpu.PrefetchScalarGridSpec(
        num_scalar_prefetch=2,
        grid=(nb,),
        in_specs=[
            half,
            half,
            pl.BlockSpec((1, d, 2 * D_EXPERT), lambda i, be, nv: (be[i], 0, 0)),
            pl.BlockSpec((1, 1, 2 * D_EXPERT), lambda i, be, nv: (be[i], 0, 0)),
            pl.BlockSpec((1, D_EXPERT, d), lambda i, be, nv: (be[i], 0, 0)),
            pl.BlockSpec((1, 1, d), lambda i, be, nv: (be[i], 0, 0)),
        ],
        out_specs=[half, half],
    )
    return pl.pallas_call(
        _expert_kernel,
        grid_spec=grid_spec,
        out_shape=[jax.ShapeDtypeStruct((cap, PACK_W), jnp.uint32)] * 2,
        compiler_params=pltpu.CompilerParams(
            dimension_semantics=("arbitrary",), vmem_limit_bytes=VMEM_LIMIT),
        name="experts",
    )(block_e, n_valid, xa, xb, w_gu, b_gu, w_down, b_down)


def _final_kernel(x2_ref, ya_ref, yb_ref, tw_ref, gt2_ref, fg_ref, *rest):
    o_ref = rest[-1]
    tw = tw_ref[...]
    moe = tw[:, 0:1] * _unpack_halves(ya_ref[0], yb_ref[0])
    for k in range(1, TOP_K):
        moe = moe + tw[:, k:k + 1] * _unpack_halves(ya_ref[k], yb_ref[k])
    x3 = x2_ref[...] + gt2_ref[0] * moe
    ms = jnp.mean(x3 * x3, axis=-1, keepdims=True)
    o_ref[...] = (x3 * lax.rsqrt(ms + EPS)) * fg_ref[...]


def _final(x2, yga, ygb, tw_tok, gt2, final_g, seq, batch0, n_total, out_prev):
    n, d = x2.shape
    tiles_per_seq = seq // FINAL_TM
    tile0 = batch0 * tiles_per_seq
    half = pl.BlockSpec((TOP_K, FINAL_TM, PACK_W), lambda i: (0, i, 0))
    in_specs = [
        pl.BlockSpec((FINAL_TM, d), lambda i: (i, 0)),
        half,
        half,
        pl.BlockSpec((FINAL_TM, TOP_K), lambda i: (i, 0)),
        pl.BlockSpec((1, 1, d), lambda i: (batch0 + i // tiles_per_seq, 0, 0)),
        pl.BlockSpec((1, d), lambda i: (0, 0)),
    ]
    args = [x2, yga, ygb, tw_tok, gt2, final_g]
    aliases = {}
    if out_prev is not None:
        in_specs.append(pl.BlockSpec(memory_space=pl.ANY))
        aliases = {len(args): 0}
        args.append(out_prev)
    return pl.pallas_call(
        _final_kernel,
        grid=(n // FINAL_TM,),
        in_specs=in_specs,
        out_specs=pl.BlockSpec((FINAL_TM, d), lambda i: (tile0 + i, 0)),
        out_shape=jax.ShapeDtypeStruct((n_total, d), F32),
        input_output_aliases=aliases,
        compiler_params=pltpu.CompilerParams(
            dimension_semantics=("arbitrary",), vmem_limit_bytes=VMEM_LIMIT),
        name="final",
    )(*args)


def _block_diag(w):
    per = GATE_BLOCK // RNN_HEAD_DIM
    w4 = w.reshape(N_GATE_BLOCKS, per, RNN_HEAD_DIM, RNN_HEAD_DIM)
    eye = jnp.eye(per, dtype=w.dtype)
    bd = jnp.einsum("gpij,pq->gpiqj", w4, eye)
    return bd.reshape(N_GATE_BLOCKS, GATE_BLOCK, GATE_BLOCK)


def _layer(x2d, c, batch, seq, ada_w, ada_b, norm1_g, w_in, conv_w, conv_b, lru_wa, lru_ba,
           lru_wx, lru_bx, lru_lam, sg_ln_g, sg_ln_b, sg_ws, sg_bs, w_br_rnn, w_br_sg, w_out,
           norm2_g, w_router, b_router, w_gu, b_gu, w_down, b_down, final_g):
    n_total, d = x2d.shape
    mod = _ada(c, ada_w, ada_b)
    sh1, sc1, gt1, sh2, sc2, gt2 = [
        mod[:, i * d:(i + 1) * d].reshape(batch, 1, d) for i in range(N_MOD)]
    row = lambda v: v.reshape(1, -1)

    bs_tile = jnp.repeat(sg_bs.T, SG_GROUP_DIM, axis=1)
    wr_t = w_router.T
    wr_hi = wr_t.astype(BF16)
    wr_lo = (wr_t - wr_hi.astype(F32)).astype(BF16)
    t_out = lax.broadcasted_iota(I32, (MIXER_TS, MIXER_TS), 0)
    t_in = lax.broadcasted_iota(I32, (MIXER_TS, MIXER_TS), 1)
    shifts = jnp.stack([(t_out - t_in == s) for s in range(1, CONV_WIDTH)]).astype(BF16)
    mixer_weights = (
        conv_w, row(conv_b), _block_diag(lru_wa).astype(BF16), row(lru_ba),
        _block_diag(lru_wx).astype(BF16), row(lru_bx), row(lru_lam), row(sg_ln_g), row(sg_ln_b),
        sg_ws, bs_tile, w_br_rnn.astype(BF16), w_br_sg.astype(BF16), w_out.astype(BF16),
        row(norm2_g), jnp.concatenate([wr_hi, wr_lo], axis=0), b_router.reshape(N_EXPERTS, 1), shifts)
    w_in_bf = w_in.astype(BF16)
    w_gu_bf = _to_bf16(w_gu)
    w_down_bf = _to_bf16(w_down)

    groups = TOKEN_GROUPS if batch % TOKEN_GROUPS == 0 else 1
    batches = batch // groups
    n = batches * seq
    out = None
    for grp in range(groups):
        batch0 = grp * batches
        x2, h2a, h2b, top_idx, top_w, rank, counts = _mixer(
            x2d, row(norm1_g), sc1, sh1, w_in_bf, gt1, sc2, sh2, *mixer_weights, batch0, batches, seq)

        cap = n * TOP_K + N_EXPERTS * MOE_BM
        nb = cap // MOE_BM
        cnt = counts[:, 0].astype(I32)
        padded = ((cnt + MOE_BM - 1) // MOE_BM) * MOE_BM
        padded_end = jnp.cumsum(padded)
        padded_start = padded_end - padded
        e_ids = jnp.arange(N_EXPERTS, dtype=I32)
        start_of = jnp.sum(jnp.where(top_idx[None] == e_ids[:, None, None],
                                     padded_start[:, None, None], 0), axis=0)
        pos = start_of + rank
        blk_row0 = jnp.arange(nb, dtype=I32) * MOE_BM
        block_e = jnp.minimum(
            jnp.sum((padded_end[None, :] <= blk_row0[:, None]).astype(I32), axis=1), N_EXPERTS - 1)
        n_valid = jnp.clip(cnt[block_e] - (blk_row0 - padded_start[block_e]), 0, MOE_BM).astype(I32)

        xa = _sc_scatter_rows(h2a, pos, cap)
        xb = _sc_scatter_rows(h2b, pos, cap)
        ya, yb = _experts(block_e, n_valid, xa, xb, w_gu_bf, b_gu.reshape(N_EXPERTS, 1, -1),
                          w_down_bf, b_down.reshape(N_EXPERTS, 1, -1))
        flat_pos = pos.reshape(1, -1)
        yga = _sc_gather_rows(ya, flat_pos).reshape(TOP_K, n, PACK_W)
        ygb = _sc_gather_rows(yb, flat_pos).reshape(TOP_K, n, PACK_W)
        out = _final(x2, yga, ygb, top_w.T, gt2, final_g.reshape(1, d), seq, batch0, n_total, out)
    return out


def kernel(x, c, ada_w, ada_b, norm1_g, w_in, conv_w, conv_b, lru_wa, lru_ba, lru_wx, lru_bx,
           lru_lam, sg_ln_g, sg_ln_b, sg_ws, sg_bs, w_br_rnn, w_br_sg, w_out, norm2_g,
           w_router, b_router, w_gu, b_gu, w_down, b_down, final_g):
    batch, seq, d = x.shape
    depth = ada_w.shape[0]
    assert depth == 1, "the combine is fused with the final norm, which follows the only layer"
    x2d = x.reshape(batch * seq, d)
    l = 0
    out = _layer(
        x2d, c, batch, seq, ada_w[l], ada_b[l], norm1_g[l], w_in[l], conv_w[l], conv_b[l],
        lru_wa[l], lru_ba[l], lru_wx[l], lru_bx[l], lru_lam[l], sg_ln_g[l], sg_ln_b[l],
        sg_ws[l], sg_bs[l], w_br_rnn[l], w_br_sg[l], w_out[l], norm2_g[l], w_router[l],
        b_router[l], w_gu[l], b_gu[l], w_down[l], b_down[l], final_g)
    return out.reshape(batch, seq, d)
```

```python
import functools

import jax
import jax.numpy as jnp
from jax import lax
from jax.experimental import pallas as pl
from jax.experimental.pallas import tpu as pltpu
from jax.experimental.pallas import tpu_sc as plsc

F32 = jnp.float32
BF16 = jnp.bfloat16
I32 = jnp.int32

D_MODEL = 1024
D_RNN = 1024
RNN_HEADS = 16
RNN_HEAD_DIM = D_RNN // RNN_HEADS
CONV_WIDTH = 4
LRU_C = 8.0
D_SG = 1024
SG_GROUPS = 8
SG_GROUP_DIM = D_SG // SG_GROUPS
SG_CHUNK = 128
N_EXPERTS = 32
TOP_K = 4
D_EXPERT = 1024
SWIGLU_LIMIT = 7.0
SWIGLU_ALPHA = 1.702
EPS = 1e-6
N_MOD = 6
D_IN = 2 * D_RNN + 2 * D_SG + 2 * D_MODEL

SUBLANES = 8
GATE_BLOCK = 256
N_GATE_BLOCKS = D_RNN // GATE_BLOCK

ADA_TN = 1536
INPROJ_TN = 1024
MIXER_TS = 256
MOE_BM = 512
MOE_HALF = MOE_BM // 2
TOKEN_GROUPS = 2
SC_WINDOW = 128
PACK_W = D_MODEL // 4
FINAL_TM = 256
VMEM_LIMIT = 58 * 1024 * 1024


def _sigmoid(x):
    return 0.5 * jnp.tanh(0.5 * x) + 0.5


def _gelu_tanh(x):
    k = 0.7978845608028654
    hx = 0.5 * x
    return hx + hx * jnp.tanh(x * (k + (k * 0.044715) * (x * x)))


def _bdot(a, b):
    return jnp.dot(a, b, preferred_element_type=F32)


def _pack_halves(v):
    word = pltpu.pack_elementwise([v[:, 2 * PACK_W:], v[:, :2 * PACK_W]], packed_dtype=BF16)
    return word[:, :PACK_W], word[:, PACK_W:]


def _unpack_halves(wa, wb):
    part = lambda w, i: pltpu.unpack_elementwise(w, index=i, packed_dtype=BF16, unpacked_dtype=F32)
    return jnp.concatenate([part(wa, 1), part(wb, 1), part(wa, 0), part(wb, 0)], axis=1)


def _ada_kernel(c_ref, w_ref, b_ref, o_ref):
    c = c_ref[...]
    s = c * _sigmoid(c)
    o_ref[...] = jnp.dot(s, w_ref[...], preferred_element_type=F32,
                         precision=lax.Precision.HIGHEST) + b_ref[...]


def _ada(c, ada_w, ada_b):
    b, d = c.shape
    n = ada_w.shape[1]
    return pl.pallas_call(
        _ada_kernel,
        grid=(n // ADA_TN,),
        in_specs=[
            pl.BlockSpec((b, d), lambda j: (0, 0)),
            pl.BlockSpec((d, ADA_TN), lambda j: (0, j)),
            pl.BlockSpec((1, ADA_TN), lambda j: (0, j)),
        ],
        out_specs=pl.BlockSpec((b, ADA_TN), lambda j: (0, j)),
        out_shape=jax.ShapeDtypeStruct((b, n), F32),
        compiler_params=pltpu.CompilerParams(
            dimension_semantics=("arbitrary",), vmem_limit_bytes=VMEM_LIMIT),
        name="ada",
    )(c, ada_w, ada_b.reshape(1, n))


def _norm_mod(x, g, sc, sh):
    ms = jnp.mean(x * x, axis=-1, keepdims=True)
    y = x * lax.rsqrt(ms + EPS)
    return (y * g) * (1.0 + sc) + sh


N_MIXER_INPUTS = 31


def _mixer_kernel(*refs, pairs_per_seq, n_aliased):
    (x_ref, xn_ref, n1g_ref, sc1_ref, sh1_ref, sc1n_ref, sh1n_ref, win_ref,
     gt1_ref, sc2_ref, sh2_ref,
     convw_ref, convb_ref, wa_ref, ba_ref, wx_ref, bx_ref, lam_ref,
     lng_ref, lnb_ref, ws_ref, bs_ref, wbr_ref, wbs_ref, wout_ref,
     n2g_ref, wr_ref, br_ref, shift_ref, wgu_ref, wd_ref) = refs[:N_MIXER_INPUTS]
    (x2_ref, h2a_ref, h2b_ref, idx_ref, tw_ref, rank_ref, cnt_ref, wgu_bf_ref, wd_bf_ref,
     z0_ref, z1_ref, hn0_ref, hn1_ref, xp_ref, xc_ref, a_ref, hh_ref, sv_ref,
     hstate_ref) = refs[N_MIXER_INPUTS + n_aliased:]
    ts = MIXER_TS
    s = pl.program_id(0)

    wgu_bf_ref[...] = wgu_ref[...].astype(BF16)
    wd_bf_ref[...] = wd_ref[...].astype(BF16)

    def inproj_norm(hn_ref, x_rows, sc, sh):
        hn_ref[...] = _norm_mod(x_rows, n1g_ref[...], sc, sh).astype(BF16)

    def inproj_chunks(z_dst, hn_ref, first, last):
        for c in range(first * INPROJ_TN, last * INPROJ_TN, INPROJ_TN):
            z_dst[:, c:c + INPROJ_TN] = _bdot(hn_ref[...], win_ref[:, c:c + INPROJ_TN]).astype(BF16)

    @pl.when(s == 0)
    def _():
        cnt_ref[...] = jnp.zeros_like(cnt_ref)
        inproj_norm(hn0_ref, x_ref[0:ts, :], sc1_ref[0], sh1_ref[0])
        inproj_chunks(z0_ref, hn0_ref, 0, D_IN // INPROJ_TN)

    @pl.when(s % pairs_per_seq == 0)
    def _():
        xp_ref[0:SUBLANES, :] = jnp.zeros((SUBLANES, D_RNN), F32)
        hstate_ref[...] = jnp.zeros_like(hstate_ref)

    def tile(k, z_ref, z_next_ref, hn_ref, next_rows, next_sc, next_sh):
        rows = slice(k * ts, (k + 1) * ts)
        inproj_norm(hn_ref, next_rows, next_sc, next_sh)

        x16 = z_ref[:, 0:D_RNN]
        rnn_x = x16.astype(F32)
        cw = convw_ref[...]
        xc = cw[3:4] * rnn_x + convb_ref[...]
        for sft in range(1, CONV_WIDTH):
            xc = xc + cw[3 - sft:4 - sft] * _bdot(shift_ref[sft - 1], x16)
        xc_ref[...] = xc
        xp_ref[SUBLANES:2 * SUBLANES, :] = rnn_x[0:SUBLANES, :]
        xc_ref[0:SUBLANES, :] = (
            cw[3:4] * xp_ref[SUBLANES:2 * SUBLANES, :]
            + cw[2:3] * xp_ref[SUBLANES - 1:2 * SUBLANES - 1, :]
            + cw[1:2] * xp_ref[SUBLANES - 2:2 * SUBLANES - 2, :]
            + cw[0:1] * xp_ref[SUBLANES - 3:2 * SUBLANES - 3, :]) + convb_ref[...]
        xp_ref[0:SUBLANES, :] = rnn_x[ts - SUBLANES:ts, :]
        xc = xc_ref[...]

        xcb = xc.astype(BF16)
        r_parts, i_parts = [], []
        for g in range(N_GATE_BLOCKS):
            blk = xcb[:, g * GATE_BLOCK:(g + 1) * GATE_BLOCK]
            r_parts.append(_bdot(blk, wa_ref[g]))
            i_parts.append(_bdot(blk, wx_ref[g]))
        inproj_chunks(z_next_ref, hn_ref, 0, 3)
        r = _sigmoid(jnp.concatenate(r_parts, axis=1) + ba_ref[...])
        ig = _sigmoid(jnp.concatenate(i_parts, axis=1) + bx_ref[...])

        nl = -lam_ref[...]
        softplus = jnp.maximum(nl, 0.0) + jnp.log(1.0 + jnp.exp(-jnp.abs(nl)))
        a = jnp.exp(r * ((-LRU_C) * softplus))
        t = 1.0 - a * a
        u = jnp.where(t > 0.0, t * lax.rsqrt(t), 0.0) * (ig * xc)

        groups = ts // SUBLANES
        a3 = a.reshape(groups, SUBLANES, D_RNN)
        h3 = u.reshape(groups, SUBLANES, D_RNN)
        sub = lax.broadcasted_iota(I32, (groups, SUBLANES, D_RNN), 1)
        for step in (1, 2, 4):
            keep = sub >= step
            a_sh = jnp.where(keep, pltpu.roll(a3, step, 1), 1.0)
            h_sh = jnp.where(keep, pltpu.roll(h3, step, 1), 0.0)
            h3 = h3 + a3 * h_sh
            a3 = a3 * a_sh
        a_ref[...] = a3.reshape(ts, D_RNN)
        hh_ref[...] = h3.reshape(ts, D_RNN)

        def carry_body(gi, hc):
            grp = pl.ds(pl.multiple_of(gi * SUBLANES, SUBLANES), SUBLANES)
            hg = hh_ref[grp, :] + a_ref[grp, :] * hc
            hh_ref[grp, :] = hg
            return jnp.broadcast_to(hg[SUBLANES - 1:SUBLANES, :], (SUBLANES, D_RNN))

        hstate_ref[...] = lax.fori_loop(0, groups, carry_body, hstate_ref[...])

        inproj_chunks(z_next_ref, hn_ref, 3, 4)
        y_rnn = (hh_ref[...] * _gelu_tanh(z_ref[:, D_RNN:2 * D_RNN].astype(F32))).astype(BF16)

        gv = _gelu_tanh(z_ref[:, 2 * D_RNN + D_SG:2 * D_RNN + 2 * D_SG].astype(F32))
        mu = jnp.mean(gv, axis=-1, keepdims=True)
        dv = gv - mu
        var = jnp.mean(dv * dv, axis=-1, keepdims=True)
        vn = (dv * lax.rsqrt(var + EPS) * lng_ref[...] + lnb_ref[...]).astype(BF16)
        tr = lax.broadcasted_iota(I32, (SG_CHUNK, SG_CHUNK), 0)
        tc = lax.broadcasted_iota(I32, (SG_CHUNK, SG_CHUNK), 1)
        causal = tc <= tr
        for g in range(SG_GROUPS):
            wg = jnp.where(causal, ws_ref[g], 0.0).astype(BF16)
            cols = slice(g * SG_GROUP_DIM, (g + 1) * SG_GROUP_DIM)
            for n in range(ts // SG_CHUNK):
                chunk = slice(n * SG_CHUNK, (n + 1) * SG_CHUNK)
                sv_ref[chunk, cols] = _bdot(wg, vn[chunk, cols]) + bs_ref[:, cols]
        inproj_chunks(z_next_ref, hn_ref, 4, 5)
        gu = _gelu_tanh(z_ref[:, 2 * D_RNN:2 * D_RNN + D_SG].astype(F32))
        y_sg = (gu * sv_ref[...]).astype(BF16)

        g_rnn = z_ref[:, 2 * D_RNN + 2 * D_SG:2 * D_RNN + 2 * D_SG + D_MODEL].astype(F32)
        g_sg = z_ref[:, 2 * D_RNN + 2 * D_SG + D_MODEL:D_IN].astype(F32)
        m = (_sigmoid(g_rnn) * _bdot(y_rnn, wbr_ref[...])
             + _sigmoid(g_sg) * _bdot(y_sg, wbs_ref[...])).astype(BF16)
        x2 = x_ref[rows, :] + gt1_ref[0] * _bdot(m, wout_ref[...])
        x2_ref[rows, :] = x2
        inproj_chunks(z_next_ref, hn_ref, 5, 6)

        h2 = _norm_mod(x2, n2g_ref[...], sc2_ref[0], sh2_ref[0])
        h2a_ref[rows, :], h2b_ref[rows, :] = _pack_halves(h2)
        h_hi = h2.astype(BF16)
        h_lo = (h2 - h_hi.astype(F32)).astype(BF16)
        nt_dims = (((1,), (1,)), ((), ()))
        by_hi = lax.dot_general(wr_ref[...], h_hi, nt_dims, preferred_element_type=F32)
        logits = (by_hi[:N_EXPERTS] + by_hi[N_EXPERTS:]
                  + lax.dot_general(wr_ref[0:N_EXPERTS, :], h_lo, nt_dims,
                                    preferred_element_type=F32)
                  + br_ref[...])
        e_iota = lax.broadcasted_iota(I32, (N_EXPERTS, ts), 0)
        v = logits
        vals, idxs, sels = [], [], []
        for _ in range(TOP_K):
            mx = jnp.max(v, axis=0, keepdims=True)
            ik = jnp.min(jnp.where(v == mx, e_iota, N_EXPERTS), axis=0, keepdims=True)
            sel = e_iota == ik
            v = jnp.where(sel, -jnp.inf, v)
            vals.append(mx)
            idxs.append(ik)
            sels.append(sel)
        exps = [jnp.exp(val - vals[0]) for val in vals]
        denom = exps[0] + exps[1] + exps[2] + exps[3]
        idx_ref[:, rows] = jnp.concatenate(idxs, axis=0)
        tw_ref[:, rows] = jnp.concatenate([e / denom for e in exps], axis=0)

        onehot = jnp.zeros((N_EXPERTS, ts), F32)
        for sel in sels:
            onehot = jnp.where(sel, 1.0, onehot)
        sr = lax.broadcasted_iota(I32, (ts, ts), 0)
        st = lax.broadcasted_iota(I32, (ts, ts), 1)
        before = jnp.where(sr < st, 1.0, 0.0).astype(BF16)
        total = cnt_ref[...] + _bdot(onehot.astype(BF16), before)
        ranks = [jnp.sum(jnp.where(sel, total, 0.0), axis=0, keepdims=True) for sel in sels]
        rank_ref[:, rows] = jnp.concatenate(ranks, axis=0).astype(I32)
        cnt_ref[...] = cnt_ref[...] + jnp.sum(onehot, axis=1, keepdims=True)

    tile(0, z0_ref, z1_ref, hn1_ref, x_ref[ts:2 * ts, :], sc1_ref[0], sh1_ref[0])
    tile(1, z1_ref, z0_ref, hn0_ref, xn_ref[...], sc1n_ref[0], sh1n_ref[0])


def _mixer(x2d, n1g, sc1, sh1, w_in, gt1, sc2, sh2, conv_w, conv_b, wa_bd, ba, wx_bd, bx, lam,
           ln_g, ln_b, ws, bs_tile, wbr, wbs, wout, n2g, wr_split, br, shifts, w_gu, w_down,
           batch0, batches, seq, expert0, n_cast, cast_prev):
    d = x2d.shape[1]
    ts = MIXER_TS
    n = batches * seq
    tiles_per_seq = seq // ts
    pairs_per_seq = tiles_per_seq // 2
    assert tiles_per_seq % 2 == 0
    steps = batches * pairs_per_seq
    pair0 = batch0 * pairs_per_seq
    next_tile = lambda s: jnp.minimum(2 * s + 2, 2 * steps - 1)
    pair = lambda s: (s, 0)
    pairt = lambda s: (0, s)
    bvec = lambda s: (batch0 + s // pairs_per_seq, 0, 0)
    bvec_next = lambda s: (batch0 + next_tile(s) // tiles_per_seq, 0, 0)
    n_e, k_gu, n_gu = w_gu.shape
    _, k_d, n_d = w_down.shape
    assert (n_cast * k_gu) % steps == 0 and (n_cast * k_d) % steps == 0
    rows_gu, rows_d = n_cast * k_gu // steps, n_cast * k_d // steps
    cast_gu = lambda s: (expert0 * k_gu // rows_gu + s, 0)
    cast_d = lambda s: (expert0 * k_d // rows_d + s, 0)
    w_gu = w_gu.reshape(n_e * k_gu, n_gu)
    w_down = w_down.reshape(n_e * k_d, n_d)
    c2 = lambda s: (0, 0)
    c3 = lambda s: (0, 0, 0)
    in_specs = [
        pl.BlockSpec((2 * ts, d), lambda s: (pair0 + s, 0)),
        pl.BlockSpec((ts, d), lambda s: (2 * pair0 + next_tile(s), 0)),
        pl.BlockSpec((1, d), c2),
        pl.BlockSpec((1, 1, d), bvec),
        pl.BlockSpec((1, 1, d), bvec),
        pl.BlockSpec((1, 1, d), bvec_next),
        pl.BlockSpec((1, 1, d), bvec_next),
        pl.BlockSpec((d, D_IN), c2),
        pl.BlockSpec((1, 1, d), bvec),
        pl.BlockSpec((1, 1, d), bvec),
        pl.BlockSpec((1, 1, d), bvec),
        pl.BlockSpec((CONV_WIDTH, D_RNN), c2),
        pl.BlockSpec((1, D_RNN), c2),
        pl.BlockSpec((N_GATE_BLOCKS, GATE_BLOCK, GATE_BLOCK), c3),
        pl.BlockSpec((1, D_RNN), c2),
        pl.BlockSpec((N_GATE_BLOCKS, GATE_BLOCK, GATE_BLOCK), c3),
        pl.BlockSpec((1, D_RNN), c2),
        pl.BlockSpec((1, D_RNN), c2),
        pl.BlockSpec((1, D_SG), c2),
        pl.BlockSpec((1, D_SG), c2),
        pl.BlockSpec((SG_GROUPS, SG_CHUNK, SG_CHUNK), c3),
        pl.BlockSpec((SG_CHUNK, D_SG), c2),
        pl.BlockSpec((D_RNN, d), c2),
        pl.BlockSpec((D_SG, d), c2),
        pl.BlockSpec((d, d), c2),
        pl.BlockSpec((1, d), c2),
        pl.BlockSpec((2 * N_EXPERTS, d), c2),
        pl.BlockSpec((N_EXPERTS, 1), c2),
        pl.BlockSpec((CONV_WIDTH - 1, ts, ts), c3),
        pl.BlockSpec((rows_gu, n_gu), cast_gu),
        pl.BlockSpec((rows_d, n_d), cast_d),
    ]
    assert len(in_specs) == N_MIXER_INPUTS
    args = [x2d, x2d, n1g, sc1, sh1, sc1, sh1, w_in, gt1, sc2, sh2, conv_w, conv_b, wa_bd, ba,
            wx_bd, bx, lam, ln_g, ln_b, ws, bs_tile, wbr, wbs, wout, n2g, wr_split, br, shifts,
            w_gu, w_down]
    aliases = {}
    if cast_prev is not None:
        for j, prev in enumerate(cast_prev):
            in_specs.append(pl.BlockSpec(memory_space=pl.ANY))
            aliases[len(args)] = 7 + j
            args.append(prev)
    out_specs = [
        pl.BlockSpec((2 * ts, d), pair),
        pl.BlockSpec((2 * ts, PACK_W), pair),
        pl.BlockSpec((2 * ts, PACK_W), pair),
        pl.BlockSpec((TOP_K, 2 * ts), pairt),
        pl.BlockSpec((TOP_K, 2 * ts), pairt),
        pl.BlockSpec((TOP_K, 2 * ts), pairt),
        pl.BlockSpec((N_EXPERTS, 1), c2),
        pl.BlockSpec((rows_gu, n_gu), cast_gu),
        pl.BlockSpec((rows_d, n_d), cast_d),
    ]
    out_shape = [
        jax.ShapeDtypeStruct((n, d), F32),
        jax.ShapeDtypeStruct((n, PACK_W), jnp.uint32),
        jax.ShapeDtypeStruct((n, PACK_W), jnp.uint32),
        jax.ShapeDtypeStruct((TOP_K, n), I32),
        jax.ShapeDtypeStruct((TOP_K, n), F32),
        jax.ShapeDtypeStruct((TOP_K, n), I32),
        jax.ShapeDtypeStruct((N_EXPERTS, 1), F32),
        jax.ShapeDtypeStruct(w_gu.shape, BF16),
        jax.ShapeDtypeStruct(w_down.shape, BF16),
    ]
    scratch = [
        pltpu.VMEM((ts, D_IN), BF16),
        pltpu.VMEM((ts, D_IN), BF16),
        pltpu.VMEM((ts, d), BF16),
        pltpu.VMEM((ts, d), BF16),
        pltpu.VMEM((2 * SUBLANES, D_RNN), F32),
        pltpu.VMEM((ts, D_RNN), F32),
        pltpu.VMEM((ts, D_RNN), F32),
        pltpu.VMEM((ts, D_RNN), F32),
        pltpu.VMEM((ts, D_SG), F32),
        pltpu.VMEM((SUBLANES, D_RNN), F32),
    ]
    return pl.pallas_call(
        functools.partial(_mixer_kernel, pairs_per_seq=pairs_per_seq, n_aliased=len(aliases)),
        grid=(steps,),
        in_specs=in_specs,
        out_specs=out_specs,
        out_shape=out_shape,
        scratch_shapes=scratch,
        input_output_aliases=aliases,
        compiler_params=pltpu.CompilerParams(
            dimension_semantics=("arbitrary",), vmem_limit_bytes=VMEM_LIMIT),
        name="mixer",
    )(*args)


def _sc_mesh():
    return plsc.VectorSubcoreMesh(core_axis_name="core", subcore_axis_name="subcore")


def _sc_scatter_rows(rows, pos, cap):
    n, d = rows.shape
    kk = pos.shape[0]

    @functools.partial(
        pl.kernel, out_type=jax.ShapeDtypeStruct((cap, d), rows.dtype), mesh=_sc_mesh(),
        scratch_types=[], name="sc_scatter_rows")
    def scatter(x_hbm, i_hbm, o_hbm):
        def body(x_vmem, i_vmem):
            pltpu.sync_copy(x_vmem, o_hbm.at[i_vmem.at[0]])

        pltpu.emit_pipeline(
            body,
            grid=(n // SC_WINDOW, kk),
            in_specs=[pl.BlockSpec((SC_WINDOW, d), lambda i, k: (i, 0)),
                      pl.BlockSpec((1, SC_WINDOW), lambda i, k: (k, i))],
            out_specs=[],
            core_axis_name=("core", "subcore"),
            dimension_semantics=(pltpu.PARALLEL, pltpu.ARBITRARY),
        )(x_hbm, i_hbm)

    return scatter(rows, pos)


def _sc_gather_rows(table, idx):
    m = idx.shape[1]
    d = table.shape[1]

    @functools.partial(
        pl.kernel, out_type=jax.ShapeDtypeStruct((m, d), table.dtype), mesh=_sc_mesh(),
        scratch_types=[], name="sc_gather_rows")
    def gather(x_hbm, i_hbm, o_hbm):
        def body(i_vmem, o_vmem):
            pltpu.sync_copy(x_hbm.at[i_vmem.at[0]], o_vmem)

        pltpu.emit_pipeline(
            body,
            grid=(m // SC_WINDOW,),
            in_specs=[pl.BlockSpec((1, SC_WINDOW), lambda i: (0, i))],
            out_specs=[pl.BlockSpec((SC_WINDOW, d), lambda i: (i, 0))],
            core_axis_name=("core", "subcore"),
            dimension_semantics=(pltpu.PARALLEL,),
        )(i_hbm, o_hbm)

    return gather(table, idx)


def _expert_kernel(be_ref, nv_ref, xa_ref, xb_ref, wgu_ref, bgu_ref, wd_ref, bd_ref,
                   ya_ref, yb_ref):
    nvalid = nv_ref[pl.program_id(0)]

    def mlp_rows(h):
        rows = slice(h * MOE_HALF, (h + 1) * MOE_HALF)
        live = lax.broadcasted_iota(I32, (MOE_HALF, 1), 0) < nvalid - h * MOE_HALF
        xb = jnp.where(live, _unpack_halves(xa_ref[rows, :], xb_ref[rows, :]), 0.0).astype(BF16)
        gu = _bdot(xb, wgu_ref[0]) + bgu_ref[0]
        gate = jnp.minimum(gu[:, :D_EXPERT], SWIGLU_LIMIT)
        up = jnp.clip(gu[:, D_EXPERT:], -SWIGLU_LIMIT, SWIGLU_LIMIT)
        act = (up + 1.0) * (gate * _sigmoid(SWIGLU_ALPHA * gate))
        y = _bdot(act.astype(BF16), wd_ref[0]) + bd_ref[0]
        ya_ref[rows, :], yb_ref[rows, :] = _pack_halves(y)

    def zero_rows(h):
        rows = slice(h * MOE_HALF, (h + 1) * MOE_HALF)
        ya_ref[rows, :] = jnp.zeros((MOE_HALF, PACK_W), jnp.uint32)
        yb_ref[rows, :] = jnp.zeros((MOE_HALF, PACK_W), jnp.uint32)

    @pl.when(nvalid > MOE_HALF)
    def _():
        mlp_rows(0)
        mlp_rows(1)

    @pl.when((nvalid > 0) & (nvalid <= MOE_HALF))
    def _():
        mlp_rows(0)
        zero_rows(1)

    @pl.when(nvalid <= 0)
    def _():
        zero_rows(0)
        zero_rows(1)


def _experts(block_e, n_valid, xa, xb, w_gu, b_gu, w_down, b_down):
    cap = xa.shape[0]
    d = D_MODEL
    nb = cap // MOE_BM
    half = pl.BlockSpec((MOE_BM, PACK_W), lambda i, be, nv: (i, 0))
    grid_spec = pltpu.PrefetchScalarGridSpec(
        num_scalar_prefetch=2,
        grid=(nb,),
        in_specs=[
            half,
            half,
            pl.BlockSpec((1, d, 2 * D_EXPERT), lambda i, be, nv: (be[i], 0, 0)),
            pl.BlockSpec((1, 1, 2 * D_EXPERT), lambda i, be, nv: (be[i], 0, 0)),
            pl.BlockSpec((1, D_EXPERT, d), lambda i, be, nv: (be[i], 0, 0)),
            pl.BlockSpec((1, 1, d), lambda i, be, nv: (be[i], 0, 0)),
        ],
        out_specs=[half, half],
    )
    return pl.pallas_call(
        _expert_kernel,
        grid_spec=grid_spec,
        out_shape=[jax.ShapeDtypeStruct((cap, PACK_W), jnp.uint32)] * 2,
        compiler_params=pltpu.CompilerParams(
            dimension_semantics=("arbitrary",), vmem_limit_bytes=VMEM_LIMIT),
        name="experts",
    )(block_e, n_valid, xa, xb, w_gu, b_gu, w_down, b_down)


def _final_kernel(x2_ref, ya_ref, yb_ref, tw_ref, gt2_ref, fg_ref, *rest):
    o_ref = rest[-1]
    tw = tw_ref[...]
    moe = tw[:, 0:1] * _unpack_halves(ya_ref[0], yb_ref[0])
    for k in range(1, TOP_K):
        moe = moe + tw[:, k:k + 1] * _unpack_halves(ya_ref[k], yb_ref[k])
    x3 = x2_ref[...] + gt2_ref[0] * moe
    ms = jnp.mean(x3 * x3, axis=-1, keepdims=True)
    o_ref[...] = (x3 * lax.rsqrt(ms + EPS)) * fg_ref[...]


def _final(x2, yga, ygb, tw_tok, gt2, final_g, seq, batch0, n_total, out_prev):
    n, d = x2.shape
    tiles_per_seq = seq // FINAL_TM
    tile0 = batch0 * tiles_per_seq
    half = pl.BlockSpec((TOP_K, FINAL_TM, PACK_W), lambda i: (0, i, 0))
    in_specs = [
        pl.BlockSpec((FINAL_TM, d), lambda i: (i, 0)),
        half,
        half,
        pl.BlockSpec((FINAL_TM, TOP_K), lambda i: (i, 0)),
        pl.BlockSpec((1, 1, d), lambda i: (batch0 + i // tiles_per_seq, 0, 0)),
        pl.BlockSpec((1, d), lambda i: (0, 0)),
    ]
    args = [x2, yga, ygb, tw_tok, gt2, final_g]
    aliases = {}
    if out_prev is not None:
        in_specs.append(pl.BlockSpec(memory_space=pl.ANY))
        aliases = {len(args): 0}
        args.append(out_prev)
    return pl.pallas_call(
        _final_kernel,
        grid=(n // FINAL_TM,),
        in_specs=in_specs,
        out_specs=pl.BlockSpec((FINAL_TM, d), lambda i: (tile0 + i, 0)),
        out_shape=jax.ShapeDtypeStruct((n_total, d), F32),
        input_output_aliases=aliases,
        compiler_params=pltpu.CompilerParams(
            dimension_semantics=("arbitrary",), vmem_limit_bytes=VMEM_LIMIT),
        name="final",
    )(*args)


def _block_diag(w):
    per = GATE_BLOCK // RNN_HEAD_DIM
    w4 = w.reshape(N_GATE_BLOCKS, per, RNN_HEAD_DIM, RNN_HEAD_DIM)
    eye = jnp.eye(per, dtype=w.dtype)
    bd = jnp.einsum("gpij,pq->gpiqj", w4, eye)
    return bd.reshape(N_GATE_BLOCKS, GATE_BLOCK, GATE_BLOCK)


def _layer(x2d, c, batch, seq, ada_w, ada_b, norm1_g, w_in, conv_w, conv_b, lru_wa, lru_ba,
           lru_wx, lru_bx, lru_lam, sg_ln_g, sg_ln_b, sg_ws, sg_bs, w_br_rnn, w_br_sg, w_out,
           norm2_g, w_router, b_router, w_gu, b_gu, w_down, b_down, final_g):
    n_total, d = x2d.shape
    mod = _ada(c, ada_w, ada_b)
    sh1, sc1, gt1, sh2, sc2, gt2 = [
        mod[:, i * d:(i + 1) * d].reshape(batch, 1, d) for i in range(N_MOD)]
    row = lambda v: v.reshape(1, -1)

    bs_tile = jnp.repeat(sg_bs.T, SG_GROUP_DIM, axis=1)
    wr_t = w_router.T
    wr_hi = wr_t.astype(BF16)
    wr_lo = (wr_t - wr_hi.astype(F32)).astype(BF16)
    t_out = lax.broadcasted_iota(I32, (MIXER_TS, MIXER_TS), 0)
    t_in = lax.broadcasted_iota(I32, (MIXER_TS, MIXER_TS), 1)
    shifts = jnp.stack([(t_out - t_in == s) for s in range(1, CONV_WIDTH)]).astype(BF16)
    mixer_weights = (
        conv_w, row(conv_b), _block_diag(lru_wa).astype(BF16), row(lru_ba),
        _block_diag(lru_wx).astype(BF16), row(lru_bx), row(lru_lam), row(sg_ln_g), row(sg_ln_b),
        sg_ws, bs_tile, w_br_rnn.astype(BF16), w_br_sg.astype(BF16), w_out.astype(BF16),
        row(norm2_g), jnp.concatenate([wr_hi, wr_lo], axis=0), b_router.reshape(N_EXPERTS, 1), shifts)
    w_in_bf = w_in.astype(BF16)

    groups = TOKEN_GROUPS if batch % TOKEN_GROUPS == 0 and N_EXPERTS % TOKEN_GROUPS == 0 else 1
    batches = batch // groups
    n_cast = N_EXPERTS // groups
    n = batches * seq
    mixed, cast = [], None
    for grp in range(groups):
        *outs, w_gu_bf, w_down_bf = _mixer(
            x2d, row(norm1_g), sc1, sh1, w_in_bf, gt1, sc2, sh2, *mixer_weights, w_gu, w_down,
            grp * batches, batches, seq, grp * n_cast, n_cast, cast)
        cast = (w_gu_bf, w_down_bf)
        mixed.append(outs)

    out = None
    for grp in range(groups):
        batch0 = grp * batches
        x2, h2a, h2b, top_idx, top_w, rank, counts = mixed[grp]

        cap = n * TOP_K + N_EXPERTS * MOE_BM
        nb = cap // MOE_BM
        cnt = counts[:, 0].astype(I32)
        padded = ((cnt + MOE_BM - 1) // MOE_BM) * MOE_BM
        padded_end = jnp.cumsum(padded)
        padded_start = padded_end - padded
        e_ids = jnp.arange(N_EXPERTS, dtype=I32)
        start_of = jnp.sum(jnp.where(top_idx[None] == e_ids[:, None, None],
                                     padded_start[:, None, None], 0), axis=0)
        pos = start_of + rank
        blk_row0 = jnp.arange(nb, dtype=I32) * MOE_BM
        block_e = jnp.minimum(
            jnp.sum((padded_end[None, :] <= blk_row0[:, None]).astype(I32), axis=1), N_EXPERTS - 1)
        n_valid = jnp.clip(cnt[block_e] - (blk_row0 - padded_start[block_e]), 0, MOE_BM).astype(I32)

        xa = _sc_scatter_rows(h2a, pos, cap)
        xb = _sc_scatter_rows(h2b, pos, cap)
        ya, yb = _experts(block_e, n_valid, xa, xb, cast[0].reshape(w_gu.shape),
                          b_gu.reshape(N_EXPERTS, 1, -1), cast[1].reshape(w_down.shape),
                          b_down.reshape(N_EXPERTS, 1, -1))
        flat_pos = pos.reshape(1, -1)
        yga = _sc_gather_rows(ya, flat_pos).reshape(TOP_K, n, PACK_W)
        ygb = _sc_gather_rows(yb, flat_pos).reshape(TOP_K, n, PACK_W)
        out = _final(x2, yga, ygb, top_w.T, gt2, final_g.reshape(1, d), seq, batch0, n_total, out)
    return out


def kernel(x, c, ada_w, ada_b, norm1_g, w_in, conv_w, conv_b, lru_wa, lru_ba, lru_wx, lru_bx,
           lru_lam, sg_ln_g, sg_ln_b, sg_ws, sg_bs, w_br_rnn, w_br_sg, w_out, norm2_g,
           w_router, b_router, w_gu, b_gu, w_down, b_down, final_g):
    batch, seq, d = x.shape
    depth = ada_w.shape[0]
    assert depth == 1, "the combine is fused with the final norm, which follows the only layer"
    x2d = x.reshape(batch * seq, d)
    l = 0
    out = _layer(
        x2d, c, batch, seq, ada_w[l], ada_b[l], norm1_g[l], w_in[l], conv_w[l], conv_b[l],
        lru_wa[l], lru_ba[l], lru_wx[l], lru_bx[l], lru_lam[l], sg_ln_g[l], sg_ln_b[l],
        sg_ws[l], sg_bs[l], w_br_rnn[l], w_br_sg[l], w_out[l], norm2_g[l], w_router[l],
        b_router[l], w_gu[l], b_gu[l], w_down[l], b_down[l], final_g)
    return out.reshape(batch, seq, d)
```

```python
import functools

import jax
import jax.numpy as jnp
from jax import lax
from jax.experimental import pallas as pl
from jax.experimental.pallas import tpu as pltpu
from jax.experimental.pallas import tpu_sc as plsc

F32 = jnp.float32
BF16 = jnp.bfloat16
I32 = jnp.int32

D_MODEL = 1024
D_RNN = 1024
RNN_HEADS = 16
RNN_HEAD_DIM = D_RNN // RNN_HEADS
CONV_WIDTH = 4
LRU_C = 8.0
D_SG = 1024
SG_GROUPS = 8
SG_GROUP_DIM = D_SG // SG_GROUPS
SG_CHUNK = 128
N_EXPERTS = 32
TOP_K = 4
D_EXPERT = 1024
SWIGLU_LIMIT = 7.0
SWIGLU_ALPHA = 1.702
EPS = 1e-6
N_MOD = 6
D_IN = 2 * D_RNN + 2 * D_SG + 2 * D_MODEL

SUBLANES = 8
GATE_BLOCK = 256
N_GATE_BLOCKS = D_RNN // GATE_BLOCK

ADA_TN = 1536
INPROJ_TN = 1024
MIXER_TS = 256
MOE_BM = 512
MOE_HALF = MOE_BM // 2
TOKEN_GROUPS = 2
SC_WINDOW = 128
PACK_W = D_MODEL // 4
FINAL_TM = 256
POS_TN = 8192
VMEM_LIMIT = 58 * 1024 * 1024


def _sigmoid(x):
    return 0.5 * jnp.tanh(0.5 * x) + 0.5


def _gelu_tanh(x):
    k = 0.7978845608028654
    hx = 0.5 * x
    return hx + hx * jnp.tanh(x * (k + (k * 0.044715) * (x * x)))


def _bdot(a, b):
    return jnp.dot(a, b, preferred_element_type=F32)


def _pack_halves(v):
    word = pltpu.pack_elementwise([v[:, 2 * PACK_W:], v[:, :2 * PACK_W]], packed_dtype=BF16)
    return word[:, :PACK_W], word[:, PACK_W:]


def _unpack_halves(wa, wb):
    part = lambda w, i: pltpu.unpack_elementwise(w, index=i, packed_dtype=BF16, unpacked_dtype=F32)
    return jnp.concatenate([part(wa, 1), part(wb, 1), part(wa, 0), part(wb, 0)], axis=1)


def _ada_kernel(c_ref, w_ref, b_ref, o_ref):
    c = c_ref[...]
    s = c * _sigmoid(c)
    o_ref[...] = jnp.dot(s, w_ref[...], preferred_element_type=F32,
                         precision=lax.Precision.HIGHEST) + b_ref[...]


def _ada(c, ada_w, ada_b):
    b, d = c.shape
    n = ada_w.shape[1]
    return pl.pallas_call(
        _ada_kernel,
        grid=(n // ADA_TN,),
        in_specs=[
            pl.BlockSpec((b, d), lambda j: (0, 0)),
            pl.BlockSpec((d, ADA_TN), lambda j: (0, j)),
            pl.BlockSpec((1, ADA_TN), lambda j: (0, j)),
        ],
        out_specs=pl.BlockSpec((b, ADA_TN), lambda j: (0, j)),
        out_shape=jax.ShapeDtypeStruct((b, n), F32),
        compiler_params=pltpu.CompilerParams(
            dimension_semantics=("arbitrary",), vmem_limit_bytes=VMEM_LIMIT),
        name="ada",
    )(c, ada_w, ada_b.reshape(1, n))


def _norm_mod(x, g, sc, sh):
    ms = jnp.mean(x * x, axis=-1, keepdims=True)
    y = x * lax.rsqrt(ms + EPS)
    return (y * g) * (1.0 + sc) + sh


N_MIXER_INPUTS = 31


def _mixer_kernel(*refs, pairs_per_seq, n_aliased):
    (x_ref, xn_ref, n1g_ref, sc1_ref, sh1_ref, sc1n_ref, sh1n_ref, win_ref,
     gt1_ref, sc2_ref, sh2_ref,
     convw_ref, convb_ref, wa_ref, ba_ref, wx_ref, bx_ref, lam_ref,
     lng_ref, lnb_ref, ws_ref, bs_ref, wbr_ref, wbs_ref, wout_ref,
     n2g_ref, wr_ref, br_ref, shift_ref, wgu_ref, wd_ref) = refs[:N_MIXER_INPUTS]
    (x2_ref, h2a_ref, h2b_ref, idx_ref, tw_ref, rank_ref, cnt_ref, wgu_bf_ref, wd_bf_ref,
     z0_ref, z1_ref, hn0_ref, hn1_ref, xp_ref, xc_ref, a_ref, hh_ref, sv_ref,
     hstate_ref) = refs[N_MIXER_INPUTS + n_aliased:]
    ts = MIXER_TS
    s = pl.program_id(0)

    wgu_bf_ref[...] = wgu_ref[...].astype(BF16)
    wd_bf_ref[...] = wd_ref[...].astype(BF16)

    def inproj_norm(hn_ref, x_rows, sc, sh):
        hn_ref[...] = _norm_mod(x_rows, n1g_ref[...], sc, sh).astype(BF16)

    def inproj_chunks(z_dst, hn_ref, first, last):
        for c in range(first * INPROJ_TN, last * INPROJ_TN, INPROJ_TN):
            z_dst[:, c:c + INPROJ_TN] = _bdot(hn_ref[...], win_ref[:, c:c + INPROJ_TN]).astype(BF16)

    @pl.when(s == 0)
    def _():
        cnt_ref[...] = jnp.zeros_like(cnt_ref)
        inproj_norm(hn0_ref, x_ref[0:ts, :], sc1_ref[0], sh1_ref[0])
        inproj_chunks(z0_ref, hn0_ref, 0, D_IN // INPROJ_TN)

    @pl.when(s % pairs_per_seq == 0)
    def _():
        xp_ref[0:SUBLANES, :] = jnp.zeros((SUBLANES, D_RNN), F32)
        hstate_ref[...] = jnp.zeros_like(hstate_ref)

    def tile(k, z_ref, z_next_ref, hn_ref, next_rows, next_sc, next_sh):
        rows = slice(k * ts, (k + 1) * ts)
        inproj_norm(hn_ref, next_rows, next_sc, next_sh)

        x16 = z_ref[:, 0:D_RNN]
        rnn_x = x16.astype(F32)
        cw = convw_ref[...]
        xc = cw[3:4] * rnn_x + convb_ref[...]
        for sft in range(1, CONV_WIDTH):
            xc = xc + cw[3 - sft:4 - sft] * _bdot(shift_ref[sft - 1], x16)
        xc_ref[...] = xc
        xp_ref[SUBLANES:2 * SUBLANES, :] = rnn_x[0:SUBLANES, :]
        xc_ref[0:SUBLANES, :] = (
            cw[3:4] * xp_ref[SUBLANES:2 * SUBLANES, :]
            + cw[2:3] * xp_ref[SUBLANES - 1:2 * SUBLANES - 1, :]
            + cw[1:2] * xp_ref[SUBLANES - 2:2 * SUBLANES - 2, :]
            + cw[0:1] * xp_ref[SUBLANES - 3:2 * SUBLANES - 3, :]) + convb_ref[...]
        xp_ref[0:SUBLANES, :] = rnn_x[ts - SUBLANES:ts, :]
        xc = xc_ref[...]

        xcb = xc.astype(BF16)
        r_parts, i_parts = [], []
        for g in range(N_GATE_BLOCKS):
            blk = xcb[:, g * GATE_BLOCK:(g + 1) * GATE_BLOCK]
            r_parts.append(_bdot(blk, wa_ref[g]))
            i_parts.append(_bdot(blk, wx_ref[g]))
        inproj_chunks(z_next_ref, hn_ref, 0, 3)
        r = _sigmoid(jnp.concatenate(r_parts, axis=1) + ba_ref[...])
        ig = _sigmoid(jnp.concatenate(i_parts, axis=1) + bx_ref[...])

        nl = -lam_ref[...]
        softplus = jnp.maximum(nl, 0.0) + jnp.log(1.0 + jnp.exp(-jnp.abs(nl)))
        a = jnp.exp(r * ((-LRU_C) * softplus))
        t = 1.0 - a * a
        u = jnp.where(t > 0.0, t * lax.rsqrt(t), 0.0) * (ig * xc)

        groups = ts // SUBLANES
        a3 = a.reshape(groups, SUBLANES, D_RNN)
        h3 = u.reshape(groups, SUBLANES, D_RNN)
        sub = lax.broadcasted_iota(I32, (groups, SUBLANES, D_RNN), 1)
        for step in (1, 2, 4):
            keep = sub >= step
            a_sh = jnp.where(keep, pltpu.roll(a3, step, 1), 1.0)
            h_sh = jnp.where(keep, pltpu.roll(h3, step, 1), 0.0)
            h3 = h3 + a3 * h_sh
            a3 = a3 * a_sh
        a_ref[...] = a3.reshape(ts, D_RNN)
        hh_ref[...] = h3.reshape(ts, D_RNN)

        def carry_body(gi, hc):
            grp = pl.ds(pl.multiple_of(gi * SUBLANES, SUBLANES), SUBLANES)
            hg = hh_ref[grp, :] + a_ref[grp, :] * hc
            hh_ref[grp, :] = hg
            return jnp.broadcast_to(hg[SUBLANES - 1:SUBLANES, :], (SUBLANES, D_RNN))

        hstate_ref[...] = lax.fori_loop(0, groups, carry_body, hstate_ref[...])

        inproj_chunks(z_next_ref, hn_ref, 3, 4)
        y_rnn = (hh_ref[...] * _gelu_tanh(z_ref[:, D_RNN:2 * D_RNN].astype(F32))).astype(BF16)

        gv = _gelu_tanh(z_ref[:, 2 * D_RNN + D_SG:2 * D_RNN + 2 * D_SG].astype(F32))
        mu = jnp.mean(gv, axis=-1, keepdims=True)
        dv = gv - mu
        var = jnp.mean(dv * dv, axis=-1, keepdims=True)
        vn = (dv * lax.rsqrt(var + EPS) * lng_ref[...] + lnb_ref[...]).astype(BF16)
        tr = lax.broadcasted_iota(I32, (SG_CHUNK, SG_CHUNK), 0)
        tc = lax.broadcasted_iota(I32, (SG_CHUNK, SG_CHUNK), 1)
        causal = tc <= tr
        for g in range(SG_GROUPS):
            wg = jnp.where(causal, ws_ref[g], 0.0).astype(BF16)
            cols = slice(g * SG_GROUP_DIM, (g + 1) * SG_GROUP_DIM)
            for n in range(ts // SG_CHUNK):
                chunk = slice(n * SG_CHUNK, (n + 1) * SG_CHUNK)
                sv_ref[chunk, cols] = _bdot(wg, vn[chunk, cols]) + bs_ref[:, cols]
        inproj_chunks(z_next_ref, hn_ref, 4, 5)
        gu = _gelu_tanh(z_ref[:, 2 * D_RNN:2 * D_RNN + D_SG].astype(F32))
        y_sg = (gu * sv_ref[...]).astype(BF16)

        g_rnn = z_ref[:, 2 * D_RNN + 2 * D_SG:2 * D_RNN + 2 * D_SG + D_MODEL].astype(F32)
        g_sg = z_ref[:, 2 * D_RNN + 2 * D_SG + D_MODEL:D_IN].astype(F32)
        m = (_sigmoid(g_rnn) * _bdot(y_rnn, wbr_ref[...])
             + _sigmoid(g_sg) * _bdot(y_sg, wbs_ref[...])).astype(BF16)
        x2 = x_ref[rows, :] + gt1_ref[0] * _bdot(m, wout_ref[...])
        x2_ref[rows, :] = x2
        inproj_chunks(z_next_ref, hn_ref, 5, 6)

        h2 = _norm_mod(x2, n2g_ref[...], sc2_ref[0], sh2_ref[0])
        h2a_ref[rows, :], h2b_ref[rows, :] = _pack_halves(h2)
        h_hi = h2.astype(BF16)
        h_lo = (h2 - h_hi.astype(F32)).astype(BF16)
        nt_dims = (((1,), (1,)), ((), ()))
        by_hi = lax.dot_general(wr_ref[...], h_hi, nt_dims, preferred_element_type=F32)
        logits = (by_hi[:N_EXPERTS] + by_hi[N_EXPERTS:]
                  + lax.dot_general(wr_ref[0:N_EXPERTS, :], h_lo, nt_dims,
                                    preferred_element_type=F32)
                  + br_ref[...])
        e_iota = lax.broadcasted_iota(I32, (N_EXPERTS, ts), 0)
        v = logits
        vals, idxs, sels = [], [], []
        for _ in range(TOP_K):
            mx = jnp.max(v, axis=0, keepdims=True)
            ik = jnp.min(jnp.where(v == mx, e_iota, N_EXPERTS), axis=0, keepdims=True)
            sel = e_iota == ik
            v = jnp.where(sel, -jnp.inf, v)
            vals.append(mx)
            idxs.append(ik)
            sels.append(sel)
        exps = [jnp.exp(val - vals[0]) for val in vals]
        denom = exps[0] + exps[1] + exps[2] + exps[3]
        idx_ref[:, rows] = jnp.concatenate(idxs, axis=0)
        tw_ref[:, rows] = jnp.concatenate([e / denom for e in exps], axis=0)

        onehot = jnp.zeros((N_EXPERTS, ts), F32)
        for sel in sels:
            onehot = jnp.where(sel, 1.0, onehot)
        sr = lax.broadcasted_iota(I32, (ts, ts), 0)
        st = lax.broadcasted_iota(I32, (ts, ts), 1)
        before = jnp.where(sr < st, 1.0, 0.0).astype(BF16)
        total = cnt_ref[...] + _bdot(onehot.astype(BF16), before)
        ranks = [jnp.sum(jnp.where(sel, total, 0.0), axis=0, keepdims=True) for sel in sels]
        rank_ref[:, rows] = jnp.concatenate(ranks, axis=0).astype(I32)
        cnt_ref[...] = cnt_ref[...] + jnp.sum(onehot, axis=1, keepdims=True)

    tile(0, z0_ref, z1_ref, hn1_ref, x_ref[ts:2 * ts, :], sc1_ref[0], sh1_ref[0])
    tile(1, z1_ref, z0_ref, hn0_ref, xn_ref[...], sc1n_ref[0], sh1n_ref[0])


def _mixer(x2d, n1g, sc1, sh1, w_in, gt1, sc2, sh2, conv_w, conv_b, wa_bd, ba, wx_bd, bx, lam,
           ln_g, ln_b, ws, bs_tile, wbr, wbs, wout, n2g, wr_split, br, shifts, w_gu, w_down,
           batch0, batches, seq, expert0, n_cast, cast_prev):
    d = x2d.shape[1]
    ts = MIXER_TS
    n = batches * seq
    tiles_per_seq = seq // ts
    pairs_per_seq = tiles_per_seq // 2
    assert tiles_per_seq % 2 == 0
    steps = batches * pairs_per_seq
    pair0 = batch0 * pairs_per_seq
    next_tile = lambda s: jnp.minimum(2 * s + 2, 2 * steps - 1)
    pair = lambda s: (s, 0)
    pairt = lambda s: (0, s)
    bvec = lambda s: (batch0 + s // pairs_per_seq, 0, 0)
    bvec_next = lambda s: (batch0 + next_tile(s) // tiles_per_seq, 0, 0)
    n_e, k_gu, n_gu = w_gu.shape
    _, k_d, n_d = w_down.shape
    assert (n_cast * k_gu) % steps == 0 and (n_cast * k_d) % steps == 0
    rows_gu, rows_d = n_cast * k_gu // steps, n_cast * k_d // steps
    cast_gu = lambda s: (expert0 * k_gu // rows_gu + s, 0)
    cast_d = lambda s: (expert0 * k_d // rows_d + s, 0)
    w_gu = w_gu.reshape(n_e * k_gu, n_gu)
    w_down = w_down.reshape(n_e * k_d, n_d)
    c2 = lambda s: (0, 0)
    c3 = lambda s: (0, 0, 0)
    in_specs = [
        pl.BlockSpec((2 * ts, d), lambda s: (pair0 + s, 0)),
        pl.BlockSpec((ts, d), lambda s: (2 * pair0 + next_tile(s), 0)),
        pl.BlockSpec((1, d), c2),
        pl.BlockSpec((1, 1, d), bvec),
        pl.BlockSpec((1, 1, d), bvec),
        pl.BlockSpec((1, 1, d), bvec_next),
        pl.BlockSpec((1, 1, d), bvec_next),
        pl.BlockSpec((d, D_IN), c2),
        pl.BlockSpec((1, 1, d), bvec),
        pl.BlockSpec((1, 1, d), bvec),
        pl.BlockSpec((1, 1, d), bvec),
        pl.BlockSpec((CONV_WIDTH, D_RNN), c2),
        pl.BlockSpec((1, D_RNN), c2),
        pl.BlockSpec((N_GATE_BLOCKS, GATE_BLOCK, GATE_BLOCK), c3),
        pl.BlockSpec((1, D_RNN), c2),
        pl.BlockSpec((N_GATE_BLOCKS, GATE_BLOCK, GATE_BLOCK), c3),
        pl.BlockSpec((1, D_RNN), c2),
        pl.BlockSpec((1, D_RNN), c2),
        pl.BlockSpec((1, D_SG), c2),
        pl.BlockSpec((1, D_SG), c2),
        pl.BlockSpec((SG_GROUPS, SG_CHUNK, SG_CHUNK), c3),
        pl.BlockSpec((SG_CHUNK, D_SG), c2),
        pl.BlockSpec((D_RNN, d), c2),
        pl.BlockSpec((D_SG, d), c2),
        pl.BlockSpec((d, d), c2),
        pl.BlockSpec((1, d), c2),
        pl.BlockSpec((2 * N_EXPERTS, d), c2),
        pl.BlockSpec((N_EXPERTS, 1), c2),
        pl.BlockSpec((CONV_WIDTH - 1, ts, ts), c3),
        pl.BlockSpec((rows_gu, n_gu), cast_gu),
        pl.BlockSpec((rows_d, n_d), cast_d),
    ]
    assert len(in_specs) == N_MIXER_INPUTS
    args = [x2d, x2d, n1g, sc1, sh1, sc1, sh1, w_in, gt1, sc2, sh2, conv_w, conv_b, wa_bd, ba,
            wx_bd, bx, lam, ln_g, ln_b, ws, bs_tile, wbr, wbs, wout, n2g, wr_split, br, shifts,
            w_gu, w_down]
    aliases = {}
    if cast_prev is not None:
        for j, prev in enumerate(cast_prev):
            in_specs.append(pl.BlockSpec(memory_space=pl.ANY))
            aliases[len(args)] = 7 + j
            args.append(prev)
    out_specs = [
        pl.BlockSpec((2 * ts, d), pair),
        pl.BlockSpec((2 * ts, PACK_W), pair),
        pl.BlockSpec((2 * ts, PACK_W), pair),
        pl.BlockSpec((TOP_K, 2 * ts), pairt),
        pl.BlockSpec((TOP_K, 2 * ts), pairt),
        pl.BlockSpec((TOP_K, 2 * ts), pairt),
        pl.BlockSpec((N_EXPERTS, 1), c2),
        pl.BlockSpec((rows_gu, n_gu), cast_gu),
        pl.BlockSpec((rows_d, n_d), cast_d),
    ]
    out_shape = [
        jax.ShapeDtypeStruct((n, d), F32),
        jax.ShapeDtypeStruct((n, PACK_W), jnp.uint32),
        jax.ShapeDtypeStruct((n, PACK_W), jnp.uint32),
        jax.ShapeDtypeStruct((TOP_K, n), I32),
        jax.ShapeDtypeStruct((TOP_K, n), F32),
        jax.ShapeDtypeStruct((TOP_K, n), I32),
        jax.ShapeDtypeStruct((N_EXPERTS, 1), F32),
        jax.ShapeDtypeStruct(w_gu.shape, BF16),
        jax.ShapeDtypeStruct(w_down.shape, BF16),
    ]
    scratch = [
        pltpu.VMEM((ts, D_IN), BF16),
        pltpu.VMEM((ts, D_IN), BF16),
        pltpu.VMEM((ts, d), BF16),
        pltpu.VMEM((ts, d), BF16),
        pltpu.VMEM((2 * SUBLANES, D_RNN), F32),
        pltpu.VMEM((ts, D_RNN), F32),
        pltpu.VMEM((ts, D_RNN), F32),
        pltpu.VMEM((ts, D_RNN), F32),
        pltpu.VMEM((ts, D_SG), F32),
        pltpu.VMEM((SUBLANES, D_RNN), F32),
    ]
    return pl.pallas_call(
        functools.partial(_mixer_kernel, pairs_per_seq=pairs_per_seq, n_aliased=len(aliases)),
        grid=(steps,),
        in_specs=in_specs,
        out_specs=out_specs,
        out_shape=out_shape,
        scratch_shapes=scratch,
        input_output_aliases=aliases,
        compiler_params=pltpu.CompilerParams(
            dimension_semantics=("arbitrary",), vmem_limit_bytes=VMEM_LIMIT),
        name="mixer",
    )(*args)


def _sc_mesh():
    return plsc.VectorSubcoreMesh(core_axis_name="core", subcore_axis_name="subcore")


def _sc_scatter_rows(rows, pos, cap):
    n, d = rows.shape
    kk = pos.shape[0]

    @functools.partial(
        pl.kernel, out_type=jax.ShapeDtypeStruct((cap, d), rows.dtype), mesh=_sc_mesh(),
        scratch_types=[], name="sc_scatter_rows")
    def scatter(x_hbm, i_hbm, o_hbm):
        def body(x_vmem, i_vmem):
            pltpu.sync_copy(x_vmem, o_hbm.at[i_vmem.at[0]])

        pltpu.emit_pipeline(
            body,
            grid=(n // SC_WINDOW, kk),
            in_specs=[pl.BlockSpec((SC_WINDOW, d), lambda i, k: (i, 0)),
                      pl.BlockSpec((1, SC_WINDOW), lambda i, k: (k, i))],
            out_specs=[],
            core_axis_name=("core", "subcore"),
            dimension_semantics=(pltpu.PARALLEL, pltpu.ARBITRARY),
        )(x_hbm, i_hbm)

    return scatter(rows, pos)


def _sc_gather_rows(table, idx):
    m = idx.shape[1]
    d = table.shape[1]

    @functools.partial(
        pl.kernel, out_type=jax.ShapeDtypeStruct((m, d), table.dtype), mesh=_sc_mesh(),
        scratch_types=[], name="sc_gather_rows")
    def gather(x_hbm, i_hbm, o_hbm):
        def body(i_vmem, o_vmem):
            pltpu.sync_copy(x_hbm.at[i_vmem.at[0]], o_vmem)

        pltpu.emit_pipeline(
            body,
            grid=(m // SC_WINDOW,),
            in_specs=[pl.BlockSpec((1, SC_WINDOW), lambda i: (0, i))],
            out_specs=[pl.BlockSpec((SC_WINDOW, d), lambda i: (i, 0))],
            core_axis_name=("core", "subcore"),
            dimension_semantics=(pltpu.PARALLEL,),
        )(i_hbm, o_hbm)

    return gather(table, idx)


def _pos_kernel(start_ref, idx_ref, rank_ref, pos_ref):
    idx = idx_ref[...]
    pos = rank_ref[...]
    for e in range(N_EXPERTS):
        pos = pos + jnp.where(idx == e, start_ref[e], 0)
    pos_ref[...] = pos


def _slot_positions(padded_start, top_idx, rank):
    k, n = top_idx.shape
    tn = min(POS_TN, n)
    spec = pl.BlockSpec((k, tn), lambda i, st: (0, i))
    return pl.pallas_call(
        _pos_kernel,
        grid_spec=pltpu.PrefetchScalarGridSpec(
            num_scalar_prefetch=1, grid=(n // tn,), in_specs=[spec, spec], out_specs=spec),
        out_shape=jax.ShapeDtypeStruct((k, n), I32),
        compiler_params=pltpu.CompilerParams(dimension_semantics=("arbitrary",)),
        name="slot_pos",
    )(padded_start, top_idx, rank)


def _expert_kernel(be_ref, nv_ref, xa_ref, xb_ref, wgu_ref, bgu_ref, wd_ref, bd_ref,
                   ya_ref, yb_ref):
    nvalid = nv_ref[pl.program_id(0)]

    def mlp_rows(h):
        rows = slice(h * MOE_HALF, (h + 1) * MOE_HALF)
        live = lax.broadcasted_iota(I32, (MOE_HALF, 1), 0) < nvalid - h * MOE_HALF
        xb = jnp.where(live, _unpack_halves(xa_ref[rows, :], xb_ref[rows, :]), 0.0).astype(BF16)
        gu = _bdot(xb, wgu_ref[0]) + bgu_ref[0]
        gate = jnp.minimum(gu[:, :D_EXPERT], SWIGLU_LIMIT)
        up = jnp.clip(gu[:, D_EXPERT:], -SWIGLU_LIMIT, SWIGLU_LIMIT)
        act = (up + 1.0) * (gate * _sigmoid(SWIGLU_ALPHA * gate))
        y = _bdot(act.astype(BF16), wd_ref[0]) + bd_ref[0]
        ya_ref[rows, :], yb_ref[rows, :] = _pack_halves(y)

    def zero_rows(h):
        rows = slice(h * MOE_HALF, (h + 1) * MOE_HALF)
        ya_ref[rows, :] = jnp.zeros((MOE_HALF, PACK_W), jnp.uint32)
        yb_ref[rows, :] = jnp.zeros((MOE_HALF, PACK_W), jnp.uint32)

    @pl.when(nvalid > MOE_HALF)
    def _():
        mlp_rows(0)
        mlp_rows(1)

    @pl.when((nvalid > 0) & (nvalid <= MOE_HALF))
    def _():
        mlp_rows(0)
        zero_rows(1)

    @pl.when(nvalid <= 0)
    def _():
        zero_rows(0)
        zero_rows(1)


def _experts(block_e, n_valid, xa, xb, w_gu, b_gu, w_down, b_down):
    cap = xa.shape[0]
    d = D_MODEL
    nb = cap // MOE_BM
    half = pl.BlockSpec((MOE_BM, PACK_W), lambda i, be, nv: (i, 0))
    grid_spec = pltpu.PrefetchScalarGridSpec(
        num_scalar_prefetch=2,
        grid=(nb,),
        in_specs=[
            half,
            half,
            pl.BlockSpec((1, d, 2 * D_EXPERT), lambda i, be, nv: (be[i], 0, 0)),
            pl.BlockSpec((1, 1, 2 * D_EXPERT), lambda i, be, nv: (be[i], 0, 0)),
            pl.BlockSpec((1, D_EXPERT, d), lambda i, be, nv: (be[i], 0, 0)),
            pl.BlockSpec((1, 1, d), lambda i, be, nv: (be[i], 0, 0)),
        ],
        out_specs=[half, half],
    )
    return pl.pallas_call(
        _expert_kernel,
        grid_spec=grid_spec,
        out_shape=[jax.ShapeDtypeStruct((cap, PACK_W), jnp.uint32)] * 2,
        compiler_params=pltpu.CompilerParams(
            dimension_semantics=("arbitrary",), vmem_limit_bytes=VMEM_LIMIT),
        name="experts",
    )(block_e, n_valid, xa, xb, w_gu, b_gu, w_down, b_down)


def _final_kernel(x2_ref, ya_ref, yb_ref, tw_ref, gt2_ref, fg_ref, *rest):
    o_ref = rest[-1]
    tw = tw_ref[...].T
    moe = tw[:, 0:1] * _unpack_halves(ya_ref[0], yb_ref[0])
    for k in range(1, TOP_K):
        moe = moe + tw[:, k:k + 1] * _unpack_halves(ya_ref[k], yb_ref[k])
    x3 = x2_ref[...] + gt2_ref[0] * moe
    ms = jnp.mean(x3 * x3, axis=-1, keepdims=True)
    o_ref[...] = (x3 * lax.rsqrt(ms + EPS)) * fg_ref[...]


def _final(x2, yga, ygb, tw_tok, gt2, final_g, seq, batch0, n_total, out_prev):
    n, d = x2.shape
    tiles_per_seq = seq // FINAL_TM
    tile0 = batch0 * tiles_per_seq
    half = pl.BlockSpec((TOP_K, FINAL_TM, PACK_W), lambda i: (0, i, 0))
    in_specs = [
        pl.BlockSpec((FINAL_TM, d), lambda i: (i, 0)),
        half,
        half,
        pl.BlockSpec((TOP_K, FINAL_TM), lambda i: (0, i)),
        pl.BlockSpec((1, 1, d), lambda i: (batch0 + i // tiles_per_seq, 0, 0)),
        pl.BlockSpec((1, d), lambda i: (0, 0)),
    ]
    args = [x2, yga, ygb, tw_tok, gt2, final_g]
    aliases = {}
    if out_prev is not None:
        in_specs.append(pl.BlockSpec(memory_space=pl.ANY))
        aliases = {len(args): 0}
        args.append(out_prev)
    return pl.pallas_call(
        _final_kernel,
        grid=(n // FINAL_TM,),
        in_specs=in_specs,
        out_specs=pl.BlockSpec((FINAL_TM, d), lambda i: (tile0 + i, 0)),
        out_shape=jax.ShapeDtypeStruct((n_total, d), F32),
        input_output_aliases=aliases,
        compiler_params=pltpu.CompilerParams(
            dimension_semantics=("arbitrary",), vmem_limit_bytes=VMEM_LIMIT),
        name="final",
    )(*args)


def _block_diag(w):
    per = GATE_BLOCK // RNN_HEAD_DIM
    w4 = w.reshape(N_GATE_BLOCKS, per, RNN_HEAD_DIM, RNN_HEAD_DIM)
    eye = jnp.eye(per, dtype=w.dtype)
    bd = jnp.einsum("gpij,pq->gpiqj", w4, eye)
    return bd.reshape(N_GATE_BLOCKS, GATE_BLOCK, GATE_BLOCK)


def _layer(x2d, c, batch, seq, ada_w, ada_b, norm1_g, w_in, conv_w, conv_b, lru_wa, lru_ba,
           lru_wx, lru_bx, lru_lam, sg_ln_g, sg_ln_b, sg_ws, sg_bs, w_br_rnn, w_br_sg, w_out,
           norm2_g, w_router, b_router, w_gu, b_gu, w_down, b_down, final_g):
    n_total, d = x2d.shape
    mod = _ada(c, ada_w, ada_b)
    sh1, sc1, gt1, sh2, sc2, gt2 = [
        mod[:, i * d:(i + 1) * d].reshape(batch, 1, d) for i in range(N_MOD)]
    row = lambda v: v.reshape(1, -1)

    bs_tile = jnp.repeat(sg_bs.T, SG_GROUP_DIM, axis=1)
    wr_t = w_router.T
    wr_hi = wr_t.astype(BF16)
    wr_lo = (wr_t - wr_hi.astype(F32)).astype(BF16)
    t_out = lax.broadcasted_iota(I32, (MIXER_TS, MIXER_TS), 0)
    t_in = lax.broadcasted_iota(I32, (MIXER_TS, MIXER_TS), 1)
    shifts = jnp.stack([(t_out - t_in == s) for s in range(1, CONV_WIDTH)]).astype(BF16)
    mixer_weights = (
        conv_w, row(conv_b), _block_diag(lru_wa).astype(BF16), row(lru_ba),
        _block_diag(lru_wx).astype(BF16), row(lru_bx), row(lru_lam), row(sg_ln_g), row(sg_ln_b),
        sg_ws, bs_tile, w_br_rnn.astype(BF16), w_br_sg.astype(BF16), w_out.astype(BF16),
        row(norm2_g), jnp.concatenate([wr_hi, wr_lo], axis=0), b_router.reshape(N_EXPERTS, 1), shifts)
    w_in_bf = w_in.astype(BF16)

    groups = TOKEN_GROUPS if batch % TOKEN_GROUPS == 0 and N_EXPERTS % TOKEN_GROUPS == 0 else 1
    batches = batch // groups
    n_cast = N_EXPERTS // groups
    n = batches * seq
    mixed, cast = [], None
    for grp in range(groups):
        *outs, w_gu_bf, w_down_bf = _mixer(
            x2d, row(norm1_g), sc1, sh1, w_in_bf, gt1, sc2, sh2, *mixer_weights, w_gu, w_down,
            grp * batches, batches, seq, grp * n_cast, n_cast, cast)
        cast = (w_gu_bf, w_down_bf)
        mixed.append(outs)

    out = None
    for grp in range(groups):
        batch0 = grp * batches
        x2, h2a, h2b, top_idx, top_w, rank, counts = mixed[grp]

        cap = n * TOP_K + N_EXPERTS * MOE_BM
        nb = cap // MOE_BM
        cnt = counts[:, 0].astype(I32)
        padded = ((cnt + MOE_BM - 1) // MOE_BM) * MOE_BM
        padded_end = jnp.cumsum(padded)
        padded_start = padded_end - padded
        pos = _slot_positions(padded_start, top_idx, rank)
        blk_row0 = jnp.arange(nb, dtype=I32) * MOE_BM
        block_e = jnp.minimum(
            jnp.sum((padded_end[None, :] <= blk_row0[:, None]).astype(I32), axis=1), N_EXPERTS - 1)
        n_valid = jnp.clip(cnt[block_e] - (blk_row0 - padded_start[block_e]), 0, MOE_BM).astype(I32)

        xa = _sc_scatter_rows(h2a, pos, cap)
        xb = _sc_scatter_rows(h2b, pos, cap)
        ya, yb = _experts(block_e, n_valid, xa, xb, cast[0].reshape(w_gu.shape),
                          b_gu.reshape(N_EXPERTS, 1, -1), cast[1].reshape(w_down.shape),
                          b_down.reshape(N_EXPERTS, 1, -1))
        flat_pos = pos.reshape(1, -1)
        yga = _sc_gather_rows(ya, flat_pos).reshape(TOP_K, n, PACK_W)
        ygb = _sc_gather_rows(yb, flat_pos).reshape(TOP_K, n, PACK_W)
        out = _final(x2, yga, ygb, top_w, gt2, final_g.reshape(1, d), seq, batch0, n_total, out)
    return out


def kernel(x, c, ada_w, ada_b, norm1_g, w_in, conv_w, conv_b, lru_wa, lru_ba, lru_wx, lru_bx,
           lru_lam, sg_ln_g, sg_ln_b, sg_ws, sg_bs, w_br_rnn, w_br_sg, w_out, norm2_g,
           w_router, b_router, w_gu, b_gu, w_down, b_down, final_g):
    batch, seq, d = x.shape
    depth = ada_w.shape[0]
    assert depth == 1, "the combine is fused with the final norm, which follows the only layer"
    x2d = x.reshape(batch * seq, d)
    l = 0
    out = _layer(
        x2d, c, batch, seq, ada_w[l], ada_b[l], norm1_g[l], w_in[l], conv_w[l], conv_b[l],
        lru_wa[l], lru_ba[l], lru_wx[l], lru_bx[l], lru_lam[l], sg_ln_g[l], sg_ln_b[l],
        sg_ws[l], sg_bs[l], w_br_rnn[l], w_br_sg[l], w_out[l], norm2_g[l], w_router[l],
        b_router[l], w_gu[l], b_gu[l], w_down[l], b_down[l], final_g)
    return out.reshape(batch, seq, d)
```

```python
import functools

import jax
import jax.numpy as jnp
from jax import lax
from jax.experimental import pallas as pl
from jax.experimental.pallas import tpu as pltpu
from jax.experimental.pallas import tpu_sc as plsc

F32 = jnp.float32
BF16 = jnp.bfloat16
I32 = jnp.int32

D_MODEL = 1024
D_RNN = 1024
RNN_HEADS = 16
RNN_HEAD_DIM = D_RNN // RNN_HEADS
CONV_WIDTH = 4
LRU_C = 8.0
D_SG = 1024
SG_GROUPS = 8
SG_GROUP_DIM = D_SG // SG_GROUPS
SG_CHUNK = 128
N_EXPERTS = 32
TOP_K = 4
D_EXPERT = 1024
SWIGLU_LIMIT = 7.0
SWIGLU_ALPHA = 1.702
EPS = 1e-6
N_MOD = 6
D_IN = 2 * D_RNN + 2 * D_SG + 2 * D_MODEL

SUBLANES = 8
LANES = 128
GATE_BLOCK = 256
N_GATE_BLOCKS = D_RNN // GATE_BLOCK

ADA_TN = 1536
INPROJ_TN = 1024
MIXER_TS = 256
MOE_BM = 512
MOE_HALF = MOE_BM // 2
TOKEN_GROUPS = 2
SC_WINDOW = 128
PACK_W = D_MODEL // 4
FINAL_TM = 512
POS_TN = 8192
VMEM_LIMIT = 58 * 1024 * 1024


def _sigmoid(x):
    return 0.5 * jnp.tanh(0.5 * x) + 0.5


def _gelu_tanh(x):
    k = 0.7978845608028654
    hx = 0.5 * x
    return hx + hx * jnp.tanh(x * (k + (k * 0.044715) * (x * x)))


def _bdot(a, b):
    return jnp.dot(a, b, preferred_element_type=F32)


def _pack_halves(v):
    word = pltpu.pack_elementwise([v[:, 2 * PACK_W:], v[:, :2 * PACK_W]], packed_dtype=BF16)
    return word[:, :PACK_W], word[:, PACK_W:]


def _unpack_halves(wa, wb):
    part = lambda w, i: pltpu.unpack_elementwise(w, index=i, packed_dtype=BF16, unpacked_dtype=F32)
    return jnp.concatenate([part(wa, 1), part(wb, 1), part(wa, 0), part(wb, 0)], axis=1)


def _ada_kernel(c_ref, w_ref, b_ref, o_ref):
    c = c_ref[...]
    s = c * _sigmoid(c)
    o_ref[...] = jnp.dot(s, w_ref[...], preferred_element_type=F32,
                         precision=lax.Precision.HIGHEST) + b_ref[...]


def _ada(c, ada_w, ada_b):
    b, d = c.shape
    n = ada_w.shape[1]
    return pl.pallas_call(
        _ada_kernel,
        grid=(n // ADA_TN,),
        in_specs=[
            pl.BlockSpec((b, d), lambda j: (0, 0)),
            pl.BlockSpec((d, ADA_TN), lambda j: (0, j)),
            pl.BlockSpec((1, ADA_TN), lambda j: (0, j)),
        ],
        out_specs=pl.BlockSpec((b, ADA_TN), lambda j: (0, j)),
        out_shape=jax.ShapeDtypeStruct((b, n), F32),
        compiler_params=pltpu.CompilerParams(
            dimension_semantics=("arbitrary",), vmem_limit_bytes=VMEM_LIMIT),
        name="ada",
    )(c, ada_w, ada_b.reshape(1, n))


def _norm_mod(x, g, sc, sh):
    ms = jnp.mean(x * x, axis=-1, keepdims=True)
    y = x * lax.rsqrt(ms + EPS)
    return (y * g) * (1.0 + sc) + sh


N_MIXER_INPUTS = 31


def _mixer_kernel(*refs, pairs_per_seq, n_aliased):
    (x_ref, xn_ref, n1g_ref, sc1_ref, sh1_ref, sc1n_ref, sh1n_ref, win_ref,
     gt1_ref, sc2_ref, sh2_ref,
     convw_ref, convb_ref, wa_ref, ba_ref, wx_ref, bx_ref, lam_ref,
     lng_ref, lnb_ref, ws_ref, bs_ref, wbr_ref, wbs_ref, wout_ref,
     n2g_ref, wr_ref, br_ref, shift_ref, wgu_ref, wd_ref) = refs[:N_MIXER_INPUTS]
    (x2_ref, h2a_ref, h2b_ref, idx_ref, tw_ref, rank_ref, cnt_ref, wgu_bf_ref, wd_bf_ref,
     z0_ref, z1_ref, hn0_ref, hn1_ref, xp_ref, xc_ref, a_ref, hh_ref, sv_ref,
     hstate_ref) = refs[N_MIXER_INPUTS + n_aliased:]
    ts = MIXER_TS
    s = pl.program_id(0)

    wgu_bf_ref[...] = wgu_ref[...].astype(BF16)
    wd_bf_ref[...] = wd_ref[...].astype(BF16)

    def inproj_norm(hn_ref, x_rows, sc, sh):
        hn_ref[...] = _norm_mod(x_rows, n1g_ref[...], sc, sh).astype(BF16)

    def inproj_chunks(z_dst, hn_ref, first, last):
        for c in range(first * INPROJ_TN, last * INPROJ_TN, INPROJ_TN):
            z_dst[:, c:c + INPROJ_TN] = _bdot(hn_ref[...], win_ref[:, c:c + INPROJ_TN]).astype(BF16)

    @pl.when(s == 0)
    def _():
        cnt_ref[...] = jnp.zeros_like(cnt_ref)
        inproj_norm(hn0_ref, x_ref[0:ts, :], sc1_ref[0], sh1_ref[0])
        inproj_chunks(z0_ref, hn0_ref, 0, D_IN // INPROJ_TN)

    @pl.when(s % pairs_per_seq == 0)
    def _():
        xp_ref[0:SUBLANES, :] = jnp.zeros((SUBLANES, D_RNN), F32)
        hstate_ref[...] = jnp.zeros_like(hstate_ref)

    def tile(k, z_ref, z_next_ref, hn_ref, next_rows, next_sc, next_sh):
        rows = slice(k * ts, (k + 1) * ts)
        inproj_norm(hn_ref, next_rows, next_sc, next_sh)

        x16 = z_ref[:, 0:D_RNN]
        rnn_x = x16.astype(F32)
        cw = convw_ref[...]
        xc = cw[3:4] * rnn_x + convb_ref[...]
        for sft in range(1, CONV_WIDTH):
            xc = xc + cw[3 - sft:4 - sft] * _bdot(shift_ref[sft - 1], x16)
        xc_ref[...] = xc
        xp_ref[SUBLANES:2 * SUBLANES, :] = rnn_x[0:SUBLANES, :]
        xc_ref[0:SUBLANES, :] = (
            cw[3:4] * xp_ref[SUBLANES:2 * SUBLANES, :]
            + cw[2:3] * xp_ref[SUBLANES - 1:2 * SUBLANES - 1, :]
            + cw[1:2] * xp_ref[SUBLANES - 2:2 * SUBLANES - 2, :]
            + cw[0:1] * xp_ref[SUBLANES - 3:2 * SUBLANES - 3, :]) + convb_ref[...]
        xp_ref[0:SUBLANES, :] = rnn_x[ts - SUBLANES:ts, :]
        xc = xc_ref[...]

        xcb = xc.astype(BF16)
        r_parts, i_parts = [], []
        for g in range(N_GATE_BLOCKS):
            blk = xcb[:, g * GATE_BLOCK:(g + 1) * GATE_BLOCK]
            r_parts.append(_bdot(blk, wa_ref[g]))
            i_parts.append(_bdot(blk, wx_ref[g]))
        inproj_chunks(z_next_ref, hn_ref, 0, 3)
        r = _sigmoid(jnp.concatenate(r_parts, axis=1) + ba_ref[...])
        ig = _sigmoid(jnp.concatenate(i_parts, axis=1) + bx_ref[...])

        nl = -lam_ref[...]
        softplus = jnp.maximum(nl, 0.0) + jnp.log(1.0 + jnp.exp(-jnp.abs(nl)))
        a = jnp.exp(r * ((-LRU_C) * softplus))
        t = 1.0 - a * a
        u = jnp.where(t > 0.0, t * lax.rsqrt(t), 0.0) * (ig * xc)

        groups = ts // SUBLANES
        a3 = a.reshape(groups, SUBLANES, D_RNN)
        h3 = u.reshape(groups, SUBLANES, D_RNN)
        sub = lax.broadcasted_iota(I32, (groups, SUBLANES, D_RNN), 1)
        for step in (1, 2, 4):
            keep = sub >= step
            a_sh = jnp.where(keep, pltpu.roll(a3, step, 1), 1.0)
            h_sh = jnp.where(keep, pltpu.roll(h3, step, 1), 0.0)
            h3 = h3 + a3 * h_sh
            a3 = a3 * a_sh
        a_ref[...] = a3.reshape(ts, D_RNN)
        hh_ref[...] = h3.reshape(ts, D_RNN)

        def carry_body(gi, hc):
            grp = pl.ds(pl.multiple_of(gi * SUBLANES, SUBLANES), SUBLANES)
            hg = hh_ref[grp, :] + a_ref[grp, :] * hc
            hh_ref[grp, :] = hg
            return jnp.broadcast_to(hg[SUBLANES - 1:SUBLANES, :], (SUBLANES, D_RNN))

        hstate_ref[...] = lax.fori_loop(0, groups, carry_body, hstate_ref[...])

        inproj_chunks(z_next_ref, hn_ref, 3, 4)
        y_rnn = (hh_ref[...] * _gelu_tanh(z_ref[:, D_RNN:2 * D_RNN].astype(F32))).astype(BF16)

        gv = _gelu_tanh(z_ref[:, 2 * D_RNN + D_SG:2 * D_RNN + 2 * D_SG].astype(F32))
        mu = jnp.mean(gv, axis=-1, keepdims=True)
        dv = gv - mu
        var = jnp.mean(dv * dv, axis=-1, keepdims=True)
        vn = (dv * lax.rsqrt(var + EPS) * lng_ref[...] + lnb_ref[...]).astype(BF16)
        tr = lax.broadcasted_iota(I32, (SG_CHUNK, SG_CHUNK), 0)
        tc = lax.broadcasted_iota(I32, (SG_CHUNK, SG_CHUNK), 1)
        causal = tc <= tr
        for g in range(SG_GROUPS):
            wg = jnp.where(causal, ws_ref[g], 0.0).astype(BF16)
            cols = slice(g * SG_GROUP_DIM, (g + 1) * SG_GROUP_DIM)
            for n in range(ts // SG_CHUNK):
                chunk = slice(n * SG_CHUNK, (n + 1) * SG_CHUNK)
                sv_ref[chunk, cols] = _bdot(wg, vn[chunk, cols]) + bs_ref[:, cols]
        inproj_chunks(z_next_ref, hn_ref, 4, 5)
        gu = _gelu_tanh(z_ref[:, 2 * D_RNN:2 * D_RNN + D_SG].astype(F32))
        y_sg = (gu * sv_ref[...]).astype(BF16)

        g_rnn = z_ref[:, 2 * D_RNN + 2 * D_SG:2 * D_RNN + 2 * D_SG + D_MODEL].astype(F32)
        g_sg = z_ref[:, 2 * D_RNN + 2 * D_SG + D_MODEL:D_IN].astype(F32)
        m = (_sigmoid(g_rnn) * _bdot(y_rnn, wbr_ref[...])
             + _sigmoid(g_sg) * _bdot(y_sg, wbs_ref[...])).astype(BF16)
        x2 = x_ref[rows, :] + gt1_ref[0] * _bdot(m, wout_ref[...])
        x2_ref[rows, :] = x2
        inproj_chunks(z_next_ref, hn_ref, 5, 6)

        h2 = _norm_mod(x2, n2g_ref[...], sc2_ref[0], sh2_ref[0])
        h2a_ref[rows, :], h2b_ref[rows, :] = _pack_halves(h2)
        h_hi = h2.astype(BF16)
        h_lo = (h2 - h_hi.astype(F32)).astype(BF16)
        nt_dims = (((1,), (1,)), ((), ()))
        by_hi = lax.dot_general(wr_ref[...], h_hi, nt_dims, preferred_element_type=F32)
        logits = (by_hi[:N_EXPERTS] + by_hi[N_EXPERTS:]
                  + lax.dot_general(wr_ref[0:N_EXPERTS, :], h_lo, nt_dims,
                                    preferred_element_type=F32)
                  + br_ref[...])
        e_iota = lax.broadcasted_iota(I32, (N_EXPERTS, ts), 0)
        v = logits
        vals, idxs, sels = [], [], []
        for _ in range(TOP_K):
            mx = jnp.max(v, axis=0, keepdims=True)
            ik = jnp.min(jnp.where(v == mx, e_iota, N_EXPERTS), axis=0, keepdims=True)
            sel = e_iota == ik
            v = jnp.where(sel, -jnp.inf, v)
            vals.append(mx)
            idxs.append(ik)
            sels.append(sel)
        exps = [jnp.exp(val - vals[0]) for val in vals]
        denom = exps[0] + exps[1] + exps[2] + exps[3]
        idx_ref[:, rows] = jnp.concatenate(idxs, axis=0)
        tw_ref[:, rows] = jnp.concatenate([e / denom for e in exps], axis=0)

        onehot = jnp.zeros((N_EXPERTS, ts), F32)
        for sel in sels:
            onehot = jnp.where(sel, 1.0, onehot)
        sr = lax.broadcasted_iota(I32, (ts, ts), 0)
        st = lax.broadcasted_iota(I32, (ts, ts), 1)
        before = jnp.where(sr < st, 1.0, 0.0).astype(BF16)
        total = cnt_ref[...] + _bdot(onehot.astype(BF16), before)
        ranks = [jnp.sum(jnp.where(sel, total, 0.0), axis=0, keepdims=True) for sel in sels]
        rank_ref[:, rows] = jnp.concatenate(ranks, axis=0).astype(I32)
        cnt_ref[...] = cnt_ref[...] + jnp.sum(onehot, axis=1, keepdims=True)

    tile(0, z0_ref, z1_ref, hn1_ref, x_ref[ts:2 * ts, :], sc1_ref[0], sh1_ref[0])
    tile(1, z1_ref, z0_ref, hn0_ref, xn_ref[...], sc1n_ref[0], sh1n_ref[0])


def _mixer(x2d, n1g, sc1, sh1, w_in, gt1, sc2, sh2, conv_w, conv_b, wa_bd, ba, wx_bd, bx, lam,
           ln_g, ln_b, ws, bs_tile, wbr, wbs, wout, n2g, wr_split, br, shifts, w_gu, w_down,
           batch0, batches, seq, expert0, n_cast, cast_prev):
    d = x2d.shape[1]
    ts = MIXER_TS
    n = batches * seq
    tiles_per_seq = seq // ts
    pairs_per_seq = tiles_per_seq // 2
    assert tiles_per_seq % 2 == 0
    steps = batches * pairs_per_seq
    pair0 = batch0 * pairs_per_seq
    next_tile = lambda s: jnp.minimum(2 * s + 2, 2 * steps - 1)
    pair = lambda s: (s, 0)
    pairt = lambda s: (0, s)
    bvec = lambda s: (batch0 + s // pairs_per_seq, 0, 0)
    bvec_next = lambda s: (batch0 + next_tile(s) // tiles_per_seq, 0, 0)
    n_e, k_gu, n_gu = w_gu.shape
    _, k_d, n_d = w_down.shape
    assert (n_cast * k_gu) % steps == 0 and (n_cast * k_d) % steps == 0
    rows_gu, rows_d = n_cast * k_gu // steps, n_cast * k_d // steps
    cast_gu = lambda s: (expert0 * k_gu // rows_gu + s, 0)
    cast_d = lambda s: (expert0 * k_d // rows_d + s, 0)
    w_gu = w_gu.reshape(n_e * k_gu, n_gu)
    w_down = w_down.reshape(n_e * k_d, n_d)
    c2 = lambda s: (0, 0)
    c3 = lambda s: (0, 0, 0)
    in_specs = [
        pl.BlockSpec((2 * ts, d), lambda s: (pair0 + s, 0)),
        pl.BlockSpec((ts, d), lambda s: (2 * pair0 + next_tile(s), 0)),
        pl.BlockSpec((1, d), c2),
        pl.BlockSpec((1, 1, d), bvec),
        pl.BlockSpec((1, 1, d), bvec),
        pl.BlockSpec((1, 1, d), bvec_next),
        pl.BlockSpec((1, 1, d), bvec_next),
        pl.BlockSpec((d, D_IN), c2),
        pl.BlockSpec((1, 1, d), bvec),
        pl.BlockSpec((1, 1, d), bvec),
        pl.BlockSpec((1, 1, d), bvec),
        pl.BlockSpec((CONV_WIDTH, D_RNN), c2),
        pl.BlockSpec((1, D_RNN), c2),
        pl.BlockSpec((N_GATE_BLOCKS, GATE_BLOCK, GATE_BLOCK), c3),
        pl.BlockSpec((1, D_RNN), c2),
        pl.BlockSpec((N_GATE_BLOCKS, GATE_BLOCK, GATE_BLOCK), c3),
        pl.BlockSpec((1, D_RNN), c2),
        pl.BlockSpec((1, D_RNN), c2),
        pl.BlockSpec((1, D_SG), c2),
        pl.BlockSpec((1, D_SG), c2),
        pl.BlockSpec((SG_GROUPS, SG_CHUNK, SG_CHUNK), c3),
        pl.BlockSpec((SG_CHUNK, D_SG), c2),
        pl.BlockSpec((D_RNN, d), c2),
        pl.BlockSpec((D_SG, d), c2),
        pl.BlockSpec((d, d), c2),
        pl.BlockSpec((1, d), c2),
        pl.BlockSpec((2 * N_EXPERTS, d), c2),
        pl.BlockSpec((N_EXPERTS, 1), c2),
        pl.BlockSpec((CONV_WIDTH - 1, ts, ts), c3),
        pl.BlockSpec((rows_gu, n_gu), cast_gu),
        pl.BlockSpec((rows_d, n_d), cast_d),
    ]
    assert len(in_specs) == N_MIXER_INPUTS
    args = [x2d, x2d, n1g, sc1, sh1, sc1, sh1, w_in, gt1, sc2, sh2, conv_w, conv_b, wa_bd, ba,
            wx_bd, bx, lam, ln_g, ln_b, ws, bs_tile, wbr, wbs, wout, n2g, wr_split, br, shifts,
            w_gu, w_down]
    aliases = {}
    if cast_prev is not None:
        for j, prev in enumerate(cast_prev):
            in_specs.append(pl.BlockSpec(memory_space=pl.ANY))
            aliases[len(args)] = 7 + j
            args.append(prev)
    out_specs = [
        pl.BlockSpec((2 * ts, d), pair),
        pl.BlockSpec((2 * ts, PACK_W), pair),
        pl.BlockSpec((2 * ts, PACK_W), pair),
        pl.BlockSpec((TOP_K, 2 * ts), pairt),
        pl.BlockSpec((TOP_K, 2 * ts), pairt),
        pl.BlockSpec((TOP_K, 2 * ts), pairt),
        pl.BlockSpec((N_EXPERTS, 1), c2),
        pl.BlockSpec((rows_gu, n_gu), cast_gu),
        pl.BlockSpec((rows_d, n_d), cast_d),
    ]
    out_shape = [
        jax.ShapeDtypeStruct((n, d), F32),
        jax.ShapeDtypeStruct((n, PACK_W), jnp.uint32),
        jax.ShapeDtypeStruct((n, PACK_W), jnp.uint32),
        jax.ShapeDtypeStruct((TOP_K, n), I32),
        jax.ShapeDtypeStruct((TOP_K, n), F32),
        jax.ShapeDtypeStruct((TOP_K, n), I32),
        jax.ShapeDtypeStruct((N_EXPERTS, 1), F32),
        jax.ShapeDtypeStruct(w_gu.shape, BF16),
        jax.ShapeDtypeStruct(w_down.shape, BF16),
    ]
    scratch = [
        pltpu.VMEM((ts, D_IN), BF16),
        pltpu.VMEM((ts, D_IN), BF16),
        pltpu.VMEM((ts, d), BF16),
        pltpu.VMEM((ts, d), BF16),
        pltpu.VMEM((2 * SUBLANES, D_RNN), F32),
        pltpu.VMEM((ts, D_RNN), F32),
        pltpu.VMEM((ts, D_RNN), F32),
        pltpu.VMEM((ts, D_RNN), F32),
        pltpu.VMEM((ts, D_SG), F32),
        pltpu.VMEM((SUBLANES, D_RNN), F32),
    ]
    return pl.pallas_call(
        functools.partial(_mixer_kernel, pairs_per_seq=pairs_per_seq, n_aliased=len(aliases)),
        grid=(steps,),
        in_specs=in_specs,
        out_specs=out_specs,
        out_shape=out_shape,
        scratch_shapes=scratch,
        input_output_aliases=aliases,
        compiler_params=pltpu.CompilerParams(
            dimension_semantics=("arbitrary",), vmem_limit_bytes=VMEM_LIMIT),
        name="mixer",
    )(*args)


def _sc_mesh():
    return plsc.VectorSubcoreMesh(core_axis_name="core", subcore_axis_name="subcore")


def _sc_scatter_rows(rows, pos, cap):
    n, d = rows.shape
    kk = pos.shape[0]

    @functools.partial(
        pl.kernel, out_type=jax.ShapeDtypeStruct((cap, d), rows.dtype), mesh=_sc_mesh(),
        scratch_types=[], name="sc_scatter_rows")
    def scatter(x_hbm, i_hbm, o_hbm):
        def body(x_vmem, i_vmem):
            pltpu.sync_copy(x_vmem, o_hbm.at[i_vmem.at[0]])

        pltpu.emit_pipeline(
            body,
            grid=(n // SC_WINDOW, kk),
            in_specs=[pl.BlockSpec((SC_WINDOW, d), lambda i, k: (i, 0)),
                      pl.BlockSpec((1, SC_WINDOW), lambda i, k: (k, i))],
            out_specs=[],
            core_axis_name=("core", "subcore"),
            dimension_semantics=(pltpu.PARALLEL, pltpu.ARBITRARY),
        )(x_hbm, i_hbm)

    return scatter(rows, pos)


def _sc_gather_rows(table, idx):
    m = idx.shape[1]
    d = table.shape[1]

    @functools.partial(
        pl.kernel, out_type=jax.ShapeDtypeStruct((m, d), table.dtype), mesh=_sc_mesh(),
        scratch_types=[], name="sc_gather_rows")
    def gather(x_hbm, i_hbm, o_hbm):
        def body(i_vmem, o_vmem):
            pltpu.sync_copy(x_hbm.at[i_vmem.at[0]], o_vmem)

        pltpu.emit_pipeline(
            body,
            grid=(m // SC_WINDOW,),
            in_specs=[pl.BlockSpec((1, SC_WINDOW), lambda i: (0, i))],
            out_specs=[pl.BlockSpec((SC_WINDOW, d), lambda i: (i, 0))],
            core_axis_name=("core", "subcore"),
            dimension_semantics=(pltpu.PARALLEL,),
        )(i_hbm, o_hbm)

    return gather(table, idx)


def _pos_kernel(cnt_ref, idx_ref, rank_ref, pos_ref, be_ref, nv_ref):
    assert MOE_BM & (MOE_BM - 1) == 0
    starts, ends, run = [], [], jnp.int32(0)
    for e in range(N_EXPERTS):
        starts.append(run)
        run = run + ((cnt_ref[e] + (MOE_BM - 1)) & jnp.int32(-MOE_BM))
        ends.append(run)

    idx = idx_ref[...]
    pos = rank_ref[...]
    for e in range(N_EXPERTS):
        pos = pos + jnp.where(idx == e, starts[e], 0)
    pos_ref[...] = pos

    row0 = lax.broadcasted_iota(I32, be_ref.shape, 1) * MOE_BM
    be = jnp.zeros(be_ref.shape, I32)
    for e in range(N_EXPERTS - 1):
        be = be + jnp.where(ends[e] <= row0, 1, 0)
    start = jnp.zeros(be_ref.shape, I32)
    count = jnp.zeros(be_ref.shape, I32)
    for e in range(N_EXPERTS):
        here = be == e
        start = jnp.where(here, starts[e], start)
        count = jnp.where(here, cnt_ref[e], count)
    be_ref[...] = be
    nv_ref[...] = jnp.clip(count - (row0 - start), 0, MOE_BM)


def _slot_plan(counts, top_idx, rank, nb):
    k, n = top_idx.shape
    tn = min(POS_TN, n)
    nbp = -(-nb // LANES) * LANES
    spec = pl.BlockSpec((k, tn), lambda i, c: (0, i))
    blocks = pl.BlockSpec((1, nbp), lambda i, c: (0, 0))
    pos, be, nv = pl.pallas_call(
        _pos_kernel,
        grid_spec=pltpu.PrefetchScalarGridSpec(
            num_scalar_prefetch=1, grid=(n // tn,), in_specs=[spec, spec],
            out_specs=[spec, blocks, blocks]),
        out_shape=[jax.ShapeDtypeStruct((k, n), I32), jax.ShapeDtypeStruct((1, nbp), I32),
                   jax.ShapeDtypeStruct((1, nbp), I32)],
        compiler_params=pltpu.CompilerParams(dimension_semantics=("arbitrary",)),
        name="slot_plan",
    )(counts, top_idx, rank)
    return pos, be.reshape(nbp), nv.reshape(nbp)


def _expert_kernel(be_ref, nv_ref, xa_ref, xb_ref, wgu_ref, bgu_ref, wd_ref, bd_ref,
                   ya_ref, yb_ref):
    nvalid = nv_ref[pl.program_id(0)]

    def mlp_rows(h):
        rows = slice(h * MOE_HALF, (h + 1) * MOE_HALF)
        live = lax.broadcasted_iota(I32, (MOE_HALF, 1), 0) < nvalid - h * MOE_HALF
        xb = jnp.where(live, _unpack_halves(xa_ref[rows, :], xb_ref[rows, :]), 0.0).astype(BF16)
        gu = _bdot(xb, wgu_ref[0]) + bgu_ref[0]
        gate = jnp.minimum(gu[:, :D_EXPERT], SWIGLU_LIMIT)
        up = jnp.clip(gu[:, D_EXPERT:], -SWIGLU_LIMIT, SWIGLU_LIMIT)
        act = (up + 1.0) * (gate * _sigmoid(SWIGLU_ALPHA * gate))
        y = _bdot(act.astype(BF16), wd_ref[0]) + bd_ref[0]
        ya_ref[rows, :], yb_ref[rows, :] = _pack_halves(y)

    def zero_rows(h):
        rows = slice(h * MOE_HALF, (h + 1) * MOE_HALF)
        ya_ref[rows, :] = jnp.zeros((MOE_HALF, PACK_W), jnp.uint32)
        yb_ref[rows, :] = jnp.zeros((MOE_HALF, PACK_W), jnp.uint32)

    @pl.when(nvalid > MOE_HALF)
    def _():
        mlp_rows(0)
        mlp_rows(1)

    @pl.when((nvalid > 0) & (nvalid <= MOE_HALF))
    def _():
        mlp_rows(0)
        zero_rows(1)

    @pl.when(nvalid <= 0)
    def _():
        zero_rows(0)
        zero_rows(1)


def _experts(block_e, n_valid, xa, xb, w_gu, b_gu, w_down, b_down):
    cap = xa.shape[0]
    d = D_MODEL
    nb = cap // MOE_BM
    half = pl.BlockSpec((MOE_BM, PACK_W), lambda i, be, nv: (i, 0))
    grid_spec = pltpu.PrefetchScalarGridSpec(
        num_scalar_prefetch=2,
        grid=(nb,),
        in_specs=[
            half,
            half,
            pl.BlockSpec((1, d, 2 * D_EXPERT), lambda i, be, nv: (be[i], 0, 0)),
            pl.BlockSpec((1, 1, 2 * D_EXPERT), lambda i, be, nv: (be[i], 0, 0)),
            pl.BlockSpec((1, D_EXPERT, d), lambda i, be, nv: (be[i], 0, 0)),
            pl.BlockSpec((1, 1, d), lambda i, be, nv: (be[i], 0, 0)),
        ],
        out_specs=[half, half],
    )
    return pl.pallas_call(
        _expert_kernel,
        grid_spec=grid_spec,
        out_shape=[jax.ShapeDtypeStruct((cap, PACK_W), jnp.uint32)] * 2,
        compiler_params=pltpu.CompilerParams(
            dimension_semantics=("arbitrary",), vmem_limit_bytes=VMEM_LIMIT),
        name="experts",
    )(block_e, n_valid, xa, xb, w_gu, b_gu, w_down, b_down)


def _final_kernel(x2_ref, ya_ref, yb_ref, tw_ref, gt2_ref, fg_ref, *rest):
    o_ref = rest[-1]
    tw = tw_ref[...].T
    moe = tw[:, 0:1] * _unpack_halves(ya_ref[0], yb_ref[0])
    for k in range(1, TOP_K):
        moe = moe + tw[:, k:k + 1] * _unpack_halves(ya_ref[k], yb_ref[k])
    x3 = x2_ref[...] + gt2_ref[0] * moe
    ms = jnp.mean(x3 * x3, axis=-1, keepdims=True)
    o_ref[...] = (x3 * lax.rsqrt(ms + EPS)) * fg_ref[...]


def _final(x2, yga, ygb, tw_tok, gt2, final_g, seq, batch0, n_total, out_prev):
    n, d = x2.shape
    tiles_per_seq = seq // FINAL_TM
    tile0 = batch0 * tiles_per_seq
    half = pl.BlockSpec((TOP_K, FINAL_TM, PACK_W), lambda i: (0, i, 0))
    in_specs = [
        pl.BlockSpec((FINAL_TM, d), lambda i: (i, 0)),
        half,
        half,
        pl.BlockSpec((TOP_K, FINAL_TM), lambda i: (0, i)),
        pl.BlockSpec((1, 1, d), lambda i: (batch0 + i // tiles_per_seq, 0, 0)),
        pl.BlockSpec((1, d), lambda i: (0, 0)),
    ]
    args = [x2, yga, ygb, tw_tok, gt2, final_g]
    aliases = {}
    if out_prev is not None:
        in_specs.append(pl.BlockSpec(memory_space=pl.ANY))
        aliases = {len(args): 0}
        args.append(out_prev)
    return pl.pallas_call(
        _final_kernel,
        grid=(n // FINAL_TM,),
        in_specs=in_specs,
        out_specs=pl.BlockSpec((FINAL_TM, d), lambda i: (tile0 + i, 0)),
        out_shape=jax.ShapeDtypeStruct((n_total, d), F32),
        input_output_aliases=aliases,
        compiler_params=pltpu.CompilerParams(
            dimension_semantics=("arbitrary",), vmem_limit_bytes=VMEM_LIMIT),
        name="final",
    )(*args)


def _block_diag(w):
    per = GATE_BLOCK // RNN_HEAD_DIM
    w4 = w.reshape(N_GATE_BLOCKS, per, RNN_HEAD_DIM, RNN_HEAD_DIM)
    eye = jnp.eye(per, dtype=w.dtype)
    bd = jnp.einsum("gpij,pq->gpiqj", w4, eye)
    return bd.reshape(N_GATE_BLOCKS, GATE_BLOCK, GATE_BLOCK)


def _layer(x2d, c, batch, seq, ada_w, ada_b, norm1_g, w_in, conv_w, conv_b, lru_wa, lru_ba,
           lru_wx, lru_bx, lru_lam, sg_ln_g, sg_ln_b, sg_ws, sg_bs, w_br_rnn, w_br_sg, w_out,
           norm2_g, w_router, b_router, w_gu, b_gu, w_down, b_down, final_g):
    n_total, d = x2d.shape
    mod = _ada(c, ada_w, ada_b)
    sh1, sc1, gt1, sh2, sc2, gt2 = [
        mod[:, i * d:(i + 1) * d].reshape(batch, 1, d) for i in range(N_MOD)]
    row = lambda v: v.reshape(1, -1)

    bs_tile = jnp.repeat(sg_bs.T, SG_GROUP_DIM, axis=1)
    wr_t = w_router.T
    wr_hi = wr_t.astype(BF16)
    wr_lo = (wr_t - wr_hi.astype(F32)).astype(BF16)
    t_out = lax.broadcasted_iota(I32, (MIXER_TS, MIXER_TS), 0)
    t_in = lax.broadcasted_iota(I32, (MIXER_TS, MIXER_TS), 1)
    shifts = jnp.stack([(t_out - t_in == s) for s in range(1, CONV_WIDTH)]).astype(BF16)
    mixer_weights = (
        conv_w, row(conv_b), _block_diag(lru_wa).astype(BF16), row(lru_ba),
        _block_diag(lru_wx).astype(BF16), row(lru_bx), row(lru_lam), row(sg_ln_g), row(sg_ln_b),
        sg_ws, bs_tile, w_br_rnn.astype(BF16), w_br_sg.astype(BF16), w_out.astype(BF16),
        row(norm2_g), jnp.concatenate([wr_hi, wr_lo], axis=0), b_router.reshape(N_EXPERTS, 1), shifts)
    w_in_bf = w_in.astype(BF16)

    groups = TOKEN_GROUPS if batch % TOKEN_GROUPS == 0 and N_EXPERTS % TOKEN_GROUPS == 0 else 1
    batches = batch // groups
    n_cast = N_EXPERTS // groups
    n = batches * seq
    mixed, cast = [], None
    for grp in range(groups):
        *outs, w_gu_bf, w_down_bf = _mixer(
            x2d, row(norm1_g), sc1, sh1, w_in_bf, gt1, sc2, sh2, *mixer_weights, w_gu, w_down,
            grp * batches, batches, seq, grp * n_cast, n_cast, cast)
        cast = (w_gu_bf, w_down_bf)
        mixed.append(outs)

    out = None
    for grp in range(groups):
        batch0 = grp * batches
        x2, h2a, h2b, top_idx, top_w, rank, counts = mixed[grp]

        cap = n * TOP_K + N_EXPERTS * MOE_BM
        nb = cap // MOE_BM
        pos, block_e, n_valid = _slot_plan(counts[:, 0].astype(I32), top_idx, rank, nb)

        xa = _sc_scatter_rows(h2a, pos, cap)
        xb = _sc_scatter_rows(h2b, pos, cap)
        ya, yb = _experts(block_e, n_valid, xa, xb, cast[0].reshape(w_gu.shape),
                          b_gu.reshape(N_EXPERTS, 1, -1), cast[1].reshape(w_down.shape),
                          b_down.reshape(N_EXPERTS, 1, -1))
        flat_pos = pos.reshape(1, -1)
        yga = _sc_gather_rows(ya, flat_pos).reshape(TOP_K, n, PACK_W)
        ygb = _sc_gather_rows(yb, flat_pos).reshape(TOP_K, n, PACK_W)
        out = _final(x2, yga, ygb, top_w, gt2, final_g.reshape(1, d), seq, batch0, n_total, out)
    return out


def kernel(x, c, ada_w, ada_b, norm1_g, w_in, conv_w, conv_b, lru_wa, lru_ba, lru_wx, lru_bx,
           lru_lam, sg_ln_g, sg_ln_b, sg_ws, sg_bs, w_br_rnn, w_br_sg, w_out, norm2_g,
           w_router, b_router, w_gu, b_gu, w_down, b_down, final_g):
    batch, seq, d = x.shape
    depth = ada_w.shape[0]
    assert depth == 1, "the combine is fused with the final norm, which follows the only layer"
    x2d = x.reshape(batch * seq, d)
    l = 0
    out = _layer(
        x2d, c, batch, seq, ada_w[l], ada_b[l], norm1_g[l], w_in[l], conv_w[l], conv_b[l],
        lru_wa[l], lru_ba[l], lru_wx[l], lru_bx[l], lru_lam[l], sg_ln_g[l], sg_ln_b[l],
        sg_ws[l], sg_bs[l], w_br_rnn[l], w_br_sg[l], w_out[l], norm2_g[l], w_router[l],
        b_router[l], w_gu[l], b_gu[l], w_down[l], b_down[l], final_g)
    return out.reshape(batch, seq, d)
```

```python
import functools

import jax
import jax.numpy as jnp
from jax import lax
from jax.experimental import pallas as pl
from jax.experimental.pallas import tpu as pltpu
from jax.experimental.pallas import tpu_sc as plsc

F32 = jnp.float32
BF16 = jnp.bfloat16
I32 = jnp.int32

D_MODEL = 1024
D_RNN = 1024
RNN_HEADS = 16
RNN_HEAD_DIM = D_RNN // RNN_HEADS
CONV_WIDTH = 4
LRU_C = 8.0
D_SG = 1024
SG_GROUPS = 8
SG_GROUP_DIM = D_SG // SG_GROUPS
SG_CHUNK = 128
N_EXPERTS = 32
TOP_K = 4
D_EXPERT = 1024
SWIGLU_LIMIT = 7.0
SWIGLU_ALPHA = 1.702
EPS = 1e-6
N_MOD = 6
D_IN = 2 * D_RNN + 2 * D_SG + 2 * D_MODEL

SUBLANES = 8
LANES = 128
GATE_BLOCK = 256
N_GATE_BLOCKS = D_RNN // GATE_BLOCK

ADA_TN = 1536
INPROJ_TN = 1024
MIXER_TS = 256
MOE_BM = 512
MOE_HALF = MOE_BM // 2
TOKEN_GROUPS = 2
SC_WINDOW = 128
PACK_W = D_MODEL // 4
FINAL_TM = 512
POS_TN = 8192
VMEM_LIMIT = 58 * 1024 * 1024


def _sigmoid(x):
    return 0.5 * jnp.tanh(0.5 * x) + 0.5


def _gelu_tanh(x):
    k = 0.7978845608028654
    hx = 0.5 * x
    return hx + hx * jnp.tanh(x * (k + (k * 0.044715) * (x * x)))


def _bdot(a, b):
    return jnp.dot(a, b, preferred_element_type=F32)


def _pack_halves(v):
    word = pltpu.pack_elementwise([v[:, 2 * PACK_W:], v[:, :2 * PACK_W]], packed_dtype=BF16)
    return word[:, :PACK_W], word[:, PACK_W:]


def _unpack_halves(wa, wb):
    part = lambda w, i: pltpu.unpack_elementwise(w, index=i, packed_dtype=BF16, unpacked_dtype=F32)
    return jnp.concatenate([part(wa, 1), part(wb, 1), part(wa, 0), part(wb, 0)], axis=1)


def _ada_kernel(c_ref, w_ref, b_ref, o_ref):
    c = c_ref[...]
    s = c * _sigmoid(c)
    o_ref[...] = jnp.dot(s, w_ref[...], preferred_element_type=F32,
                         precision=lax.Precision.HIGHEST) + b_ref[...]


def _ada(c, ada_w, ada_b):
    b, d = c.shape
    n = ada_w.shape[1]
    return pl.pallas_call(
        _ada_kernel,
        grid=(n // ADA_TN,),
        in_specs=[
            pl.BlockSpec((b, d), lambda j: (0, 0)),
            pl.BlockSpec((d, ADA_TN), lambda j: (0, j)),
            pl.BlockSpec((1, ADA_TN), lambda j: (0, j)),
        ],
        out_specs=pl.BlockSpec((b, ADA_TN), lambda j: (0, j)),
        out_shape=jax.ShapeDtypeStruct((b, n), F32),
        compiler_params=pltpu.CompilerParams(
            dimension_semantics=("arbitrary",), vmem_limit_bytes=VMEM_LIMIT),
        name="ada",
    )(c, ada_w, ada_b.reshape(1, n))


def _norm_mod(x, g, sc, sh):
    ms = jnp.mean(x * x, axis=-1, keepdims=True)
    y = x * lax.rsqrt(ms + EPS)
    return (y * g) * (1.0 + sc) + sh


N_MIXER_INPUTS = 31


def _mixer_kernel(*refs, pairs_per_seq, n_aliased):
    (x_ref, xn_ref, n1g_ref, sc1_ref, sh1_ref, sc1n_ref, sh1n_ref, win_ref,
     gt1_ref, sc2_ref, sh2_ref,
     convw_ref, convb_ref, wa_ref, ba_ref, wx_ref, bx_ref, lam_ref,
     lng_ref, lnb_ref, ws_ref, bs_ref, wbr_ref, wbs_ref, wout_ref,
     n2g_ref, wr_ref, br_ref, shift_ref, wgu_ref, wd_ref) = refs[:N_MIXER_INPUTS]
    (x2_ref, h2a_ref, h2b_ref, idx_ref, tw_ref, rank_ref, cnt_ref, wgu_bf_ref, wd_bf_ref,
     z0_ref, z1_ref, hn0_ref, hn1_ref, xp_ref, xc_ref, a_ref, hh_ref, sv_ref,
     hstate_ref) = refs[N_MIXER_INPUTS + n_aliased:]
    ts = MIXER_TS
    s = pl.program_id(0)

    wgu_bf_ref[...] = wgu_ref[...].astype(BF16)
    wd_bf_ref[...] = wd_ref[...].astype(BF16)

    def inproj_norm(hn_ref, x_rows, sc, sh):
        hn_ref[...] = _norm_mod(x_rows, n1g_ref[...], sc, sh).astype(BF16)

    def inproj_chunks(z_dst, hn_ref, first, last):
        for c in range(first * INPROJ_TN, last * INPROJ_TN, INPROJ_TN):
            z_dst[:, c:c + INPROJ_TN] = _bdot(hn_ref[...], win_ref[:, c:c + INPROJ_TN]).astype(BF16)

    @pl.when(s == 0)
    def _():
        cnt_ref[...] = jnp.zeros_like(cnt_ref)
        inproj_norm(hn0_ref, x_ref[0:ts, :], sc1_ref[0], sh1_ref[0])
        inproj_chunks(z0_ref, hn0_ref, 0, D_IN // INPROJ_TN)

    @pl.when(s % pairs_per_seq == 0)
    def _():
        xp_ref[0:SUBLANES, :] = jnp.zeros((SUBLANES, D_RNN), F32)
        hstate_ref[...] = jnp.zeros_like(hstate_ref)

    def tile(k, z_ref, z_next_ref, hn_ref, next_rows, next_sc, next_sh):
        rows = slice(k * ts, (k + 1) * ts)
        inproj_norm(hn_ref, next_rows, next_sc, next_sh)

        x16 = z_ref[:, 0:D_RNN]
        rnn_x = x16.astype(F32)
        cw = convw_ref[...]
        xc = cw[3:4] * rnn_x + convb_ref[...]
        for sft in range(1, CONV_WIDTH):
            xc = xc + cw[3 - sft:4 - sft] * _bdot(shift_ref[sft - 1], x16)
        xc_ref[...] = xc
        xp_ref[SUBLANES:2 * SUBLANES, :] = rnn_x[0:SUBLANES, :]
        xc_ref[0:SUBLANES, :] = (
            cw[3:4] * xp_ref[SUBLANES:2 * SUBLANES, :]
            + cw[2:3] * xp_ref[SUBLANES - 1:2 * SUBLANES - 1, :]
            + cw[1:2] * xp_ref[SUBLANES - 2:2 * SUBLANES - 2, :]
            + cw[0:1] * xp_ref[SUBLANES - 3:2 * SUBLANES - 3, :]) + convb_ref[...]
        xp_ref[0:SUBLANES, :] = rnn_x[ts - SUBLANES:ts, :]
        xc = xc_ref[...]

        xcb = xc.astype(BF16)
        r_parts, i_parts = [], []
        for g in range(N_GATE_BLOCKS):
            blk = xcb[:, g * GATE_BLOCK:(g + 1) * GATE_BLOCK]
            r_parts.append(_bdot(blk, wa_ref[g]))
            i_parts.append(_bdot(blk, wx_ref[g]))
        inproj_chunks(z_next_ref, hn_ref, 0, 3)
        r = _sigmoid(jnp.concatenate(r_parts, axis=1) + ba_ref[...])
        ig = _sigmoid(jnp.concatenate(i_parts, axis=1) + bx_ref[...])

        nl = -lam_ref[...]
        softplus = jnp.maximum(nl, 0.0) + jnp.log(1.0 + jnp.exp(-jnp.abs(nl)))
        a = jnp.exp(r * ((-LRU_C) * softplus))
        t = 1.0 - a * a
        u = jnp.where(t > 0.0, t * lax.rsqrt(t), 0.0) * (ig * xc)

        groups = ts // SUBLANES
        a3 = a.reshape(groups, SUBLANES, D_RNN)
        h3 = u.reshape(groups, SUBLANES, D_RNN)
        sub = lax.broadcasted_iota(I32, (groups, SUBLANES, D_RNN), 1)
        for step in (1, 2, 4):
            keep = sub >= step
            a_sh = jnp.where(keep, pltpu.roll(a3, step, 1), 1.0)
            h_sh = jnp.where(keep, pltpu.roll(h3, step, 1), 0.0)
            h3 = h3 + a3 * h_sh
            a3 = a3 * a_sh
        a_ref[...] = a3.reshape(ts, D_RNN)
        hh_ref[...] = h3.reshape(ts, D_RNN)

        hc = hstate_ref[...]
        for gi in range(groups):
            grp = slice(gi * SUBLANES, (gi + 1) * SUBLANES)
            hg = hh_ref[grp, :] + a_ref[grp, :] * hc
            hh_ref[grp, :] = hg
            hc = jnp.broadcast_to(hg[SUBLANES - 1:SUBLANES, :], (SUBLANES, D_RNN))
        hstate_ref[...] = hc

        inproj_chunks(z_next_ref, hn_ref, 3, 4)
        y_rnn = (hh_ref[...] * _gelu_tanh(z_ref[:, D_RNN:2 * D_RNN].astype(F32))).astype(BF16)

        gv = _gelu_tanh(z_ref[:, 2 * D_RNN + D_SG:2 * D_RNN + 2 * D_SG].astype(F32))
        mu = jnp.mean(gv, axis=-1, keepdims=True)
        dv = gv - mu
        var = jnp.mean(dv * dv, axis=-1, keepdims=True)
        vn = (dv * lax.rsqrt(var + EPS) * lng_ref[...] + lnb_ref[...]).astype(BF16)
        tr = lax.broadcasted_iota(I32, (SG_CHUNK, SG_CHUNK), 0)
        tc = lax.broadcasted_iota(I32, (SG_CHUNK, SG_CHUNK), 1)
        causal = tc <= tr
        for g in range(SG_GROUPS):
            wg = jnp.where(causal, ws_ref[g], 0.0).astype(BF16)
            cols = slice(g * SG_GROUP_DIM, (g + 1) * SG_GROUP_DIM)
            for n in range(ts // SG_CHUNK):
                chunk = slice(n * SG_CHUNK, (n + 1) * SG_CHUNK)
                sv_ref[chunk, cols] = _bdot(wg, vn[chunk, cols]) + bs_ref[:, cols]
        inproj_chunks(z_next_ref, hn_ref, 4, 5)
        gu = _gelu_tanh(z_ref[:, 2 * D_RNN:2 * D_RNN + D_SG].astype(F32))
        y_sg = (gu * sv_ref[...]).astype(BF16)

        g_rnn = z_ref[:, 2 * D_RNN + 2 * D_SG:2 * D_RNN + 2 * D_SG + D_MODEL].astype(F32)
        g_sg = z_ref[:, 2 * D_RNN + 2 * D_SG + D_MODEL:D_IN].astype(F32)
        m = (_sigmoid(g_rnn) * _bdot(y_rnn, wbr_ref[...])
             + _sigmoid(g_sg) * _bdot(y_sg, wbs_ref[...])).astype(BF16)
        x2 = x_ref[rows, :] + gt1_ref[0] * _bdot(m, wout_ref[...])
        x2_ref[rows, :] = x2
        inproj_chunks(z_next_ref, hn_ref, 5, 6)

        h2 = _norm_mod(x2, n2g_ref[...], sc2_ref[0], sh2_ref[0])
        h2a_ref[rows, :], h2b_ref[rows, :] = _pack_halves(h2)
        h_hi = h2.astype(BF16)
        h_lo = (h2 - h_hi.astype(F32)).astype(BF16)
        nt_dims = (((1,), (1,)), ((), ()))
        by_hi = lax.dot_general(wr_ref[...], h_hi, nt_dims, preferred_element_type=F32)
        logits = (by_hi[:N_EXPERTS] + by_hi[N_EXPERTS:]
                  + lax.dot_general(wr_ref[0:N_EXPERTS, :], h_lo, nt_dims,
                                    preferred_element_type=F32)
                  + br_ref[...])
        e_iota = lax.broadcasted_iota(I32, (N_EXPERTS, ts), 0)
        v = logits
        vals, idxs, sels = [], [], []
        for _ in range(TOP_K):
            mx = jnp.max(v, axis=0, keepdims=True)
            ik = jnp.min(jnp.where(v == mx, e_iota, N_EXPERTS), axis=0, keepdims=True)
            sel = e_iota == ik
            v = jnp.where(sel, -jnp.inf, v)
            vals.append(mx)
            idxs.append(ik)
            sels.append(sel)
        exps = [jnp.exp(val - vals[0]) for val in vals]
        denom = exps[0] + exps[1] + exps[2] + exps[3]
        idx_ref[:, rows] = jnp.concatenate(idxs, axis=0)
        tw_ref[:, rows] = jnp.concatenate([e / denom for e in exps], axis=0)

        onehot = jnp.zeros((N_EXPERTS, ts), F32)
        for sel in sels:
            onehot = jnp.where(sel, 1.0, onehot)
        sr = lax.broadcasted_iota(I32, (ts, ts), 0)
        st = lax.broadcasted_iota(I32, (ts, ts), 1)
        before = jnp.where(sr < st, 1.0, 0.0).astype(BF16)
        total = cnt_ref[...] + _bdot(onehot.astype(BF16), before)
        ranks = [jnp.sum(jnp.where(sel, total, 0.0), axis=0, keepdims=True) for sel in sels]
        rank_ref[:, rows] = jnp.concatenate(ranks, axis=0).astype(I32)
        cnt_ref[...] = cnt_ref[...] + jnp.sum(onehot, axis=1, keepdims=True)

    tile(0, z0_ref, z1_ref, hn1_ref, x_ref[ts:2 * ts, :], sc1_ref[0], sh1_ref[0])
    tile(1, z1_ref, z0_ref, hn0_ref, xn_ref[...], sc1n_ref[0], sh1n_ref[0])


def _mixer(x2d, n1g, sc1, sh1, w_in, gt1, sc2, sh2, conv_w, conv_b, wa_bd, ba, wx_bd, bx, lam,
           ln_g, ln_b, ws, bs_tile, wbr, wbs, wout, n2g, wr_split, br, shifts, w_gu, w_down,
           batch0, batches, seq, expert0, n_cast, cast_prev):
    d = x2d.shape[1]
    ts = MIXER_TS
    n = batches * seq
    tiles_per_seq = seq // ts
    pairs_per_seq = tiles_per_seq // 2
    assert tiles_per_seq % 2 == 0
    steps = batches * pairs_per_seq
    pair0 = batch0 * pairs_per_seq
    next_tile = lambda s: jnp.minimum(2 * s + 2, 2 * steps - 1)
    pair = lambda s: (s, 0)
    pairt = lambda s: (0, s)
    bvec = lambda s: (batch0 + s // pairs_per_seq, 0, 0)
    bvec_next = lambda s: (batch0 + next_tile(s) // tiles_per_seq, 0, 0)
    n_e, k_gu, n_gu = w_gu.shape
    _, k_d, n_d = w_down.shape
    assert (n_cast * k_gu) % steps == 0 and (n_cast * k_d) % steps == 0
    rows_gu, rows_d = n_cast * k_gu // steps, n_cast * k_d // steps
    cast_gu = lambda s: (expert0 * k_gu // rows_gu + s, 0)
    cast_d = lambda s: (expert0 * k_d // rows_d + s, 0)
    w_gu = w_gu.reshape(n_e * k_gu, n_gu)
    w_down = w_down.reshape(n_e * k_d, n_d)
    c2 = lambda s: (0, 0)
    c3 = lambda s: (0, 0, 0)
    in_specs = [
        pl.BlockSpec((2 * ts, d), lambda s: (pair0 + s, 0)),
        pl.BlockSpec((ts, d), lambda s: (2 * pair0 + next_tile(s), 0)),
        pl.BlockSpec((1, d), c2),
        pl.BlockSpec((1, 1, d), bvec),
        pl.BlockSpec((1, 1, d), bvec),
        pl.BlockSpec((1, 1, d), bvec_next),
        pl.BlockSpec((1, 1, d), bvec_next),
        pl.BlockSpec((d, D_IN), c2),
        pl.BlockSpec((1, 1, d), bvec),
        pl.BlockSpec((1, 1, d), bvec),
        pl.BlockSpec((1, 1, d), bvec),
        pl.BlockSpec((CONV_WIDTH, D_RNN), c2),
        pl.BlockSpec((1, D_RNN), c2),
        pl.BlockSpec((N_GATE_BLOCKS, GATE_BLOCK, GATE_BLOCK), c3),
        pl.BlockSpec((1, D_RNN), c2),
        pl.BlockSpec((N_GATE_BLOCKS, GATE_BLOCK, GATE_BLOCK), c3),
        pl.BlockSpec((1, D_RNN), c2),
        pl.BlockSpec((1, D_RNN), c2),
        pl.BlockSpec((1, D_SG), c2),
        pl.BlockSpec((1, D_SG), c2),
        pl.BlockSpec((SG_GROUPS, SG_CHUNK, SG_CHUNK), c3),
        pl.BlockSpec((SG_CHUNK, D_SG), c2),
        pl.BlockSpec((D_RNN, d), c2),
        pl.BlockSpec((D_SG, d), c2),
        pl.BlockSpec((d, d), c2),
        pl.BlockSpec((1, d), c2),
        pl.BlockSpec((2 * N_EXPERTS, d), c2),
        pl.BlockSpec((N_EXPERTS, 1), c2),
        pl.BlockSpec((CONV_WIDTH - 1, ts, ts), c3),
        pl.BlockSpec((rows_gu, n_gu), cast_gu),
        pl.BlockSpec((rows_d, n_d), cast_d),
    ]
    assert len(in_specs) == N_MIXER_INPUTS
    args = [x2d, x2d, n1g, sc1, sh1, sc1, sh1, w_in, gt1, sc2, sh2, conv_w, conv_b, wa_bd, ba,
            wx_bd, bx, lam, ln_g, ln_b, ws, bs_tile, wbr, wbs, wout, n2g, wr_split, br, shifts,
            w_gu, w_down]
    aliases = {}
    if cast_prev is not None:
        for j, prev in enumerate(cast_prev):
            in_specs.append(pl.BlockSpec(memory_space=pl.ANY))
            aliases[len(args)] = 7 + j
            args.append(prev)
    out_specs = [
        pl.BlockSpec((2 * ts, d), pair),
        pl.BlockSpec((2 * ts, PACK_W), pair),
        pl.BlockSpec((2 * ts, PACK_W), pair),
        pl.BlockSpec((TOP_K, 2 * ts), pairt),
        pl.BlockSpec((TOP_K, 2 * ts), pairt),
        pl.BlockSpec((TOP_K, 2 * ts), pairt),
        pl.BlockSpec((N_EXPERTS, 1), c2),
        pl.BlockSpec((rows_gu, n_gu), cast_gu),
        pl.BlockSpec((rows_d, n_d), cast_d),
    ]
    out_shape = [
        jax.ShapeDtypeStruct((n, d), F32),
        jax.ShapeDtypeStruct((n, PACK_W), jnp.uint32),
        jax.ShapeDtypeStruct((n, PACK_W), jnp.uint32),
        jax.ShapeDtypeStruct((TOP_K, n), I32),
        jax.ShapeDtypeStruct((TOP_K, n), F32),
        jax.ShapeDtypeStruct((TOP_K, n), I32),
        jax.ShapeDtypeStruct((N_EXPERTS, 1), F32),
        jax.ShapeDtypeStruct(w_gu.shape, BF16),
        jax.ShapeDtypeStruct(w_down.shape, BF16),
    ]
    scratch = [
        pltpu.VMEM((ts, D_IN), BF16),
        pltpu.VMEM((ts, D_IN), BF16),
        pltpu.VMEM((ts, d), BF16),
        pltpu.VMEM((ts, d), BF16),
        pltpu.VMEM((2 * SUBLANES, D_RNN), F32),
        pltpu.VMEM((ts, D_RNN), F32),
        pltpu.VMEM((ts, D_RNN), F32),
        pltpu.VMEM((ts, D_RNN), F32),
        pltpu.VMEM((ts, D_SG), F32),
        pltpu.VMEM((SUBLANES, D_RNN), F32),
    ]
    return pl.pallas_call(
        functools.partial(_mixer_kernel, pairs_per_seq=pairs_per_seq, n_aliased=len(aliases)),
        grid=(steps,),
        in_specs=in_specs,
        out_specs=out_specs,
        out_shape=out_shape,
        scratch_shapes=scratch,
        input_output_aliases=aliases,
        compiler_params=pltpu.CompilerParams(
            dimension_semantics=("arbitrary",), vmem_limit_bytes=VMEM_LIMIT),
        name="mixer",
    )(*args)


def _sc_mesh():
    return plsc.VectorSubcoreMesh(core_axis_name="core", subcore_axis_name="subcore")


def _sc_scatter_rows(rows, pos, cap):
    n, d = rows.shape
    kk = pos.shape[0]

    @functools.partial(
        pl.kernel, out_type=jax.ShapeDtypeStruct((cap, d), rows.dtype), mesh=_sc_mesh(),
        scratch_types=[], name="sc_scatter_rows")
    def scatter(x_hbm, i_hbm, o_hbm):
        def body(x_vmem, i_vmem):
            pltpu.sync_copy(x_vmem, o_hbm.at[i_vmem.at[0]])

        pltpu.emit_pipeline(
            body,
            grid=(n // SC_WINDOW, kk),
            in_specs=[pl.BlockSpec((SC_WINDOW, d), lambda i, k: (i, 0)),
                      pl.BlockSpec((1, SC_WINDOW), lambda i, k: (k, i))],
            out_specs=[],
            core_axis_name=("core", "subcore"),
            dimension_semantics=(pltpu.PARALLEL, pltpu.ARBITRARY),
        )(x_hbm, i_hbm)

    return scatter(rows, pos)


def _sc_gather_rows(table, idx):
    m = idx.shape[1]
    d = table.shape[1]

    @functools.partial(
        pl.kernel, out_type=jax.ShapeDtypeStruct((m, d), table.dtype), mesh=_sc_mesh(),
        scratch_types=[], name="sc_gather_rows")
    def gather(x_hbm, i_hbm, o_hbm):
        def body(i_vmem, o_vmem):
            pltpu.sync_copy(x_hbm.at[i_vmem.at[0]], o_vmem)

        pltpu.emit_pipeline(
            body,
            grid=(m // SC_WINDOW,),
            in_specs=[pl.BlockSpec((1, SC_WINDOW), lambda i: (0, i))],
            out_specs=[pl.BlockSpec((SC_WINDOW, d), lambda i: (i, 0))],
            core_axis_name=("core", "subcore"),
            dimension_semantics=(pltpu.PARALLEL,),
        )(i_hbm, o_hbm)

    return gather(table, idx)


def _pos_kernel(cnt_ref, idx_ref, rank_ref, pos_ref, be_ref, nv_ref):
    assert MOE_BM & (MOE_BM - 1) == 0
    starts, ends, run = [], [], jnp.int32(0)
    for e in range(N_EXPERTS):
        starts.append(run)
        run = run + ((cnt_ref[e] + (MOE_BM - 1)) & jnp.int32(-MOE_BM))
        ends.append(run)

    idx = idx_ref[...]
    pos = rank_ref[...]
    for e in range(N_EXPERTS):
        pos = pos + jnp.where(idx == e, starts[e], 0)
    pos_ref[...] = pos

    row0 = lax.broadcasted_iota(I32, be_ref.shape, 1) * MOE_BM
    be = jnp.zeros(be_ref.shape, I32)
    for e in range(N_EXPERTS - 1):
        be = be + jnp.where(ends[e] <= row0, 1, 0)
    start = jnp.zeros(be_ref.shape, I32)
    count = jnp.zeros(be_ref.shape, I32)
    for e in range(N_EXPERTS):
        here = be == e
        start = jnp.where(here, starts[e], start)
        count = jnp.where(here, cnt_ref[e], count)
    be_ref[...] = be
    nv_ref[...] = jnp.clip(count - (row0 - start), 0, MOE_BM)


def _slot_plan(counts, top_idx, rank, nb):
    k, n = top_idx.shape
    tn = min(POS_TN, n)
    nbp = -(-nb // LANES) * LANES
    spec = pl.BlockSpec((k, tn), lambda i, c: (0, i))
    blocks = pl.BlockSpec((1, nbp), lambda i, c: (0, 0))
    pos, be, nv = pl.pallas_call(
        _pos_kernel,
        grid_spec=pltpu.PrefetchScalarGridSpec(
            num_scalar_prefetch=1, grid=(n // tn,), in_specs=[spec, spec],
            out_specs=[spec, blocks, blocks]),
        out_shape=[jax.ShapeDtypeStruct((k, n), I32), jax.ShapeDtypeStruct((1, nbp), I32),
                   jax.ShapeDtypeStruct((1, nbp), I32)],
        compiler_params=pltpu.CompilerParams(dimension_semantics=("arbitrary",)),
        name="slot_plan",
    )(counts, top_idx, rank)
    return pos, be.reshape(nbp), nv.reshape(nbp)


def _expert_kernel(be_ref, nv_ref, xa_ref, xb_ref, wgu_ref, bgu_ref, wd_ref, bd_ref,
                   ya_ref, yb_ref):
    nvalid = nv_ref[pl.program_id(0)]

    def mlp_rows(h):
        rows = slice(h * MOE_HALF, (h + 1) * MOE_HALF)
        live = lax.broadcasted_iota(I32, (MOE_HALF, 1), 0) < nvalid - h * MOE_HALF
        xb = jnp.where(live, _unpack_halves(xa_ref[rows, :], xb_ref[rows, :]), 0.0).astype(BF16)
        gu = _bdot(xb, wgu_ref[0]) + bgu_ref[0]
        gate = jnp.minimum(gu[:, :D_EXPERT], SWIGLU_LIMIT)
        up = jnp.clip(gu[:, D_EXPERT:], -SWIGLU_LIMIT, SWIGLU_LIMIT)
        act = (up + 1.0) * (gate * _sigmoid(SWIGLU_ALPHA * gate))
        y = _bdot(act.astype(BF16), wd_ref[0]) + bd_ref[0]
        ya_ref[rows, :], yb_ref[rows, :] = _pack_halves(y)

    def zero_rows(h):
        rows = slice(h * MOE_HALF, (h + 1) * MOE_HALF)
        ya_ref[rows, :] = jnp.zeros((MOE_HALF, PACK_W), jnp.uint32)
        yb_ref[rows, :] = jnp.zeros((MOE_HALF, PACK_W), jnp.uint32)

    @pl.when(nvalid > MOE_HALF)
    def _():
        mlp_rows(0)
        mlp_rows(1)

    @pl.when((nvalid > 0) & (nvalid <= MOE_HALF))
    def _():
        mlp_rows(0)
        zero_rows(1)

    @pl.when(nvalid <= 0)
    def _():
        zero_rows(0)
        zero_rows(1)


def _experts(block_e, n_valid, xa, xb, w_gu, b_gu, w_down, b_down):
    cap = xa.shape[0]
    d = D_MODEL
    nb = cap // MOE_BM
    half = pl.BlockSpec((MOE_BM, PACK_W), lambda i, be, nv: (i, 0))
    grid_spec = pltpu.PrefetchScalarGridSpec(
        num_scalar_prefetch=2,
        grid=(nb,),
        in_specs=[
            half,
            half,
            pl.BlockSpec((1, d, 2 * D_EXPERT), lambda i, be, nv: (be[i], 0, 0)),
            pl.BlockSpec((1, 1, 2 * D_EXPERT), lambda i, be, nv: (be[i], 0, 0)),
            pl.BlockSpec((1, D_EXPERT, d), lambda i, be, nv: (be[i], 0, 0)),
            pl.BlockSpec((1, 1, d), lambda i, be, nv: (be[i], 0, 0)),
        ],
        out_specs=[half, half],
    )
    return pl.pallas_call(
        _expert_kernel,
        grid_spec=grid_spec,
        out_shape=[jax.ShapeDtypeStruct((cap, PACK_W), jnp.uint32)] * 2,
        compiler_params=pltpu.CompilerParams(
            dimension_semantics=("arbitrary",), vmem_limit_bytes=VMEM_LIMIT),
        name="experts",
    )(block_e, n_valid, xa, xb, w_gu, b_gu, w_down, b_down)


def _final_kernel(x2_ref, ya_ref, yb_ref, tw_ref, gt2_ref, fg_ref, *rest):
    o_ref = rest[-1]
    tw = tw_ref[...].T
    moe = tw[:, 0:1] * _unpack_halves(ya_ref[0], yb_ref[0])
    for k in range(1, TOP_K):
        moe = moe + tw[:, k:k + 1] * _unpack_halves(ya_ref[k], yb_ref[k])
    x3 = x2_ref[...] + gt2_ref[0] * moe
    ms = jnp.mean(x3 * x3, axis=-1, keepdims=True)
    o_ref[...] = (x3 * lax.rsqrt(ms + EPS)) * fg_ref[...]


def _final(x2, yga, ygb, tw_tok, gt2, final_g, seq, batch0, n_total, out_prev):
    n, d = x2.shape
    tiles_per_seq = seq // FINAL_TM
    tile0 = batch0 * tiles_per_seq
    half = pl.BlockSpec((TOP_K, FINAL_TM, PACK_W), lambda i: (0, i, 0))
    in_specs = [
        pl.BlockSpec((FINAL_TM, d), lambda i: (i, 0)),
        half,
        half,
        pl.BlockSpec((TOP_K, FINAL_TM), lambda i: (0, i)),
        pl.BlockSpec((1, 1, d), lambda i: (batch0 + i // tiles_per_seq, 0, 0)),
        pl.BlockSpec((1, d), lambda i: (0, 0)),
    ]
    args = [x2, yga, ygb, tw_tok, gt2, final_g]
    aliases = {}
    if out_prev is not None:
        in_specs.append(pl.BlockSpec(memory_space=pl.ANY))
        aliases = {len(args): 0}
        args.append(out_prev)
    return pl.pallas_call(
        _final_kernel,
        grid=(n // FINAL_TM,),
        in_specs=in_specs,
        out_specs=pl.BlockSpec((FINAL_TM, d), lambda i: (tile0 + i, 0)),
        out_shape=jax.ShapeDtypeStruct((n_total, d), F32),
        input_output_aliases=aliases,
        compiler_params=pltpu.CompilerParams(
            dimension_semantics=("arbitrary",), vmem_limit_bytes=VMEM_LIMIT),
        name="final",
    )(*args)


def _block_diag(w):
    per = GATE_BLOCK // RNN_HEAD_DIM
    w4 = w.reshape(N_GATE_BLOCKS, per, RNN_HEAD_DIM, RNN_HEAD_DIM)
    eye = jnp.eye(per, dtype=w.dtype)
    bd = jnp.einsum("gpij,pq->gpiqj", w4, eye)
    return bd.reshape(N_GATE_BLOCKS, GATE_BLOCK, GATE_BLOCK)


def _layer(x2d, c, batch, seq, ada_w, ada_b, norm1_g, w_in, conv_w, conv_b, lru_wa, lru_ba,
           lru_wx, lru_bx, lru_lam, sg_ln_g, sg_ln_b, sg_ws, sg_bs, w_br_rnn, w_br_sg, w_out,
           norm2_g, w_router, b_router, w_gu, b_gu, w_down, b_down, final_g):
    n_total, d = x2d.shape
    mod = _ada(c, ada_w, ada_b)
    sh1, sc1, gt1, sh2, sc2, gt2 = [
        mod[:, i * d:(i + 1) * d].reshape(batch, 1, d) for i in range(N_MOD)]
    row = lambda v: v.reshape(1, -1)

    bs_tile = jnp.repeat(sg_bs.T, SG_GROUP_DIM, axis=1)
    wr_t = w_router.T
    wr_hi = wr_t.astype(BF16)
    wr_lo = (wr_t - wr_hi.astype(F32)).astype(BF16)
    t_out = lax.broadcasted_iota(I32, (MIXER_TS, MIXER_TS), 0)
    t_in = lax.broadcasted_iota(I32, (MIXER_TS, MIXER_TS), 1)
    shifts = jnp.stack([(t_out - t_in == s) for s in range(1, CONV_WIDTH)]).astype(BF16)
    mixer_weights = (
        conv_w, row(conv_b), _block_diag(lru_wa).astype(BF16), row(lru_ba),
        _block_diag(lru_wx).astype(BF16), row(lru_bx), row(lru_lam), row(sg_ln_g), row(sg_ln_b),
        sg_ws, bs_tile, w_br_rnn.astype(BF16), w_br_sg.astype(BF16), w_out.astype(BF16),
        row(norm2_g), jnp.concatenate([wr_hi, wr_lo], axis=0), b_router.reshape(N_EXPERTS, 1), shifts)
    w_in_bf = w_in.astype(BF16)

    groups = TOKEN_GROUPS if batch % TOKEN_GROUPS == 0 and N_EXPERTS % TOKEN_GROUPS == 0 else 1
    batches = batch // groups
    n_cast = N_EXPERTS // groups
    n = batches * seq
    mixed, cast = [], None
    for grp in range(groups):
        *outs, w_gu_bf, w_down_bf = _mixer(
            x2d, row(norm1_g), sc1, sh1, w_in_bf, gt1, sc2, sh2, *mixer_weights, w_gu, w_down,
            grp * batches, batches, seq, grp * n_cast, n_cast, cast)
        cast = (w_gu_bf, w_down_bf)
        mixed.append(outs)

    out = None
    for grp in range(groups):
        batch0 = grp * batches
        x2, h2a, h2b, top_idx, top_w, rank, counts = mixed[grp]

        cap = n * TOP_K + N_EXPERTS * MOE_BM
        nb = cap // MOE_BM
        pos, block_e, n_valid = _slot_plan(counts[:, 0].astype(I32), top_idx, rank, nb)

        xa = _sc_scatter_rows(h2a, pos, cap)
        xb = _sc_scatter_rows(h2b, pos, cap)
        ya, yb = _experts(block_e, n_valid, xa, xb, cast[0].reshape(w_gu.shape),
                          b_gu.reshape(N_EXPERTS, 1, -1), cast[1].reshape(w_down.shape),
                          b_down.reshape(N_EXPERTS, 1, -1))
        flat_pos = pos.reshape(1, -1)
        yga = _sc_gather_rows(ya, flat_pos).reshape(TOP_K, n, PACK_W)
        ygb = _sc_gather_rows(yb, flat_pos).reshape(TOP_K, n, PACK_W)
        out = _final(x2, yga, ygb, top_w, gt2, final_g.reshape(1, d), seq, batch0, n_total, out)
    return out


def kernel(x, c, ada_w, ada_b, norm1_g, w_in, conv_w, conv_b, lru_wa, lru_ba, lru_wx, lru_bx,
           lru_lam, sg_ln_g, sg_ln_b, sg_ws, sg_bs, w_br_rnn, w_br_sg, w_out, norm2_g,
           w_router, b_router, w_gu, b_gu, w_down, b_down, final_g):
    batch, seq, d = x.shape
    depth = ada_w.shape[0]
    assert depth == 1, "the combine is fused with the final norm, which follows the only layer"
    x2d = x.reshape(batch * seq, d)
    l = 0
    out = _layer(
        x2d, c, batch, seq, ada_w[l], ada_b[l], norm1_g[l], w_in[l], conv_w[l], conv_b[l],
        lru_wa[l], lru_ba[l], lru_wx[l], lru_bx[l], lru_lam[l], sg_ln_g[l], sg_ln_b[l],
        sg_ws[l], sg_bs[l], w_br_rnn[l], w_br_sg[l], w_out[l], norm2_g[l], w_router[l],
        b_router[l], w_gu[l], b_gu[l], w_down[l], b_down[l], final_g)
    return out.reshape(batch, seq, d)
```

```python
import functools

import jax
import jax.numpy as jnp
from jax import lax
from jax.experimental import pallas as pl
from jax.experimental.pallas import tpu as pltpu
from jax.experimental.pallas import tpu_sc as plsc

F32 = jnp.float32
BF16 = jnp.bfloat16
I32 = jnp.int32

D_MODEL = 1024
D_RNN = 1024
RNN_HEADS = 16
RNN_HEAD_DIM = D_RNN // RNN_HEADS
CONV_WIDTH = 4
LRU_C = 8.0
D_SG = 1024
SG_GROUPS = 8
SG_GROUP_DIM = D_SG // SG_GROUPS
SG_CHUNK = 128
N_EXPERTS = 32
TOP_K = 4
D_EXPERT = 1024
SWIGLU_LIMIT = 7.0
SWIGLU_ALPHA = 1.702
EPS = 1e-6
N_MOD = 6
D_IN = 2 * D_RNN + 2 * D_SG + 2 * D_MODEL

SUBLANES = 8
LANES = 128
GATE_BLOCK = 256
N_GATE_BLOCKS = D_RNN // GATE_BLOCK

ADA_TN = 1536
INPROJ_TN = 1024
MIXER_TS = 256
MOE_BM = 512
MOE_HALF = MOE_BM // 2
TOKEN_GROUPS = 2
SC_WINDOW = 128
PACK_W = D_MODEL // 4
FINAL_TM = 512
POS_TN = 8192
VMEM_LIMIT = 58 * 1024 * 1024


def _sigmoid(x):
    return 0.5 * jnp.tanh(0.5 * x) + 0.5


def _gelu_tanh_x2(x):
    k = 0.7978845608028654
    return x + x * jnp.tanh(x * (k + (k * 0.044715) * (x * x)))


def _gelu_tanh(x):
    return 0.5 * _gelu_tanh_x2(x)


def _bdot(a, b):
    return jnp.dot(a, b, preferred_element_type=F32)


def _pack_halves(v):
    word = pltpu.pack_elementwise([v[:, 2 * PACK_W:], v[:, :2 * PACK_W]], packed_dtype=BF16)
    return word[:, :PACK_W], word[:, PACK_W:]


def _unpack_halves(wa, wb):
    part = lambda w, i: pltpu.unpack_elementwise(w, index=i, packed_dtype=BF16, unpacked_dtype=F32)
    return jnp.concatenate([part(wa, 1), part(wb, 1), part(wa, 0), part(wb, 0)], axis=1)


def _ada_kernel(c_ref, w_ref, b_ref, o_ref):
    c = c_ref[...]
    s = c * _sigmoid(c)
    o_ref[...] = jnp.dot(s, w_ref[...], preferred_element_type=F32,
                         precision=lax.Precision.HIGHEST) + b_ref[...]


def _ada(c, ada_w, ada_b):
    b, d = c.shape
    n = ada_w.shape[1]
    return pl.pallas_call(
        _ada_kernel,
        grid=(n // ADA_TN,),
        in_specs=[
            pl.BlockSpec((b, d), lambda j: (0, 0)),
            pl.BlockSpec((d, ADA_TN), lambda j: (0, j)),
            pl.BlockSpec((1, ADA_TN), lambda j: (0, j)),
        ],
        out_specs=pl.BlockSpec((b, ADA_TN), lambda j: (0, j)),
        out_shape=jax.ShapeDtypeStruct((b, n), F32),
        compiler_params=pltpu.CompilerParams(
            dimension_semantics=("arbitrary",), vmem_limit_bytes=VMEM_LIMIT),
        name="ada",
    )(c, ada_w, ada_b.reshape(1, n))


def _norm_mod(x, g, sc, sh):
    ms = jnp.mean(x * x, axis=-1, keepdims=True)
    y = x * lax.rsqrt(ms + EPS)
    return y * (g * (1.0 + sc)) + sh


N_MIXER_INPUTS = 31


def _mixer_kernel(*refs, pairs_per_seq, n_aliased):
    (x_ref, xn_ref, n1g_ref, sc1_ref, sh1_ref, sc1n_ref, sh1n_ref, win_ref,
     gt1_ref, sc2_ref, sh2_ref,
     convw_ref, convb_ref, wa_ref, ba_ref, wx_ref, bx_ref, lam_ref,
     lng_ref, lnb_ref, ws_ref, bs_ref, wbr_ref, wbs_ref, wout_ref,
     n2g_ref, wr_ref, br_ref, shift_ref, wgu_ref, wd_ref) = refs[:N_MIXER_INPUTS]
    (x2_ref, h2a_ref, h2b_ref, idx_ref, tw_ref, rank_ref, cnt_ref, wgu_bf_ref, wd_bf_ref,
     z0_ref, z1_ref, hn0_ref, hn1_ref, xp_ref, xc_ref, a_ref, hh_ref, sv_ref,
     hstate_ref) = refs[N_MIXER_INPUTS + n_aliased:]
    ts = MIXER_TS
    s = pl.program_id(0)

    wgu_bf_ref[...] = wgu_ref[...].astype(BF16)
    wd_bf_ref[...] = wd_ref[...].astype(BF16)

    def inproj_norm(hn_ref, x_rows, sc, sh):
        hn_ref[...] = _norm_mod(x_rows, n1g_ref[...], sc, sh).astype(BF16)

    def inproj_chunks(z_dst, hn_ref, first, last):
        for c in range(first * INPROJ_TN, last * INPROJ_TN, INPROJ_TN):
            z_dst[:, c:c + INPROJ_TN] = _bdot(hn_ref[...], win_ref[:, c:c + INPROJ_TN]).astype(BF16)

    @pl.when(s == 0)
    def _():
        cnt_ref[...] = jnp.zeros_like(cnt_ref)
        inproj_norm(hn0_ref, x_ref[0:ts, :], sc1_ref[0], sh1_ref[0])
        inproj_chunks(z0_ref, hn0_ref, 0, D_IN // INPROJ_TN)

    @pl.when(s % pairs_per_seq == 0)
    def _():
        xp_ref[0:SUBLANES, :] = jnp.zeros((SUBLANES, D_RNN), F32)
        hstate_ref[...] = jnp.zeros_like(hstate_ref)

    def tile(k, z_ref, z_next_ref, hn_ref, next_rows, next_sc, next_sh):
        rows = slice(k * ts, (k + 1) * ts)
        inproj_norm(hn_ref, next_rows, next_sc, next_sh)

        x16 = z_ref[:, 0:D_RNN]
        rnn_x = x16.astype(F32)
        cw = convw_ref[...]
        xc = cw[3:4] * rnn_x + convb_ref[...]
        for sft in range(1, CONV_WIDTH):
            xc = xc + cw[3 - sft:4 - sft] * _bdot(shift_ref[sft - 1], x16)
        xc_ref[...] = xc
        xp_ref[SUBLANES:2 * SUBLANES, :] = rnn_x[0:SUBLANES, :]
        xc_ref[0:SUBLANES, :] = (
            cw[3:4] * xp_ref[SUBLANES:2 * SUBLANES, :]
            + cw[2:3] * xp_ref[SUBLANES - 1:2 * SUBLANES - 1, :]
            + cw[1:2] * xp_ref[SUBLANES - 2:2 * SUBLANES - 2, :]
            + cw[0:1] * xp_ref[SUBLANES - 3:2 * SUBLANES - 3, :]) + convb_ref[...]
        xp_ref[0:SUBLANES, :] = rnn_x[ts - SUBLANES:ts, :]
        xc = xc_ref[...]

        xcb = xc.astype(BF16)
        r_parts, i_parts = [], []
        for g in range(N_GATE_BLOCKS):
            blk = xcb[:, g * GATE_BLOCK:(g + 1) * GATE_BLOCK]
            r_parts.append(_bdot(blk, wa_ref[g]))
            i_parts.append(_bdot(blk, wx_ref[g]))
        inproj_chunks(z_next_ref, hn_ref, 0, 3)
        r_t = jnp.tanh(jnp.concatenate(r_parts, axis=1) + ba_ref[...])
        i_t = jnp.tanh(jnp.concatenate(i_parts, axis=1) + bx_ref[...])

        nl = -lam_ref[...]
        softplus = jnp.maximum(nl, 0.0) + jnp.log(1.0 + jnp.exp(-jnp.abs(nl)))
        half_c = (-0.5 * LRU_C) * softplus
        a = jnp.exp(r_t * half_c + half_c)
        t = 1.0 - a * a
        u = jnp.where(t > 0.0, t * lax.rsqrt(t), 0.0) * ((i_t + 1.0) * xc)

        groups = ts // SUBLANES
        a3 = a.reshape(groups, SUBLANES, D_RNN)
        h3 = u.reshape(groups, SUBLANES, D_RNN)
        sub = lax.broadcasted_iota(I32, (groups, SUBLANES, D_RNN), 1)
        for step in (1, 2, 4):
            keep = sub >= step
            a_sh = jnp.where(keep, pltpu.roll(a3, step, 1), 1.0)
            h_sh = jnp.where(keep, pltpu.roll(h3, step, 1), 0.0)
            h3 = h3 + a3 * h_sh
            a3 = a3 * a_sh
        a_ref[...] = a3.reshape(ts, D_RNN)
        hh_ref[...] = h3.reshape(ts, D_RNN)

        hc = hstate_ref[...]
        for gi in range(groups):
            grp = slice(gi * SUBLANES, (gi + 1) * SUBLANES)
            hg = hh_ref[grp, :] + a_ref[grp, :] * hc
            hh_ref[grp, :] = hg
            hc = jnp.broadcast_to(hg[SUBLANES - 1:SUBLANES, :], (SUBLANES, D_RNN))
        hstate_ref[...] = hc

        inproj_chunks(z_next_ref, hn_ref, 3, 4)
        y_rnn = (hh_ref[...] * _gelu_tanh_x2(z_ref[:, D_RNN:2 * D_RNN].astype(F32))).astype(BF16)

        gv = _gelu_tanh(z_ref[:, 2 * D_RNN + D_SG:2 * D_RNN + 2 * D_SG].astype(F32))
        mu = jnp.mean(gv, axis=-1, keepdims=True)
        dv = gv - mu
        var = jnp.mean(dv * dv, axis=-1, keepdims=True)
        vn = (dv * lax.rsqrt(var + EPS) * lng_ref[...] + lnb_ref[...]).astype(BF16)
        tr = lax.broadcasted_iota(I32, (SG_CHUNK, SG_CHUNK), 0)
        tc = lax.broadcasted_iota(I32, (SG_CHUNK, SG_CHUNK), 1)
        causal = tc <= tr
        for g in range(SG_GROUPS):
            wg = jnp.where(causal, ws_ref[g], 0.0).astype(BF16)
            cols = slice(g * SG_GROUP_DIM, (g + 1) * SG_GROUP_DIM)
            for n in range(ts // SG_CHUNK):
                chunk = slice(n * SG_CHUNK, (n + 1) * SG_CHUNK)
                sv_ref[chunk, cols] = _bdot(wg, vn[chunk, cols]) + bs_ref[:, cols]
        inproj_chunks(z_next_ref, hn_ref, 4, 5)
        gu = _gelu_tanh_x2(z_ref[:, 2 * D_RNN:2 * D_RNN + D_SG].astype(F32))
        y_sg = (gu * sv_ref[...]).astype(BF16)

        g_rnn = z_ref[:, 2 * D_RNN + 2 * D_SG:2 * D_RNN + 2 * D_SG + D_MODEL].astype(F32)
        g_sg = z_ref[:, 2 * D_RNN + 2 * D_SG + D_MODEL:D_IN].astype(F32)
        m = ((jnp.tanh(g_rnn) + 1.0) * _bdot(y_rnn, wbr_ref[...])
             + (jnp.tanh(g_sg) + 1.0) * _bdot(y_sg, wbs_ref[...])).astype(BF16)
        x2 = x_ref[rows, :] + gt1_ref[0] * _bdot(m, wout_ref[...])
        x2_ref[rows, :] = x2
        inproj_chunks(z_next_ref, hn_ref, 5, 6)

        h2 = _norm_mod(x2, n2g_ref[...], sc2_ref[0], sh2_ref[0])
        h2a_ref[rows, :], h2b_ref[rows, :] = _pack_halves(h2)
        h_hi = h2.astype(BF16)
        h_lo = (h2 - h_hi.astype(F32)).astype(BF16)
        nt_dims = (((1,), (1,)), ((), ()))
        by_hi = lax.dot_general(wr_ref[...], h_hi, nt_dims, preferred_element_type=F32)
        logits = (by_hi[:N_EXPERTS] + by_hi[N_EXPERTS:]
                  + lax.dot_general(wr_ref[0:N_EXPERTS, :], h_lo, nt_dims,
                                    preferred_element_type=F32)
                  + br_ref[...])
        e_iota = lax.broadcasted_iota(I32, (N_EXPERTS, ts), 0)
        v = logits
        vals, idxs, sels = [], [], []
        for _ in range(TOP_K):
            mx = jnp.max(v, axis=0, keepdims=True)
            ik = jnp.min(jnp.where(v == mx, e_iota, N_EXPERTS), axis=0, keepdims=True)
            sel = e_iota == ik
            v = jnp.where(sel, -jnp.inf, v)
            vals.append(mx)
            idxs.append(ik)
            sels.append(sel)
        exps = [jnp.exp(val - vals[0]) for val in vals]
        denom = exps[0] + exps[1] + exps[2] + exps[3]
        idx_ref[:, rows] = jnp.concatenate(idxs, axis=0)
        tw_ref[:, rows] = jnp.concatenate([e / denom for e in exps], axis=0)

        onehot = jnp.zeros((N_EXPERTS, ts), F32)
        for sel in sels:
            onehot = jnp.where(sel, 1.0, onehot)
        sr = lax.broadcasted_iota(I32, (ts, ts), 0)
        st = lax.broadcasted_iota(I32, (ts, ts), 1)
        before = jnp.where(sr < st, 1.0, 0.0).astype(BF16)
        total = cnt_ref[...] + _bdot(onehot.astype(BF16), before)
        ranks = [jnp.sum(jnp.where(sel, total, 0.0), axis=0, keepdims=True) for sel in sels]
        rank_ref[:, rows] = jnp.concatenate(ranks, axis=0).astype(I32)
        cnt_ref[...] = cnt_ref[...] + jnp.sum(onehot, axis=1, keepdims=True)

    tile(0, z0_ref, z1_ref, hn1_ref, x_ref[ts:2 * ts, :], sc1_ref[0], sh1_ref[0])
    tile(1, z1_ref, z0_ref, hn0_ref, xn_ref[...], sc1n_ref[0], sh1n_ref[0])


def _mixer(x2d, n1g, sc1, sh1, w_in, gt1, sc2, sh2, conv_w, conv_b, wa_bd, ba, wx_bd, bx, lam,
           ln_g, ln_b, ws, bs_tile, wbr, wbs, wout, n2g, wr_split, br, shifts, w_gu, w_down,
           batch0, batches, seq, expert0, n_cast, cast_prev):
    d = x2d.shape[1]
    ts = MIXER_TS
    n = batches * seq
    tiles_per_seq = seq // ts
    pairs_per_seq = tiles_per_seq // 2
    assert tiles_per_seq % 2 == 0
    steps = batches * pairs_per_seq
    pair0 = batch0 * pairs_per_seq
    next_tile = lambda s: jnp.minimum(2 * s + 2, 2 * steps - 1)
    pair = lambda s: (s, 0)
    pairt = lambda s: (0, s)
    bvec = lambda s: (batch0 + s // pairs_per_seq, 0, 0)
    bvec_next = lambda s: (batch0 + next_tile(s) // tiles_per_seq, 0, 0)
    n_e, k_gu, n_gu = w_gu.shape
    _, k_d, n_d = w_down.shape
    assert (n_cast * k_gu) % steps == 0 and (n_cast * k_d) % steps == 0
    rows_gu, rows_d = n_cast * k_gu // steps, n_cast * k_d // steps
    cast_gu = lambda s: (expert0 * k_gu // rows_gu + s, 0)
    cast_d = lambda s: (expert0 * k_d // rows_d + s, 0)
    w_gu = w_gu.reshape(n_e * k_gu, n_gu)
    w_down = w_down.reshape(n_e * k_d, n_d)
    c2 = lambda s: (0, 0)
    c3 = lambda s: (0, 0, 0)
    in_specs = [
        pl.BlockSpec((2 * ts, d), lambda s: (pair0 + s, 0)),
        pl.BlockSpec((ts, d), lambda s: (2 * pair0 + next_tile(s), 0)),
        pl.BlockSpec((1, d), c2),
        pl.BlockSpec((1, 1, d), bvec),
        pl.BlockSpec((1, 1, d), bvec),
        pl.BlockSpec((1, 1, d), bvec_next),
        pl.BlockSpec((1, 1, d), bvec_next),
        pl.BlockSpec((d, D_IN), c2),
        pl.BlockSpec((1, 1, d), bvec),
        pl.BlockSpec((1, 1, d), bvec),
        pl.BlockSpec((1, 1, d), bvec),
        pl.BlockSpec((CONV_WIDTH, D_RNN), c2),
        pl.BlockSpec((1, D_RNN), c2),
        pl.BlockSpec((N_GATE_BLOCKS, GATE_BLOCK, GATE_BLOCK), c3),
        pl.BlockSpec((1, D_RNN), c2),
        pl.BlockSpec((N_GATE_BLOCKS, GATE_BLOCK, GATE_BLOCK), c3),
        pl.BlockSpec((1, D_RNN), c2),
        pl.BlockSpec((1, D_RNN), c2),
        pl.BlockSpec((1, D_SG), c2),
        pl.BlockSpec((1, D_SG), c2),
        pl.BlockSpec((SG_GROUPS, SG_CHUNK, SG_CHUNK), c3),
        pl.BlockSpec((SG_CHUNK, D_SG), c2),
        pl.BlockSpec((D_RNN, d), c2),
        pl.BlockSpec((D_SG, d), c2),
        pl.BlockSpec((d, d), c2),
        pl.BlockSpec((1, d), c2),
        pl.BlockSpec((2 * N_EXPERTS, d), c2),
        pl.BlockSpec((N_EXPERTS, 1), c2),
        pl.BlockSpec((CONV_WIDTH - 1, ts, ts), c3),
        pl.BlockSpec((rows_gu, n_gu), cast_gu),
        pl.BlockSpec((rows_d, n_d), cast_d),
    ]
    assert len(in_specs) == N_MIXER_INPUTS
    args = [x2d, x2d, n1g, sc1, sh1, sc1, sh1, w_in, gt1, sc2, sh2, conv_w, conv_b, wa_bd, ba,
            wx_bd, bx, lam, ln_g, ln_b, ws, bs_tile, wbr, wbs, wout, n2g, wr_split, br, shifts,
            w_gu, w_down]
    aliases = {}
    if cast_prev is not None:
        for j, prev in enumerate(cast_prev):
            in_specs.append(pl.BlockSpec(memory_space=pl.ANY))
            aliases[len(args)] = 7 + j
            args.append(prev)
    out_specs = [
        pl.BlockSpec((2 * ts, d), pair),
        pl.BlockSpec((2 * ts, PACK_W), pair),
        pl.BlockSpec((2 * ts, PACK_W), pair),
        pl.BlockSpec((TOP_K, 2 * ts), pairt),
        pl.BlockSpec((TOP_K, 2 * ts), pairt),
        pl.BlockSpec((TOP_K, 2 * ts), pairt),
        pl.BlockSpec((N_EXPERTS, 1), c2),
        pl.BlockSpec((rows_gu, n_gu), cast_gu),
        pl.BlockSpec((rows_d, n_d), cast_d),
    ]
    out_shape = [
        jax.ShapeDtypeStruct((n, d), F32),
        jax.ShapeDtypeStruct((n, PACK_W), jnp.uint32),
        jax.ShapeDtypeStruct((n, PACK_W), jnp.uint32),
        jax.ShapeDtypeStruct((TOP_K, n), I32),
        jax.ShapeDtypeStruct((TOP_K, n), F32),
        jax.ShapeDtypeStruct((TOP_K, n), I32),
        jax.ShapeDtypeStruct((N_EXPERTS, 1), F32),
        jax.ShapeDtypeStruct(w_gu.shape, BF16),
        jax.ShapeDtypeStruct(w_down.shape, BF16),
    ]
    scratch = [
        pltpu.VMEM((ts, D_IN), BF16),
        pltpu.VMEM((ts, D_IN), BF16),
        pltpu.VMEM((ts, d), BF16),
        pltpu.VMEM((ts, d), BF16),
        pltpu.VMEM((2 * SUBLANES, D_RNN), F32),
        pltpu.VMEM((ts, D_RNN), F32),
        pltpu.VMEM((ts, D_RNN), F32),
        pltpu.VMEM((ts, D_RNN), F32),
        pltpu.VMEM((ts, D_SG), F32),
        pltpu.VMEM((SUBLANES, D_RNN), F32),
    ]
    return pl.pallas_call(
        functools.partial(_mixer_kernel, pairs_per_seq=pairs_per_seq, n_aliased=len(aliases)),
        grid=(steps,),
        in_specs=in_specs,
        out_specs=out_specs,
        out_shape=out_shape,
        scratch_shapes=scratch,
        input_output_aliases=aliases,
        compiler_params=pltpu.CompilerParams(
            dimension_semantics=("arbitrary",), vmem_limit_bytes=VMEM_LIMIT),
        name="mixer",
    )(*args)


def _sc_mesh():
    return plsc.VectorSubcoreMesh(core_axis_name="core", subcore_axis_name="subcore")


def _sc_scatter_rows(rows, pos, cap):
    n, d = rows.shape
    kk = pos.shape[0]

    @functools.partial(
        pl.kernel, out_type=jax.ShapeDtypeStruct((cap, d), rows.dtype), mesh=_sc_mesh(),
        scratch_types=[], name="sc_scatter_rows")
    def scatter(x_hbm, i_hbm, o_hbm):
        def body(x_vmem, i_vmem):
            pltpu.sync_copy(x_vmem, o_hbm.at[i_vmem.at[0]])

        pltpu.emit_pipeline(
            body,
            grid=(n // SC_WINDOW, kk),
            in_specs=[pl.BlockSpec((SC_WINDOW, d), lambda i, k: (i, 0)),
                      pl.BlockSpec((1, SC_WINDOW), lambda i, k: (k, i))],
            out_specs=[],
            core_axis_name=("core", "subcore"),
            dimension_semantics=(pltpu.PARALLEL, pltpu.ARBITRARY),
        )(x_hbm, i_hbm)

    return scatter(rows, pos)


def _sc_gather_rows(table, idx):
    m = idx.shape[1]
    d = table.shape[1]

    @functools.partial(
        pl.kernel, out_type=jax.ShapeDtypeStruct((m, d), table.dtype), mesh=_sc_mesh(),
        scratch_types=[], name="sc_gather_rows")
    def gather(x_hbm, i_hbm, o_hbm):
        def body(i_vmem, o_vmem):
            pltpu.sync_copy(x_hbm.at[i_vmem.at[0]], o_vmem)

        pltpu.emit_pipeline(
            body,
            grid=(m // SC_WINDOW,),
            in_specs=[pl.BlockSpec((1, SC_WINDOW), lambda i: (0, i))],
            out_specs=[pl.BlockSpec((SC_WINDOW, d), lambda i: (i, 0))],
            core_axis_name=("core", "subcore"),
            dimension_semantics=(pltpu.PARALLEL,),
        )(i_hbm, o_hbm)

    return gather(table, idx)


def _pos_kernel(cnt_ref, idx_ref, rank_ref, pos_ref, be_ref, nv_ref):
    assert MOE_BM & (MOE_BM - 1) == 0
    starts, ends, run = [], [], jnp.int32(0)
    for e in range(N_EXPERTS):
        starts.append(run)
        run = run + ((cnt_ref[e] + (MOE_BM - 1)) & jnp.int32(-MOE_BM))
        ends.append(run)

    idx = idx_ref[...]
    pos = rank_ref[...]
    for e in range(N_EXPERTS):
        pos = pos + jnp.where(idx == e, starts[e], 0)
    pos_ref[...] = pos

    row0 = lax.broadcasted_iota(I32, be_ref.shape, 1) * MOE_BM
    be = jnp.zeros(be_ref.shape, I32)
    for e in range(N_EXPERTS - 1):
        be = be + jnp.where(ends[e] <= row0, 1, 0)
    start = jnp.zeros(be_ref.shape, I32)
    count = jnp.zeros(be_ref.shape, I32)
    for e in range(N_EXPERTS):
        here = be == e
        start = jnp.where(here, starts[e], start)
        count = jnp.where(here, cnt_ref[e], count)
    be_ref[...] = be
    nv_ref[...] = jnp.clip(count - (row0 - start), 0, MOE_BM)


def _slot_plan(counts, top_idx, rank, nb):
    k, n = top_idx.shape
    tn = min(POS_TN, n)
    nbp = -(-nb // LANES) * LANES
    spec = pl.BlockSpec((k, tn), lambda i, c: (0, i))
    blocks = pl.BlockSpec((1, nbp), lambda i, c: (0, 0))
    pos, be, nv = pl.pallas_call(
        _pos_kernel,
        grid_spec=pltpu.PrefetchScalarGridSpec(
            num_scalar_prefetch=1, grid=(n // tn,), in_specs=[spec, spec],
            out_specs=[spec, blocks, blocks]),
        out_shape=[jax.ShapeDtypeStruct((k, n), I32), jax.ShapeDtypeStruct((1, nbp), I32),
                   jax.ShapeDtypeStruct((1, nbp), I32)],
        compiler_params=pltpu.CompilerParams(dimension_semantics=("arbitrary",)),
        name="slot_plan",
    )(counts, top_idx, rank)
    return pos, be.reshape(nbp), nv.reshape(nbp)


def _expert_kernel(be_ref, nv_ref, xa_ref, xb_ref, wgu_ref, bgu_ref, wd_ref, bd_ref,
                   ya_ref, yb_ref):
    nvalid = nv_ref[pl.program_id(0)]

    def mlp_rows(h):
        rows = slice(h * MOE_HALF, (h + 1) * MOE_HALF)
        live = lax.broadcasted_iota(I32, (MOE_HALF, 1), 0) < nvalid - h * MOE_HALF
        xb = jnp.where(live, _unpack_halves(xa_ref[rows, :], xb_ref[rows, :]), 0.0).astype(BF16)
        gu = _bdot(xb, wgu_ref[0]) + bgu_ref[0]
        gate = jnp.minimum(gu[:, :D_EXPERT], SWIGLU_LIMIT)
        up = jnp.clip(gu[:, D_EXPERT:], -SWIGLU_LIMIT, SWIGLU_LIMIT)
        act = (up + 1.0) * (gate * _sigmoid(SWIGLU_ALPHA * gate))
        y = _bdot(act.astype(BF16), wd_ref[0]) + bd_ref[0]
        ya_ref[rows, :], yb_ref[rows, :] = _pack_halves(y)

    def zero_rows(h):
        rows = slice(h * MOE_HALF, (h + 1) * MOE_HALF)
        ya_ref[rows, :] = jnp.zeros((MOE_HALF, PACK_W), jnp.uint32)
        yb_ref[rows, :] = jnp.zeros((MOE_HALF, PACK_W), jnp.uint32)

    @pl.when(nvalid > MOE_HALF)
    def _():
        mlp_rows(0)
        mlp_rows(1)

    @pl.when((nvalid > 0) & (nvalid <= MOE_HALF))
    def _():
        mlp_rows(0)
        zero_rows(1)

    @pl.when(nvalid <= 0)
    def _():
        zero_rows(0)
        zero_rows(1)


def _experts(block_e, n_valid, xa, xb, w_gu, b_gu, w_down, b_down):
    cap = xa.shape[0]
    d = D_MODEL
    nb = cap // MOE_BM
    half = pl.BlockSpec((MOE_BM, PACK_W), lambda i, be, nv: (i, 0))
    grid_spec = pltpu.PrefetchScalarGridSpec(
        num_scalar_prefetch=2,
        grid=(nb,),
        in_specs=[
            half,
            half,
            pl.BlockSpec((1, d, 2 * D_EXPERT), lambda i, be, nv: (be[i], 0, 0)),
            pl.BlockSpec((1, 1, 2 * D_EXPERT), lambda i, be, nv: (be[i], 0, 0)),
            pl.BlockSpec((1, D_EXPERT, d), lambda i, be, nv: (be[i], 0, 0)),
            pl.BlockSpec((1, 1, d), lambda i, be, nv: (be[i], 0, 0)),
        ],
        out_specs=[half, half],
    )
    return pl.pallas_call(
        _expert_kernel,
        grid_spec=grid_spec,
        out_shape=[jax.ShapeDtypeStruct((cap, PACK_W), jnp.uint32)] * 2,
        compiler_params=pltpu.CompilerParams(
            dimension_semantics=("arbitrary",), vmem_limit_bytes=VMEM_LIMIT),
        name="experts",
    )(block_e, n_valid, xa, xb, w_gu, b_gu, w_down, b_down)


def _final_kernel(x2_ref, ya_ref, yb_ref, tw_ref, gt2_ref, fg_ref, *rest):
    o_ref = rest[-1]
    tw = tw_ref[...].T
    moe = tw[:, 0:1] * _unpack_halves(ya_ref[0], yb_ref[0])
    for k in range(1, TOP_K):
        moe = moe + tw[:, k:k + 1] * _unpack_halves(ya_ref[k], yb_ref[k])
    x3 = x2_ref[...] + gt2_ref[0] * moe
    ms = jnp.mean(x3 * x3, axis=-1, keepdims=True)
    o_ref[...] = (x3 * lax.rsqrt(ms + EPS)) * fg_ref[...]


def _final(x2, yga, ygb, tw_tok, gt2, final_g, seq, batch0, n_total, out_prev):
    n, d = x2.shape
    tiles_per_seq = seq // FINAL_TM
    tile0 = batch0 * tiles_per_seq
    half = pl.BlockSpec((TOP_K, FINAL_TM, PACK_W), lambda i: (0, i, 0))
    in_specs = [
        pl.BlockSpec((FINAL_TM, d), lambda i: (i, 0)),
        half,
        half,
        pl.BlockSpec((TOP_K, FINAL_TM), lambda i: (0, i)),
        pl.BlockSpec((1, 1, d), lambda i: (batch0 + i // tiles_per_seq, 0, 0)),
        pl.BlockSpec((1, d), lambda i: (0, 0)),
    ]
    args = [x2, yga, ygb, tw_tok, gt2, final_g]
    aliases = {}
    if out_prev is not None:
        in_specs.append(pl.BlockSpec(memory_space=pl.ANY))
        aliases = {len(args): 0}
        args.append(out_prev)
    return pl.pallas_call(
        _final_kernel,
        grid=(n // FINAL_TM,),
        in_specs=in_specs,
        out_specs=pl.BlockSpec((FINAL_TM, d), lambda i: (tile0 + i, 0)),
        out_shape=jax.ShapeDtypeStruct((n_total, d), F32),
        input_output_aliases=aliases,
        compiler_params=pltpu.CompilerParams(
            dimension_semantics=("arbitrary",), vmem_limit_bytes=VMEM_LIMIT),
        name="final",
    )(*args)


def _block_diag(w):
    per = GATE_BLOCK // RNN_HEAD_DIM
    w4 = w.reshape(N_GATE_BLOCKS, per, RNN_HEAD_DIM, RNN_HEAD_DIM)
    eye = jnp.eye(per, dtype=w.dtype)
    bd = jnp.einsum("gpij,pq->gpiqj", w4, eye)
    return bd.reshape(N_GATE_BLOCKS, GATE_BLOCK, GATE_BLOCK)


def _layer(x2d, c, batch, seq, ada_w, ada_b, norm1_g, w_in, conv_w, conv_b, lru_wa, lru_ba,
           lru_wx, lru_bx, lru_lam, sg_ln_g, sg_ln_b, sg_ws, sg_bs, w_br_rnn, w_br_sg, w_out,
           norm2_g, w_router, b_router, w_gu, b_gu, w_down, b_down, final_g):
    n_total, d = x2d.shape
    mod = _ada(c, ada_w, ada_b)
    sh1, sc1, gt1, sh2, sc2, gt2 = [
        mod[:, i * d:(i + 1) * d].reshape(batch, 1, d) for i in range(N_MOD)]
    row = lambda v: v.reshape(1, -1)

    bs_tile = jnp.repeat(sg_bs.T, SG_GROUP_DIM, axis=1)
    wr_t = w_router.T
    wr_hi = wr_t.astype(BF16)
    wr_lo = (wr_t - wr_hi.astype(F32)).astype(BF16)
    t_out = lax.broadcasted_iota(I32, (MIXER_TS, MIXER_TS), 0)
    t_in = lax.broadcasted_iota(I32, (MIXER_TS, MIXER_TS), 1)
    shifts = jnp.stack([(t_out - t_in == s) for s in range(1, CONV_WIDTH)]).astype(BF16)
    mixer_weights = (
        conv_w, row(conv_b), _block_diag(0.5 * lru_wa).astype(BF16), row(0.5 * lru_ba),
        _block_diag(0.5 * lru_wx).astype(BF16), row(0.5 * lru_bx), row(lru_lam), row(sg_ln_g),
        row(sg_ln_b), sg_ws, bs_tile, (0.25 * w_br_rnn).astype(BF16), (0.5 * w_br_sg).astype(BF16),
        (0.5 * w_out).astype(BF16), row(norm2_g), jnp.concatenate([wr_hi, wr_lo], axis=0),
        b_router.reshape(N_EXPERTS, 1), shifts)
    gate_cols = 2 * D_RNN + 2 * D_SG
    col_scale = jnp.where(jnp.arange(D_IN) >= gate_cols, 0.5, 1.0).astype(F32)
    w_in_bf = (w_in * col_scale[None, :]).astype(BF16)

    groups = TOKEN_GROUPS if batch % TOKEN_GROUPS == 0 and N_EXPERTS % TOKEN_GROUPS == 0 else 1
    batches = batch // groups
    n_cast = N_EXPERTS // groups
    n = batches * seq
    mixed, cast = [], None
    for grp in range(groups):
        *outs, w_gu_bf, w_down_bf = _mixer(
            x2d, row(norm1_g), sc1, sh1, w_in_bf, gt1, sc2, sh2, *mixer_weights, w_gu, w_down,
            grp * batches, batches, seq, grp * n_cast, n_cast, cast)
        cast = (w_gu_bf, w_down_bf)
        mixed.append(outs)

    out = None
    for grp in range(groups):
        batch0 = grp * batches
        x2, h2a, h2b, top_idx, top_w, rank, counts = mixed[grp]

        cap = n * TOP_K + N_EXPERTS * MOE_BM
        nb = cap // MOE_BM
        pos, block_e, n_valid = _slot_plan(counts[:, 0].astype(I32), top_idx, rank, nb)

        xa = _sc_scatter_rows(h2a, pos, cap)
        xb = _sc_scatter_rows(h2b, pos, cap)
        ya, yb = _experts(block_e, n_valid, xa, xb, cast[0].reshape(w_gu.shape),
                          b_gu.reshape(N_EXPERTS, 1, -1), cast[1].reshape(w_down.shape),
                          b_down.reshape(N_EXPERTS, 1, -1))
        flat_pos = pos.reshape(1, -1)
        yga = _sc_gather_rows(ya, flat_pos).reshape(TOP_K, n, PACK_W)
        ygb = _sc_gather_rows(yb, flat_pos).reshape(TOP_K, n, PACK_W)
        out = _final(x2, yga, ygb, top_w, gt2, final_g.reshape(1, d), seq, batch0, n_total, out)
    return out


def kernel(x, c, ada_w, ada_b, norm1_g, w_in, conv_w, conv_b, lru_wa, lru_ba, lru_wx, lru_bx,
           lru_lam, sg_ln_g, sg_ln_b, sg_ws, sg_bs, w_br_rnn, w_br_sg, w_out, norm2_g,
           w_router, b_router, w_gu, b_gu, w_down, b_down, final_g):
    batch, seq, d = x.shape
    depth = ada_w.shape[0]
    assert depth == 1, "the combine is fused with the final norm, which follows the only layer"
    x2d = x.reshape(batch * seq, d)
    l = 0
    out = _layer(
        x2d, c, batch, seq, ada_w[l], ada_b[l], norm1_g[l], w_in[l], conv_w[l], conv_b[l],
        lru_wa[l], lru_ba[l], lru_wx[l], lru_bx[l], lru_lam[l], sg_ln_g[l], sg_ln_b[l],
        sg_ws[l], sg_bs[l], w_br_rnn[l], w_br_sg[l], w_out[l], norm2_g[l], w_router[l],
        b_router[l], w_gu[l], b_gu[l], w_down[l], b_down[l], final_g)
    return out.reshape(batch, seq, d)
```

```python
import functools

import jax
import jax.numpy as jnp
from jax import lax
from jax.experimental import pallas as pl
from jax.experimental.pallas import tpu as pltpu
from jax.experimental.pallas import tpu_sc as plsc

F32 = jnp.float32
BF16 = jnp.bfloat16
I32 = jnp.int32

D_MODEL = 1024
D_RNN = 1024
RNN_HEADS = 16
RNN_HEAD_DIM = D_RNN // RNN_HEADS
CONV_WIDTH = 4
LRU_C = 8.0
D_SG = 1024
SG_GROUPS = 8
SG_GROUP_DIM = D_SG // SG_GROUPS
SG_CHUNK = 128
N_EXPERTS = 32
TOP_K = 4
D_EXPERT = 1024
SWIGLU_LIMIT = 7.0
SWIGLU_ALPHA = 1.702
EPS = 1e-6
N_MOD = 6
D_IN = 2 * D_RNN + 2 * D_SG + 2 * D_MODEL

SUBLANES = 8
LANES = 128
GATE_BLOCK = 256
N_GATE_BLOCKS = D_RNN // GATE_BLOCK

ADA_TN = 1536
INPROJ_TN = 1024
MIXER_TS = 256
MOE_BM = 512
MOE_HALF = MOE_BM // 2
TOKEN_GROUPS = 2
SC_WINDOW = 128
PACK_W = D_MODEL // 4
FINAL_TM = 512
POS_TN = 8192
VMEM_LIMIT = 58 * 1024 * 1024


def _sigmoid(x):
    return 0.5 * jnp.tanh(0.5 * x) + 0.5


def _gelu_tanh_x2(x):
    k = 0.7978845608028654
    return x + x * jnp.tanh(x * (k + (k * 0.044715) * (x * x)))


def _bdot(a, b):
    return jnp.dot(a, b, preferred_element_type=F32)


def _pack_halves(v):
    word = pltpu.pack_elementwise([v[:, 2 * PACK_W:], v[:, :2 * PACK_W]], packed_dtype=BF16)
    return word[:, :PACK_W], word[:, PACK_W:]


def _unpack_halves(wa, wb):
    part = lambda w, i: pltpu.unpack_elementwise(w, index=i, packed_dtype=BF16, unpacked_dtype=F32)
    return jnp.concatenate([part(wa, 1), part(wb, 1), part(wa, 0), part(wb, 0)], axis=1)


def _ada_kernel(c_ref, w_ref, b_ref, o_ref):
    c = c_ref[...]
    s = c * _sigmoid(c)
    o_ref[...] = jnp.dot(s, w_ref[...], preferred_element_type=F32,
                         precision=lax.Precision.HIGHEST) + b_ref[...]


def _ada(c, ada_w, ada_b):
    b, d = c.shape
    n = ada_w.shape[1]
    return pl.pallas_call(
        _ada_kernel,
        grid=(n // ADA_TN,),
        in_specs=[
            pl.BlockSpec((b, d), lambda j: (0, 0)),
            pl.BlockSpec((d, ADA_TN), lambda j: (0, j)),
            pl.BlockSpec((1, ADA_TN), lambda j: (0, j)),
        ],
        out_specs=pl.BlockSpec((b, ADA_TN), lambda j: (0, j)),
        out_shape=jax.ShapeDtypeStruct((b, n), F32),
        compiler_params=pltpu.CompilerParams(
            dimension_semantics=("arbitrary",), vmem_limit_bytes=VMEM_LIMIT),
        name="ada",
    )(c, ada_w, ada_b.reshape(1, n))


def _norm_mod(x, g, sc, sh):
    ms = jnp.mean(x * x, axis=-1, keepdims=True)
    y = x * lax.rsqrt(ms + EPS)
    return y * (g * (1.0 + sc)) + sh


N_MIXER_INPUTS = 32


def _mixer_kernel(*refs, pairs_per_seq, n_aliased):
    (x_ref, xn_ref, n1g_ref, sc1_ref, sh1_ref, sc1n_ref, sh1n_ref, win_ref,
     gt1_ref, sc2_ref, sh2_ref,
     convw_ref, convb_ref, wa_ref, ba_ref, wx_ref, bx_ref, lam_ref,
     lng_ref, lnb_ref, ws_ref, bs_ref, wbr_ref, wbs_ref, wout_ref,
     n2g_ref, wr_ref, br_ref, shift_ref, before_ref, wgu_ref, wd_ref) = refs[:N_MIXER_INPUTS]
    (x2_ref, h2a_ref, h2b_ref, idx_ref, tw_ref, rank_ref, cnt_ref, wgu_bf_ref, wd_bf_ref,
     z0_ref, z1_ref, hn0_ref, hn1_ref, xp_ref, xc_ref, a_ref, hh_ref, sv_ref,
     hstate_ref) = refs[N_MIXER_INPUTS + n_aliased:]
    ts = MIXER_TS
    s = pl.program_id(0)

    wgu_bf_ref[...] = wgu_ref[...].astype(BF16)
    wd_bf_ref[...] = wd_ref[...].astype(BF16)

    def inproj_norm(hn_ref, x_rows, sc, sh):
        hn_ref[...] = _norm_mod(x_rows, n1g_ref[...], sc, sh).astype(BF16)

    def inproj_chunks(z_dst, hn_ref, first, last):
        for c in range(first * INPROJ_TN, last * INPROJ_TN, INPROJ_TN):
            z_dst[:, c:c + INPROJ_TN] = _bdot(hn_ref[...], win_ref[:, c:c + INPROJ_TN]).astype(BF16)

    @pl.when(s == 0)
    def _():
        cnt_ref[...] = jnp.zeros_like(cnt_ref)
        inproj_norm(hn0_ref, x_ref[0:ts, :], sc1_ref[0], sh1_ref[0])
        inproj_chunks(z0_ref, hn0_ref, 0, D_IN // INPROJ_TN)

    @pl.when(s % pairs_per_seq == 0)
    def _():
        xp_ref[0:SUBLANES, :] = jnp.zeros((SUBLANES, D_RNN), F32)
        hstate_ref[...] = jnp.zeros_like(hstate_ref)

    def tile(k, z_ref, z_next_ref, hn_ref, next_rows, next_sc, next_sh):
        rows = slice(k * ts, (k + 1) * ts)
        inproj_norm(hn_ref, next_rows, next_sc, next_sh)

        x16 = z_ref[:, 0:D_RNN]
        rnn_x = x16.astype(F32)
        cw = convw_ref[...]
        xc = cw[3:4] * rnn_x + convb_ref[...]
        for sft in range(1, CONV_WIDTH):
            xc = xc + cw[3 - sft:4 - sft] * _bdot(shift_ref[sft - 1], x16)
        xc_ref[...] = xc
        xp_ref[SUBLANES:2 * SUBLANES, :] = rnn_x[0:SUBLANES, :]
        xc_ref[0:SUBLANES, :] = (
            cw[3:4] * xp_ref[SUBLANES:2 * SUBLANES, :]
            + cw[2:3] * xp_ref[SUBLANES - 1:2 * SUBLANES - 1, :]
            + cw[1:2] * xp_ref[SUBLANES - 2:2 * SUBLANES - 2, :]
            + cw[0:1] * xp_ref[SUBLANES - 3:2 * SUBLANES - 3, :]) + convb_ref[...]
        xp_ref[0:SUBLANES, :] = rnn_x[ts - SUBLANES:ts, :]
        xc = xc_ref[...]

        xcb = xc.astype(BF16)
        r_parts, i_parts = [], []
        for g in range(N_GATE_BLOCKS):
            blk = xcb[:, g * GATE_BLOCK:(g + 1) * GATE_BLOCK]
            r_parts.append(_bdot(blk, wa_ref[g]))
            i_parts.append(_bdot(blk, wx_ref[g]))
        inproj_chunks(z_next_ref, hn_ref, 0, 3)
        r_t = jnp.tanh(jnp.concatenate(r_parts, axis=1) + ba_ref[...])
        i_t = jnp.tanh(jnp.concatenate(i_parts, axis=1) + bx_ref[...])

        nl = -lam_ref[...]
        softplus = jnp.maximum(nl, 0.0) + jnp.log(1.0 + jnp.exp(-jnp.abs(nl)))
        half_c = (-0.5 * LRU_C) * softplus
        a = jnp.exp(r_t * half_c + half_c)
        t = 1.0 - a * a
        u = jnp.where(t > 0.0, t * lax.rsqrt(t), 0.0) * ((i_t + 1.0) * xc)

        groups = ts // SUBLANES
        a3 = a.reshape(groups, SUBLANES, D_RNN)
        h3 = u.reshape(groups, SUBLANES, D_RNN)
        sub = lax.broadcasted_iota(I32, (groups, SUBLANES, D_RNN), 1)
        for step in (1, 2, 4):
            keep = sub >= step
            a_sh = jnp.where(keep, pltpu.roll(a3, step, 1), 1.0)
            h_sh = jnp.where(keep, pltpu.roll(h3, step, 1), 0.0)
            h3 = h3 + a3 * h_sh
            a3 = a3 * a_sh
        a_ref[...] = a3.reshape(ts, D_RNN)
        hh_ref[...] = h3.reshape(ts, D_RNN)

        hc = hstate_ref[...]
        for gi in range(groups):
            grp = slice(gi * SUBLANES, (gi + 1) * SUBLANES)
            hg = hh_ref[grp, :] + a_ref[grp, :] * hc
            hh_ref[grp, :] = hg
            hc = jnp.broadcast_to(hg[SUBLANES - 1:SUBLANES, :], (SUBLANES, D_RNN))
        hstate_ref[...] = hc

        inproj_chunks(z_next_ref, hn_ref, 3, 4)
        y_rnn = (hh_ref[...] * _gelu_tanh_x2(z_ref[:, D_RNN:2 * D_RNN].astype(F32))).astype(BF16)

        gv = _gelu_tanh_x2(z_ref[:, 2 * D_RNN + D_SG:2 * D_RNN + 2 * D_SG].astype(F32))
        mu = jnp.mean(gv, axis=-1, keepdims=True)
        dv = gv - mu
        var = jnp.mean(dv * dv, axis=-1, keepdims=True)
        vn = (dv * lax.rsqrt(var + 4.0 * EPS) * lng_ref[...] + lnb_ref[...]).astype(BF16)
        for g in range(SG_GROUPS):
            wg = ws_ref[g]
            cols = slice(g * SG_GROUP_DIM, (g + 1) * SG_GROUP_DIM)
            for n in range(ts // SG_CHUNK):
                chunk = slice(n * SG_CHUNK, (n + 1) * SG_CHUNK)
                sv_ref[chunk, cols] = _bdot(wg, vn[chunk, cols]) + bs_ref[:, cols]
        inproj_chunks(z_next_ref, hn_ref, 4, 5)
        gu = _gelu_tanh_x2(z_ref[:, 2 * D_RNN:2 * D_RNN + D_SG].astype(F32))
        y_sg = (gu * sv_ref[...]).astype(BF16)

        g_rnn = z_ref[:, 2 * D_RNN + 2 * D_SG:2 * D_RNN + 2 * D_SG + D_MODEL].astype(F32)
        g_sg = z_ref[:, 2 * D_RNN + 2 * D_SG + D_MODEL:D_IN].astype(F32)
        m = ((jnp.tanh(g_rnn) + 1.0) * _bdot(y_rnn, wbr_ref[...])
             + (jnp.tanh(g_sg) + 1.0) * _bdot(y_sg, wbs_ref[...])).astype(BF16)
        x2 = x_ref[rows, :] + gt1_ref[0] * _bdot(m, wout_ref[...])
        x2_ref[rows, :] = x2
        inproj_chunks(z_next_ref, hn_ref, 5, 6)

        h2 = _norm_mod(x2, n2g_ref[...], sc2_ref[0], sh2_ref[0])
        h2a_ref[rows, :], h2b_ref[rows, :] = _pack_halves(h2)
        h_hi = h2.astype(BF16)
        h_lo = (h2 - h_hi.astype(F32)).astype(BF16)
        nt_dims = (((1,), (1,)), ((), ()))
        by_hi = lax.dot_general(wr_ref[...], h_hi, nt_dims, preferred_element_type=F32)
        logits = (by_hi[:N_EXPERTS] + by_hi[N_EXPERTS:]
                  + lax.dot_general(wr_ref[0:N_EXPERTS, :], h_lo, nt_dims,
                                    preferred_element_type=F32)
                  + br_ref[...])
        e_iota = lax.broadcasted_iota(I32, (N_EXPERTS, ts), 0)
        v = logits
        vals, idxs, sels = [], [], []
        for _ in range(TOP_K):
            mx = jnp.max(v, axis=0, keepdims=True)
            ik = jnp.min(jnp.where(v == mx, e_iota, N_EXPERTS), axis=0, keepdims=True)
            sel = e_iota == ik
            v = jnp.where(sel, -jnp.inf, v)
            vals.append(mx)
            idxs.append(ik)
            sels.append(sel)
        exps = [jnp.exp(val - vals[0]) for val in vals]
        denom = exps[0] + exps[1] + exps[2] + exps[3]
        idx_ref[:, rows] = jnp.concatenate(idxs, axis=0)
        tw_ref[:, rows] = jnp.concatenate([e / denom for e in exps], axis=0)

        onehot = jnp.zeros((N_EXPERTS, ts), F32)
        for sel in sels:
            onehot = jnp.where(sel, 1.0, onehot)
        total = cnt_ref[...] + _bdot(onehot.astype(BF16), before_ref[...])
        ranks = [jnp.sum(jnp.where(sel, total, 0.0), axis=0, keepdims=True) for sel in sels]
        rank_ref[:, rows] = jnp.concatenate(ranks, axis=0).astype(I32)
        cnt_ref[...] = cnt_ref[...] + jnp.sum(onehot, axis=1, keepdims=True)

    tile(0, z0_ref, z1_ref, hn1_ref, x_ref[ts:2 * ts, :], sc1_ref[0], sh1_ref[0])
    tile(1, z1_ref, z0_ref, hn0_ref, xn_ref[...], sc1n_ref[0], sh1n_ref[0])


def _mixer(x2d, n1g, sc1, sh1, w_in, gt1, sc2, sh2, conv_w, conv_b, wa_bd, ba, wx_bd, bx, lam,
           ln_g, ln_b, ws, bs_tile, wbr, wbs, wout, n2g, wr_split, br, shifts, before, w_gu, w_down,
           batch0, batches, seq, expert0, n_cast, cast_prev):
    d = x2d.shape[1]
    ts = MIXER_TS
    n = batches * seq
    tiles_per_seq = seq // ts
    pairs_per_seq = tiles_per_seq // 2
    assert tiles_per_seq % 2 == 0
    steps = batches * pairs_per_seq
    pair0 = batch0 * pairs_per_seq
    next_tile = lambda s: jnp.minimum(2 * s + 2, 2 * steps - 1)
    pair = lambda s: (s, 0)
    pairt = lambda s: (0, s)
    bvec = lambda s: (batch0 + s // pairs_per_seq, 0, 0)
    bvec_next = lambda s: (batch0 + next_tile(s) // tiles_per_seq, 0, 0)
    n_e, k_gu, n_gu = w_gu.shape
    _, k_d, n_d = w_down.shape
    assert (n_cast * k_gu) % steps == 0 and (n_cast * k_d) % steps == 0
    rows_gu, rows_d = n_cast * k_gu // steps, n_cast * k_d // steps
    cast_gu = lambda s: (expert0 * k_gu // rows_gu + s, 0)
    cast_d = lambda s: (expert0 * k_d // rows_d + s, 0)
    w_gu = w_gu.reshape(n_e * k_gu, n_gu)
    w_down = w_down.reshape(n_e * k_d, n_d)
    c2 = lambda s: (0, 0)
    c3 = lambda s: (0, 0, 0)
    in_specs = [
        pl.BlockSpec((2 * ts, d), lambda s: (pair0 + s, 0)),
        pl.BlockSpec((ts, d), lambda s: (2 * pair0 + next_tile(s), 0)),
        pl.BlockSpec((1, d), c2),
        pl.BlockSpec((1, 1, d), bvec),
        pl.BlockSpec((1, 1, d), bvec),
        pl.BlockSpec((1, 1, d), bvec_next),
        pl.BlockSpec((1, 1, d), bvec_next),
        pl.BlockSpec((d, D_IN), c2),
        pl.BlockSpec((1, 1, d), bvec),
        pl.BlockSpec((1, 1, d), bvec),
        pl.BlockSpec((1, 1, d), bvec),
        pl.BlockSpec((CONV_WIDTH, D_RNN), c2),
        pl.BlockSpec((1, D_RNN), c2),
        pl.BlockSpec((N_GATE_BLOCKS, GATE_BLOCK, GATE_BLOCK), c3),
        pl.BlockSpec((1, D_RNN), c2),
        pl.BlockSpec((N_GATE_BLOCKS, GATE_BLOCK, GATE_BLOCK), c3),
        pl.BlockSpec((1, D_RNN), c2),
        pl.BlockSpec((1, D_RNN), c2),
        pl.BlockSpec((1, D_SG), c2),
        pl.BlockSpec((1, D_SG), c2),
        pl.BlockSpec((SG_GROUPS, SG_CHUNK, SG_CHUNK), c3),
        pl.BlockSpec((SG_CHUNK, D_SG), c2),
        pl.BlockSpec((D_RNN, d), c2),
        pl.BlockSpec((D_SG, d), c2),
        pl.BlockSpec((d, d), c2),
        pl.BlockSpec((1, d), c2),
        pl.BlockSpec((2 * N_EXPERTS, d), c2),
        pl.BlockSpec((N_EXPERTS, 1), c2),
        pl.BlockSpec((CONV_WIDTH - 1, ts, ts), c3),
        pl.BlockSpec((ts, ts), c2),
        pl.BlockSpec((rows_gu, n_gu), cast_gu),
        pl.BlockSpec((rows_d, n_d), cast_d),
    ]
    assert len(in_specs) == N_MIXER_INPUTS
    args = [x2d, x2d, n1g, sc1, sh1, sc1, sh1, w_in, gt1, sc2, sh2, conv_w, conv_b, wa_bd, ba,
            wx_bd, bx, lam, ln_g, ln_b, ws, bs_tile, wbr, wbs, wout, n2g, wr_split, br, shifts,
            before, w_gu, w_down]
    aliases = {}
    if cast_prev is not None:
        for j, prev in enumerate(cast_prev):
            in_specs.append(pl.BlockSpec(memory_space=pl.ANY))
            aliases[len(args)] = 7 + j
            args.append(prev)
    out_specs = [
        pl.BlockSpec((2 * ts, d), pair),
        pl.BlockSpec((2 * ts, PACK_W), pair),
        pl.BlockSpec((2 * ts, PACK_W), pair),
        pl.BlockSpec((TOP_K, 2 * ts), pairt),
        pl.BlockSpec((TOP_K, 2 * ts), pairt),
        pl.BlockSpec((TOP_K, 2 * ts), pairt),
        pl.BlockSpec((N_EXPERTS, 1), c2),
        pl.BlockSpec((rows_gu, n_gu), cast_gu),
        pl.BlockSpec((rows_d, n_d), cast_d),
    ]
    out_shape = [
        jax.ShapeDtypeStruct((n, d), F32),
        jax.ShapeDtypeStruct((n, PACK_W), jnp.uint32),
        jax.ShapeDtypeStruct((n, PACK_W), jnp.uint32),
        jax.ShapeDtypeStruct((TOP_K, n), I32),
        jax.ShapeDtypeStruct((TOP_K, n), F32),
        jax.ShapeDtypeStruct((TOP_K, n), I32),
        jax.ShapeDtypeStruct((N_EXPERTS, 1), F32),
        jax.ShapeDtypeStruct(w_gu.shape, BF16),
        jax.ShapeDtypeStruct(w_down.shape, BF16),
    ]
    scratch = [
        pltpu.VMEM((ts, D_IN), BF16),
        pltpu.VMEM((ts, D_IN), BF16),
        pltpu.VMEM((ts, d), BF16),
        pltpu.VMEM((ts, d), BF16),
        pltpu.VMEM((2 * SUBLANES, D_RNN), F32),
        pltpu.VMEM((ts, D_RNN), F32),
        pltpu.VMEM((ts, D_RNN), F32),
        pltpu.VMEM((ts, D_RNN), F32),
        pltpu.VMEM((ts, D_SG), F32),
        pltpu.VMEM((SUBLANES, D_RNN), F32),
    ]
    return pl.pallas_call(
        functools.partial(_mixer_kernel, pairs_per_seq=pairs_per_seq, n_aliased=len(aliases)),
        grid=(steps,),
        in_specs=in_specs,
        out_specs=out_specs,
        out_shape=out_shape,
        scratch_shapes=scratch,
        input_output_aliases=aliases,
        compiler_params=pltpu.CompilerParams(
            dimension_semantics=("arbitrary",), vmem_limit_bytes=VMEM_LIMIT),
        name="mixer",
    )(*args)


def _sc_mesh():
    return plsc.VectorSubcoreMesh(core_axis_name="core", subcore_axis_name="subcore")


def _sc_scatter_rows(rows, pos, cap):
    n, d = rows.shape
    kk = pos.shape[0]

    @functools.partial(
        pl.kernel, out_type=jax.ShapeDtypeStruct((cap, d), rows.dtype), mesh=_sc_mesh(),
        scratch_types=[], name="sc_scatter_rows")
    def scatter(x_hbm, i_hbm, o_hbm):
        def body(x_vmem, i_vmem):
            pltpu.sync_copy(x_vmem, o_hbm.at[i_vmem.at[0]])

        pltpu.emit_pipeline(
            body,
            grid=(n // SC_WINDOW, kk),
            in_specs=[pl.BlockSpec((SC_WINDOW, d), lambda i, k: (i, 0)),
                      pl.BlockSpec((1, SC_WINDOW), lambda i, k: (k, i))],
            out_specs=[],
            core_axis_name=("core", "subcore"),
            dimension_semantics=(pltpu.PARALLEL, pltpu.ARBITRARY),
        )(x_hbm, i_hbm)

    return scatter(rows, pos)


def _sc_gather_rows(table, idx):
    m = idx.shape[1]
    d = table.shape[1]

    @functools.partial(
        pl.kernel, out_type=jax.ShapeDtypeStruct((m, d), table.dtype), mesh=_sc_mesh(),
        scratch_types=[], name="sc_gather_rows")
    def gather(x_hbm, i_hbm, o_hbm):
        def body(i_vmem, o_vmem):
            pltpu.sync_copy(x_hbm.at[i_vmem.at[0]], o_vmem)

        pltpu.emit_pipeline(
            body,
            grid=(m // SC_WINDOW,),
            in_specs=[pl.BlockSpec((1, SC_WINDOW), lambda i: (0, i))],
            out_specs=[pl.BlockSpec((SC_WINDOW, d), lambda i: (i, 0))],
            core_axis_name=("core", "subcore"),
            dimension_semantics=(pltpu.PARALLEL,),
        )(i_hbm, o_hbm)

    return gather(table, idx)


def _pos_kernel(cnt_ref, idx_ref, rank_ref, pos_ref, be_ref, nv_ref):
    assert MOE_BM & (MOE_BM - 1) == 0
    starts, ends, run = [], [], jnp.int32(0)
    for e in range(N_EXPERTS):
        starts.append(run)
        run = run + ((cnt_ref[e] + (MOE_BM - 1)) & jnp.int32(-MOE_BM))
        ends.append(run)

    idx = idx_ref[...]
    pos = rank_ref[...]
    for e in range(N_EXPERTS):
        pos = pos + jnp.where(idx == e, starts[e], 0)
    pos_ref[...] = pos

    row0 = lax.broadcasted_iota(I32, be_ref.shape, 1) * MOE_BM
    be = jnp.zeros(be_ref.shape, I32)
    for e in range(N_EXPERTS - 1):
        be = be + jnp.where(ends[e] <= row0, 1, 0)
    start = jnp.zeros(be_ref.shape, I32)
    count = jnp.zeros(be_ref.shape, I32)
    for e in range(N_EXPERTS):
        here = be == e
        start = jnp.where(here, starts[e], start)
        count = jnp.where(here, cnt_ref[e], count)
    be_ref[...] = be
    nv_ref[...] = jnp.clip(count - (row0 - start), 0, MOE_BM)


def _slot_plan(counts, top_idx, rank, nb):
    k, n = top_idx.shape
    tn = min(POS_TN, n)
    nbp = -(-nb // LANES) * LANES
    spec = pl.BlockSpec((k, tn), lambda i, c: (0, i))
    blocks = pl.BlockSpec((1, nbp), lambda i, c: (0, 0))
    pos, be, nv = pl.pallas_call(
        _pos_kernel,
        grid_spec=pltpu.PrefetchScalarGridSpec(
            num_scalar_prefetch=1, grid=(n // tn,), in_specs=[spec, spec],
            out_specs=[spec, blocks, blocks]),
        out_shape=[jax.ShapeDtypeStruct((k, n), I32), jax.ShapeDtypeStruct((1, nbp), I32),
                   jax.ShapeDtypeStruct((1, nbp), I32)],
        compiler_params=pltpu.CompilerParams(dimension_semantics=("arbitrary",)),
        name="slot_plan",
    )(counts, top_idx, rank)
    return pos, be.reshape(nbp), nv.reshape(nbp)


def _expert_kernel(be_ref, nv_ref, xa_ref, xb_ref, wgu_ref, bgu_ref, wd_ref, bd_ref,
                   ya_ref, yb_ref):
    nvalid = nv_ref[pl.program_id(0)]

    def mlp_rows(h):
        rows = slice(h * MOE_HALF, (h + 1) * MOE_HALF)
        live = lax.broadcasted_iota(I32, (MOE_HALF, 1), 0) < nvalid - h * MOE_HALF
        xb = jnp.where(live, _unpack_halves(xa_ref[rows, :], xb_ref[rows, :]), 0.0).astype(BF16)
        gu = _bdot(xb, wgu_ref[0]) + bgu_ref[0]
        gate = jnp.minimum(gu[:, :D_EXPERT], SWIGLU_LIMIT)
        up = jnp.clip(gu[:, D_EXPERT:], -SWIGLU_LIMIT, SWIGLU_LIMIT)
        act = (up + 1.0) * (gate * _sigmoid(SWIGLU_ALPHA * gate))
        y = _bdot(act.astype(BF16), wd_ref[0]) + bd_ref[0]
        ya_ref[rows, :], yb_ref[rows, :] = _pack_halves(y)

    def zero_rows(h):
        rows = slice(h * MOE_HALF, (h + 1) * MOE_HALF)
        ya_ref[rows, :] = jnp.zeros((MOE_HALF, PACK_W), jnp.uint32)
        yb_ref[rows, :] = jnp.zeros((MOE_HALF, PACK_W), jnp.uint32)

    @pl.when(nvalid > MOE_HALF)
    def _():
        mlp_rows(0)
        mlp_rows(1)

    @pl.when((nvalid > 0) & (nvalid <= MOE_HALF))
    def _():
        mlp_rows(0)
        zero_rows(1)

    @pl.when(nvalid <= 0)
    def _():
        zero_rows(0)
        zero_rows(1)


def _experts(block_e, n_valid, xa, xb, w_gu, b_gu, w_down, b_down):
    cap = xa.shape[0]
    d = D_MODEL
    nb = cap // MOE_BM
    half = pl.BlockSpec((MOE_BM, PACK_W), lambda i, be, nv: (i, 0))
    grid_spec = pltpu.PrefetchScalarGridSpec(
        num_scalar_prefetch=2,
        grid=(nb,),
        in_specs=[
            half,
            half,
            pl.BlockSpec((1, d, 2 * D_EXPERT), lambda i, be, nv: (be[i], 0, 0)),
            pl.BlockSpec((1, 1, 2 * D_EXPERT), lambda i, be, nv: (be[i], 0, 0)),
            pl.BlockSpec((1, D_EXPERT, d), lambda i, be, nv: (be[i], 0, 0)),
            pl.BlockSpec((1, 1, d), lambda i, be, nv: (be[i], 0, 0)),
        ],
        out_specs=[half, half],
    )
    return pl.pallas_call(
        _expert_kernel,
        grid_spec=grid_spec,
        out_shape=[jax.ShapeDtypeStruct((cap, PACK_W), jnp.uint32)] * 2,
        compiler_params=pltpu.CompilerParams(
            dimension_semantics=("arbitrary",), vmem_limit_bytes=VMEM_LIMIT),
        name="experts",
    )(block_e, n_valid, xa, xb, w_gu, b_gu, w_down, b_down)


def _final_kernel(x2_ref, ya_ref, yb_ref, tw_ref, gt2_ref, fg_ref, *rest):
    o_ref = rest[-1]
    tw = tw_ref[...].T
    moe = tw[:, 0:1] * _unpack_halves(ya_ref[0], yb_ref[0])
    for k in range(1, TOP_K):
        moe = moe + tw[:, k:k + 1] * _unpack_halves(ya_ref[k], yb_ref[k])
    x3 = x2_ref[...] + gt2_ref[0] * moe
    ms = jnp.mean(x3 * x3, axis=-1, keepdims=True)
    o_ref[...] = (x3 * lax.rsqrt(ms + EPS)) * fg_ref[...]


def _final(x2, yga, ygb, tw_tok, gt2, final_g, seq, batch0, n_total, out_prev):
    n, d = x2.shape
    tiles_per_seq = seq // FINAL_TM
    tile0 = batch0 * tiles_per_seq
    half = pl.BlockSpec((TOP_K, FINAL_TM, PACK_W), lambda i: (0, i, 0))
    in_specs = [
        pl.BlockSpec((FINAL_TM, d), lambda i: (i, 0)),
        half,
        half,
        pl.BlockSpec((TOP_K, FINAL_TM), lambda i: (0, i)),
        pl.BlockSpec((1, 1, d), lambda i: (batch0 + i // tiles_per_seq, 0, 0)),
        pl.BlockSpec((1, d), lambda i: (0, 0)),
    ]
    args = [x2, yga, ygb, tw_tok, gt2, final_g]
    aliases = {}
    if out_prev is not None:
        in_specs.append(pl.BlockSpec(memory_space=pl.ANY))
        aliases = {len(args): 0}
        args.append(out_prev)
    return pl.pallas_call(
        _final_kernel,
        grid=(n // FINAL_TM,),
        in_specs=in_specs,
        out_specs=pl.BlockSpec((FINAL_TM, d), lambda i: (tile0 + i, 0)),
        out_shape=jax.ShapeDtypeStruct((n_total, d), F32),
        input_output_aliases=aliases,
        compiler_params=pltpu.CompilerParams(
            dimension_semantics=("arbitrary",), vmem_limit_bytes=VMEM_LIMIT),
        name="final",
    )(*args)


def _block_diag(w):
    per = GATE_BLOCK // RNN_HEAD_DIM
    w4 = w.reshape(N_GATE_BLOCKS, per, RNN_HEAD_DIM, RNN_HEAD_DIM)
    eye = jnp.eye(per, dtype=w.dtype)
    bd = jnp.einsum("gpij,pq->gpiqj", w4, eye)
    return bd.reshape(N_GATE_BLOCKS, GATE_BLOCK, GATE_BLOCK)


def _layer(x2d, c, batch, seq, ada_w, ada_b, norm1_g, w_in, conv_w, conv_b, lru_wa, lru_ba,
           lru_wx, lru_bx, lru_lam, sg_ln_g, sg_ln_b, sg_ws, sg_bs, w_br_rnn, w_br_sg, w_out,
           norm2_g, w_router, b_router, w_gu, b_gu, w_down, b_down, final_g):
    n_total, d = x2d.shape
    mod = _ada(c, ada_w, ada_b)
    sh1, sc1, gt1, sh2, sc2, gt2 = [
        mod[:, i * d:(i + 1) * d].reshape(batch, 1, d) for i in range(N_MOD)]
    row = lambda v: v.reshape(1, -1)

    bs_tile = jnp.repeat(sg_bs.T, SG_GROUP_DIM, axis=1)
    wr_t = w_router.T
    wr_hi = wr_t.astype(BF16)
    wr_lo = (wr_t - wr_hi.astype(F32)).astype(BF16)
    t_out = lax.broadcasted_iota(I32, (MIXER_TS, MIXER_TS), 0)
    t_in = lax.broadcasted_iota(I32, (MIXER_TS, MIXER_TS), 1)
    shifts = jnp.stack([(t_out - t_in == s) for s in range(1, CONV_WIDTH)]).astype(BF16)
    before = (t_out < t_in).astype(BF16)
    causal = jnp.tril(jnp.ones((SG_CHUNK, SG_CHUNK), dtype=bool))
    ws_causal = jnp.where(causal[None], sg_ws, 0.0).astype(BF16)
    mixer_weights = (
        conv_w, row(conv_b), _block_diag(0.5 * lru_wa).astype(BF16), row(0.5 * lru_ba),
        _block_diag(0.5 * lru_wx).astype(BF16), row(0.5 * lru_bx), row(lru_lam), row(sg_ln_g),
        row(sg_ln_b), ws_causal, bs_tile, (0.25 * w_br_rnn).astype(BF16), (0.5 * w_br_sg).astype(BF16),
        (0.5 * w_out).astype(BF16), row(norm2_g), jnp.concatenate([wr_hi, wr_lo], axis=0),
        b_router.reshape(N_EXPERTS, 1), shifts, before)
    gate_cols = 2 * D_RNN + 2 * D_SG
    col_scale = jnp.where(jnp.arange(D_IN) >= gate_cols, 0.5, 1.0).astype(F32)
    w_in_bf = (w_in * col_scale[None, :]).astype(BF16)

    groups = TOKEN_GROUPS if batch % TOKEN_GROUPS == 0 and N_EXPERTS % TOKEN_GROUPS == 0 else 1
    batches = batch // groups
    n_cast = N_EXPERTS // groups
    n = batches * seq
    mixed, cast = [], None
    for grp in range(groups):
        *outs, w_gu_bf, w_down_bf = _mixer(
            x2d, row(norm1_g), sc1, sh1, w_in_bf, gt1, sc2, sh2, *mixer_weights, w_gu, w_down,
            grp * batches, batches, seq, grp * n_cast, n_cast, cast)
        cast = (w_gu_bf, w_down_bf)
        mixed.append(outs)

    out = None
    for grp in range(groups):
        batch0 = grp * batches
        x2, h2a, h2b, top_idx, top_w, rank, counts = mixed[grp]

        cap = n * TOP_K + N_EXPERTS * MOE_BM
        nb = cap // MOE_BM
        pos, block_e, n_valid = _slot_plan(counts[:, 0].astype(I32), top_idx, rank, nb)

        xa = _sc_scatter_rows(h2a, pos, cap)
        xb = _sc_scatter_rows(h2b, pos, cap)
        ya, yb = _experts(block_e, n_valid, xa, xb, cast[0].reshape(w_gu.shape),
                          b_gu.reshape(N_EXPERTS, 1, -1), cast[1].reshape(w_down.shape),
                          b_down.reshape(N_EXPERTS, 1, -1))
        flat_pos = pos.reshape(1, -1)
        yga = _sc_gather_rows(ya, flat_pos).reshape(TOP_K, n, PACK_W)
        ygb = _sc_gather_rows(yb, flat_pos).reshape(TOP_K, n, PACK_W)
        out = _final(x2, yga, ygb, top_w, gt2, final_g.reshape(1, d), seq, batch0, n_total, out)
    return out


def kernel(x, c, ada_w, ada_b, norm1_g, w_in, conv_w, conv_b, lru_wa, lru_ba, lru_wx, lru_bx,
           lru_lam, sg_ln_g, sg_ln_b, sg_ws, sg_bs, w_br_rnn, w_br_sg, w_out, norm2_g,
           w_router, b_router, w_gu, b_gu, w_down, b_down, final_g):
    batch, seq, d = x.shape
    depth = ada_w.shape[0]
    assert depth == 1, "the combine is fused with the final norm, which follows the only layer"
    x2d = x.reshape(batch * seq, d)
    l = 0
    out = _layer(
        x2d, c, batch, seq, ada_w[l], ada_b[l], norm1_g[l], w_in[l], conv_w[l], conv_b[l],
        lru_wa[l], lru_ba[l], lru_wx[l], lru_bx[l], lru_lam[l], sg_ln_g[l], sg_ln_b[l],
        sg_ws[l], sg_bs[l], w_br_rnn[l], w_br_sg[l], w_out[l], norm2_g[l], w_router[l],
        b_router[l], w_gu[l], b_gu[l], w_down[l], b_down[l], final_g)
    return out.reshape(batch, seq, d)
```

```python
import functools

import jax
import jax.numpy as jnp
from jax import lax
from jax.experimental import pallas as pl
from jax.experimental.pallas import tpu as pltpu
from jax.experimental.pallas import tpu_sc as plsc

F32 = jnp.float32
BF16 = jnp.bfloat16
I32 = jnp.int32

D_MODEL = 1024
D_RNN = 1024
RNN_HEADS = 16
RNN_HEAD_DIM = D_RNN // RNN_HEADS
CONV_WIDTH = 4
LRU_C = 8.0
D_SG = 1024
SG_GROUPS = 8
SG_GROUP_DIM = D_SG // SG_GROUPS
SG_CHUNK = 128
N_EXPERTS = 32
TOP_K = 4
D_EXPERT = 1024
SWIGLU_LIMIT = 7.0
SWIGLU_ALPHA = 1.702
EPS = 1e-6
N_MOD = 6
D_IN = 2 * D_RNN + 2 * D_SG + 2 * D_MODEL

SUBLANES = 8
LANES = 128
GATE_BLOCK = 256
N_GATE_BLOCKS = D_RNN // GATE_BLOCK

ADA_TN = 1536
INPROJ_TN = 1024
MIXER_TS = 256
MOE_BM = 512
MOE_HALF = MOE_BM // 2
TOKEN_GROUPS = 2
SC_WINDOW = 128
PACK_W = D_MODEL // 4
FINAL_TM = 512
POS_TN = 8192
VMEM_LIMIT = 58 * 1024 * 1024


def _sigmoid(x):
    return 0.5 * jnp.tanh(0.5 * x) + 0.5


def _gelu_tanh_x2(x):
    k = 0.7978845608028654
    return x + x * jnp.tanh(x * (k + (k * 0.044715) * (x * x)))


def _bdot(a, b):
    return jnp.dot(a, b, preferred_element_type=F32)


def _pack_halves(v):
    word = pltpu.pack_elementwise([v[:, 2 * PACK_W:], v[:, :2 * PACK_W]], packed_dtype=BF16)
    return word[:, :PACK_W], word[:, PACK_W:]


def _unpack_halves(wa, wb):
    part = lambda w, i: pltpu.unpack_elementwise(w, index=i, packed_dtype=BF16, unpacked_dtype=F32)
    return jnp.concatenate([part(wa, 1), part(wb, 1), part(wa, 0), part(wb, 0)], axis=1)


def _ada_kernel(c_ref, w_ref, b_ref, o_ref):
    c = c_ref[...]
    s = c * _sigmoid(c)
    o_ref[...] = jnp.dot(s, w_ref[...], preferred_element_type=F32,
                         precision=lax.Precision.HIGHEST) + b_ref[...]


def _ada(c, ada_w, ada_b):
    b, d = c.shape
    n = ada_w.shape[1]
    return pl.pallas_call(
        _ada_kernel,
        grid=(n // ADA_TN,),
        in_specs=[
            pl.BlockSpec((b, d), lambda j: (0, 0)),
            pl.BlockSpec((d, ADA_TN), lambda j: (0, j)),
            pl.BlockSpec((1, ADA_TN), lambda j: (0, j)),
        ],
        out_specs=pl.BlockSpec((b, ADA_TN), lambda j: (0, j)),
        out_shape=jax.ShapeDtypeStruct((b, n), F32),
        compiler_params=pltpu.CompilerParams(
            dimension_semantics=("arbitrary",), vmem_limit_bytes=VMEM_LIMIT),
        name="ada",
    )(c, ada_w, ada_b.reshape(1, n))


def _norm_mod(x, g, sc, sh):
    ms = jnp.mean(x * x, axis=-1, keepdims=True)
    y = x * lax.rsqrt(ms + EPS)
    return y * (g * (1.0 + sc)) + sh


N_MIXER_INPUTS = 32
N_MIXER_TOKEN_OUTPUTS = 7


def _mixer_kernel(*refs, pairs_per_seq, n_aliased):
    (x_ref, xn_ref, n1g_ref, sc1_ref, sh1_ref, sc1n_ref, sh1n_ref, win_ref,
     gt1_ref, sc2_ref, sh2_ref,
     convw_ref, convb_ref, wa_ref, ba_ref, wx_ref, bx_ref, lam_ref,
     lng_ref, lnb_ref, ws_ref, bs_ref, wbr_ref, wbs_ref, wout_ref,
     n2g_ref, wr_ref, br_ref, shift_ref, before_ref, wgu_ref, wd_ref) = refs[:N_MIXER_INPUTS]
    (x2_ref, h2a_ref, h2b_ref, idx_ref, tw_ref, rank_ref, cnt_ref, wgu_bf_ref, wd_bf_ref,
     z0_ref, z1_ref, hn0_ref, hn1_ref, xp_ref, xc_ref, a_ref, hh_ref, sv_ref,
     hstate_ref) = refs[N_MIXER_INPUTS + n_aliased:]
    ts = MIXER_TS
    s = pl.program_id(0)

    wgu_bf_ref[...] = wgu_ref[...].astype(BF16)
    wd_bf_ref[...] = wd_ref[...].astype(BF16)

    def inproj_norm(hn_ref, x_rows, sc, sh):
        hn_ref[...] = _norm_mod(x_rows, n1g_ref[...], sc, sh).astype(BF16)

    def inproj_chunks(z_dst, hn_ref, first, last):
        for c in range(first * INPROJ_TN, last * INPROJ_TN, INPROJ_TN):
            z_dst[:, c:c + INPROJ_TN] = _bdot(hn_ref[...], win_ref[:, c:c + INPROJ_TN]).astype(BF16)

    @pl.when(s == 0)
    def _():
        cnt_ref[...] = jnp.zeros_like(cnt_ref)
        inproj_norm(hn0_ref, x_ref[0:ts, :], sc1_ref[0], sh1_ref[0])
        inproj_chunks(z0_ref, hn0_ref, 0, D_IN // INPROJ_TN)

    @pl.when(s % pairs_per_seq == 0)
    def _():
        xp_ref[0:SUBLANES, :] = jnp.zeros((SUBLANES, D_RNN), F32)
        hstate_ref[...] = jnp.zeros_like(hstate_ref)

    def tile(k, z_ref, z_next_ref, hn_ref, next_rows, next_sc, next_sh):
        rows = slice(k * ts, (k + 1) * ts)
        inproj_norm(hn_ref, next_rows, next_sc, next_sh)

        x16 = z_ref[:, 0:D_RNN]
        rnn_x = x16.astype(F32)
        cw = convw_ref[...]
        xc = cw[3:4] * rnn_x + convb_ref[...]
        for sft in range(1, CONV_WIDTH):
            xc = xc + cw[3 - sft:4 - sft] * _bdot(shift_ref[sft - 1], x16)
        xc_ref[...] = xc
        xp_ref[SUBLANES:2 * SUBLANES, :] = rnn_x[0:SUBLANES, :]
        xc_ref[0:SUBLANES, :] = (
            cw[3:4] * xp_ref[SUBLANES:2 * SUBLANES, :]
            + cw[2:3] * xp_ref[SUBLANES - 1:2 * SUBLANES - 1, :]
            + cw[1:2] * xp_ref[SUBLANES - 2:2 * SUBLANES - 2, :]
            + cw[0:1] * xp_ref[SUBLANES - 3:2 * SUBLANES - 3, :]) + convb_ref[...]
        xp_ref[0:SUBLANES, :] = rnn_x[ts - SUBLANES:ts, :]
        xc = xc_ref[...]

        xcb = xc.astype(BF16)
        r_parts, i_parts = [], []
        for g in range(N_GATE_BLOCKS):
            blk = xcb[:, g * GATE_BLOCK:(g + 1) * GATE_BLOCK]
            r_parts.append(_bdot(blk, wa_ref[g]))
            i_parts.append(_bdot(blk, wx_ref[g]))
        inproj_chunks(z_next_ref, hn_ref, 0, 3)
        r_t = jnp.tanh(jnp.concatenate(r_parts, axis=1) + ba_ref[...])
        i_t = jnp.tanh(jnp.concatenate(i_parts, axis=1) + bx_ref[...])

        nl = -lam_ref[...]
        softplus = jnp.maximum(nl, 0.0) + jnp.log(1.0 + jnp.exp(-jnp.abs(nl)))
        half_c = (-0.5 * LRU_C) * softplus
        a = jnp.exp(r_t * half_c + half_c)
        t = 1.0 - a * a
        u = jnp.where(t > 0.0, t * lax.rsqrt(t), 0.0) * ((i_t + 1.0) * xc)

        groups = ts // SUBLANES
        a3 = a.reshape(groups, SUBLANES, D_RNN)
        h3 = u.reshape(groups, SUBLANES, D_RNN)
        sub = lax.broadcasted_iota(I32, (groups, SUBLANES, D_RNN), 1)
        for step in (1, 2, 4):
            keep = sub >= step
            a_sh = jnp.where(keep, pltpu.roll(a3, step, 1), 1.0)
            h_sh = jnp.where(keep, pltpu.roll(h3, step, 1), 0.0)
            h3 = h3 + a3 * h_sh
            a3 = a3 * a_sh
        a_ref[...] = a3.reshape(ts, D_RNN)
        hh_ref[...] = h3.reshape(ts, D_RNN)

        hc = hstate_ref[...]
        for gi in range(groups):
            grp = slice(gi * SUBLANES, (gi + 1) * SUBLANES)
            hg = hh_ref[grp, :] + a_ref[grp, :] * hc
            hh_ref[grp, :] = hg
            hc = jnp.broadcast_to(hg[SUBLANES - 1:SUBLANES, :], (SUBLANES, D_RNN))
        hstate_ref[...] = hc

        inproj_chunks(z_next_ref, hn_ref, 3, 4)
        y_rnn = (hh_ref[...] * _gelu_tanh_x2(z_ref[:, D_RNN:2 * D_RNN].astype(F32))).astype(BF16)

        gv = _gelu_tanh_x2(z_ref[:, 2 * D_RNN + D_SG:2 * D_RNN + 2 * D_SG].astype(F32))
        mu = jnp.mean(gv, axis=-1, keepdims=True)
        dv = gv - mu
        var = jnp.mean(dv * dv, axis=-1, keepdims=True)
        vn = (dv * lax.rsqrt(var + 4.0 * EPS) * lng_ref[...] + lnb_ref[...]).astype(BF16)
        for g in range(SG_GROUPS):
            wg = ws_ref[g]
            cols = slice(g * SG_GROUP_DIM, (g + 1) * SG_GROUP_DIM)
            for n in range(ts // SG_CHUNK):
                chunk = slice(n * SG_CHUNK, (n + 1) * SG_CHUNK)
                sv_ref[chunk, cols] = _bdot(wg, vn[chunk, cols]) + bs_ref[:, cols]
        inproj_chunks(z_next_ref, hn_ref, 4, 5)
        gu = _gelu_tanh_x2(z_ref[:, 2 * D_RNN:2 * D_RNN + D_SG].astype(F32))
        y_sg = (gu * sv_ref[...]).astype(BF16)

        g_rnn = z_ref[:, 2 * D_RNN + 2 * D_SG:2 * D_RNN + 2 * D_SG + D_MODEL].astype(F32)
        g_sg = z_ref[:, 2 * D_RNN + 2 * D_SG + D_MODEL:D_IN].astype(F32)
        m = ((jnp.tanh(g_rnn) + 1.0) * _bdot(y_rnn, wbr_ref[...])
             + (jnp.tanh(g_sg) + 1.0) * _bdot(y_sg, wbs_ref[...])).astype(BF16)
        x2 = x_ref[rows, :] + gt1_ref[0] * _bdot(m, wout_ref[...])
        x2_ref[rows, :] = x2
        inproj_chunks(z_next_ref, hn_ref, 5, 6)

        h2 = _norm_mod(x2, n2g_ref[...], sc2_ref[0], sh2_ref[0])
        h2a_ref[rows, :], h2b_ref[rows, :] = _pack_halves(h2)
        h_hi = h2.astype(BF16)
        h_lo = (h2 - h_hi.astype(F32)).astype(BF16)
        nt_dims = (((1,), (1,)), ((), ()))
        by_hi = lax.dot_general(wr_ref[...], h_hi, nt_dims, preferred_element_type=F32)
        logits = (by_hi[:N_EXPERTS] + by_hi[N_EXPERTS:]
                  + lax.dot_general(wr_ref[0:N_EXPERTS, :], h_lo, nt_dims,
                                    preferred_element_type=F32)
                  + br_ref[...])
        e_iota = lax.broadcasted_iota(I32, (N_EXPERTS, ts), 0)
        v = logits
        vals, idxs, sels = [], [], []
        for _ in range(TOP_K):
            mx = jnp.max(v, axis=0, keepdims=True)
            ik = jnp.min(jnp.where(v == mx, e_iota, N_EXPERTS), axis=0, keepdims=True)
            sel = e_iota == ik
            v = jnp.where(sel, -jnp.inf, v)
            vals.append(mx)
            idxs.append(ik)
            sels.append(sel)
        exps = [jnp.exp(val - vals[0]) for val in vals]
        denom = exps[0] + exps[1] + exps[2] + exps[3]
        idx_ref[:, rows] = jnp.concatenate(idxs, axis=0)
        tw_ref[:, rows] = jnp.concatenate([e / denom for e in exps], axis=0)

        onehot = jnp.zeros((N_EXPERTS, ts), F32)
        for sel in sels:
            onehot = jnp.where(sel, 1.0, onehot)
        total = cnt_ref[...] + _bdot(onehot.astype(BF16), before_ref[...])
        ranks = [jnp.sum(jnp.where(sel, total, 0.0), axis=0, keepdims=True) for sel in sels]
        rank_ref[:, rows] = jnp.concatenate(ranks, axis=0).astype(I32)
        cnt_ref[...] = cnt_ref[...] + jnp.sum(onehot, axis=1, keepdims=True)

    tile(0, z0_ref, z1_ref, hn1_ref, x_ref[ts:2 * ts, :], sc1_ref[0], sh1_ref[0])
    tile(1, z1_ref, z0_ref, hn0_ref, xn_ref[...], sc1n_ref[0], sh1n_ref[0])


def _mixer(x2d, n1g, sc1, sh1, w_in, gt1, sc2, sh2, conv_w, conv_b, wa_bd, ba, wx_bd, bx, lam,
           ln_g, ln_b, ws, bs_tile, wbr, wbs, wout, n2g, wr_split, br, shifts, before, w_gu, w_down,
           batch0, batches, seq, expert0, n_cast, cast_prev):
    d = x2d.shape[1]
    ts = MIXER_TS
    n = batches * seq
    tiles_per_seq = seq // ts
    pairs_per_seq = tiles_per_seq // 2
    assert tiles_per_seq % 2 == 0
    steps = batches * pairs_per_seq
    pair0 = batch0 * pairs_per_seq
    next_tile = lambda s: jnp.minimum(2 * s + 2, 2 * steps - 1)
    pair = lambda s: (s, 0)
    pairt = lambda s: (0, s)
    bvec = lambda s: (batch0 + s // pairs_per_seq, 0, 0)
    bvec_next = lambda s: (batch0 + next_tile(s) // tiles_per_seq, 0, 0)
    n_e, k_gu, n_gu = w_gu.shape
    _, k_d, n_d = w_down.shape
    assert (n_cast * k_gu) % steps == 0 and (n_cast * k_d) % steps == 0
    rows_gu, rows_d = n_cast * k_gu // steps, n_cast * k_d // steps
    cast_gu = lambda s: (expert0 * k_gu // rows_gu + s, 0)
    cast_d = lambda s: (expert0 * k_d // rows_d + s, 0)
    w_gu = w_gu.reshape(n_e * k_gu, n_gu)
    w_down = w_down.reshape(n_e * k_d, n_d)
    c2 = lambda s: (0, 0)
    c3 = lambda s: (0, 0, 0)
    in_specs = [
        pl.BlockSpec((2 * ts, d), lambda s: (pair0 + s, 0)),
        pl.BlockSpec((ts, d), lambda s: (2 * pair0 + next_tile(s), 0)),
        pl.BlockSpec((1, d), c2),
        pl.BlockSpec((1, 1, d), bvec),
        pl.BlockSpec((1, 1, d), bvec),
        pl.BlockSpec((1, 1, d), bvec_next),
        pl.BlockSpec((1, 1, d), bvec_next),
        pl.BlockSpec((d, D_IN), c2),
        pl.BlockSpec((1, 1, d), bvec),
        pl.BlockSpec((1, 1, d), bvec),
        pl.BlockSpec((1, 1, d), bvec),
        pl.BlockSpec((CONV_WIDTH, D_RNN), c2),
        pl.BlockSpec((1, D_RNN), c2),
        pl.BlockSpec((N_GATE_BLOCKS, GATE_BLOCK, GATE_BLOCK), c3),
        pl.BlockSpec((1, D_RNN), c2),
        pl.BlockSpec((N_GATE_BLOCKS, GATE_BLOCK, GATE_BLOCK), c3),
        pl.BlockSpec((1, D_RNN), c2),
        pl.BlockSpec((1, D_RNN), c2),
        pl.BlockSpec((1, D_SG), c2),
        pl.BlockSpec((1, D_SG), c2),
        pl.BlockSpec((SG_GROUPS, SG_CHUNK, SG_CHUNK), c3),
        pl.BlockSpec((SG_CHUNK, D_SG), c2),
        pl.BlockSpec((D_RNN, d), c2),
        pl.BlockSpec((D_SG, d), c2),
        pl.BlockSpec((d, d), c2),
        pl.BlockSpec((1, d), c2),
        pl.BlockSpec((2 * N_EXPERTS, d), c2),
        pl.BlockSpec((N_EXPERTS, 1), c2),
        pl.BlockSpec((CONV_WIDTH - 1, ts, ts), c3),
        pl.BlockSpec((ts, ts), c2),
        pl.BlockSpec((rows_gu, n_gu), cast_gu),
        pl.BlockSpec((rows_d, n_d), cast_d),
    ]
    assert len(in_specs) == N_MIXER_INPUTS
    args = [x2d, x2d, n1g, sc1, sh1, sc1, sh1, w_in, gt1, sc2, sh2, conv_w, conv_b, wa_bd, ba,
            wx_bd, bx, lam, ln_g, ln_b, ws, bs_tile, wbr, wbs, wout, n2g, wr_split, br, shifts,
            before, w_gu, w_down]
    aliases = {}
    if cast_prev is not None:
        for j, prev in enumerate(cast_prev):
            in_specs.append(pl.BlockSpec(memory_space=pl.ANY))
            aliases[len(args)] = N_MIXER_TOKEN_OUTPUTS + j
            args.append(prev)
    out_specs = [
        pl.BlockSpec((2 * ts, d), pair),
        pl.BlockSpec((2 * ts, PACK_W), pair),
        pl.BlockSpec((2 * ts, PACK_W), pair),
        pl.BlockSpec((TOP_K, 2 * ts), pairt),
        pl.BlockSpec((TOP_K, 2 * ts), pairt),
        pl.BlockSpec((TOP_K, 2 * ts), pairt),
        pl.BlockSpec((N_EXPERTS, 1), c2),
        pl.BlockSpec((rows_gu, n_gu), cast_gu),
        pl.BlockSpec((rows_d, n_d), cast_d),
    ]
    out_shape = [
        jax.ShapeDtypeStruct((n, d), F32),
        jax.ShapeDtypeStruct((n, PACK_W), jnp.uint32),
        jax.ShapeDtypeStruct((n, PACK_W), jnp.uint32),
        jax.ShapeDtypeStruct((TOP_K, n), I32),
        jax.ShapeDtypeStruct((TOP_K, n), F32),
        jax.ShapeDtypeStruct((TOP_K, n), I32),
        jax.ShapeDtypeStruct((N_EXPERTS, 1), F32),
        jax.ShapeDtypeStruct(w_gu.shape, BF16),
        jax.ShapeDtypeStruct(w_down.shape, BF16),
    ]
    scratch = [
        pltpu.VMEM((ts, D_IN), BF16),
        pltpu.VMEM((ts, D_IN), BF16),
        pltpu.VMEM((ts, d), BF16),
        pltpu.VMEM((ts, d), BF16),
        pltpu.VMEM((2 * SUBLANES, D_RNN), F32),
        pltpu.VMEM((ts, D_RNN), F32),
        pltpu.VMEM((ts, D_RNN), F32),
        pltpu.VMEM((ts, D_RNN), F32),
        pltpu.VMEM((ts, D_SG), F32),
        pltpu.VMEM((SUBLANES, D_RNN), F32),
    ]
    return pl.pallas_call(
        functools.partial(_mixer_kernel, pairs_per_seq=pairs_per_seq, n_aliased=len(aliases)),
        grid=(steps,),
        in_specs=in_specs,
        out_specs=out_specs,
        out_shape=out_shape,
        scratch_shapes=scratch,
        input_output_aliases=aliases,
        compiler_params=pltpu.CompilerParams(
            dimension_semantics=("arbitrary",), vmem_limit_bytes=VMEM_LIMIT),
        name="mixer",
    )(*args)


def _sc_mesh():
    return plsc.VectorSubcoreMesh(core_axis_name="core", subcore_axis_name="subcore")


def _sc_scatter_rows(rows, pos, cap):
    n, d = rows.shape
    kk = pos.shape[0]

    @functools.partial(
        pl.kernel, out_type=jax.ShapeDtypeStruct((cap, d), rows.dtype), mesh=_sc_mesh(),
        scratch_types=[], name="sc_scatter_rows")
    def scatter(x_hbm, i_hbm, o_hbm):
        def body(x_vmem, i_vmem):
            pltpu.sync_copy(x_vmem, o_hbm.at[i_vmem.at[0]])

        pltpu.emit_pipeline(
            body,
            grid=(n // SC_WINDOW, kk),
            in_specs=[pl.BlockSpec((SC_WINDOW, d), lambda i, k: (i, 0)),
                      pl.BlockSpec((1, SC_WINDOW), lambda i, k: (k, i))],
            out_specs=[],
            core_axis_name=("core", "subcore"),
            dimension_semantics=(pltpu.PARALLEL, pltpu.ARBITRARY),
        )(x_hbm, i_hbm)

    return scatter(rows, pos)


def _sc_gather_rows(table, idx):
    m = idx.shape[1]
    d = table.shape[1]

    @functools.partial(
        pl.kernel, out_type=jax.ShapeDtypeStruct((m, d), table.dtype), mesh=_sc_mesh(),
        scratch_types=[], name="sc_gather_rows")
    def gather(x_hbm, i_hbm, o_hbm):
        def body(i_vmem, o_vmem):
            pltpu.sync_copy(x_hbm.at[i_vmem.at[0]], o_vmem)

        pltpu.emit_pipeline(
            body,
            grid=(m // SC_WINDOW,),
            in_specs=[pl.BlockSpec((1, SC_WINDOW), lambda i: (0, i))],
            out_specs=[pl.BlockSpec((SC_WINDOW, d), lambda i: (i, 0))],
            core_axis_name=("core", "subcore"),
            dimension_semantics=(pltpu.PARALLEL,),
        )(i_hbm, o_hbm)

    return gather(table, idx)


def _pos_kernel(cnt_ref, idx_ref, rank_ref, pos_ref, be_ref, nv_ref):
    assert MOE_BM & (MOE_BM - 1) == 0
    starts, ends, run = [], [], jnp.int32(0)
    for e in range(N_EXPERTS):
        starts.append(run)
        run = run + ((cnt_ref[e] + (MOE_BM - 1)) & jnp.int32(-MOE_BM))
        ends.append(run)

    idx = idx_ref[...]
    pos = rank_ref[...]
    for e in range(N_EXPERTS):
        pos = pos + jnp.where(idx == e, starts[e], 0)
    pos_ref[...] = pos

    row0 = lax.broadcasted_iota(I32, be_ref.shape, 1) * MOE_BM
    be = jnp.zeros(be_ref.shape, I32)
    for e in range(N_EXPERTS - 1):
        be = be + jnp.where(ends[e] <= row0, 1, 0)
    start = jnp.zeros(be_ref.shape, I32)
    count = jnp.zeros(be_ref.shape, I32)
    for e in range(N_EXPERTS):
        here = be == e
        start = jnp.where(here, starts[e], start)
        count = jnp.where(here, cnt_ref[e], count)
    be_ref[...] = be
    nv_ref[...] = jnp.clip(count - (row0 - start), 0, MOE_BM)


def _slot_plan(counts, top_idx, rank, nb):
    k, n = top_idx.shape
    tn = min(POS_TN, n)
    nbp = -(-nb // LANES) * LANES
    spec = pl.BlockSpec((k, tn), lambda i, c: (0, i))
    blocks = pl.BlockSpec((1, nbp), lambda i, c: (0, 0))
    pos, be, nv = pl.pallas_call(
        _pos_kernel,
        grid_spec=pltpu.PrefetchScalarGridSpec(
            num_scalar_prefetch=1, grid=(n // tn,), in_specs=[spec, spec],
            out_specs=[spec, blocks, blocks]),
        out_shape=[jax.ShapeDtypeStruct((k, n), I32), jax.ShapeDtypeStruct((1, nbp), I32),
                   jax.ShapeDtypeStruct((1, nbp), I32)],
        compiler_params=pltpu.CompilerParams(dimension_semantics=("arbitrary",)),
        name="slot_plan",
    )(counts, top_idx, rank)
    return pos, be.reshape(nbp), nv.reshape(nbp)


def _expert_kernel(be_ref, nv_ref, xa_ref, xb_ref, wgu_ref, bgu_ref, wd_ref, bd_ref,
                   ya_ref, yb_ref):
    nvalid = nv_ref[pl.program_id(0)]

    def mlp_rows(h):
        rows = slice(h * MOE_HALF, (h + 1) * MOE_HALF)
        live = lax.broadcasted_iota(I32, (MOE_HALF, 1), 0) < nvalid - h * MOE_HALF
        xb = jnp.where(live, _unpack_halves(xa_ref[rows, :], xb_ref[rows, :]), 0.0).astype(BF16)
        gu = _bdot(xb, wgu_ref[0]) + bgu_ref[0]
        gate = jnp.minimum(gu[:, :D_EXPERT], SWIGLU_LIMIT)
        up = jnp.clip(gu[:, D_EXPERT:], -SWIGLU_LIMIT, SWIGLU_LIMIT)
        act = (up + 1.0) * (gate * _sigmoid(SWIGLU_ALPHA * gate))
        y = _bdot(act.astype(BF16), wd_ref[0]) + bd_ref[0]
        ya_ref[rows, :], yb_ref[rows, :] = _pack_halves(y)

    def zero_rows(h):
        rows = slice(h * MOE_HALF, (h + 1) * MOE_HALF)
        ya_ref[rows, :] = jnp.zeros((MOE_HALF, PACK_W), jnp.uint32)
        yb_ref[rows, :] = jnp.zeros((MOE_HALF, PACK_W), jnp.uint32)

    @pl.when(nvalid > MOE_HALF)
    def _():
        mlp_rows(0)
        mlp_rows(1)

    @pl.when((nvalid > 0) & (nvalid <= MOE_HALF))
    def _():
        mlp_rows(0)
        zero_rows(1)

    @pl.when(nvalid <= 0)
    def _():
        zero_rows(0)
        zero_rows(1)


def _experts(block_e, n_valid, xa, xb, w_gu, b_gu, w_down, b_down):
    cap = xa.shape[0]
    d = D_MODEL
    nb = cap // MOE_BM
    half = pl.BlockSpec((MOE_BM, PACK_W), lambda i, be, nv: (i, 0))
    grid_spec = pltpu.PrefetchScalarGridSpec(
        num_scalar_prefetch=2,
        grid=(nb,),
        in_specs=[
            half,
            half,
            pl.BlockSpec((1, d, 2 * D_EXPERT), lambda i, be, nv: (be[i], 0, 0)),
            pl.BlockSpec((1, 1, 2 * D_EXPERT), lambda i, be, nv: (be[i], 0, 0)),
            pl.BlockSpec((1, D_EXPERT, d), lambda i, be, nv: (be[i], 0, 0)),
            pl.BlockSpec((1, 1, d), lambda i, be, nv: (be[i], 0, 0)),
        ],
        out_specs=[half, half],
    )
    return pl.pallas_call(
        _expert_kernel,
        grid_spec=grid_spec,
        out_shape=[jax.ShapeDtypeStruct((cap, PACK_W), jnp.uint32)] * 2,
        compiler_params=pltpu.CompilerParams(
            dimension_semantics=("arbitrary",), vmem_limit_bytes=VMEM_LIMIT),
        name="experts",
    )(block_e, n_valid, xa, xb, w_gu, b_gu, w_down, b_down)


def _final_kernel(x2_ref, ya_ref, yb_ref, tw_ref, gt2_ref, fg_ref, *rest):
    o_ref = rest[-1]
    tw = tw_ref[...].T
    moe = tw[:, 0:1] * _unpack_halves(ya_ref[0], yb_ref[0])
    for k in range(1, TOP_K):
        moe = moe + tw[:, k:k + 1] * _unpack_halves(ya_ref[k], yb_ref[k])
    x3 = x2_ref[...] + gt2_ref[0] * moe
    ms = jnp.mean(x3 * x3, axis=-1, keepdims=True)
    o_ref[...] = (x3 * lax.rsqrt(ms + EPS)) * fg_ref[...]


def _final(x2, yga, ygb, tw_tok, gt2, final_g, seq, batch0, n_total, out_prev):
    n, d = x2.shape
    tiles_per_seq = seq // FINAL_TM
    tile0 = batch0 * tiles_per_seq
    half = pl.BlockSpec((TOP_K, FINAL_TM, PACK_W), lambda i: (0, i, 0))
    in_specs = [
        pl.BlockSpec((FINAL_TM, d), lambda i: (i, 0)),
        half,
        half,
        pl.BlockSpec((TOP_K, FINAL_TM), lambda i: (0, i)),
        pl.BlockSpec((1, 1, d), lambda i: (batch0 + i // tiles_per_seq, 0, 0)),
        pl.BlockSpec((1, d), lambda i: (0, 0)),
    ]
    args = [x2, yga, ygb, tw_tok, gt2, final_g]
    aliases = {}
    if out_prev is not None:
        in_specs.append(pl.BlockSpec(memory_space=pl.ANY))
        aliases = {len(args): 0}
        args.append(out_prev)
    return pl.pallas_call(
        _final_kernel,
        grid=(n // FINAL_TM,),
        in_specs=in_specs,
        out_specs=pl.BlockSpec((FINAL_TM, d), lambda i: (tile0 + i, 0)),
        out_shape=jax.ShapeDtypeStruct((n_total, d), F32),
        input_output_aliases=aliases,
        compiler_params=pltpu.CompilerParams(
            dimension_semantics=("arbitrary",), vmem_limit_bytes=VMEM_LIMIT),
        name="final",
    )(*args)


def _block_diag(w):
    per = GATE_BLOCK // RNN_HEAD_DIM
    w4 = w.reshape(N_GATE_BLOCKS, per, RNN_HEAD_DIM, RNN_HEAD_DIM)
    eye = jnp.eye(per, dtype=w.dtype)
    bd = jnp.einsum("gpij,pq->gpiqj", w4, eye)
    return bd.reshape(N_GATE_BLOCKS, GATE_BLOCK, GATE_BLOCK)


def _layer(x2d, c, batch, seq, ada_w, ada_b, norm1_g, w_in, conv_w, conv_b, lru_wa, lru_ba,
           lru_wx, lru_bx, lru_lam, sg_ln_g, sg_ln_b, sg_ws, sg_bs, w_br_rnn, w_br_sg, w_out,
           norm2_g, w_router, b_router, w_gu, b_gu, w_down, b_down, final_g):
    n_total, d = x2d.shape
    mod = _ada(c, ada_w, ada_b)
    sh1, sc1, gt1, sh2, sc2, gt2 = [
        mod[:, i * d:(i + 1) * d].reshape(batch, 1, d) for i in range(N_MOD)]
    row = lambda v: v.reshape(1, -1)

    bs_tile = jnp.repeat(sg_bs.T, SG_GROUP_DIM, axis=1)
    wr_t = w_router.T
    wr_hi = wr_t.astype(BF16)
    wr_lo = (wr_t - wr_hi.astype(F32)).astype(BF16)
    t_out = lax.broadcasted_iota(I32, (MIXER_TS, MIXER_TS), 0)
    t_in = lax.broadcasted_iota(I32, (MIXER_TS, MIXER_TS), 1)
    shifts = jnp.stack([(t_out - t_in == s) for s in range(1, CONV_WIDTH)]).astype(BF16)
    before = (t_out < t_in).astype(BF16)
    causal = jnp.tril(jnp.ones((SG_CHUNK, SG_CHUNK), dtype=bool))
    ws_causal = jnp.where(causal[None], sg_ws, 0.0).astype(BF16)
    mixer_weights = (
        conv_w, row(conv_b), _block_diag(0.5 * lru_wa).astype(BF16), row(0.5 * lru_ba),
        _block_diag(0.5 * lru_wx).astype(BF16), row(0.5 * lru_bx), row(lru_lam), row(sg_ln_g),
        row(sg_ln_b), ws_causal, bs_tile, (0.25 * w_br_rnn).astype(BF16), (0.5 * w_br_sg).astype(BF16),
        (0.5 * w_out).astype(BF16), row(norm2_g), jnp.concatenate([wr_hi, wr_lo], axis=0),
        b_router.reshape(N_EXPERTS, 1), shifts, before)
    gate_cols = 2 * D_RNN + 2 * D_SG
    col_scale = jnp.where(jnp.arange(D_IN) >= gate_cols, 0.5, 1.0).astype(F32)
    w_in_bf = (w_in * col_scale[None, :]).astype(BF16)

    groups = TOKEN_GROUPS if batch % TOKEN_GROUPS == 0 and N_EXPERTS % TOKEN_GROUPS == 0 else 1
    batches = batch // groups
    n_cast = N_EXPERTS // groups
    n = batches * seq
    mixed, cast = [], None
    for grp in range(groups):
        *outs, w_gu_bf, w_down_bf = _mixer(
            x2d, row(norm1_g), sc1, sh1, w_in_bf, gt1, sc2, sh2, *mixer_weights, w_gu, w_down,
            grp * batches, batches, seq, grp * n_cast, n_cast, cast)
        cast = (w_gu_bf, w_down_bf)
        mixed.append(outs)

    out = None
    for grp in range(groups):
        batch0 = grp * batches
        x2, h2a, h2b, top_idx, top_w, rank, counts = mixed[grp]

        cap = n * TOP_K + N_EXPERTS * MOE_BM
        nb = cap // MOE_BM
        pos, block_e, n_valid = _slot_plan(counts[:, 0].astype(I32), top_idx, rank, nb)

        xa = _sc_scatter_rows(h2a, pos, cap)
        xb = _sc_scatter_rows(h2b, pos, cap)
        ya, yb = _experts(block_e, n_valid, xa, xb, cast[0].reshape(w_gu.shape),
                          b_gu.reshape(N_EXPERTS, 1, -1), cast[1].reshape(w_down.shape),
                          b_down.reshape(N_EXPERTS, 1, -1))
        flat_pos = pos.reshape(1, -1)
        yga = _sc_gather_rows(ya, flat_pos).reshape(TOP_K, n, PACK_W)
        ygb = _sc_gather_rows(yb, flat_pos).reshape(TOP_K, n, PACK_W)
        out = _final(x2, yga, ygb, top_w, gt2, final_g.reshape(1, d), seq, batch0, n_total, out)
    return out


def kernel(x, c, ada_w, ada_b, norm1_g, w_in, conv_w, conv_b, lru_wa, lru_ba, lru_wx, lru_bx,
           lru_lam, sg_ln_g, sg_ln_b, sg_ws, sg_bs, w_br_rnn, w_br_sg, w_out, norm2_g,
           w_router, b_router, w_gu, b_gu, w_down, b_down, final_g):
    batch, seq, d = x.shape
    depth = ada_w.shape[0]
    assert depth == 1, "the combine is fused with the final norm, which follows the only layer"
    x2d = x.reshape(batch * seq, d)
    l = 0
    out = _layer(
        x2d, c, batch, seq, ada_w[l], ada_b[l], norm1_g[l], w_in[l], conv_w[l], conv_b[l],
        lru_wa[l], lru_ba[l], lru_wx[l], lru_bx[l], lru_lam[l], sg_ln_g[l], sg_ln_b[l],
        sg_ws[l], sg_bs[l], w_br_rnn[l], w_br_sg[l], w_out[l], norm2_g[l], w_router[l],
        b_router[l], w_gu[l], b_gu[l], w_down[l], b_down[l], final_g)
    return out.reshape(batch, seq, d)
```

```python
import functools

import jax
import jax.numpy as jnp
from jax import lax
from jax.experimental import pallas as pl
from jax.experimental.pallas import tpu as pltpu
from jax.experimental.pallas import tpu_sc as plsc

F32 = jnp.float32
BF16 = jnp.bfloat16
I32 = jnp.int32

D_MODEL = 1024
D_RNN = 1024
RNN_HEADS = 16
RNN_HEAD_DIM = D_RNN // RNN_HEADS
CONV_WIDTH = 4
LRU_C = 8.0
D_SG = 1024
SG_GROUPS = 8
SG_GROUP_DIM = D_SG // SG_GROUPS
SG_CHUNK = 128
N_EXPERTS = 32
TOP_K = 4
D_EXPERT = 1024
SWIGLU_LIMIT = 7.0
SWIGLU_ALPHA = 1.702
EPS = 1e-6
N_MOD = 6
D_IN = 2 * D_RNN + 2 * D_SG + 2 * D_MODEL

SUBLANES = 8
LANES = 128
GATE_BLOCK = 256
N_GATE_BLOCKS = D_RNN // GATE_BLOCK

ADA_TN = 1536
INPROJ_TN = 1024
MIXER_TS = 256
MOE_BM = 512
MOE_HALF = MOE_BM // 2
TOKEN_GROUPS = 2
SC_WINDOW = 128
PACK_W = D_MODEL // 4
FINAL_TM = 512
POS_TN = 8192
VMEM_LIMIT = 60 * 1024 * 1024


def _sigmoid(x):
    return 0.5 * jnp.tanh(0.5 * x) + 0.5


def _gelu_tanh_x2(x):
    k = 0.7978845608028654
    return x + x * jnp.tanh(x * (k + (k * 0.044715) * (x * x)))


def _bdot(a, b):
    return jnp.dot(a, b, preferred_element_type=F32)


def _pack_halves(v):
    word = pltpu.pack_elementwise([v[:, 2 * PACK_W:], v[:, :2 * PACK_W]], packed_dtype=BF16)
    return word[:, :PACK_W], word[:, PACK_W:]


def _unpack_halves(wa, wb):
    part = lambda w, i: pltpu.unpack_elementwise(w, index=i, packed_dtype=BF16, unpacked_dtype=F32)
    return jnp.concatenate([part(wa, 1), part(wb, 1), part(wa, 0), part(wb, 0)], axis=1)


def _ada_kernel(c_ref, w_ref, b_ref, o_ref):
    c = c_ref[...]
    s = c * _sigmoid(c)
    o_ref[...] = jnp.dot(s, w_ref[...], preferred_element_type=F32,
                         precision=lax.Precision.HIGHEST) + b_ref[...]


def _ada(c, ada_w, ada_b):
    b, d = c.shape
    n = ada_w.shape[1]
    return pl.pallas_call(
        _ada_kernel,
        grid=(n // ADA_TN,),
        in_specs=[
            pl.BlockSpec((b, d), lambda j: (0, 0)),
            pl.BlockSpec((d, ADA_TN), lambda j: (0, j)),
            pl.BlockSpec((1, ADA_TN), lambda j: (0, j)),
        ],
        out_specs=pl.BlockSpec((b, ADA_TN), lambda j: (0, j)),
        out_shape=jax.ShapeDtypeStruct((b, n), F32),
        compiler_params=pltpu.CompilerParams(
            dimension_semantics=("arbitrary",), vmem_limit_bytes=VMEM_LIMIT),
        name="ada",
    )(c, ada_w, ada_b.reshape(1, n))


def _norm_mod(x, g, sc, sh):
    ms = jnp.mean(x * x, axis=-1, keepdims=True)
    y = x * lax.rsqrt(ms + EPS)
    return y * (g * (1.0 + sc)) + sh


N_MIXER_INPUTS = 32
N_MIXER_TOKEN_OUTPUTS = 7


def _mixer_kernel(*refs, pairs_per_seq, n_aliased):
    (x_ref, xn_ref, n1g_ref, sc1_ref, sh1_ref, sc1n_ref, sh1n_ref, win_ref,
     gt1_ref, sc2_ref, sh2_ref,
     convw_ref, convb_ref, wa_ref, ba_ref, wx_ref, bx_ref, lam_ref,
     lng_ref, lnb_ref, ws_ref, bs_ref, wbr_ref, wbs_ref, wout_ref,
     n2g_ref, wr_ref, br_ref, shift_ref, before_ref, wgu_ref, wd_ref) = refs[:N_MIXER_INPUTS]
    (x2_ref, h2a_ref, h2b_ref, idx_ref, tw_ref, rank_ref, cnt_ref, wgu_bf_ref, wd_bf_ref,
     z0_ref, z1_ref, hn0_ref, hn1_ref, xp_ref, xc_ref, a_ref, hh_ref, sv_ref,
     hstate_ref) = refs[N_MIXER_INPUTS + n_aliased:]
    ts = MIXER_TS
    s = pl.program_id(0)

    wgu_bf_ref[...] = wgu_ref[...].astype(BF16)
    wd_bf_ref[...] = wd_ref[...].astype(BF16)

    def inproj_norm(hn_ref, x_rows, sc, sh):
        hn_ref[...] = _norm_mod(x_rows, n1g_ref[...], sc, sh).astype(BF16)

    def inproj_chunks(z_dst, hn_ref, first, last):
        for c in range(first * INPROJ_TN, last * INPROJ_TN, INPROJ_TN):
            z_dst[:, c:c + INPROJ_TN] = _bdot(hn_ref[...], win_ref[:, c:c + INPROJ_TN])

    @pl.when(s == 0)
    def _():
        cnt_ref[...] = jnp.zeros_like(cnt_ref)
        inproj_norm(hn0_ref, x_ref[0:ts, :], sc1_ref[0], sh1_ref[0])
        inproj_chunks(z0_ref, hn0_ref, 0, D_IN // INPROJ_TN)

    @pl.when(s % pairs_per_seq == 0)
    def _():
        xp_ref[0:SUBLANES, :] = jnp.zeros((SUBLANES, D_RNN), F32)
        hstate_ref[...] = jnp.zeros_like(hstate_ref)

    def tile(k, z_ref, z_next_ref, hn_ref, next_rows, next_sc, next_sh):
        rows = slice(k * ts, (k + 1) * ts)
        inproj_norm(hn_ref, next_rows, next_sc, next_sh)

        x16 = z_ref[:, 0:D_RNN].astype(BF16)
        rnn_x = x16.astype(F32)
        cw = convw_ref[...]
        xc = cw[3:4] * rnn_x + convb_ref[...]
        for sft in range(1, CONV_WIDTH):
            xc = xc + cw[3 - sft:4 - sft] * _bdot(shift_ref[sft - 1], x16)
        xc_ref[...] = xc
        xp_ref[SUBLANES:2 * SUBLANES, :] = rnn_x[0:SUBLANES, :]
        xc_ref[0:SUBLANES, :] = (
            cw[3:4] * xp_ref[SUBLANES:2 * SUBLANES, :]
            + cw[2:3] * xp_ref[SUBLANES - 1:2 * SUBLANES - 1, :]
            + cw[1:2] * xp_ref[SUBLANES - 2:2 * SUBLANES - 2, :]
            + cw[0:1] * xp_ref[SUBLANES - 3:2 * SUBLANES - 3, :]) + convb_ref[...]
        xp_ref[0:SUBLANES, :] = rnn_x[ts - SUBLANES:ts, :]
        xc = xc_ref[...]

        xcb = xc.astype(BF16)
        r_parts, i_parts = [], []
        for g in range(N_GATE_BLOCKS):
            blk = xcb[:, g * GATE_BLOCK:(g + 1) * GATE_BLOCK]
            r_parts.append(_bdot(blk, wa_ref[g]))
            i_parts.append(_bdot(blk, wx_ref[g]))
        inproj_chunks(z_next_ref, hn_ref, 0, 3)
        r_t = jnp.tanh(jnp.concatenate(r_parts, axis=1) + ba_ref[...])
        i_t = jnp.tanh(jnp.concatenate(i_parts, axis=1) + bx_ref[...])

        nl = -lam_ref[...]
        softplus = jnp.maximum(nl, 0.0) + jnp.log(1.0 + jnp.exp(-jnp.abs(nl)))
        half_c = (-0.5 * LRU_C) * softplus
        a = jnp.exp(r_t * half_c + half_c)
        t = 1.0 - a * a
        u = jnp.where(t > 0.0, t * lax.rsqrt(t), 0.0) * ((i_t + 1.0) * xc)

        groups = ts // SUBLANES
        a3 = a.reshape(groups, SUBLANES, D_RNN)
        h3 = u.reshape(groups, SUBLANES, D_RNN)
        sub = lax.broadcasted_iota(I32, (groups, SUBLANES, D_RNN), 1)
        for step in (1, 2, 4):
            keep = sub >= step
            a_sh = jnp.where(keep, pltpu.roll(a3, step, 1), 1.0)
            h_sh = jnp.where(keep, pltpu.roll(h3, step, 1), 0.0)
            h3 = h3 + a3 * h_sh
            a3 = a3 * a_sh
        a_ref[...] = a3.reshape(ts, D_RNN)
        hh_ref[...] = h3.reshape(ts, D_RNN)

        hc = hstate_ref[...]
        for gi in range(groups):
            grp = slice(gi * SUBLANES, (gi + 1) * SUBLANES)
            hg = hh_ref[grp, :] + a_ref[grp, :] * hc
            hh_ref[grp, :] = hg
            hc = jnp.broadcast_to(hg[SUBLANES - 1:SUBLANES, :], (SUBLANES, D_RNN))
        hstate_ref[...] = hc

        inproj_chunks(z_next_ref, hn_ref, 3, 4)
        y_rnn = (hh_ref[...] * _gelu_tanh_x2(z_ref[:, D_RNN:2 * D_RNN].astype(F32))).astype(BF16)

        gv = _gelu_tanh_x2(z_ref[:, 2 * D_RNN + D_SG:2 * D_RNN + 2 * D_SG].astype(F32))
        mu = jnp.mean(gv, axis=-1, keepdims=True)
        dv = gv - mu
        var = jnp.mean(dv * dv, axis=-1, keepdims=True)
        vn = (dv * lax.rsqrt(var + 4.0 * EPS) * lng_ref[...] + lnb_ref[...]).astype(BF16)
        for g in range(SG_GROUPS):
            wg = ws_ref[g]
            cols = slice(g * SG_GROUP_DIM, (g + 1) * SG_GROUP_DIM)
            for n in range(ts // SG_CHUNK):
                chunk = slice(n * SG_CHUNK, (n + 1) * SG_CHUNK)
                sv_ref[chunk, cols] = _bdot(wg, vn[chunk, cols]) + bs_ref[:, cols]
        inproj_chunks(z_next_ref, hn_ref, 4, 5)
        gu = _gelu_tanh_x2(z_ref[:, 2 * D_RNN:2 * D_RNN + D_SG].astype(F32))
        y_sg = (gu * sv_ref[...]).astype(BF16)

        g_rnn = z_ref[:, 2 * D_RNN + 2 * D_SG:2 * D_RNN + 2 * D_SG + D_MODEL].astype(F32)
        g_sg = z_ref[:, 2 * D_RNN + 2 * D_SG + D_MODEL:D_IN].astype(F32)
        m = ((jnp.tanh(g_rnn) + 1.0) * _bdot(y_rnn, wbr_ref[...])
             + (jnp.tanh(g_sg) + 1.0) * _bdot(y_sg, wbs_ref[...])).astype(BF16)
        x2 = x_ref[rows, :] + gt1_ref[0] * _bdot(m, wout_ref[...])
        x2_ref[rows, :] = x2
        inproj_chunks(z_next_ref, hn_ref, 5, 6)

        h2 = _norm_mod(x2, n2g_ref[...], sc2_ref[0], sh2_ref[0])
        h2a_ref[rows, :], h2b_ref[rows, :] = _pack_halves(h2)
        h_hi = h2.astype(BF16)
        h_lo = (h2 - h_hi.astype(F32)).astype(BF16)
        nt_dims = (((1,), (1,)), ((), ()))
        by_hi = lax.dot_general(wr_ref[...], h_hi, nt_dims, preferred_element_type=F32)
        logits = (by_hi[:N_EXPERTS] + by_hi[N_EXPERTS:]
                  + lax.dot_general(wr_ref[0:N_EXPERTS, :], h_lo, nt_dims,
                                    preferred_element_type=F32)
                  + br_ref[...])
        e_iota = lax.broadcasted_iota(I32, (N_EXPERTS, ts), 0)
        v = logits
        vals, idxs, sels = [], [], []
        for _ in range(TOP_K):
            mx = jnp.max(v, axis=0, keepdims=True)
            ik = jnp.min(jnp.where(v == mx, e_iota, N_EXPERTS), axis=0, keepdims=True)
            sel = e_iota == ik
            v = jnp.where(sel, -jnp.inf, v)
            vals.append(mx)
            idxs.append(ik)
            sels.append(sel)
        exps = [jnp.exp(val - vals[0]) for val in vals]
        denom = exps[0] + exps[1] + exps[2] + exps[3]
        idx_ref[:, rows] = jnp.concatenate(idxs, axis=0)
        tw_ref[:, rows] = jnp.concatenate([e / denom for e in exps], axis=0)

        onehot = jnp.zeros((N_EXPERTS, ts), F32)
        for sel in sels:
            onehot = jnp.where(sel, 1.0, onehot)
        total = cnt_ref[...] + _bdot(onehot.astype(BF16), before_ref[...])
        ranks = [jnp.sum(jnp.where(sel, total, 0.0), axis=0, keepdims=True) for sel in sels]
        rank_ref[:, rows] = jnp.concatenate(ranks, axis=0).astype(I32)
        cnt_ref[...] = cnt_ref[...] + jnp.sum(onehot, axis=1, keepdims=True)

    tile(0, z0_ref, z1_ref, hn1_ref, x_ref[ts:2 * ts, :], sc1_ref[0], sh1_ref[0])
    tile(1, z1_ref, z0_ref, hn0_ref, xn_ref[...], sc1n_ref[0], sh1n_ref[0])


def _mixer(x2d, n1g, sc1, sh1, w_in, gt1, sc2, sh2, conv_w, conv_b, wa_bd, ba, wx_bd, bx, lam,
           ln_g, ln_b, ws, bs_tile, wbr, wbs, wout, n2g, wr_split, br, shifts, before, w_gu, w_down,
           batch0, batches, seq, expert0, n_cast, cast_prev):
    d = x2d.shape[1]
    ts = MIXER_TS
    n = batches * seq
    tiles_per_seq = seq // ts
    pairs_per_seq = tiles_per_seq // 2
    assert tiles_per_seq % 2 == 0
    steps = batches * pairs_per_seq
    pair0 = batch0 * pairs_per_seq
    next_tile = lambda s: jnp.minimum(2 * s + 2, 2 * steps - 1)
    pair = lambda s: (s, 0)
    pairt = lambda s: (0, s)
    bvec = lambda s: (batch0 + s // pairs_per_seq, 0, 0)
    bvec_next = lambda s: (batch0 + next_tile(s) // tiles_per_seq, 0, 0)
    n_e, k_gu, n_gu = w_gu.shape
    _, k_d, n_d = w_down.shape
    assert (n_cast * k_gu) % steps == 0 and (n_cast * k_d) % steps == 0
    rows_gu, rows_d = n_cast * k_gu // steps, n_cast * k_d // steps
    cast_gu = lambda s: (expert0 * k_gu // rows_gu + s, 0)
    cast_d = lambda s: (expert0 * k_d // rows_d + s, 0)
    w_gu = w_gu.reshape(n_e * k_gu, n_gu)
    w_down = w_down.reshape(n_e * k_d, n_d)
    c2 = lambda s: (0, 0)
    c3 = lambda s: (0, 0, 0)
    in_specs = [
        pl.BlockSpec((2 * ts, d), lambda s: (pair0 + s, 0)),
        pl.BlockSpec((ts, d), lambda s: (2 * pair0 + next_tile(s), 0)),
        pl.BlockSpec((1, d), c2),
        pl.BlockSpec((1, 1, d), bvec),
        pl.BlockSpec((1, 1, d), bvec),
        pl.BlockSpec((1, 1, d), bvec_next),
        pl.BlockSpec((1, 1, d), bvec_next),
        pl.BlockSpec((d, D_IN), c2),
        pl.BlockSpec((1, 1, d), bvec),
        pl.BlockSpec((1, 1, d), bvec),
        pl.BlockSpec((1, 1, d), bvec),
        pl.BlockSpec((CONV_WIDTH, D_RNN), c2),
        pl.BlockSpec((1, D_RNN), c2),
        pl.BlockSpec((N_GATE_BLOCKS, GATE_BLOCK, GATE_BLOCK), c3),
        pl.BlockSpec((1, D_RNN), c2),
        pl.BlockSpec((N_GATE_BLOCKS, GATE_BLOCK, GATE_BLOCK), c3),
        pl.BlockSpec((1, D_RNN), c2),
        pl.BlockSpec((1, D_RNN), c2),
        pl.BlockSpec((1, D_SG), c2),
        pl.BlockSpec((1, D_SG), c2),
        pl.BlockSpec((SG_GROUPS, SG_CHUNK, SG_CHUNK), c3),
        pl.BlockSpec((SG_CHUNK, D_SG), c2),
        pl.BlockSpec((D_RNN, d), c2),
        pl.BlockSpec((D_SG, d), c2),
        pl.BlockSpec((d, d), c2),
        pl.BlockSpec((1, d), c2),
        pl.BlockSpec((2 * N_EXPERTS, d), c2),
        pl.BlockSpec((N_EXPERTS, 1), c2),
        pl.BlockSpec((CONV_WIDTH - 1, ts, ts), c3),
        pl.BlockSpec((ts, ts), c2),
        pl.BlockSpec((rows_gu, n_gu), cast_gu),
        pl.BlockSpec((rows_d, n_d), cast_d),
    ]
    assert len(in_specs) == N_MIXER_INPUTS
    args = [x2d, x2d, n1g, sc1, sh1, sc1, sh1, w_in, gt1, sc2, sh2, conv_w, conv_b, wa_bd, ba,
            wx_bd, bx, lam, ln_g, ln_b, ws, bs_tile, wbr, wbs, wout, n2g, wr_split, br, shifts,
            before, w_gu, w_down]
    aliases = {}
    if cast_prev is not None:
        for j, prev in enumerate(cast_prev):
            in_specs.append(pl.BlockSpec(memory_space=pl.ANY))
            aliases[len(args)] = N_MIXER_TOKEN_OUTPUTS + j
            args.append(prev)
    out_specs = [
        pl.BlockSpec((2 * ts, d), pair),
        pl.BlockSpec((2 * ts, PACK_W), pair),
        pl.BlockSpec((2 * ts, PACK_W), pair),
        pl.BlockSpec((TOP_K, 2 * ts), pairt),
        pl.BlockSpec((TOP_K, 2 * ts), pairt),
        pl.BlockSpec((TOP_K, 2 * ts), pairt),
        pl.BlockSpec((N_EXPERTS, 1), c2),
        pl.BlockSpec((rows_gu, n_gu), cast_gu),
        pl.BlockSpec((rows_d, n_d), cast_d),
    ]
    out_shape = [
        jax.ShapeDtypeStruct((n, d), F32),
        jax.ShapeDtypeStruct((n, PACK_W), jnp.uint32),
        jax.ShapeDtypeStruct((n, PACK_W), jnp.uint32),
        jax.ShapeDtypeStruct((TOP_K, n), I32),
        jax.ShapeDtypeStruct((TOP_K, n), F32),
        jax.ShapeDtypeStruct((TOP_K, n), I32),
        jax.ShapeDtypeStruct((N_EXPERTS, 1), F32),
        jax.ShapeDtypeStruct(w_gu.shape, BF16),
        jax.ShapeDtypeStruct(w_down.shape, BF16),
    ]
    scratch = [
        pltpu.VMEM((ts, D_IN), F32),
        pltpu.VMEM((ts, D_IN), F32),
        pltpu.VMEM((ts, d), BF16),
        pltpu.VMEM((ts, d), BF16),
        pltpu.VMEM((2 * SUBLANES, D_RNN), F32),
        pltpu.VMEM((ts, D_RNN), F32),
        pltpu.VMEM((ts, D_RNN), F32),
        pltpu.VMEM((ts, D_RNN), F32),
        pltpu.VMEM((ts, D_SG), F32),
        pltpu.VMEM((SUBLANES, D_RNN), F32),
    ]
    return pl.pallas_call(
        functools.partial(_mixer_kernel, pairs_per_seq=pairs_per_seq, n_aliased=len(aliases)),
        grid=(steps,),
        in_specs=in_specs,
        out_specs=out_specs,
        out_shape=out_shape,
        scratch_shapes=scratch,
        input_output_aliases=aliases,
        compiler_params=pltpu.CompilerParams(
            dimension_semantics=("arbitrary",), vmem_limit_bytes=VMEM_LIMIT),
        name="mixer",
    )(*args)


def _sc_mesh():
    return plsc.VectorSubcoreMesh(core_axis_name="core", subcore_axis_name="subcore")


def _sc_scatter_rows(rows, pos, cap):
    n, d = rows.shape
    kk = pos.shape[0]

    @functools.partial(
        pl.kernel, out_type=jax.ShapeDtypeStruct((cap, d), rows.dtype), mesh=_sc_mesh(),
        scratch_types=[], name="sc_scatter_rows")
    def scatter(x_hbm, i_hbm, o_hbm):
        def body(x_vmem, i_vmem):
            pltpu.sync_copy(x_vmem, o_hbm.at[i_vmem.at[0]])

        pltpu.emit_pipeline(
            body,
            grid=(n // SC_WINDOW, kk),
            in_specs=[pl.BlockSpec((SC_WINDOW, d), lambda i, k: (i, 0)),
                      pl.BlockSpec((1, SC_WINDOW), lambda i, k: (k, i))],
            out_specs=[],
            core_axis_name=("core", "subcore"),
            dimension_semantics=(pltpu.PARALLEL, pltpu.ARBITRARY),
        )(x_hbm, i_hbm)

    return scatter(rows, pos)


def _sc_gather_rows(table, idx):
    m = idx.shape[1]
    d = table.shape[1]

    @functools.partial(
        pl.kernel, out_type=jax.ShapeDtypeStruct((m, d), table.dtype), mesh=_sc_mesh(),
        scratch_types=[], name="sc_gather_rows")
    def gather(x_hbm, i_hbm, o_hbm):
        def body(i_vmem, o_vmem):
            pltpu.sync_copy(x_hbm.at[i_vmem.at[0]], o_vmem)

        pltpu.emit_pipeline(
            body,
            grid=(m // SC_WINDOW,),
            in_specs=[pl.BlockSpec((1, SC_WINDOW), lambda i: (0, i))],
            out_specs=[pl.BlockSpec((SC_WINDOW, d), lambda i: (i, 0))],
            core_axis_name=("core", "subcore"),
            dimension_semantics=(pltpu.PARALLEL,),
        )(i_hbm, o_hbm)

    return gather(table, idx)


def _pos_kernel(cnt_ref, idx_ref, rank_ref, pos_ref, be_ref, nv_ref):
    assert MOE_BM & (MOE_BM - 1) == 0
    starts, ends, run = [], [], jnp.int32(0)
    for e in range(N_EXPERTS):
        starts.append(run)
        run = run + ((cnt_ref[e] + (MOE_BM - 1)) & jnp.int32(-MOE_BM))
        ends.append(run)

    idx = idx_ref[...]
    pos = rank_ref[...]
    for e in range(N_EXPERTS):
        pos = pos + jnp.where(idx == e, starts[e], 0)
    pos_ref[...] = pos

    row0 = lax.broadcasted_iota(I32, be_ref.shape, 1) * MOE_BM
    be = jnp.zeros(be_ref.shape, I32)
    for e in range(N_EXPERTS - 1):
        be = be + jnp.where(ends[e] <= row0, 1, 0)
    start = jnp.zeros(be_ref.shape, I32)
    count = jnp.zeros(be_ref.shape, I32)
    for e in range(N_EXPERTS):
        here = be == e
        start = jnp.where(here, starts[e], start)
        count = jnp.where(here, cnt_ref[e], count)
    be_ref[...] = be
    nv_ref[...] = jnp.clip(count - (row0 - start), 0, MOE_BM)


def _slot_plan(counts, top_idx, rank, nb):
    k, n = top_idx.shape
    tn = min(POS_TN, n)
    nbp = -(-nb // LANES) * LANES
    spec = pl.BlockSpec((k, tn), lambda i, c: (0, i))
    blocks = pl.BlockSpec((1, nbp), lambda i, c: (0, 0))
    pos, be, nv = pl.pallas_call(
        _pos_kernel,
        grid_spec=pltpu.PrefetchScalarGridSpec(
            num_scalar_prefetch=1, grid=(n // tn,), in_specs=[spec, spec],
            out_specs=[spec, blocks, blocks]),
        out_shape=[jax.ShapeDtypeStruct((k, n), I32), jax.ShapeDtypeStruct((1, nbp), I32),
                   jax.ShapeDtypeStruct((1, nbp), I32)],
        compiler_params=pltpu.CompilerParams(dimension_semantics=("arbitrary",)),
        name="slot_plan",
    )(counts, top_idx, rank)
    return pos, be.reshape(nbp), nv.reshape(nbp)


def _expert_kernel(be_ref, nv_ref, xa_ref, xb_ref, wgu_ref, bgu_ref, wd_ref, bd_ref,
                   ya_ref, yb_ref):
    nvalid = nv_ref[pl.program_id(0)]

    def mlp_rows(h):
        rows = slice(h * MOE_HALF, (h + 1) * MOE_HALF)
        live = lax.broadcasted_iota(I32, (MOE_HALF, 1), 0) < nvalid - h * MOE_HALF
        xb = jnp.where(live, _unpack_halves(xa_ref[rows, :], xb_ref[rows, :]), 0.0).astype(BF16)
        gu = _bdot(xb, wgu_ref[0]) + bgu_ref[0]
        gate = jnp.minimum(gu[:, :D_EXPERT], SWIGLU_LIMIT)
        up = jnp.clip(gu[:, D_EXPERT:], -SWIGLU_LIMIT, SWIGLU_LIMIT)
        act = (up + 1.0) * (gate * _sigmoid(SWIGLU_ALPHA * gate))
        y = _bdot(act.astype(BF16), wd_ref[0]) + bd_ref[0]
        ya_ref[rows, :], yb_ref[rows, :] = _pack_halves(y)

    def zero_rows(h):
        rows = slice(h * MOE_HALF, (h + 1) * MOE_HALF)
        ya_ref[rows, :] = jnp.zeros((MOE_HALF, PACK_W), jnp.uint32)
        yb_ref[rows, :] = jnp.zeros((MOE_HALF, PACK_W), jnp.uint32)

    @pl.when(nvalid > MOE_HALF)
    def _():
        mlp_rows(0)
        mlp_rows(1)

    @pl.when((nvalid > 0) & (nvalid <= MOE_HALF))
    def _():
        mlp_rows(0)
        zero_rows(1)

    @pl.when(nvalid <= 0)
    def _():
        zero_rows(0)
        zero_rows(1)


def _experts(block_e, n_valid, xa, xb, w_gu, b_gu, w_down, b_down):
    cap = xa.shape[0]
    d = D_MODEL
    nb = cap // MOE_BM
    half = pl.BlockSpec((MOE_BM, PACK_W), lambda i, be, nv: (i, 0))
    grid_spec = pltpu.PrefetchScalarGridSpec(
        num_scalar_prefetch=2,
        grid=(nb,),
        in_specs=[
            half,
            half,
            pl.BlockSpec((1, d, 2 * D_EXPERT), lambda i, be, nv: (be[i], 0, 0)),
            pl.BlockSpec((1, 1, 2 * D_EXPERT), lambda i, be, nv: (be[i], 0, 0)),
            pl.BlockSpec((1, D_EXPERT, d), lambda i, be, nv: (be[i], 0, 0)),
            pl.BlockSpec((1, 1, d), lambda i, be, nv: (be[i], 0, 0)),
        ],
        out_specs=[half, half],
    )
    return pl.pallas_call(
        _expert_kernel,
        grid_spec=grid_spec,
        out_shape=[jax.ShapeDtypeStruct((cap, PACK_W), jnp.uint32)] * 2,
        compiler_params=pltpu.CompilerParams(
            dimension_semantics=("arbitrary",), vmem_limit_bytes=VMEM_LIMIT),
        name="experts",
    )(block_e, n_valid, xa, xb, w_gu, b_gu, w_down, b_down)


def _final_kernel(x2_ref, ya_ref, yb_ref, tw_ref, gt2_ref, fg_ref, *rest):
    o_ref = rest[-1]
    tw = tw_ref[...].T
    moe = tw[:, 0:1] * _unpack_halves(ya_ref[0], yb_ref[0])
    for k in range(1, TOP_K):
        moe = moe + tw[:, k:k + 1] * _unpack_halves(ya_ref[k], yb_ref[k])
    x3 = x2_ref[...] + gt2_ref[0] * moe
    ms = jnp.mean(x3 * x3, axis=-1, keepdims=True)
    o_ref[...] = (x3 * lax.rsqrt(ms + EPS)) * fg_ref[...]


def _final(x2, yga, ygb, tw_tok, gt2, final_g, seq, batch0, n_total, out_prev):
    n, d = x2.shape
    tiles_per_seq = seq // FINAL_TM
    tile0 = batch0 * tiles_per_seq
    half = pl.BlockSpec((TOP_K, FINAL_TM, PACK_W), lambda i: (0, i, 0))
    in_specs = [
        pl.BlockSpec((FINAL_TM, d), lambda i: (i, 0)),
        half,
        half,
        pl.BlockSpec((TOP_K, FINAL_TM), lambda i: (0, i)),
        pl.BlockSpec((1, 1, d), lambda i: (batch0 + i // tiles_per_seq, 0, 0)),
        pl.BlockSpec((1, d), lambda i: (0, 0)),
    ]
    args = [x2, yga, ygb, tw_tok, gt2, final_g]
    aliases = {}
    if out_prev is not None:
        in_specs.append(pl.BlockSpec(memory_space=pl.ANY))
        aliases = {len(args): 0}
        args.append(out_prev)
    return pl.pallas_call(
        _final_kernel,
        grid=(n // FINAL_TM,),
        in_specs=in_specs,
        out_specs=pl.BlockSpec((FINAL_TM, d), lambda i: (tile0 + i, 0)),
        out_shape=jax.ShapeDtypeStruct((n_total, d), F32),
        input_output_aliases=aliases,
        compiler_params=pltpu.CompilerParams(
            dimension_semantics=("arbitrary",), vmem_limit_bytes=VMEM_LIMIT),
        name="final",
    )(*args)


def _block_diag(w):
    per = GATE_BLOCK // RNN_HEAD_DIM
    w4 = w.reshape(N_GATE_BLOCKS, per, RNN_HEAD_DIM, RNN_HEAD_DIM)
    eye = jnp.eye(per, dtype=w.dtype)
    bd = jnp.einsum("gpij,pq->gpiqj", w4, eye)
    return bd.reshape(N_GATE_BLOCKS, GATE_BLOCK, GATE_BLOCK)


def _layer(x2d, c, batch, seq, ada_w, ada_b, norm1_g, w_in, conv_w, conv_b, lru_wa, lru_ba,
           lru_wx, lru_bx, lru_lam, sg_ln_g, sg_ln_b, sg_ws, sg_bs, w_br_rnn, w_br_sg, w_out,
           norm2_g, w_router, b_router, w_gu, b_gu, w_down, b_down, final_g):
    n_total, d = x2d.shape
    mod = _ada(c, ada_w, ada_b)
    sh1, sc1, gt1, sh2, sc2, gt2 = [
        mod[:, i * d:(i + 1) * d].reshape(batch, 1, d) for i in range(N_MOD)]
    row = lambda v: v.reshape(1, -1)

    bs_tile = jnp.repeat(sg_bs.T, SG_GROUP_DIM, axis=1)
    wr_t = w_router.T
    wr_hi = wr_t.astype(BF16)
    wr_lo = (wr_t - wr_hi.astype(F32)).astype(BF16)
    t_out = lax.broadcasted_iota(I32, (MIXER_TS, MIXER_TS), 0)
    t_in = lax.broadcasted_iota(I32, (MIXER_TS, MIXER_TS), 1)
    shifts = jnp.stack([(t_out - t_in == s) for s in range(1, CONV_WIDTH)]).astype(BF16)
    before = (t_out < t_in).astype(BF16)
    causal = jnp.tril(jnp.ones((SG_CHUNK, SG_CHUNK), dtype=bool))
    ws_causal = jnp.where(causal[None], sg_ws, 0.0).astype(BF16)
    mixer_weights = (
        conv_w, row(conv_b), _block_diag(0.5 * lru_wa).astype(BF16), row(0.5 * lru_ba),
        _block_diag(0.5 * lru_wx).astype(BF16), row(0.5 * lru_bx), row(lru_lam), row(sg_ln_g),
        row(sg_ln_b), ws_causal, bs_tile, (0.25 * w_br_rnn).astype(BF16), (0.5 * w_br_sg).astype(BF16),
        (0.5 * w_out).astype(BF16), row(norm2_g), jnp.concatenate([wr_hi, wr_lo], axis=0),
        b_router.reshape(N_EXPERTS, 1), shifts, before)
    gate_cols = 2 * D_RNN + 2 * D_SG
    col_scale = jnp.where(jnp.arange(D_IN) >= gate_cols, 0.5, 1.0).astype(F32)
    w_in_bf = (w_in * col_scale[None, :]).astype(BF16)

    groups = TOKEN_GROUPS if batch % TOKEN_GROUPS == 0 and N_EXPERTS % TOKEN_GROUPS == 0 else 1
    batches = batch // groups
    n_cast = N_EXPERTS // groups
    n = batches * seq
    mixed, cast = [], None
    for grp in range(groups):
        *outs, w_gu_bf, w_down_bf = _mixer(
            x2d, row(norm1_g), sc1, sh1, w_in_bf, gt1, sc2, sh2, *mixer_weights, w_gu, w_down,
            grp * batches, batches, seq, grp * n_cast, n_cast, cast)
        cast = (w_gu_bf, w_down_bf)
        mixed.append(outs)

    out = None
    for grp in range(groups):
        batch0 = grp * batches
        x2, h2a, h2b, top_idx, top_w, rank, counts = mixed[grp]

        cap = n * TOP_K + N_EXPERTS * MOE_BM
        nb = cap // MOE_BM
        pos, block_e, n_valid = _slot_plan(counts[:, 0].astype(I32), top_idx, rank, nb)

        xa = _sc_scatter_rows(h2a, pos, cap)
        xb = _sc_scatter_rows(h2b, pos, cap)
        ya, yb = _experts(block_e, n_valid, xa, xb, cast[0].reshape(w_gu.shape),
                          b_gu.reshape(N_EXPERTS, 1, -1), cast[1].reshape(w_down.shape),
                          b_down.reshape(N_EXPERTS, 1, -1))
        flat_pos = pos.reshape(1, -1)
        yga = _sc_gather_rows(ya, flat_pos).reshape(TOP_K, n, PACK_W)
        ygb = _sc_gather_rows(yb, flat_pos).reshape(TOP_K, n, PACK_W)
        out = _final(x2, yga, ygb, top_w, gt2, final_g.reshape(1, d), seq, batch0, n_total, out)
    return out


def kernel(x, c, ada_w, ada_b, norm1_g, w_in, conv_w, conv_b, lru_wa, lru_ba, lru_wx, lru_bx,
           lru_lam, sg_ln_g, sg_ln_b, sg_ws, sg_bs, w_br_rnn, w_br_sg, w_out, norm2_g,
           w_router, b_router, w_gu, b_gu, w_down, b_down, final_g):
    batch, seq, d = x.shape
    depth = ada_w.shape[0]
    assert depth == 1, "the combine is fused with the final norm, which follows the only layer"
    x2d = x.reshape(batch * seq, d)
    l = 0
    out = _layer(
        x2d, c, batch, seq, ada_w[l], ada_b[l], norm1_g[l], w_in[l], conv_w[l], conv_b[l],
        lru_wa[l], lru_ba[l], lru_wx[l], lru_bx[l], lru_lam[l], sg_ln_g[l], sg_ln_b[l],
        sg_ws[l], sg_bs[l], w_br_rnn[l], w_br_sg[l], w_out[l], norm2_g[l], w_router[l],
        b_router[l], w_gu[l], b_gu[l], w_down[l], b_down[l], final_g)
    return out.reshape(batch, seq, d)
```

```python
import functools

import jax
import jax.numpy as jnp
from jax import lax
from jax.experimental import pallas as pl
from jax.experimental.pallas import tpu as pltpu
from jax.experimental.pallas import tpu_sc as plsc

F32 = jnp.float32
BF16 = jnp.bfloat16
I32 = jnp.int32

D_MODEL = 1024
D_RNN = 1024
RNN_HEADS = 16
RNN_HEAD_DIM = D_RNN // RNN_HEADS
CONV_WIDTH = 4
LRU_C = 8.0
D_SG = 1024
SG_GROUPS = 8
SG_GROUP_DIM = D_SG // SG_GROUPS
SG_CHUNK = 128
N_EXPERTS = 32
TOP_K = 4
D_EXPERT = 1024
SWIGLU_LIMIT = 7.0
SWIGLU_ALPHA = 1.702
EPS = 1e-6
N_MOD = 6
D_IN = 2 * D_RNN + 2 * D_SG + 2 * D_MODEL

SUBLANES = 8
LANES = 128
GATE_BLOCK = 256
N_GATE_BLOCKS = D_RNN // GATE_BLOCK

ADA_TN = 1536
INPROJ_TN = 1024
MIXER_TS = 256
MOE_BM = 512
MOE_HALF = MOE_BM // 2
TOKEN_GROUPS = 2
SC_WINDOW = 128
PACK_W = D_MODEL // 4
FINAL_TM = 512
POS_TN = 8192
VMEM_LIMIT = 60 * 1024 * 1024


def _sigmoid(x):
    return 0.5 * jnp.tanh(0.5 * x) + 0.5


def _gelu_tanh_x2(x):
    k = 0.7978845608028654
    return x + x * jnp.tanh(x * (k + (k * 0.044715) * (x * x)))


def _bdot(a, b):
    return jnp.dot(a, b, preferred_element_type=F32)


def _pack_halves(v):
    word = pltpu.pack_elementwise([v[:, 2 * PACK_W:], v[:, :2 * PACK_W]], packed_dtype=BF16)
    return word[:, :PACK_W], word[:, PACK_W:]


def _unpack_halves(wa, wb):
    part = lambda w, i: pltpu.unpack_elementwise(w, index=i, packed_dtype=BF16, unpacked_dtype=F32)
    return jnp.concatenate([part(wa, 1), part(wb, 1), part(wa, 0), part(wb, 0)], axis=1)


def _ada_kernel(c_ref, w_ref, b_ref, o_ref):
    c = c_ref[...]
    s = c * _sigmoid(c)
    o_ref[...] = jnp.dot(s, w_ref[...], preferred_element_type=F32,
                         precision=lax.Precision.HIGHEST) + b_ref[...]


def _ada(c, ada_w, ada_b):
    b, d = c.shape
    n = ada_w.shape[1]
    return pl.pallas_call(
        _ada_kernel,
        grid=(n // ADA_TN,),
        in_specs=[
            pl.BlockSpec((b, d), lambda j: (0, 0)),
            pl.BlockSpec((d, ADA_TN), lambda j: (0, j)),
            pl.BlockSpec((1, ADA_TN), lambda j: (0, j)),
        ],
        out_specs=pl.BlockSpec((b, ADA_TN), lambda j: (0, j)),
        out_shape=jax.ShapeDtypeStruct((b, n), F32),
        compiler_params=pltpu.CompilerParams(
            dimension_semantics=("arbitrary",), vmem_limit_bytes=VMEM_LIMIT),
        name="ada",
    )(c, ada_w, ada_b.reshape(1, n))


def _norm_mod(x, g, sc, sh):
    ms = jnp.mean(x * x, axis=-1, keepdims=True)
    y = x * lax.rsqrt(ms + EPS)
    return y * (g * (1.0 + sc)) + sh


N_MIXER_INPUTS = 32
N_MIXER_TOKEN_OUTPUTS = 7


def _mixer_kernel(*refs, pairs_per_seq, n_aliased):
    (x_ref, xn_ref, n1g_ref, sc1_ref, sh1_ref, sc1n_ref, sh1n_ref, win_ref,
     gt1_ref, sc2_ref, sh2_ref,
     convw_ref, convb_ref, wa_ref, ba_ref, wx_ref, bx_ref, lam_ref,
     lng_ref, lnb_ref, ws_ref, bs_ref, wbr_ref, wbs_ref, wout_ref,
     n2g_ref, wr_ref, br_ref, shift_ref, before_ref, wgu_ref, wd_ref) = refs[:N_MIXER_INPUTS]
    (x2_ref, h2a_ref, h2b_ref, idx_ref, tw_ref, rank_ref, cnt_ref, wgu_bf_ref, wd_bf_ref,
     z0_ref, z1_ref, hn0_ref, hn1_ref, xp_ref, xc_ref, a_ref, hh_ref, sv_ref,
     hstate_ref) = refs[N_MIXER_INPUTS + n_aliased:]
    ts = MIXER_TS
    s = pl.program_id(0)

    wgu_bf_ref[...] = wgu_ref[...].astype(BF16)
    wd_bf_ref[...] = wd_ref[...].astype(BF16)

    def inproj_norm(hn_ref, x_rows, sc, sh):
        hn_ref[...] = _norm_mod(x_rows, n1g_ref[...], sc, sh).astype(BF16)

    def inproj_chunks(z_dst, hn_ref, first, last):
        for c in range(first * INPROJ_TN, last * INPROJ_TN, INPROJ_TN):
            z_dst[:, c:c + INPROJ_TN] = _bdot(hn_ref[...], win_ref[:, c:c + INPROJ_TN])

    @pl.when(s == 0)
    def _():
        cnt_ref[...] = jnp.zeros_like(cnt_ref)
        inproj_norm(hn0_ref, x_ref[0:ts, :], sc1_ref[0], sh1_ref[0])
        inproj_chunks(z0_ref, hn0_ref, 0, D_IN // INPROJ_TN)

    @pl.when(s % pairs_per_seq == 0)
    def _():
        xp_ref[0:SUBLANES, :] = jnp.zeros((SUBLANES, D_RNN), F32)
        hstate_ref[...] = jnp.zeros_like(hstate_ref)

    def tile(k, z_ref, z_next_ref, hn_ref, next_rows, next_sc, next_sh):
        rows = slice(k * ts, (k + 1) * ts)
        inproj_norm(hn_ref, next_rows, next_sc, next_sh)

        x16 = z_ref[:, 0:D_RNN].astype(BF16)
        rnn_x = x16.astype(F32)
        cw = convw_ref[...]
        xc = cw[3:4] * rnn_x + convb_ref[...]
        for sft in range(1, CONV_WIDTH):
            xc = xc + cw[3 - sft:4 - sft] * _bdot(shift_ref[sft - 1], x16)
        xc_ref[...] = xc
        xp_ref[SUBLANES:2 * SUBLANES, :] = rnn_x[0:SUBLANES, :]
        xc_ref[0:SUBLANES, :] = (
            cw[3:4] * xp_ref[SUBLANES:2 * SUBLANES, :]
            + cw[2:3] * xp_ref[SUBLANES - 1:2 * SUBLANES - 1, :]
            + cw[1:2] * xp_ref[SUBLANES - 2:2 * SUBLANES - 2, :]
            + cw[0:1] * xp_ref[SUBLANES - 3:2 * SUBLANES - 3, :]) + convb_ref[...]
        xp_ref[0:SUBLANES, :] = rnn_x[ts - SUBLANES:ts, :]
        xc = xc_ref[...]

        xcb = xc.astype(BF16)
        r_parts, i_parts = [], []
        for g in range(N_GATE_BLOCKS):
            blk = xcb[:, g * GATE_BLOCK:(g + 1) * GATE_BLOCK]
            r_parts.append(_bdot(blk, wa_ref[g]))
            i_parts.append(_bdot(blk, wx_ref[g]))
        inproj_chunks(z_next_ref, hn_ref, 0, 3)
        r_t = jnp.tanh(jnp.concatenate(r_parts, axis=1) + ba_ref[...])
        i_t = jnp.tanh(jnp.concatenate(i_parts, axis=1) + bx_ref[...])

        nl = -lam_ref[...]
        softplus = jnp.maximum(nl, 0.0) + jnp.log(1.0 + jnp.exp(-jnp.abs(nl)))
        half_c = (-0.5 * LRU_C) * softplus
        a = jnp.exp(r_t * half_c + half_c)
        t = 1.0 - a * a
        u = jnp.where(t > 0.0, t * lax.rsqrt(t), 0.0) * ((i_t + 1.0) * xc)

        groups = ts // SUBLANES
        a3 = a.reshape(groups, SUBLANES, D_RNN)
        h3 = u.reshape(groups, SUBLANES, D_RNN)
        sub = lax.broadcasted_iota(I32, (groups, SUBLANES, D_RNN), 1)
        for step in (1, 2, 4):
            keep = sub >= step
            a_sh = jnp.where(keep, pltpu.roll(a3, step, 1), 1.0)
            h_sh = jnp.where(keep, pltpu.roll(h3, step, 1), 0.0)
            h3 = h3 + a3 * h_sh
            a3 = a3 * a_sh
        a_ref[...] = a3.reshape(ts, D_RNN)
        hh_ref[...] = h3.reshape(ts, D_RNN)

        hc = hstate_ref[...]
        for gi in range(groups):
            grp = slice(gi * SUBLANES, (gi + 1) * SUBLANES)
            hg = hh_ref[grp, :] + a_ref[grp, :] * hc
            hh_ref[grp, :] = hg
            hc = jnp.broadcast_to(hg[SUBLANES - 1:SUBLANES, :], (SUBLANES, D_RNN))
        hstate_ref[...] = hc

        inproj_chunks(z_next_ref, hn_ref, 3, 4)
        y_rnn = (hh_ref[...] * _gelu_tanh_x2(z_ref[:, D_RNN:2 * D_RNN].astype(F32))).astype(BF16)

        gv = _gelu_tanh_x2(z_ref[:, 2 * D_RNN + D_SG:2 * D_RNN + 2 * D_SG].astype(F32))
        mu = jnp.mean(gv, axis=-1, keepdims=True)
        dv = gv - mu
        var = jnp.mean(dv * dv, axis=-1, keepdims=True)
        vn = (dv * lax.rsqrt(var + 4.0 * EPS) * lng_ref[...] + lnb_ref[...]).astype(BF16)
        for g in range(SG_GROUPS):
            wg = ws_ref[g]
            cols = slice(g * SG_GROUP_DIM, (g + 1) * SG_GROUP_DIM)
            for n in range(ts // SG_CHUNK):
                chunk = slice(n * SG_CHUNK, (n + 1) * SG_CHUNK)
                sv_ref[chunk, cols] = _bdot(wg, vn[chunk, cols]) + bs_ref[:, cols]
        inproj_chunks(z_next_ref, hn_ref, 4, 5)
        gu = _gelu_tanh_x2(z_ref[:, 2 * D_RNN:2 * D_RNN + D_SG].astype(F32))
        y_sg = (gu * sv_ref[...]).astype(BF16)

        g_rnn = z_ref[:, 2 * D_RNN + 2 * D_SG:2 * D_RNN + 2 * D_SG + D_MODEL].astype(F32)
        g_sg = z_ref[:, 2 * D_RNN + 2 * D_SG + D_MODEL:D_IN].astype(F32)
        m = ((jnp.tanh(g_rnn) + 1.0) * _bdot(y_rnn, wbr_ref[...])
             + (jnp.tanh(g_sg) + 1.0) * _bdot(y_sg, wbs_ref[...])).astype(BF16)
        x2 = x_ref[rows, :] + gt1_ref[0] * _bdot(m, wout_ref[...])
        x2_ref[rows, :] = x2
        inproj_chunks(z_next_ref, hn_ref, 5, 6)

        h2 = _norm_mod(x2, n2g_ref[...], sc2_ref[0], sh2_ref[0])
        h2a_ref[rows, :], h2b_ref[rows, :] = _pack_halves(h2)
        h_hi = h2.astype(BF16)
        h_lo = (h2 - h_hi.astype(F32)).astype(BF16)
        nt_dims = (((1,), (1,)), ((), ()))
        by_hi = lax.dot_general(wr_ref[...], h_hi, nt_dims, preferred_element_type=F32)
        logits = (by_hi[:N_EXPERTS] + by_hi[N_EXPERTS:]
                  + lax.dot_general(wr_ref[0:N_EXPERTS, :], h_lo, nt_dims,
                                    preferred_element_type=F32)
                  + br_ref[...])
        e_iota = lax.broadcasted_iota(I32, (N_EXPERTS, ts), 0)
        v = logits
        vals, idxs, sels = [], [], []
        for _ in range(TOP_K):
            mx = jnp.max(v, axis=0, keepdims=True)
            ik = jnp.min(jnp.where(v == mx, e_iota, N_EXPERTS), axis=0, keepdims=True)
            sel = e_iota == ik
            v = jnp.where(sel, -jnp.inf, v)
            vals.append(mx)
            idxs.append(ik)
            sels.append(sel)
        exps = [jnp.exp(val - vals[0]) for val in vals]
        denom = exps[0] + exps[1] + exps[2] + exps[3]
        idx_ref[:, rows] = jnp.concatenate(idxs, axis=0)
        tw_ref[:, rows] = jnp.concatenate([e / denom for e in exps], axis=0)

        onehot = jnp.zeros((N_EXPERTS, ts), F32)
        for sel in sels:
            onehot = jnp.where(sel, 1.0, onehot)
        total = cnt_ref[...] + _bdot(onehot.astype(BF16), before_ref[...])
        ranks = [jnp.sum(jnp.where(sel, total, 0.0), axis=0, keepdims=True) for sel in sels]
        rank_ref[:, rows] = jnp.concatenate(ranks, axis=0).astype(I32)
        cnt_ref[...] = cnt_ref[...] + jnp.sum(onehot, axis=1, keepdims=True)

    tile(0, z0_ref, z1_ref, hn1_ref, x_ref[ts:2 * ts, :], sc1_ref[0], sh1_ref[0])
    tile(1, z1_ref, z0_ref, hn0_ref, xn_ref[...], sc1n_ref[0], sh1n_ref[0])


def _mixer(x2d, n1g, sc1, sh1, w_in, gt1, sc2, sh2, conv_w, conv_b, wa_bd, ba, wx_bd, bx, lam,
           ln_g, ln_b, ws, bs_tile, wbr, wbs, wout, n2g, wr_split, br, shifts, before, w_gu, w_down,
           batch0, batches, seq, expert0, n_cast, cast_prev):
    d = x2d.shape[1]
    ts = MIXER_TS
    n = batches * seq
    tiles_per_seq = seq // ts
    pairs_per_seq = tiles_per_seq // 2
    assert tiles_per_seq % 2 == 0
    steps = batches * pairs_per_seq
    pair0 = batch0 * pairs_per_seq
    next_tile = lambda s: jnp.minimum(2 * s + 2, 2 * steps - 1)
    pair = lambda s: (s, 0)
    pairt = lambda s: (0, s)
    bvec = lambda s: (batch0 + s // pairs_per_seq, 0, 0)
    bvec_next = lambda s: (batch0 + next_tile(s) // tiles_per_seq, 0, 0)
    n_e, k_gu, n_gu = w_gu.shape
    _, k_d, n_d = w_down.shape
    assert (n_cast * k_gu) % steps == 0 and (n_cast * k_d) % steps == 0
    rows_gu, rows_d = n_cast * k_gu // steps, n_cast * k_d // steps
    cast_gu = lambda s: (expert0 * k_gu // rows_gu + s, 0)
    cast_d = lambda s: (expert0 * k_d // rows_d + s, 0)
    w_gu = w_gu.reshape(n_e * k_gu, n_gu)
    w_down = w_down.reshape(n_e * k_d, n_d)
    c2 = lambda s: (0, 0)
    c3 = lambda s: (0, 0, 0)
    in_specs = [
        pl.BlockSpec((2 * ts, d), lambda s: (pair0 + s, 0)),
        pl.BlockSpec((ts, d), lambda s: (2 * pair0 + next_tile(s), 0)),
        pl.BlockSpec((1, d), c2),
        pl.BlockSpec((1, 1, d), bvec),
        pl.BlockSpec((1, 1, d), bvec),
        pl.BlockSpec((1, 1, d), bvec_next),
        pl.BlockSpec((1, 1, d), bvec_next),
        pl.BlockSpec((d, D_IN), c2),
        pl.BlockSpec((1, 1, d), bvec),
        pl.BlockSpec((1, 1, d), bvec),
        pl.BlockSpec((1, 1, d), bvec),
        pl.BlockSpec((CONV_WIDTH, D_RNN), c2),
        pl.BlockSpec((1, D_RNN), c2),
        pl.BlockSpec((N_GATE_BLOCKS, GATE_BLOCK, GATE_BLOCK), c3),
        pl.BlockSpec((1, D_RNN), c2),
        pl.BlockSpec((N_GATE_BLOCKS, GATE_BLOCK, GATE_BLOCK), c3),
        pl.BlockSpec((1, D_RNN), c2),
        pl.BlockSpec((1, D_RNN), c2),
        pl.BlockSpec((1, D_SG), c2),
        pl.BlockSpec((1, D_SG), c2),
        pl.BlockSpec((SG_GROUPS, SG_CHUNK, SG_CHUNK), c3),
        pl.BlockSpec((SG_CHUNK, D_SG), c2),
        pl.BlockSpec((D_RNN, d), c2),
        pl.BlockSpec((D_SG, d), c2),
        pl.BlockSpec((d, d), c2),
        pl.BlockSpec((1, d), c2),
        pl.BlockSpec((2 * N_EXPERTS, d), c2),
        pl.BlockSpec((N_EXPERTS, 1), c2),
        pl.BlockSpec((CONV_WIDTH - 1, ts, ts), c3),
        pl.BlockSpec((ts, ts), c2),
        pl.BlockSpec((rows_gu, n_gu), cast_gu),
        pl.BlockSpec((rows_d, n_d), cast_d),
    ]
    assert len(in_specs) == N_MIXER_INPUTS
    args = [x2d, x2d, n1g, sc1, sh1, sc1, sh1, w_in, gt1, sc2, sh2, conv_w, conv_b, wa_bd, ba,
            wx_bd, bx, lam, ln_g, ln_b, ws, bs_tile, wbr, wbs, wout, n2g, wr_split, br, shifts,
            before, w_gu, w_down]
    aliases = {}
    if cast_prev is not None:
        for j, prev in enumerate(cast_prev):
            in_specs.append(pl.BlockSpec(memory_space=pl.ANY))
            aliases[len(args)] = N_MIXER_TOKEN_OUTPUTS + j
            args.append(prev)
    out_specs = [
        pl.BlockSpec((2 * ts, d), pair),
        pl.BlockSpec((2 * ts, PACK_W), pair),
        pl.BlockSpec((2 * ts, PACK_W), pair),
        pl.BlockSpec((TOP_K, 2 * ts), pairt),
        pl.BlockSpec((TOP_K, 2 * ts), pairt),
        pl.BlockSpec((TOP_K, 2 * ts), pairt),
        pl.BlockSpec((N_EXPERTS, 1), c2),
        pl.BlockSpec((rows_gu, n_gu), cast_gu),
        pl.BlockSpec((rows_d, n_d), cast_d),
    ]
    out_shape = [
        jax.ShapeDtypeStruct((n, d), F32),
        jax.ShapeDtypeStruct((n, PACK_W), jnp.uint32),
        jax.ShapeDtypeStruct((n, PACK_W), jnp.uint32),
        jax.ShapeDtypeStruct((TOP_K, n), I32),
        jax.ShapeDtypeStruct((TOP_K, n), F32),
        jax.ShapeDtypeStruct((TOP_K, n), I32),
        jax.ShapeDtypeStruct((N_EXPERTS, 1), F32),
        jax.ShapeDtypeStruct(w_gu.shape, BF16),
        jax.ShapeDtypeStruct(w_down.shape, BF16),
    ]
    scratch = [
        pltpu.VMEM((ts, D_IN), F32),
        pltpu.VMEM((ts, D_IN), F32),
        pltpu.VMEM((ts, d), BF16),
        pltpu.VMEM((ts, d), BF16),
        pltpu.VMEM((2 * SUBLANES, D_RNN), F32),
        pltpu.VMEM((ts, D_RNN), F32),
        pltpu.VMEM((ts, D_RNN), F32),
        pltpu.VMEM((ts, D_RNN), F32),
        pltpu.VMEM((ts, D_SG), F32),
        pltpu.VMEM((SUBLANES, D_RNN), F32),
    ]
    return pl.pallas_call(
        functools.partial(_mixer_kernel, pairs_per_seq=pairs_per_seq, n_aliased=len(aliases)),
        grid=(steps,),
        in_specs=in_specs,
        out_specs=out_specs,
        out_shape=out_shape,
        scratch_shapes=scratch,
        input_output_aliases=aliases,
        compiler_params=pltpu.CompilerParams(
            dimension_semantics=("arbitrary",), vmem_limit_bytes=VMEM_LIMIT),
        name="mixer",
    )(*args)


def _sc_mesh():
    return plsc.VectorSubcoreMesh(core_axis_name="core", subcore_axis_name="subcore")


def _sc_scatter_rows(rows, pos, cap):
    n, d = rows.shape
    kk = pos.shape[0]

    @functools.partial(
        pl.kernel, out_type=jax.ShapeDtypeStruct((cap, d), rows.dtype), mesh=_sc_mesh(),
        scratch_types=[], name="sc_scatter_rows")
    def scatter(x_hbm, i_hbm, o_hbm):
        def body(x_vmem, i_vmem):
            pltpu.sync_copy(x_vmem, o_hbm.at[i_vmem.at[0]])

        pltpu.emit_pipeline(
            body,
            grid=(n // SC_WINDOW, kk),
            in_specs=[pl.BlockSpec((SC_WINDOW, d), lambda i, k: (i, 0)),
                      pl.BlockSpec((1, SC_WINDOW), lambda i, k: (k, i))],
            out_specs=[],
            core_axis_name=("core", "subcore"),
            dimension_semantics=(pltpu.PARALLEL, pltpu.ARBITRARY),
        )(x_hbm, i_hbm)

    return scatter(rows, pos)


def _sc_gather_rows(table, idx):
    m = idx.shape[1]
    d = table.shape[1]

    @functools.partial(
        pl.kernel, out_type=jax.ShapeDtypeStruct((m, d), table.dtype), mesh=_sc_mesh(),
        scratch_types=[], name="sc_gather_rows")
    def gather(x_hbm, i_hbm, o_hbm):
        def body(i_vmem, o_vmem):
            pltpu.sync_copy(x_hbm.at[i_vmem.at[0]], o_vmem)

        pltpu.emit_pipeline(
            body,
            grid=(m // SC_WINDOW,),
            in_specs=[pl.BlockSpec((1, SC_WINDOW), lambda i: (0, i))],
            out_specs=[pl.BlockSpec((SC_WINDOW, d), lambda i: (i, 0))],
            core_axis_name=("core", "subcore"),
            dimension_semantics=(pltpu.PARALLEL,),
        )(i_hbm, o_hbm)

    return gather(table, idx)


def _pos_kernel(cnt_ref, idx_ref, rank_ref, pos_ref, be_ref, nv_ref):
    assert MOE_BM & (MOE_BM - 1) == 0
    starts, ends, run = [], [], jnp.int32(0)
    for e in range(N_EXPERTS):
        starts.append(run)
        run = run + ((cnt_ref[e] + (MOE_BM - 1)) & jnp.int32(-MOE_BM))
        ends.append(run)

    idx = idx_ref[...]
    pos = rank_ref[...]
    for e in range(N_EXPERTS):
        pos = pos + jnp.where(idx == e, starts[e], 0)
    pos_ref[...] = pos

    row0 = lax.broadcasted_iota(I32, be_ref.shape, 1) * MOE_BM
    be = jnp.zeros(be_ref.shape, I32)
    for e in range(N_EXPERTS - 1):
        be = be + jnp.where(ends[e] <= row0, 1, 0)
    start = jnp.zeros(be_ref.shape, I32)
    count = jnp.zeros(be_ref.shape, I32)
    for e in range(N_EXPERTS):
        here = be == e
        start = jnp.where(here, starts[e], start)
        count = jnp.where(here, cnt_ref[e], count)
    be_ref[...] = be
    nv_ref[...] = jnp.clip(count - (row0 - start), 0, MOE_BM)


def _slot_plan(counts, top_idx, rank, nb):
    k, n = top_idx.shape
    tn = min(POS_TN, n)
    nbp = -(-nb // LANES) * LANES
    spec = pl.BlockSpec((k, tn), lambda i, c: (0, i))
    blocks = pl.BlockSpec((1, nbp), lambda i, c: (0, 0))
    pos, be, nv = pl.pallas_call(
        _pos_kernel,
        grid_spec=pltpu.PrefetchScalarGridSpec(
            num_scalar_prefetch=1, grid=(n // tn,), in_specs=[spec, spec],
            out_specs=[spec, blocks, blocks]),
        out_shape=[jax.ShapeDtypeStruct((k, n), I32), jax.ShapeDtypeStruct((1, nbp), I32),
                   jax.ShapeDtypeStruct((1, nbp), I32)],
        compiler_params=pltpu.CompilerParams(dimension_semantics=("arbitrary",)),
        name="slot_plan",
    )(counts, top_idx, rank)
    return pos, be.reshape(nbp), nv.reshape(nbp)


def _expert_kernel(be_ref, nv_ref, xa_ref, xb_ref, wgu_hbm, bgu_ref, wd_hbm, bd_ref,
                   ya_ref, yb_ref, wgu_buf, wd_buf, slot_ref, sems):
    i = pl.program_id(0)
    n_blocks = pl.num_programs(0)
    nvalid = nv_ref[i]
    expert = be_ref[i]

    def weight_copies(e, slot):
        return (pltpu.make_async_copy(wgu_hbm.at[e], wgu_buf.at[slot], sems.at[slot, 0]),
                pltpu.make_async_copy(wd_hbm.at[e], wd_buf.at[slot], sems.at[slot, 1]))

    first_of_expert = (nvalid > 0) & ((i == 0) | (expert != be_ref[jnp.maximum(i - 1, 0)]))

    @pl.when(first_of_expert)
    def _():
        @pl.when(i == 0)
        def _():
            slot_ref[0] = 0
            for cp in weight_copies(expert, 0):
                cp.start()

        @pl.when(i > 0)
        def _():
            slot_ref[0] = 1 - slot_ref[0]

        slot = slot_ref[0]
        for cp in weight_copies(expert, slot):
            cp.wait()
        nxt = lax.while_loop(lambda j: (j < n_blocks) & (be_ref[jnp.minimum(j, n_blocks - 1)] == expert),
                             lambda j: j + 1, i + 1)
        nxt_c = jnp.minimum(nxt, n_blocks - 1)

        @pl.when((nxt < n_blocks) & (nv_ref[nxt_c] > 0))
        def _():
            for cp in weight_copies(be_ref[nxt_c], 1 - slot):
                cp.start()

    cur = slot_ref[0]

    def mlp_rows(h):
        rows = slice(h * MOE_HALF, (h + 1) * MOE_HALF)
        live = lax.broadcasted_iota(I32, (MOE_HALF, 1), 0) < nvalid - h * MOE_HALF
        xb = jnp.where(live, _unpack_halves(xa_ref[rows, :], xb_ref[rows, :]), 0.0).astype(BF16)
        gu = _bdot(xb, wgu_buf[cur]) + bgu_ref[0]
        gate = jnp.minimum(gu[:, :D_EXPERT], SWIGLU_LIMIT)
        up = jnp.clip(gu[:, D_EXPERT:], -SWIGLU_LIMIT, SWIGLU_LIMIT)
        act = (up + 1.0) * (gate * _sigmoid(SWIGLU_ALPHA * gate))
        y = _bdot(act.astype(BF16), wd_buf[cur]) + bd_ref[0]
        ya_ref[rows, :], yb_ref[rows, :] = _pack_halves(y)

    def zero_rows(h):
        rows = slice(h * MOE_HALF, (h + 1) * MOE_HALF)
        ya_ref[rows, :] = jnp.zeros((MOE_HALF, PACK_W), jnp.uint32)
        yb_ref[rows, :] = jnp.zeros((MOE_HALF, PACK_W), jnp.uint32)

    @pl.when(nvalid > MOE_HALF)
    def _():
        mlp_rows(0)
        mlp_rows(1)

    @pl.when((nvalid > 0) & (nvalid <= MOE_HALF))
    def _():
        mlp_rows(0)
        zero_rows(1)

    @pl.when(nvalid <= 0)
    def _():
        zero_rows(0)
        zero_rows(1)


def _experts(block_e, n_valid, xa, xb, w_gu, b_gu, w_down, b_down):
    cap = xa.shape[0]
    d = D_MODEL
    nb = cap // MOE_BM
    half = pl.BlockSpec((MOE_BM, PACK_W), lambda i, be, nv: (i, 0))
    grid_spec = pltpu.PrefetchScalarGridSpec(
        num_scalar_prefetch=2,
        grid=(nb,),
        in_specs=[
            half,
            half,
            pl.BlockSpec(memory_space=pl.ANY),
            pl.BlockSpec((1, 1, 2 * D_EXPERT), lambda i, be, nv: (be[i], 0, 0)),
            pl.BlockSpec(memory_space=pl.ANY),
            pl.BlockSpec((1, 1, d), lambda i, be, nv: (be[i], 0, 0)),
        ],
        out_specs=[half, half],
        scratch_shapes=[pltpu.VMEM((2, d, 2 * D_EXPERT), BF16), pltpu.VMEM((2, D_EXPERT, d), BF16),
                        pltpu.SMEM((1,), I32), pltpu.SemaphoreType.DMA((2, 2))],
    )
    return pl.pallas_call(
        _expert_kernel,
        grid_spec=grid_spec,
        out_shape=[jax.ShapeDtypeStruct((cap, PACK_W), jnp.uint32)] * 2,
        compiler_params=pltpu.CompilerParams(
            dimension_semantics=("arbitrary",), vmem_limit_bytes=VMEM_LIMIT),
        name="experts",
    )(block_e, n_valid, xa, xb, w_gu, b_gu, w_down, b_down)


def _final_kernel(x2_ref, ya_ref, yb_ref, tw_ref, gt2_ref, fg_ref, *rest):
    o_ref = rest[-1]
    tw = tw_ref[...].T
    moe = tw[:, 0:1] * _unpack_halves(ya_ref[0], yb_ref[0])
    for k in range(1, TOP_K):
        moe = moe + tw[:, k:k + 1] * _unpack_halves(ya_ref[k], yb_ref[k])
    x3 = x2_ref[...] + gt2_ref[0] * moe
    ms = jnp.mean(x3 * x3, axis=-1, keepdims=True)
    o_ref[...] = (x3 * lax.rsqrt(ms + EPS)) * fg_ref[...]


def _final(x2, yga, ygb, tw_tok, gt2, final_g, seq, batch0, n_total, out_prev):
    n, d = x2.shape
    tiles_per_seq = seq // FINAL_TM
    tile0 = batch0 * tiles_per_seq
    half = pl.BlockSpec((TOP_K, FINAL_TM, PACK_W), lambda i: (0, i, 0))
    in_specs = [
        pl.BlockSpec((FINAL_TM, d), lambda i: (i, 0)),
        half,
        half,
        pl.BlockSpec((TOP_K, FINAL_TM), lambda i: (0, i)),
        pl.BlockSpec((1, 1, d), lambda i: (batch0 + i // tiles_per_seq, 0, 0)),
        pl.BlockSpec((1, d), lambda i: (0, 0)),
    ]
    args = [x2, yga, ygb, tw_tok, gt2, final_g]
    aliases = {}
    if out_prev is not None:
        in_specs.append(pl.BlockSpec(memory_space=pl.ANY))
        aliases = {len(args): 0}
        args.append(out_prev)
    return pl.pallas_call(
        _final_kernel,
        grid=(n // FINAL_TM,),
        in_specs=in_specs,
        out_specs=pl.BlockSpec((FINAL_TM, d), lambda i: (tile0 + i, 0)),
        out_shape=jax.ShapeDtypeStruct((n_total, d), F32),
        input_output_aliases=aliases,
        compiler_params=pltpu.CompilerParams(
            dimension_semantics=("arbitrary",), vmem_limit_bytes=VMEM_LIMIT),
        name="final",
    )(*args)


def _block_diag(w):
    per = GATE_BLOCK // RNN_HEAD_DIM
    w4 = w.reshape(N_GATE_BLOCKS, per, RNN_HEAD_DIM, RNN_HEAD_DIM)
    eye = jnp.eye(per, dtype=w.dtype)
    bd = jnp.einsum("gpij,pq->gpiqj", w4, eye)
    return bd.reshape(N_GATE_BLOCKS, GATE_BLOCK, GATE_BLOCK)


def _layer(x2d, c, batch, seq, ada_w, ada_b, norm1_g, w_in, conv_w, conv_b, lru_wa, lru_ba,
           lru_wx, lru_bx, lru_lam, sg_ln_g, sg_ln_b, sg_ws, sg_bs, w_br_rnn, w_br_sg, w_out,
           norm2_g, w_router, b_router, w_gu, b_gu, w_down, b_down, final_g):
    n_total, d = x2d.shape
    mod = _ada(c, ada_w, ada_b)
    sh1, sc1, gt1, sh2, sc2, gt2 = [
        mod[:, i * d:(i + 1) * d].reshape(batch, 1, d) for i in range(N_MOD)]
    row = lambda v: v.reshape(1, -1)

    bs_tile = jnp.repeat(sg_bs.T, SG_GROUP_DIM, axis=1)
    wr_t = w_router.T
    wr_hi = wr_t.astype(BF16)
    wr_lo = (wr_t - wr_hi.astype(F32)).astype(BF16)
    t_out = lax.broadcasted_iota(I32, (MIXER_TS, MIXER_TS), 0)
    t_in = lax.broadcasted_iota(I32, (MIXER_TS, MIXER_TS), 1)
    shifts = jnp.stack([(t_out - t_in == s) for s in range(1, CONV_WIDTH)]).astype(BF16)
    before = (t_out < t_in).astype(BF16)
    causal = jnp.tril(jnp.ones((SG_CHUNK, SG_CHUNK), dtype=bool))
    ws_causal = jnp.where(causal[None], sg_ws, 0.0).astype(BF16)
    mixer_weights = (
        conv_w, row(conv_b), _block_diag(0.5 * lru_wa).astype(BF16), row(0.5 * lru_ba),
        _block_diag(0.5 * lru_wx).astype(BF16), row(0.5 * lru_bx), row(lru_lam), row(sg_ln_g),
        row(sg_ln_b), ws_causal, bs_tile, (0.25 * w_br_rnn).astype(BF16), (0.5 * w_br_sg).astype(BF16),
        (0.5 * w_out).astype(BF16), row(norm2_g), jnp.concatenate([wr_hi, wr_lo], axis=0),
        b_router.reshape(N_EXPERTS, 1), shifts, before)
    gate_cols = 2 * D_RNN + 2 * D_SG
    col_scale = jnp.where(jnp.arange(D_IN) >= gate_cols, 0.5, 1.0).astype(F32)
    w_in_bf = (w_in * col_scale[None, :]).astype(BF16)

    groups = TOKEN_GROUPS if batch % TOKEN_GROUPS == 0 and N_EXPERTS % TOKEN_GROUPS == 0 else 1
    batches = batch // groups
    n_cast = N_EXPERTS // groups
    n = batches * seq
    mixed, cast = [], None
    for grp in range(groups):
        *outs, w_gu_bf, w_down_bf = _mixer(
            x2d, row(norm1_g), sc1, sh1, w_in_bf, gt1, sc2, sh2, *mixer_weights, w_gu, w_down,
            grp * batches, batches, seq, grp * n_cast, n_cast, cast)
        cast = (w_gu_bf, w_down_bf)
        mixed.append(outs)

    out = None
    for grp in range(groups):
        batch0 = grp * batches
        x2, h2a, h2b, top_idx, top_w, rank, counts = mixed[grp]

        cap = n * TOP_K + N_EXPERTS * MOE_BM
        nb = cap // MOE_BM
        pos, block_e, n_valid = _slot_plan(counts[:, 0].astype(I32), top_idx, rank, nb)

        xa = _sc_scatter_rows(h2a, pos, cap)
        xb = _sc_scatter_rows(h2b, pos, cap)
        ya, yb = _experts(block_e, n_valid, xa, xb, cast[0].reshape(w_gu.shape),
                          b_gu.reshape(N_EXPERTS, 1, -1), cast[1].reshape(w_down.shape),
                          b_down.reshape(N_EXPERTS, 1, -1))
        flat_pos = pos.reshape(1, -1)
        yga = _sc_gather_rows(ya, flat_pos).reshape(TOP_K, n, PACK_W)
        ygb = _sc_gather_rows(yb, flat_pos).reshape(TOP_K, n, PACK_W)
        out = _final(x2, yga, ygb, top_w, gt2, final_g.reshape(1, d), seq, batch0, n_total, out)
    return out


def kernel(x, c, ada_w, ada_b, norm1_g, w_in, conv_w, conv_b, lru_wa, lru_ba, lru_wx, lru_bx,
           lru_lam, sg_ln_g, sg_ln_b, sg_ws, sg_bs, w_br_rnn, w_br_sg, w_out, norm2_g,
           w_router, b_router, w_gu, b_gu, w_down, b_down, final_g):
    batch, seq, d = x.shape
    depth = ada_w.shape[0]
    assert depth == 1, "the combine is fused with the final norm, which follows the only layer"
    x2d = x.reshape(batch * seq, d)
    l = 0
    out = _layer(
        x2d, c, batch, seq, ada_w[l], ada_b[l], norm1_g[l], w_in[l], conv_w[l], conv_b[l],
        lru_wa[l], lru_ba[l], lru_wx[l], lru_bx[l], lru_lam[l], sg_ln_g[l], sg_ln_b[l],
        sg_ws[l], sg_bs[l], w_br_rnn[l], w_br_sg[l], w_out[l], norm2_g[l], w_router[l],
        b_router[l], w_gu[l], b_gu[l], w_down[l], b_down[l], final_g)
    return out.reshape(batch, seq, d)
```

```python
import functools

import jax
import jax.numpy as jnp
from jax import lax
from jax.experimental import pallas as pl
from jax.experimental.pallas import tpu as pltpu
from jax.experimental.pallas import tpu_sc as plsc

F32 = jnp.float32
BF16 = jnp.bfloat16
I32 = jnp.int32

D_MODEL = 1024
D_RNN = 1024
RNN_HEADS = 16
RNN_HEAD_DIM = D_RNN // RNN_HEADS
CONV_WIDTH = 4
LRU_C = 8.0
D_SG = 1024
SG_GROUPS = 8
SG_GROUP_DIM = D_SG // SG_GROUPS
SG_CHUNK = 128
N_EXPERTS = 32
TOP_K = 4
D_EXPERT = 1024
SWIGLU_LIMIT = 7.0
SWIGLU_ALPHA = 1.702
EPS = 1e-6
N_MOD = 6
D_IN = 2 * D_RNN + 2 * D_SG + 2 * D_MODEL

SUBLANES = 8
LANES = 128
GATE_BLOCK = 256
N_GATE_BLOCKS = D_RNN // GATE_BLOCK

ADA_TN = 1536
INPROJ_TN = 1024
MIXER_TS = 256
MOE_BM = 512
MOE_HALF = MOE_BM // 2
MOE_QUARTER = MOE_BM // 4
TOKEN_GROUPS = 2
SC_WINDOW = 128
PACK_W = D_MODEL // 4
FINAL_TM = 512
POS_TN = 8192
VMEM_LIMIT = 60 * 1024 * 1024


def _sigmoid(x):
    return 0.5 * jnp.tanh(0.5 * x) + 0.5


def _gelu_tanh_x2(x):
    k = 0.7978845608028654
    return x + x * jnp.tanh(x * (k + (k * 0.044715) * (x * x)))


def _bdot(a, b):
    return jnp.dot(a, b, preferred_element_type=F32)


def _pack_halves(v):
    word = pltpu.pack_elementwise([v[:, 2 * PACK_W:], v[:, :2 * PACK_W]], packed_dtype=BF16)
    return word[:, :PACK_W], word[:, PACK_W:]


def _unpack_halves(wa, wb):
    part = lambda w, i: pltpu.unpack_elementwise(w, index=i, packed_dtype=BF16, unpacked_dtype=F32)
    return jnp.concatenate([part(wa, 1), part(wb, 1), part(wa, 0), part(wb, 0)], axis=1)


def _ada_kernel(c_ref, w_ref, b_ref, o_ref):
    c = c_ref[...]
    s = c * _sigmoid(c)
    o_ref[...] = jnp.dot(s, w_ref[...], preferred_element_type=F32,
                         precision=lax.Precision.HIGHEST) + b_ref[...]


def _ada(c, ada_w, ada_b):
    b, d = c.shape
    n = ada_w.shape[1]
    return pl.pallas_call(
        _ada_kernel,
        grid=(n // ADA_TN,),
        in_specs=[
            pl.BlockSpec((b, d), lambda j: (0, 0)),
            pl.BlockSpec((d, ADA_TN), lambda j: (0, j)),
            pl.BlockSpec((1, ADA_TN), lambda j: (0, j)),
        ],
        out_specs=pl.BlockSpec((b, ADA_TN), lambda j: (0, j)),
        out_shape=jax.ShapeDtypeStruct((b, n), F32),
        compiler_params=pltpu.CompilerParams(
            dimension_semantics=("arbitrary",), vmem_limit_bytes=VMEM_LIMIT),
        name="ada",
    )(c, ada_w, ada_b.reshape(1, n))


def _norm_mod(x, g, sc, sh):
    ms = jnp.mean(x * x, axis=-1, keepdims=True)
    y = x * lax.rsqrt(ms + EPS)
    return y * (g * (1.0 + sc)) + sh


N_MIXER_INPUTS = 32
N_MIXER_TOKEN_OUTPUTS = 7


def _mixer_kernel(*refs, pairs_per_seq, n_aliased):
    (x_ref, xn_ref, n1g_ref, sc1_ref, sh1_ref, sc1n_ref, sh1n_ref, win_ref,
     gt1_ref, sc2_ref, sh2_ref,
     convw_ref, convb_ref, wa_ref, ba_ref, wx_ref, bx_ref, lam_ref,
     lng_ref, lnb_ref, ws_ref, bs_ref, wbr_ref, wbs_ref, wout_ref,
     n2g_ref, wr_ref, br_ref, shift_ref, before_ref, wgu_ref, wd_ref) = refs[:N_MIXER_INPUTS]
    (x2_ref, h2a_ref, h2b_ref, idx_ref, tw_ref, rank_ref, cnt_ref, wgu_bf_ref, wd_bf_ref,
     z0_ref, z1_ref, hn0_ref, hn1_ref, xp_ref, xc_ref, a_ref, hh_ref, sv_ref,
     hstate_ref) = refs[N_MIXER_INPUTS + n_aliased:]
    ts = MIXER_TS
    s = pl.program_id(0)

    wgu_bf_ref[...] = wgu_ref[...].astype(BF16)
    wd_bf_ref[...] = wd_ref[...].astype(BF16)

    def inproj_norm(hn_ref, x_rows, sc, sh):
        hn_ref[...] = _norm_mod(x_rows, n1g_ref[...], sc, sh).astype(BF16)

    def inproj_chunks(z_dst, hn_ref, first, last):
        for c in range(first * INPROJ_TN, last * INPROJ_TN, INPROJ_TN):
            z_dst[:, c:c + INPROJ_TN] = _bdot(hn_ref[...], win_ref[:, c:c + INPROJ_TN])

    @pl.when(s == 0)
    def _():
        cnt_ref[...] = jnp.zeros_like(cnt_ref)
        inproj_norm(hn0_ref, x_ref[0:ts, :], sc1_ref[0], sh1_ref[0])
        inproj_chunks(z0_ref, hn0_ref, 0, D_IN // INPROJ_TN)

    @pl.when(s % pairs_per_seq == 0)
    def _():
        xp_ref[0:SUBLANES, :] = jnp.zeros((SUBLANES, D_RNN), F32)
        hstate_ref[...] = jnp.zeros_like(hstate_ref)

    def tile(k, z_ref, z_next_ref, hn_ref, next_rows, next_sc, next_sh):
        rows = slice(k * ts, (k + 1) * ts)
        inproj_norm(hn_ref, next_rows, next_sc, next_sh)

        x16 = z_ref[:, 0:D_RNN].astype(BF16)
        rnn_x = x16.astype(F32)
        cw = convw_ref[...]
        xc = cw[3:4] * rnn_x + convb_ref[...]
        for sft in range(1, CONV_WIDTH):
            xc = xc + cw[3 - sft:4 - sft] * _bdot(shift_ref[sft - 1], x16)
        xc_ref[...] = xc
        xp_ref[SUBLANES:2 * SUBLANES, :] = rnn_x[0:SUBLANES, :]
        xc_ref[0:SUBLANES, :] = (
            cw[3:4] * xp_ref[SUBLANES:2 * SUBLANES, :]
            + cw[2:3] * xp_ref[SUBLANES - 1:2 * SUBLANES - 1, :]
            + cw[1:2] * xp_ref[SUBLANES - 2:2 * SUBLANES - 2, :]
            + cw[0:1] * xp_ref[SUBLANES - 3:2 * SUBLANES - 3, :]) + convb_ref[...]
        xp_ref[0:SUBLANES, :] = rnn_x[ts - SUBLANES:ts, :]
        xc = xc_ref[...]

        xcb = xc.astype(BF16)
        r_parts, i_parts = [], []
        for g in range(N_GATE_BLOCKS):
            blk = xcb[:, g * GATE_BLOCK:(g + 1) * GATE_BLOCK]
            r_parts.append(_bdot(blk, wa_ref[g]))
            i_parts.append(_bdot(blk, wx_ref[g]))
        inproj_chunks(z_next_ref, hn_ref, 0, 3)
        r_t = jnp.tanh(jnp.concatenate(r_parts, axis=1) + ba_ref[...])
        i_t = jnp.tanh(jnp.concatenate(i_parts, axis=1) + bx_ref[...])

        nl = -lam_ref[...]
        softplus = jnp.maximum(nl, 0.0) + jnp.log(1.0 + jnp.exp(-jnp.abs(nl)))
        half_c = (-0.5 * LRU_C) * softplus
        a = jnp.exp(r_t * half_c + half_c)
        t = 1.0 - a * a
        u = jnp.where(t > 0.0, t * lax.rsqrt(t), 0.0) * ((i_t + 1.0) * xc)

        groups = ts // SUBLANES
        a3 = a.reshape(groups, SUBLANES, D_RNN)
        h3 = u.reshape(groups, SUBLANES, D_RNN)
        sub = lax.broadcasted_iota(I32, (groups, SUBLANES, D_RNN), 1)
        for step in (1, 2, 4):
            keep = sub >= step
            a_sh = jnp.where(keep, pltpu.roll(a3, step, 1), 1.0)
            h_sh = jnp.where(keep, pltpu.roll(h3, step, 1), 0.0)
            h3 = h3 + a3 * h_sh
            a3 = a3 * a_sh
        a_ref[...] = a3.reshape(ts, D_RNN)
        hh_ref[...] = h3.reshape(ts, D_RNN)

        hc = hstate_ref[...]
        for gi in range(groups):
            grp = slice(gi * SUBLANES, (gi + 1) * SUBLANES)
            hg = hh_ref[grp, :] + a_ref[grp, :] * hc
            hh_ref[grp, :] = hg
            hc = jnp.broadcast_to(hg[SUBLANES - 1:SUBLANES, :], (SUBLANES, D_RNN))
        hstate_ref[...] = hc

        inproj_chunks(z_next_ref, hn_ref, 3, 4)
        y_rnn = (hh_ref[...] * _gelu_tanh_x2(z_ref[:, D_RNN:2 * D_RNN].astype(F32))).astype(BF16)

        gv = _gelu_tanh_x2(z_ref[:, 2 * D_RNN + D_SG:2 * D_RNN + 2 * D_SG].astype(F32))
        mu = jnp.mean(gv, axis=-1, keepdims=True)
        dv = gv - mu
        var = jnp.mean(dv * dv, axis=-1, keepdims=True)
        vn = (dv * lax.rsqrt(var + 4.0 * EPS) * lng_ref[...] + lnb_ref[...]).astype(BF16)
        for g in range(SG_GROUPS):
            wg = ws_ref[g]
            cols = slice(g * SG_GROUP_DIM, (g + 1) * SG_GROUP_DIM)
            for n in range(ts // SG_CHUNK):
                chunk = slice(n * SG_CHUNK, (n + 1) * SG_CHUNK)
                sv_ref[chunk, cols] = _bdot(wg, vn[chunk, cols]) + bs_ref[:, cols]
        inproj_chunks(z_next_ref, hn_ref, 4, 5)
        gu = _gelu_tanh_x2(z_ref[:, 2 * D_RNN:2 * D_RNN + D_SG].astype(F32))
        y_sg = (gu * sv_ref[...]).astype(BF16)

        g_rnn = z_ref[:, 2 * D_RNN + 2 * D_SG:2 * D_RNN + 2 * D_SG + D_MODEL].astype(F32)
        g_sg = z_ref[:, 2 * D_RNN + 2 * D_SG + D_MODEL:D_IN].astype(F32)
        m = ((jnp.tanh(g_rnn) + 1.0) * _bdot(y_rnn, wbr_ref[...])
             + (jnp.tanh(g_sg) + 1.0) * _bdot(y_sg, wbs_ref[...])).astype(BF16)
        x2 = x_ref[rows, :] + gt1_ref[0] * _bdot(m, wout_ref[...])
        x2_ref[rows, :] = x2
        inproj_chunks(z_next_ref, hn_ref, 5, 6)

        h2 = _norm_mod(x2, n2g_ref[...], sc2_ref[0], sh2_ref[0])
        h2a_ref[rows, :], h2b_ref[rows, :] = _pack_halves(h2)
        h_hi = h2.astype(BF16)
        h_lo = (h2 - h_hi.astype(F32)).astype(BF16)
        nt_dims = (((1,), (1,)), ((), ()))
        by_hi = lax.dot_general(wr_ref[...], h_hi, nt_dims, preferred_element_type=F32)
        logits = (by_hi[:N_EXPERTS] + by_hi[N_EXPERTS:]
                  + lax.dot_general(wr_ref[0:N_EXPERTS, :], h_lo, nt_dims,
                                    preferred_element_type=F32)
                  + br_ref[...])
        e_iota = lax.broadcasted_iota(I32, (N_EXPERTS, ts), 0)
        v = logits
        vals, idxs, sels = [], [], []
        for _ in range(TOP_K):
            mx = jnp.max(v, axis=0, keepdims=True)
            ik = jnp.min(jnp.where(v == mx, e_iota, N_EXPERTS), axis=0, keepdims=True)
            sel = e_iota == ik
            v = jnp.where(sel, -jnp.inf, v)
            vals.append(mx)
            idxs.append(ik)
            sels.append(sel)
        exps = [jnp.exp(val - vals[0]) for val in vals]
        denom = exps[0] + exps[1] + exps[2] + exps[3]
        idx_ref[:, rows] = jnp.concatenate(idxs, axis=0)
        tw_ref[:, rows] = jnp.concatenate([e / denom for e in exps], axis=0)

        onehot = jnp.zeros((N_EXPERTS, ts), F32)
        for sel in sels:
            onehot = jnp.where(sel, 1.0, onehot)
        total = cnt_ref[...] + _bdot(onehot.astype(BF16), before_ref[...])
        ranks = [jnp.sum(jnp.where(sel, total, 0.0), axis=0, keepdims=True) for sel in sels]
        rank_ref[:, rows] = jnp.concatenate(ranks, axis=0).astype(I32)
        cnt_ref[...] = cnt_ref[...] + jnp.sum(onehot, axis=1, keepdims=True)

    tile(0, z0_ref, z1_ref, hn1_ref, x_ref[ts:2 * ts, :], sc1_ref[0], sh1_ref[0])
    tile(1, z1_ref, z0_ref, hn0_ref, xn_ref[...], sc1n_ref[0], sh1n_ref[0])


def _mixer(x2d, n1g, sc1, sh1, w_in, gt1, sc2, sh2, conv_w, conv_b, wa_bd, ba, wx_bd, bx, lam,
           ln_g, ln_b, ws, bs_tile, wbr, wbs, wout, n2g, wr_split, br, shifts, before, w_gu, w_down,
           batch0, batches, seq, expert0, n_cast, cast_prev):
    d = x2d.shape[1]
    ts = MIXER_TS
    n = batches * seq
    tiles_per_seq = seq // ts
    pairs_per_seq = tiles_per_seq // 2
    assert tiles_per_seq % 2 == 0
    steps = batches * pairs_per_seq
    pair0 = batch0 * pairs_per_seq
    next_tile = lambda s: jnp.minimum(2 * s + 2, 2 * steps - 1)
    pair = lambda s: (s, 0)
    pairt = lambda s: (0, s)
    bvec = lambda s: (batch0 + s // pairs_per_seq, 0, 0)
    bvec_next = lambda s: (batch0 + next_tile(s) // tiles_per_seq, 0, 0)
    n_e, k_gu, n_gu = w_gu.shape
    _, k_d, n_d = w_down.shape
    assert (n_cast * k_gu) % steps == 0 and (n_cast * k_d) % steps == 0
    rows_gu, rows_d = n_cast * k_gu // steps, n_cast * k_d // steps
    cast_gu = lambda s: (expert0 * k_gu // rows_gu + s, 0)
    cast_d = lambda s: (expert0 * k_d // rows_d + s, 0)
    w_gu = w_gu.reshape(n_e * k_gu, n_gu)
    w_down = w_down.reshape(n_e * k_d, n_d)
    c2 = lambda s: (0, 0)
    c3 = lambda s: (0, 0, 0)
    in_specs = [
        pl.BlockSpec((2 * ts, d), lambda s: (pair0 + s, 0)),
        pl.BlockSpec((ts, d), lambda s: (2 * pair0 + next_tile(s), 0)),
        pl.BlockSpec((1, d), c2),
        pl.BlockSpec((1, 1, d), bvec),
        pl.BlockSpec((1, 1, d), bvec),
        pl.BlockSpec((1, 1, d), bvec_next),
        pl.BlockSpec((1, 1, d), bvec_next),
        pl.BlockSpec((d, D_IN), c2),
        pl.BlockSpec((1, 1, d), bvec),
        pl.BlockSpec((1, 1, d), bvec),
        pl.BlockSpec((1, 1, d), bvec),
        pl.BlockSpec((CONV_WIDTH, D_RNN), c2),
        pl.BlockSpec((1, D_RNN), c2),
        pl.BlockSpec((N_GATE_BLOCKS, GATE_BLOCK, GATE_BLOCK), c3),
        pl.BlockSpec((1, D_RNN), c2),
        pl.BlockSpec((N_GATE_BLOCKS, GATE_BLOCK, GATE_BLOCK), c3),
        pl.BlockSpec((1, D_RNN), c2),
        pl.BlockSpec((1, D_RNN), c2),
        pl.BlockSpec((1, D_SG), c2),
        pl.BlockSpec((1, D_SG), c2),
        pl.BlockSpec((SG_GROUPS, SG_CHUNK, SG_CHUNK), c3),
        pl.BlockSpec((SG_CHUNK, D_SG), c2),
        pl.BlockSpec((D_RNN, d), c2),
        pl.BlockSpec((D_SG, d), c2),
        pl.BlockSpec((d, d), c2),
        pl.BlockSpec((1, d), c2),
        pl.BlockSpec((2 * N_EXPERTS, d), c2),
        pl.BlockSpec((N_EXPERTS, 1), c2),
        pl.BlockSpec((CONV_WIDTH - 1, ts, ts), c3),
        pl.BlockSpec((ts, ts), c2),
        pl.BlockSpec((rows_gu, n_gu), cast_gu),
        pl.BlockSpec((rows_d, n_d), cast_d),
    ]
    assert len(in_specs) == N_MIXER_INPUTS
    args = [x2d, x2d, n1g, sc1, sh1, sc1, sh1, w_in, gt1, sc2, sh2, conv_w, conv_b, wa_bd, ba,
            wx_bd, bx, lam, ln_g, ln_b, ws, bs_tile, wbr, wbs, wout, n2g, wr_split, br, shifts,
            before, w_gu, w_down]
    aliases = {}
    if cast_prev is not None:
        for j, prev in enumerate(cast_prev):
            in_specs.append(pl.BlockSpec(memory_space=pl.ANY))
            aliases[len(args)] = N_MIXER_TOKEN_OUTPUTS + j
            args.append(prev)
    out_specs = [
        pl.BlockSpec((2 * ts, d), pair),
        pl.BlockSpec((2 * ts, PACK_W), pair),
        pl.BlockSpec((2 * ts, PACK_W), pair),
        pl.BlockSpec((TOP_K, 2 * ts), pairt),
        pl.BlockSpec((TOP_K, 2 * ts), pairt),
        pl.BlockSpec((TOP_K, 2 * ts), pairt),
        pl.BlockSpec((N_EXPERTS, 1), c2),
        pl.BlockSpec((rows_gu, n_gu), cast_gu),
        pl.BlockSpec((rows_d, n_d), cast_d),
    ]
    out_shape = [
        jax.ShapeDtypeStruct((n, d), F32),
        jax.ShapeDtypeStruct((n, PACK_W), jnp.uint32),
        jax.ShapeDtypeStruct((n, PACK_W), jnp.uint32),
        jax.ShapeDtypeStruct((TOP_K, n), I32),
        jax.ShapeDtypeStruct((TOP_K, n), F32),
        jax.ShapeDtypeStruct((TOP_K, n), I32),
        jax.ShapeDtypeStruct((N_EXPERTS, 1), F32),
        jax.ShapeDtypeStruct(w_gu.shape, BF16),
        jax.ShapeDtypeStruct(w_down.shape, BF16),
    ]
    scratch = [
        pltpu.VMEM((ts, D_IN), F32),
        pltpu.VMEM((ts, D_IN), F32),
        pltpu.VMEM((ts, d), BF16),
        pltpu.VMEM((ts, d), BF16),
        pltpu.VMEM((2 * SUBLANES, D_RNN), F32),
        pltpu.VMEM((ts, D_RNN), F32),
        pltpu.VMEM((ts, D_RNN), F32),
        pltpu.VMEM((ts, D_RNN), F32),
        pltpu.VMEM((ts, D_SG), F32),
        pltpu.VMEM((SUBLANES, D_RNN), F32),
    ]
    return pl.pallas_call(
        functools.partial(_mixer_kernel, pairs_per_seq=pairs_per_seq, n_aliased=len(aliases)),
        grid=(steps,),
        in_specs=in_specs,
        out_specs=out_specs,
        out_shape=out_shape,
        scratch_shapes=scratch,
        input_output_aliases=aliases,
        compiler_params=pltpu.CompilerParams(
            dimension_semantics=("arbitrary",), vmem_limit_bytes=VMEM_LIMIT),
        name="mixer",
    )(*args)


def _sc_mesh():
    return plsc.VectorSubcoreMesh(core_axis_name="core", subcore_axis_name="subcore")


def _sc_scatter_rows(rows, pos, cap):
    n, d = rows.shape
    kk = pos.shape[0]

    @functools.partial(
        pl.kernel, out_type=jax.ShapeDtypeStruct((cap, d), rows.dtype), mesh=_sc_mesh(),
        scratch_types=[], name="sc_scatter_rows")
    def scatter(x_hbm, i_hbm, o_hbm):
        def body(x_vmem, i_vmem):
            pltpu.sync_copy(x_vmem, o_hbm.at[i_vmem.at[0]])

        pltpu.emit_pipeline(
            body,
            grid=(n // SC_WINDOW, kk),
            in_specs=[pl.BlockSpec((SC_WINDOW, d), lambda i, k: (i, 0)),
                      pl.BlockSpec((1, SC_WINDOW), lambda i, k: (k, i))],
            out_specs=[],
            core_axis_name=("core", "subcore"),
            dimension_semantics=(pltpu.PARALLEL, pltpu.ARBITRARY),
        )(x_hbm, i_hbm)

    return scatter(rows, pos)


def _sc_gather_rows(table, idx):
    m = idx.shape[1]
    d = table.shape[1]

    @functools.partial(
        pl.kernel, out_type=jax.ShapeDtypeStruct((m, d), table.dtype), mesh=_sc_mesh(),
        scratch_types=[], name="sc_gather_rows")
    def gather(x_hbm, i_hbm, o_hbm):
        def body(i_vmem, o_vmem):
            pltpu.sync_copy(x_hbm.at[i_vmem.at[0]], o_vmem)

        pltpu.emit_pipeline(
            body,
            grid=(m // SC_WINDOW,),
            in_specs=[pl.BlockSpec((1, SC_WINDOW), lambda i: (0, i))],
            out_specs=[pl.BlockSpec((SC_WINDOW, d), lambda i: (i, 0))],
            core_axis_name=("core", "subcore"),
            dimension_semantics=(pltpu.PARALLEL,),
        )(i_hbm, o_hbm)

    return gather(table, idx)


def _pos_kernel(cnt_ref, idx_ref, rank_ref, pos_ref, be_ref, nv_ref):
    assert MOE_BM & (MOE_BM - 1) == 0
    starts, ends, run = [], [], jnp.int32(0)
    for e in range(N_EXPERTS):
        starts.append(run)
        run = run + ((cnt_ref[e] + (MOE_BM - 1)) & jnp.int32(-MOE_BM))
        ends.append(run)

    idx = idx_ref[...]
    pos = rank_ref[...]
    for e in range(N_EXPERTS):
        pos = pos + jnp.where(idx == e, starts[e], 0)
    pos_ref[...] = pos

    row0 = lax.broadcasted_iota(I32, be_ref.shape, 1) * MOE_BM
    be = jnp.zeros(be_ref.shape, I32)
    for e in range(N_EXPERTS - 1):
        be = be + jnp.where(ends[e] <= row0, 1, 0)
    start = jnp.zeros(be_ref.shape, I32)
    count = jnp.zeros(be_ref.shape, I32)
    for e in range(N_EXPERTS):
        here = be == e
        start = jnp.where(here, starts[e], start)
        count = jnp.where(here, cnt_ref[e], count)
    be_ref[...] = be
    nv_ref[...] = jnp.clip(count - (row0 - start), 0, MOE_BM)


def _slot_plan(counts, top_idx, rank, nb):
    k, n = top_idx.shape
    tn = min(POS_TN, n)
    nbp = -(-nb // LANES) * LANES
    spec = pl.BlockSpec((k, tn), lambda i, c: (0, i))
    blocks = pl.BlockSpec((1, nbp), lambda i, c: (0, 0))
    pos, be, nv = pl.pallas_call(
        _pos_kernel,
        grid_spec=pltpu.PrefetchScalarGridSpec(
            num_scalar_prefetch=1, grid=(n // tn,), in_specs=[spec, spec],
            out_specs=[spec, blocks, blocks]),
        out_shape=[jax.ShapeDtypeStruct((k, n), I32), jax.ShapeDtypeStruct((1, nbp), I32),
                   jax.ShapeDtypeStruct((1, nbp), I32)],
        compiler_params=pltpu.CompilerParams(dimension_semantics=("arbitrary",)),
        name="slot_plan",
    )(counts, top_idx, rank)
    return pos, be.reshape(nbp), nv.reshape(nbp)


def _expert_kernel(be_ref, nv_ref, xa_ref, xb_ref, wgu_hbm, bgu_ref, wd_hbm, bd_ref,
                   ya_ref, yb_ref, wgu_buf, wd_buf, slot_ref, sems):
    i = pl.program_id(0)
    n_blocks = pl.num_programs(0)
    nvalid = nv_ref[i]
    expert = be_ref[i]

    def weight_copies(e, slot):
        return (pltpu.make_async_copy(wgu_hbm.at[e], wgu_buf.at[slot], sems.at[slot, 0]),
                pltpu.make_async_copy(wd_hbm.at[e], wd_buf.at[slot], sems.at[slot, 1]))

    first_of_expert = (nvalid > 0) & ((i == 0) | (expert != be_ref[jnp.maximum(i - 1, 0)]))

    @pl.when(first_of_expert)
    def _():
        @pl.when(i == 0)
        def _():
            slot_ref[0] = 0
            for cp in weight_copies(expert, 0):
                cp.start()

        @pl.when(i > 0)
        def _():
            slot_ref[0] = 1 - slot_ref[0]

        slot = slot_ref[0]
        for cp in weight_copies(expert, slot):
            cp.wait()
        nxt = lax.while_loop(lambda j: (j < n_blocks) & (be_ref[jnp.minimum(j, n_blocks - 1)] == expert),
                             lambda j: j + 1, i + 1)
        nxt_c = jnp.minimum(nxt, n_blocks - 1)

        @pl.when((nxt < n_blocks) & (nv_ref[nxt_c] > 0))
        def _():
            for cp in weight_copies(be_ref[nxt_c], 1 - slot):
                cp.start()

    cur = slot_ref[0]

    def mlp_rows(r0, n):
        rows = slice(r0, r0 + n)
        live = lax.broadcasted_iota(I32, (n, 1), 0) < nvalid - r0
        xb = jnp.where(live, _unpack_halves(xa_ref[rows, :], xb_ref[rows, :]), 0.0).astype(BF16)
        gu = _bdot(xb, wgu_buf[cur]) + bgu_ref[0]
        gate = jnp.minimum(gu[:, :D_EXPERT], SWIGLU_LIMIT)
        up = jnp.clip(gu[:, D_EXPERT:], -SWIGLU_LIMIT, SWIGLU_LIMIT)
        act = (up + 1.0) * (gate * _sigmoid(SWIGLU_ALPHA * gate))
        y = _bdot(act.astype(BF16), wd_buf[cur]) + bd_ref[0]
        ya_ref[rows, :], yb_ref[rows, :] = _pack_halves(y)

    def zero_rows(r0, n):
        rows = slice(r0, r0 + n)
        ya_ref[rows, :] = jnp.zeros((n, PACK_W), jnp.uint32)
        yb_ref[rows, :] = jnp.zeros((n, PACK_W), jnp.uint32)

    @pl.when(nvalid > MOE_HALF + MOE_QUARTER)
    def _():
        mlp_rows(0, MOE_HALF)
        mlp_rows(MOE_HALF, MOE_HALF)

    @pl.when((nvalid > MOE_HALF) & (nvalid <= MOE_HALF + MOE_QUARTER))
    def _():
        mlp_rows(0, MOE_HALF)
        mlp_rows(MOE_HALF, MOE_QUARTER)
        zero_rows(MOE_HALF + MOE_QUARTER, MOE_QUARTER)

    @pl.when((nvalid > MOE_QUARTER) & (nvalid <= MOE_HALF))
    def _():
        mlp_rows(0, MOE_HALF)
        zero_rows(MOE_HALF, MOE_HALF)

    @pl.when((nvalid > 0) & (nvalid <= MOE_QUARTER))
    def _():
        mlp_rows(0, MOE_QUARTER)
        zero_rows(MOE_QUARTER, MOE_BM - MOE_QUARTER)

    @pl.when(nvalid <= 0)
    def _():
        zero_rows(0, MOE_BM)


def _experts(block_e, n_valid, xa, xb, w_gu, b_gu, w_down, b_down):
    cap = xa.shape[0]
    d = D_MODEL
    nb = cap // MOE_BM
    half = pl.BlockSpec((MOE_BM, PACK_W), lambda i, be, nv: (i, 0))
    grid_spec = pltpu.PrefetchScalarGridSpec(
        num_scalar_prefetch=2,
        grid=(nb,),
        in_specs=[
            half,
            half,
            pl.BlockSpec(memory_space=pl.ANY),
            pl.BlockSpec((1, 1, 2 * D_EXPERT), lambda i, be, nv: (be[i], 0, 0)),
            pl.BlockSpec(memory_space=pl.ANY),
            pl.BlockSpec((1, 1, d), lambda i, be, nv: (be[i], 0, 0)),
        ],
        out_specs=[half, half],
        scratch_shapes=[pltpu.VMEM((2, d, 2 * D_EXPERT), BF16), pltpu.VMEM((2, D_EXPERT, d), BF16),
                        pltpu.SMEM((1,), I32), pltpu.SemaphoreType.DMA((2, 2))],
    )
    return pl.pallas_call(
        _expert_kernel,
        grid_spec=grid_spec,
        out_shape=[jax.ShapeDtypeStruct((cap, PACK_W), jnp.uint32)] * 2,
        compiler_params=pltpu.CompilerParams(
            dimension_semantics=("arbitrary",), vmem_limit_bytes=VMEM_LIMIT),
        name="experts",
    )(block_e, n_valid, xa, xb, w_gu, b_gu, w_down, b_down)


def _final_kernel(x2_ref, ya_ref, yb_ref, tw_ref, gt2_ref, fg_ref, *rest):
    o_ref = rest[-1]
    tw = tw_ref[...].T
    moe = tw[:, 0:1] * _unpack_halves(ya_ref[0], yb_ref[0])
    for k in range(1, TOP_K):
        moe = moe + tw[:, k:k + 1] * _unpack_halves(ya_ref[k], yb_ref[k])
    x3 = x2_ref[...] + gt2_ref[0] * moe
    ms = jnp.mean(x3 * x3, axis=-1, keepdims=True)
    o_ref[...] = (x3 * lax.rsqrt(ms + EPS)) * fg_ref[...]


def _final(x2, yga, ygb, tw_tok, gt2, final_g, seq, batch0, n_total, out_prev):
    n, d = x2.shape
    tiles_per_seq = seq // FINAL_TM
    tile0 = batch0 * tiles_per_seq
    half = pl.BlockSpec((TOP_K, FINAL_TM, PACK_W), lambda i: (0, i, 0))
    in_specs = [
        pl.BlockSpec((FINAL_TM, d), lambda i: (i, 0)),
        half,
        half,
        pl.BlockSpec((TOP_K, FINAL_TM), lambda i: (0, i)),
        pl.BlockSpec((1, 1, d), lambda i: (batch0 + i // tiles_per_seq, 0, 0)),
        pl.BlockSpec((1, d), lambda i: (0, 0)),
    ]
    args = [x2, yga, ygb, tw_tok, gt2, final_g]
    aliases = {}
    if out_prev is not None:
        in_specs.append(pl.BlockSpec(memory_space=pl.ANY))
        aliases = {len(args): 0}
        args.append(out_prev)
    return pl.pallas_call(
        _final_kernel,
        grid=(n // FINAL_TM,),
        in_specs=in_specs,
        out_specs=pl.BlockSpec((FINAL_TM, d), lambda i: (tile0 + i, 0)),
        out_shape=jax.ShapeDtypeStruct((n_total, d), F32),
        input_output_aliases=aliases,
        compiler_params=pltpu.CompilerParams(
            dimension_semantics=("arbitrary",), vmem_limit_bytes=VMEM_LIMIT),
        name="final",
    )(*args)


def _block_diag(w):
    per = GATE_BLOCK // RNN_HEAD_DIM
    w4 = w.reshape(N_GATE_BLOCKS, per, RNN_HEAD_DIM, RNN_HEAD_DIM)
    eye = jnp.eye(per, dtype=w.dtype)
    bd = jnp.einsum("gpij,pq->gpiqj", w4, eye)
    return bd.reshape(N_GATE_BLOCKS, GATE_BLOCK, GATE_BLOCK)


def _layer(x2d, c, batch, seq, ada_w, ada_b, norm1_g, w_in, conv_w, conv_b, lru_wa, lru_ba,
           lru_wx, lru_bx, lru_lam, sg_ln_g, sg_ln_b, sg_ws, sg_bs, w_br_rnn, w_br_sg, w_out,
           norm2_g, w_router, b_router, w_gu, b_gu, w_down, b_down, final_g):
    n_total, d = x2d.shape
    mod = _ada(c, ada_w, ada_b)
    sh1, sc1, gt1, sh2, sc2, gt2 = [
        mod[:, i * d:(i + 1) * d].reshape(batch, 1, d) for i in range(N_MOD)]
    row = lambda v: v.reshape(1, -1)

    bs_tile = jnp.repeat(sg_bs.T, SG_GROUP_DIM, axis=1)
    wr_t = w_router.T
    wr_hi = wr_t.astype(BF16)
    wr_lo = (wr_t - wr_hi.astype(F32)).astype(BF16)
    t_out = lax.broadcasted_iota(I32, (MIXER_TS, MIXER_TS), 0)
    t_in = lax.broadcasted_iota(I32, (MIXER_TS, MIXER_TS), 1)
    shifts = jnp.stack([(t_out - t_in == s) for s in range(1, CONV_WIDTH)]).astype(BF16)
    before = (t_out < t_in).astype(BF16)
    causal = jnp.tril(jnp.ones((SG_CHUNK, SG_CHUNK), dtype=bool))
    ws_causal = jnp.where(causal[None], sg_ws, 0.0).astype(BF16)
    mixer_weights = (
        conv_w, row(conv_b), _block_diag(0.5 * lru_wa).astype(BF16), row(0.5 * lru_ba),
        _block_diag(0.5 * lru_wx).astype(BF16), row(0.5 * lru_bx), row(lru_lam), row(sg_ln_g),
        row(sg_ln_b), ws_causal, bs_tile, (0.25 * w_br_rnn).astype(BF16), (0.5 * w_br_sg).astype(BF16),
        (0.5 * w_out).astype(BF16), row(norm2_g), jnp.concatenate([wr_hi, wr_lo], axis=0),
        b_router.reshape(N_EXPERTS, 1), shifts, before)
    gate_cols = 2 * D_RNN + 2 * D_SG
    col_scale = jnp.where(jnp.arange(D_IN) >= gate_cols, 0.5, 1.0).astype(F32)
    w_in_bf = (w_in * col_scale[None, :]).astype(BF16)

    groups = TOKEN_GROUPS if batch % TOKEN_GROUPS == 0 and N_EXPERTS % TOKEN_GROUPS == 0 else 1
    batches = batch // groups
    n_cast = N_EXPERTS // groups
    n = batches * seq
    mixed, cast = [], None
    for grp in range(groups):
        *outs, w_gu_bf, w_down_bf = _mixer(
            x2d, row(norm1_g), sc1, sh1, w_in_bf, gt1, sc2, sh2, *mixer_weights, w_gu, w_down,
            grp * batches, batches, seq, grp * n_cast, n_cast, cast)
        cast = (w_gu_bf, w_down_bf)
        mixed.append(outs)

    out = None
    for grp in range(groups):
        batch0 = grp * batches
        x2, h2a, h2b, top_idx, top_w, rank, counts = mixed[grp]

        cap = n * TOP_K + N_EXPERTS * MOE_BM
        nb = cap // MOE_BM
        pos, block_e, n_valid = _slot_plan(counts[:, 0].astype(I32), top_idx, rank, nb)

        xa = _sc_scatter_rows(h2a, pos, cap)
        xb = _sc_scatter_rows(h2b, pos, cap)
        ya, yb = _experts(block_e, n_valid, xa, xb, cast[0].reshape(w_gu.shape),
                          b_gu.reshape(N_EXPERTS, 1, -1), cast[1].reshape(w_down.shape),
                          b_down.reshape(N_EXPERTS, 1, -1))
        flat_pos = pos.reshape(1, -1)
        yga = _sc_gather_rows(ya, flat_pos).reshape(TOP_K, n, PACK_W)
        ygb = _sc_gather_rows(yb, flat_pos).reshape(TOP_K, n, PACK_W)
        out = _final(x2, yga, ygb, top_w, gt2, final_g.reshape(1, d), seq, batch0, n_total, out)
    return out


def kernel(x, c, ada_w, ada_b, norm1_g, w_in, conv_w, conv_b, lru_wa, lru_ba, lru_wx, lru_bx,
           lru_lam, sg_ln_g, sg_ln_b, sg_ws, sg_bs, w_br_rnn, w_br_sg, w_out, norm2_g,
           w_router, b_router, w_gu, b_gu, w_down, b_down, final_g):
    batch, seq, d = x.shape
    depth = ada_w.shape[0]
    assert depth == 1, "the combine is fused with the final norm, which follows the only layer"
    x2d = x.reshape(batch * seq, d)
    l = 0
    out = _layer(
        x2d, c, batch, seq, ada_w[l], ada_b[l], norm1_g[l], w_in[l], conv_w[l], conv_b[l],
        lru_wa[l], lru_ba[l], lru_wx[l], lru_bx[l], lru_lam[l], sg_ln_g[l], sg_ln_b[l],
        sg_ws[l], sg_bs[l], w_br_rnn[l], w_br_sg[l], w_out[l], norm2_g[l], w_router[l],
        b_router[l], w_gu[l], b_gu[l], w_down[l], b_down[l], final_g)
    return out.reshape(batch, seq, d)
```

```python
import functools

import jax
import jax.numpy as jnp
from jax import lax
from jax.experimental import pallas as pl
from jax.experimental.pallas import tpu as pltpu
from jax.experimental.pallas import tpu_sc as plsc

F32 = jnp.float32
BF16 = jnp.bfloat16
I32 = jnp.int32

D_MODEL = 1024
D_RNN = 1024
RNN_HEADS = 16
RNN_HEAD_DIM = D_RNN // RNN_HEADS
CONV_WIDTH = 4
LRU_C = 8.0
D_SG = 1024
SG_GROUPS = 8
SG_GROUP_DIM = D_SG // SG_GROUPS
SG_CHUNK = 128
N_EXPERTS = 32
TOP_K = 4
D_EXPERT = 1024
SWIGLU_LIMIT = 7.0
SWIGLU_ALPHA = 1.702
EPS = 1e-6
N_MOD = 6
D_IN = 2 * D_RNN + 2 * D_SG + 2 * D_MODEL

SUBLANES = 8
LANES = 128
GATE_BLOCK = 256
N_GATE_BLOCKS = D_RNN // GATE_BLOCK

ADA_TN = 1536
INPROJ_TN = 1024
MIXER_TS = 256
MOE_BM = 512
MOE_HALF = MOE_BM // 2
MOE_QUARTER = MOE_BM // 4
TOKEN_GROUPS = 2
SC_WINDOW = 128
PACK_W = D_MODEL // 4
FINAL_TM = 512
POS_TN = 8192
VMEM_LIMIT = 60 * 1024 * 1024


def _sigmoid(x):
    return 0.5 * jnp.tanh(0.5 * x) + 0.5


def _gelu_tanh_x2(x):
    k = 0.7978845608028654
    return x + x * jnp.tanh(x * (k + (k * 0.044715) * (x * x)))


def _bdot(a, b):
    return jnp.dot(a, b, preferred_element_type=F32)


def _pack_halves(v):
    word = pltpu.pack_elementwise([v[:, 2 * PACK_W:], v[:, :2 * PACK_W]], packed_dtype=BF16)
    return word[:, :PACK_W], word[:, PACK_W:]


def _unpack_halves(wa, wb):
    part = lambda w, i: pltpu.unpack_elementwise(w, index=i, packed_dtype=BF16, unpacked_dtype=F32)
    return jnp.concatenate([part(wa, 1), part(wb, 1), part(wa, 0), part(wb, 0)], axis=1)


def _ada_kernel(c_ref, w_ref, b_ref, o_ref):
    c = c_ref[...]
    s = c * _sigmoid(c)
    o_ref[...] = jnp.dot(s, w_ref[...], preferred_element_type=F32,
                         precision=lax.Precision.HIGHEST) + b_ref[...]


def _ada(c, ada_w, ada_b):
    b, d = c.shape
    n = ada_w.shape[1]
    return pl.pallas_call(
        _ada_kernel,
        grid=(n // ADA_TN,),
        in_specs=[
            pl.BlockSpec((b, d), lambda j: (0, 0)),
            pl.BlockSpec((d, ADA_TN), lambda j: (0, j)),
            pl.BlockSpec((1, ADA_TN), lambda j: (0, j)),
        ],
        out_specs=pl.BlockSpec((b, ADA_TN), lambda j: (0, j)),
        out_shape=jax.ShapeDtypeStruct((b, n), F32),
        compiler_params=pltpu.CompilerParams(
            dimension_semantics=("arbitrary",), vmem_limit_bytes=VMEM_LIMIT),
        name="ada",
    )(c, ada_w, ada_b.reshape(1, n))


def _norm_mod(x, g, sc, sh):
    ms = jnp.mean(x * x, axis=-1, keepdims=True)
    y = x * lax.rsqrt(ms + EPS)
    return y * (g * (1.0 + sc)) + sh


N_MIXER_INPUTS = 32
N_MIXER_TOKEN_OUTPUTS = 7


def _mixer_kernel(*refs, pairs_per_seq, n_aliased):
    (x_ref, xn_ref, n1g_ref, sc1_ref, sh1_ref, sc1n_ref, sh1n_ref, win_ref,
     gt1_ref, sc2_ref, sh2_ref,
     convw_ref, convb_ref, wa_ref, ba_ref, wx_ref, bx_ref, lam_ref,
     lng_ref, lnb_ref, ws_ref, bs_ref, wbr_ref, wbs_ref, wout_ref,
     n2g_ref, wr_ref, br_ref, shift_ref, before_ref, wgu_ref, wd_ref) = refs[:N_MIXER_INPUTS]
    (x2_ref, h2a_ref, h2b_ref, idx_ref, tw_ref, rank_ref, cnt_ref, wgu_bf_ref, wd_bf_ref,
     z0_ref, z1_ref, hn0_ref, hn1_ref, xp_ref, xc_ref, a_ref, hh_ref, sv_ref,
     hstate_ref) = refs[N_MIXER_INPUTS + n_aliased:]
    ts = MIXER_TS
    s = pl.program_id(0)

    wgu_bf_ref[...] = wgu_ref[...].astype(BF16)
    wd_bf_ref[...] = wd_ref[...].astype(BF16)

    def inproj_norm(hn_ref, x_rows, sc, sh):
        hn_ref[...] = _norm_mod(x_rows, n1g_ref[...], sc, sh).astype(BF16)

    def inproj_chunks(z_dst, hn_ref, first, last):
        for c in range(first * INPROJ_TN, last * INPROJ_TN, INPROJ_TN):
            z_dst[:, c:c + INPROJ_TN] = _bdot(hn_ref[...], win_ref[:, c:c + INPROJ_TN])

    @pl.when(s == 0)
    def _():
        cnt_ref[...] = jnp.zeros_like(cnt_ref)
        inproj_norm(hn0_ref, x_ref[0:ts, :], sc1_ref[0], sh1_ref[0])
        inproj_chunks(z0_ref, hn0_ref, 0, D_IN // INPROJ_TN)

    @pl.when(s % pairs_per_seq == 0)
    def _():
        xp_ref[0:SUBLANES, :] = jnp.zeros((SUBLANES, D_RNN), F32)
        hstate_ref[...] = jnp.zeros_like(hstate_ref)

    def tile(k, z_ref, z_next_ref, hn_ref, next_rows, next_sc, next_sh):
        rows = slice(k * ts, (k + 1) * ts)
        inproj_norm(hn_ref, next_rows, next_sc, next_sh)

        x16 = z_ref[:, 0:D_RNN].astype(BF16)
        rnn_x = x16.astype(F32)
        cw = convw_ref[...]
        xc = cw[3:4] * rnn_x + convb_ref[...]
        for sft in range(1, CONV_WIDTH):
            xc = xc + cw[3 - sft:4 - sft] * _bdot(shift_ref[sft - 1], x16)
        xc_ref[...] = xc
        xp_ref[SUBLANES:2 * SUBLANES, :] = rnn_x[0:SUBLANES, :]
        xc_ref[0:SUBLANES, :] = (
            cw[3:4] * xp_ref[SUBLANES:2 * SUBLANES, :]
            + cw[2:3] * xp_ref[SUBLANES - 1:2 * SUBLANES - 1, :]
            + cw[1:2] * xp_ref[SUBLANES - 2:2 * SUBLANES - 2, :]
            + cw[0:1] * xp_ref[SUBLANES - 3:2 * SUBLANES - 3, :]) + convb_ref[...]
        xp_ref[0:SUBLANES, :] = rnn_x[ts - SUBLANES:ts, :]
        xc = xc_ref[...]

        xcb = xc.astype(BF16)
        r_parts, i_parts = [], []
        for g in range(N_GATE_BLOCKS):
            blk = xcb[:, g * GATE_BLOCK:(g + 1) * GATE_BLOCK]
            r_parts.append(_bdot(blk, wa_ref[g]))
            i_parts.append(_bdot(blk, wx_ref[g]))
        inproj_chunks(z_next_ref, hn_ref, 0, 3)
        r_t = jnp.tanh(jnp.concatenate(r_parts, axis=1) + ba_ref[...])
        i_t = jnp.tanh(jnp.concatenate(i_parts, axis=1) + bx_ref[...])

        nl = -lam_ref[...]
        softplus = jnp.maximum(nl, 0.0) + jnp.log(1.0 + jnp.exp(-jnp.abs(nl)))
        half_c = (-0.5 * LRU_C) * softplus
        a = jnp.exp(r_t * half_c + half_c)
        t = 1.0 - a * a
        u = jnp.where(t > 0.0, t * lax.rsqrt(t), 0.0) * ((i_t + 1.0) * xc)

        groups = ts // SUBLANES
        a3 = a.reshape(groups, SUBLANES, D_RNN)
        h3 = u.reshape(groups, SUBLANES, D_RNN)
        sub = lax.broadcasted_iota(I32, (groups, SUBLANES, D_RNN), 1)
        for step in (1, 2, 4):
            keep = sub >= step
            a_sh = jnp.where(keep, pltpu.roll(a3, step, 1), 1.0)
            h_sh = jnp.where(keep, pltpu.roll(h3, step, 1), 0.0)
            h3 = h3 + a3 * h_sh
            a3 = a3 * a_sh
        a_ref[...] = a3.reshape(ts, D_RNN)
        hh_ref[...] = h3.reshape(ts, D_RNN)

        hc = hstate_ref[...]
        for gi in range(groups):
            grp = slice(gi * SUBLANES, (gi + 1) * SUBLANES)
            hg = hh_ref[grp, :] + a_ref[grp, :] * hc
            hh_ref[grp, :] = hg
            hc = jnp.broadcast_to(hg[SUBLANES - 1:SUBLANES, :], (SUBLANES, D_RNN))
        hstate_ref[...] = hc

        inproj_chunks(z_next_ref, hn_ref, 3, 4)
        y_rnn = (hh_ref[...] * _gelu_tanh_x2(z_ref[:, D_RNN:2 * D_RNN].astype(F32))).astype(BF16)

        gv = _gelu_tanh_x2(z_ref[:, 2 * D_RNN + D_SG:2 * D_RNN + 2 * D_SG].astype(F32))
        mu = jnp.mean(gv, axis=-1, keepdims=True)
        dv = gv - mu
        var = jnp.mean(dv * dv, axis=-1, keepdims=True)
        vn = (dv * lax.rsqrt(var + 4.0 * EPS) * lng_ref[...] + lnb_ref[...]).astype(BF16)
        for g in range(SG_GROUPS):
            wg = ws_ref[g]
            cols = slice(g * SG_GROUP_DIM, (g + 1) * SG_GROUP_DIM)
            for n in range(ts // SG_CHUNK):
                chunk = slice(n * SG_CHUNK, (n + 1) * SG_CHUNK)
                sv_ref[chunk, cols] = _bdot(wg, vn[chunk, cols]) + bs_ref[:, cols]
        inproj_chunks(z_next_ref, hn_ref, 4, 5)
        gu = _gelu_tanh_x2(z_ref[:, 2 * D_RNN:2 * D_RNN + D_SG].astype(F32))
        y_sg = (gu * sv_ref[...]).astype(BF16)

        g_rnn = z_ref[:, 2 * D_RNN + 2 * D_SG:2 * D_RNN + 2 * D_SG + D_MODEL].astype(F32)
        g_sg = z_ref[:, 2 * D_RNN + 2 * D_SG + D_MODEL:D_IN].astype(F32)
        m = ((jnp.tanh(g_rnn) + 1.0) * _bdot(y_rnn, wbr_ref[...])
             + (jnp.tanh(g_sg) + 1.0) * _bdot(y_sg, wbs_ref[...])).astype(BF16)
        x2 = x_ref[rows, :] + gt1_ref[0] * _bdot(m, wout_ref[...])
        x2_ref[rows, :] = x2
        inproj_chunks(z_next_ref, hn_ref, 5, 6)

        h2 = _norm_mod(x2, n2g_ref[...], sc2_ref[0], sh2_ref[0])
        h2a_ref[rows, :], h2b_ref[rows, :] = _pack_halves(h2)
        h_hi = h2.astype(BF16)
        h_lo = (h2 - h_hi.astype(F32)).astype(BF16)
        nt_dims = (((1,), (1,)), ((), ()))
        by_hi = lax.dot_general(wr_ref[...], h_hi, nt_dims, preferred_element_type=F32)
        logits = (by_hi[:N_EXPERTS] + by_hi[N_EXPERTS:]
                  + lax.dot_general(wr_ref[0:N_EXPERTS, :], h_lo, nt_dims,
                                    preferred_element_type=F32)
                  + br_ref[...])
        e_iota = lax.broadcasted_iota(I32, (N_EXPERTS, ts), 0)
        v = logits
        vals, idxs, sels = [], [], []
        for _ in range(TOP_K):
            mx = jnp.max(v, axis=0, keepdims=True)
            ik = jnp.min(jnp.where(v == mx, e_iota, N_EXPERTS), axis=0, keepdims=True)
            sel = e_iota == ik
            v = jnp.where(sel, -jnp.inf, v)
            vals.append(mx)
            idxs.append(ik)
            sels.append(sel)
        exps = [jnp.exp(val - vals[0]) for val in vals]
        denom = exps[0] + exps[1] + exps[2] + exps[3]
        idx_ref[:, rows] = jnp.concatenate(idxs, axis=0)
        tw_ref[:, rows] = jnp.concatenate([e / denom for e in exps], axis=0)

        onehot = jnp.zeros((N_EXPERTS, ts), F32)
        for sel in sels:
            onehot = jnp.where(sel, 1.0, onehot)
        total = cnt_ref[...] + _bdot(onehot.astype(BF16), before_ref[...])
        ranks = [jnp.sum(jnp.where(sel, total, 0.0), axis=0, keepdims=True) for sel in sels]
        rank_ref[:, rows] = jnp.concatenate(ranks, axis=0).astype(I32)
        cnt_ref[...] = cnt_ref[...] + jnp.sum(onehot, axis=1, keepdims=True)

    tile(0, z0_ref, z1_ref, hn1_ref, x_ref[ts:2 * ts, :], sc1_ref[0], sh1_ref[0])
    tile(1, z1_ref, z0_ref, hn0_ref, xn_ref[...], sc1n_ref[0], sh1n_ref[0])


def _mixer(x2d, n1g, sc1, sh1, w_in, gt1, sc2, sh2, conv_w, conv_b, wa_bd, ba, wx_bd, bx, lam,
           ln_g, ln_b, ws, bs_tile, wbr, wbs, wout, n2g, wr_split, br, shifts, before, w_gu, w_down,
           batch0, batches, seq, expert0, n_cast, cast_prev):
    d = x2d.shape[1]
    ts = MIXER_TS
    n = batches * seq
    tiles_per_seq = seq // ts
    pairs_per_seq = tiles_per_seq // 2
    assert tiles_per_seq % 2 == 0
    steps = batches * pairs_per_seq
    pair0 = batch0 * pairs_per_seq
    next_tile = lambda s: jnp.minimum(2 * s + 2, 2 * steps - 1)
    pair = lambda s: (s, 0)
    pairt = lambda s: (0, s)
    bvec = lambda s: (batch0 + s // pairs_per_seq, 0, 0)
    bvec_next = lambda s: (batch0 + next_tile(s) // tiles_per_seq, 0, 0)
    n_e, k_gu, n_gu = w_gu.shape
    _, k_d, n_d = w_down.shape
    assert (n_cast * k_gu) % steps == 0 and (n_cast * k_d) % steps == 0
    rows_gu, rows_d = n_cast * k_gu // steps, n_cast * k_d // steps
    cast_gu = lambda s: (expert0 * k_gu // rows_gu + s, 0)
    cast_d = lambda s: (expert0 * k_d // rows_d + s, 0)
    w_gu = w_gu.reshape(n_e * k_gu, n_gu)
    w_down = w_down.reshape(n_e * k_d, n_d)
    c2 = lambda s: (0, 0)
    c3 = lambda s: (0, 0, 0)
    in_specs = [
        pl.BlockSpec((2 * ts, d), lambda s: (pair0 + s, 0)),
        pl.BlockSpec((ts, d), lambda s: (2 * pair0 + next_tile(s), 0)),
        pl.BlockSpec((1, d), c2),
        pl.BlockSpec((1, 1, d), bvec),
        pl.BlockSpec((1, 1, d), bvec),
        pl.BlockSpec((1, 1, d), bvec_next),
        pl.BlockSpec((1, 1, d), bvec_next),
        pl.BlockSpec((d, D_IN), c2),
        pl.BlockSpec((1, 1, d), bvec),
        pl.BlockSpec((1, 1, d), bvec),
        pl.BlockSpec((1, 1, d), bvec),
        pl.BlockSpec((CONV_WIDTH, D_RNN), c2),
        pl.BlockSpec((1, D_RNN), c2),
        pl.BlockSpec((N_GATE_BLOCKS, GATE_BLOCK, GATE_BLOCK), c3),
        pl.BlockSpec((1, D_RNN), c2),
        pl.BlockSpec((N_GATE_BLOCKS, GATE_BLOCK, GATE_BLOCK), c3),
        pl.BlockSpec((1, D_RNN), c2),
        pl.BlockSpec((1, D_RNN), c2),
        pl.BlockSpec((1, D_SG), c2),
        pl.BlockSpec((1, D_SG), c2),
        pl.BlockSpec((SG_GROUPS, SG_CHUNK, SG_CHUNK), c3),
        pl.BlockSpec((SG_CHUNK, D_SG), c2),
        pl.BlockSpec((D_RNN, d), c2),
        pl.BlockSpec((D_SG, d), c2),
        pl.BlockSpec((d, d), c2),
        pl.BlockSpec((1, d), c2),
        pl.BlockSpec((2 * N_EXPERTS, d), c2),
        pl.BlockSpec((N_EXPERTS, 1), c2),
        pl.BlockSpec((CONV_WIDTH - 1, ts, ts), c3),
        pl.BlockSpec((ts, ts), c2),
        pl.BlockSpec((rows_gu, n_gu), cast_gu),
        pl.BlockSpec((rows_d, n_d), cast_d),
    ]
    assert len(in_specs) == N_MIXER_INPUTS
    args = [x2d, x2d, n1g, sc1, sh1, sc1, sh1, w_in, gt1, sc2, sh2, conv_w, conv_b, wa_bd, ba,
            wx_bd, bx, lam, ln_g, ln_b, ws, bs_tile, wbr, wbs, wout, n2g, wr_split, br, shifts,
            before, w_gu, w_down]
    aliases = {}
    if cast_prev is not None:
        for j, prev in enumerate(cast_prev):
            in_specs.append(pl.BlockSpec(memory_space=pl.ANY))
            aliases[len(args)] = N_MIXER_TOKEN_OUTPUTS + j
            args.append(prev)
    out_specs = [
        pl.BlockSpec((2 * ts, d), pair),
        pl.BlockSpec((2 * ts, PACK_W), pair),
        pl.BlockSpec((2 * ts, PACK_W), pair),
        pl.BlockSpec((TOP_K, 2 * ts), pairt),
        pl.BlockSpec((TOP_K, 2 * ts), pairt),
        pl.BlockSpec((TOP_K, 2 * ts), pairt),
        pl.BlockSpec((N_EXPERTS, 1), c2),
        pl.BlockSpec((rows_gu, n_gu), cast_gu),
        pl.BlockSpec((rows_d, n_d), cast_d),
    ]
    out_shape = [
        jax.ShapeDtypeStruct((n, d), F32),
        jax.ShapeDtypeStruct((n, PACK_W), jnp.uint32),
        jax.ShapeDtypeStruct((n, PACK_W), jnp.uint32),
        jax.ShapeDtypeStruct((TOP_K, n), I32),
        jax.ShapeDtypeStruct((TOP_K, n), F32),
        jax.ShapeDtypeStruct((TOP_K, n), I32),
        jax.ShapeDtypeStruct((N_EXPERTS, 1), F32),
        jax.ShapeDtypeStruct(w_gu.shape, BF16),
        jax.ShapeDtypeStruct(w_down.shape, BF16),
    ]
    scratch = [
        pltpu.VMEM((ts, D_IN), F32),
        pltpu.VMEM((ts, D_IN), F32),
        pltpu.VMEM((ts, d), BF16),
        pltpu.VMEM((ts, d), BF16),
        pltpu.VMEM((2 * SUBLANES, D_RNN), F32),
        pltpu.VMEM((ts, D_RNN), F32),
        pltpu.VMEM((ts, D_RNN), F32),
        pltpu.VMEM((ts, D_RNN), F32),
        pltpu.VMEM((ts, D_SG), F32),
        pltpu.VMEM((SUBLANES, D_RNN), F32),
    ]
    return pl.pallas_call(
        functools.partial(_mixer_kernel, pairs_per_seq=pairs_per_seq, n_aliased=len(aliases)),
        grid=(steps,),
        in_specs=in_specs,
        out_specs=out_specs,
        out_shape=out_shape,
        scratch_shapes=scratch,
        input_output_aliases=aliases,
        compiler_params=pltpu.CompilerParams(
            dimension_semantics=("arbitrary",), vmem_limit_bytes=VMEM_LIMIT),
        name="mixer",
    )(*args)


def _sc_mesh():
    return plsc.VectorSubcoreMesh(core_axis_name="core", subcore_axis_name="subcore")


def _sc_scatter_rows(rows, pos, cap):
    n, d = rows.shape
    kk = pos.shape[0]

    @functools.partial(
        pl.kernel, out_type=jax.ShapeDtypeStruct((cap, d), rows.dtype), mesh=_sc_mesh(),
        scratch_types=[], name="sc_scatter_rows")
    def scatter(x_hbm, i_hbm, o_hbm):
        def body(x_vmem, i_vmem):
            pltpu.sync_copy(x_vmem, o_hbm.at[i_vmem.at[0]])

        pltpu.emit_pipeline(
            body,
            grid=(n // SC_WINDOW, kk),
            in_specs=[pl.BlockSpec((SC_WINDOW, d), lambda i, k: (i, 0)),
                      pl.BlockSpec((1, SC_WINDOW), lambda i, k: (k, i))],
            out_specs=[],
            core_axis_name=("core", "subcore"),
            dimension_semantics=(pltpu.PARALLEL, pltpu.ARBITRARY),
        )(x_hbm, i_hbm)

    return scatter(rows, pos)


def _sc_gather_rows(table, idx):
    m = idx.shape[1]
    d = table.shape[1]

    @functools.partial(
        pl.kernel, out_type=jax.ShapeDtypeStruct((m, d), table.dtype), mesh=_sc_mesh(),
        scratch_types=[], name="sc_gather_rows")
    def gather(x_hbm, i_hbm, o_hbm):
        def body(i_vmem, o_vmem):
            pltpu.sync_copy(x_hbm.at[i_vmem.at[0]], o_vmem)

        pltpu.emit_pipeline(
            body,
            grid=(m // SC_WINDOW,),
            in_specs=[pl.BlockSpec((1, SC_WINDOW), lambda i: (0, i))],
            out_specs=[pl.BlockSpec((SC_WINDOW, d), lambda i: (i, 0))],
            core_axis_name=("core", "subcore"),
            dimension_semantics=(pltpu.PARALLEL,),
        )(i_hbm, o_hbm)

    return gather(table, idx)


def _pos_kernel(cnt_ref, idx_ref, rank_ref, pos_ref, be_ref, nv_ref, bi_ref):
    assert MOE_BM & (MOE_BM - 1) == 0
    starts, ends, run = [], [], jnp.int32(0)
    for e in range(N_EXPERTS):
        starts.append(run)
        run = run + ((cnt_ref[e] + (MOE_BM - 1)) & jnp.int32(-MOE_BM))
        ends.append(run)

    idx = idx_ref[...]
    pos = rank_ref[...]
    for e in range(N_EXPERTS):
        pos = pos + jnp.where(idx == e, starts[e], 0)
    pos_ref[...] = pos

    row0 = lax.broadcasted_iota(I32, be_ref.shape, 1) * MOE_BM
    be = jnp.zeros(be_ref.shape, I32)
    for e in range(N_EXPERTS - 1):
        be = be + jnp.where(ends[e] <= row0, 1, 0)
    start = jnp.zeros(be_ref.shape, I32)
    count = jnp.zeros(be_ref.shape, I32)
    for e in range(N_EXPERTS):
        here = be == e
        start = jnp.where(here, starts[e], start)
        count = jnp.where(here, cnt_ref[e], count)
    be_ref[...] = be
    nv_ref[...] = jnp.clip(count - (row0 - start), 0, MOE_BM)
    n_live = jnp.maximum(ends[-1] // MOE_BM, 1)
    bi_ref[...] = jnp.minimum(lax.broadcasted_iota(I32, bi_ref.shape, 1), n_live - 1)


def _slot_plan(counts, top_idx, rank, nb):
    k, n = top_idx.shape
    tn = min(POS_TN, n)
    nbp = -(-nb // LANES) * LANES
    spec = pl.BlockSpec((k, tn), lambda i, c: (0, i))
    blocks = pl.BlockSpec((1, nbp), lambda i, c: (0, 0))
    pos, be, nv, bi = pl.pallas_call(
        _pos_kernel,
        grid_spec=pltpu.PrefetchScalarGridSpec(
            num_scalar_prefetch=1, grid=(n // tn,), in_specs=[spec, spec],
            out_specs=[spec, blocks, blocks, blocks]),
        out_shape=[jax.ShapeDtypeStruct((k, n), I32), jax.ShapeDtypeStruct((1, nbp), I32),
                   jax.ShapeDtypeStruct((1, nbp), I32), jax.ShapeDtypeStruct((1, nbp), I32)],
        compiler_params=pltpu.CompilerParams(dimension_semantics=("arbitrary",)),
        name="slot_plan",
    )(counts, top_idx, rank)
    return pos, be.reshape(nbp), nv.reshape(nbp), bi.reshape(nbp)


def _expert_kernel(be_ref, nv_ref, bi_ref, xa_ref, xb_ref, wgu_hbm, bgu_ref, wd_hbm, bd_ref,
                   ya_ref, yb_ref, wgu_buf, wd_buf, slot_ref, sems):
    i = pl.program_id(0)
    n_blocks = pl.num_programs(0)
    nvalid = nv_ref[i]
    expert = be_ref[i]

    def weight_copies(e, slot):
        return (pltpu.make_async_copy(wgu_hbm.at[e], wgu_buf.at[slot], sems.at[slot, 0]),
                pltpu.make_async_copy(wd_hbm.at[e], wd_buf.at[slot], sems.at[slot, 1]))

    first_of_expert = (nvalid > 0) & ((i == 0) | (expert != be_ref[jnp.maximum(i - 1, 0)]))

    @pl.when(first_of_expert)
    def _():
        @pl.when(i == 0)
        def _():
            slot_ref[0] = 0
            for cp in weight_copies(expert, 0):
                cp.start()

        @pl.when(i > 0)
        def _():
            slot_ref[0] = 1 - slot_ref[0]

        slot = slot_ref[0]
        for cp in weight_copies(expert, slot):
            cp.wait()
        nxt = lax.while_loop(lambda j: (j < n_blocks) & (be_ref[jnp.minimum(j, n_blocks - 1)] == expert),
                             lambda j: j + 1, i + 1)
        nxt_c = jnp.minimum(nxt, n_blocks - 1)

        @pl.when((nxt < n_blocks) & (nv_ref[nxt_c] > 0))
        def _():
            for cp in weight_copies(be_ref[nxt_c], 1 - slot):
                cp.start()

    cur = slot_ref[0]

    def mlp_rows(r0, n):
        rows = slice(r0, r0 + n)
        live = lax.broadcasted_iota(I32, (n, 1), 0) < nvalid - r0
        xb = jnp.where(live, _unpack_halves(xa_ref[rows, :], xb_ref[rows, :]), 0.0).astype(BF16)
        gu = _bdot(xb, wgu_buf[cur]) + bgu_ref[0]
        gate = jnp.minimum(gu[:, :D_EXPERT], SWIGLU_LIMIT)
        up = jnp.clip(gu[:, D_EXPERT:], -SWIGLU_LIMIT, SWIGLU_LIMIT)
        act = (up + 1.0) * (gate * _sigmoid(SWIGLU_ALPHA * gate))
        y = _bdot(act.astype(BF16), wd_buf[cur]) + bd_ref[0]
        ya_ref[rows, :], yb_ref[rows, :] = _pack_halves(y)

    def zero_rows(r0, n):
        rows = slice(r0, r0 + n)
        ya_ref[rows, :] = jnp.zeros((n, PACK_W), jnp.uint32)
        yb_ref[rows, :] = jnp.zeros((n, PACK_W), jnp.uint32)

    @pl.when(nvalid > MOE_HALF + MOE_QUARTER)
    def _():
        mlp_rows(0, MOE_HALF)
        mlp_rows(MOE_HALF, MOE_HALF)

    @pl.when((nvalid > MOE_HALF) & (nvalid <= MOE_HALF + MOE_QUARTER))
    def _():
        mlp_rows(0, MOE_HALF)
        mlp_rows(MOE_HALF, MOE_QUARTER)
        zero_rows(MOE_HALF + MOE_QUARTER, MOE_QUARTER)

    @pl.when((nvalid > MOE_QUARTER) & (nvalid <= MOE_HALF))
    def _():
        mlp_rows(0, MOE_HALF)
        zero_rows(MOE_HALF, MOE_HALF)

    @pl.when((nvalid > 0) & (nvalid <= MOE_QUARTER))
    def _():
        mlp_rows(0, MOE_QUARTER)
        zero_rows(MOE_QUARTER, MOE_BM - MOE_QUARTER)


def _experts(block_e, n_valid, block_i, xa, xb, w_gu, b_gu, w_down, b_down):
    cap = xa.shape[0]
    d = D_MODEL
    nb = cap // MOE_BM
    half = pl.BlockSpec((MOE_BM, PACK_W), lambda i, be, nv, bi: (bi[i], 0))
    grid_spec = pltpu.PrefetchScalarGridSpec(
        num_scalar_prefetch=3,
        grid=(nb,),
        in_specs=[
            half,
            half,
            pl.BlockSpec(memory_space=pl.ANY),
            pl.BlockSpec((1, 1, 2 * D_EXPERT), lambda i, be, nv, bi: (be[i], 0, 0)),
            pl.BlockSpec(memory_space=pl.ANY),
            pl.BlockSpec((1, 1, d), lambda i, be, nv, bi: (be[i], 0, 0)),
        ],
        out_specs=[half, half],
        scratch_shapes=[pltpu.VMEM((2, d, 2 * D_EXPERT), BF16), pltpu.VMEM((2, D_EXPERT, d), BF16),
                        pltpu.SMEM((1,), I32), pltpu.SemaphoreType.DMA((2, 2))],
    )
    return pl.pallas_call(
        _expert_kernel,
        grid_spec=grid_spec,
        out_shape=[jax.ShapeDtypeStruct((cap, PACK_W), jnp.uint32)] * 2,
        compiler_params=pltpu.CompilerParams(
            dimension_semantics=("arbitrary",), vmem_limit_bytes=VMEM_LIMIT),
        name="experts",
    )(block_e, n_valid, block_i, xa, xb, w_gu, b_gu, w_down, b_down)


def _final_kernel(x2_ref, ya_ref, yb_ref, tw_ref, gt2_ref, fg_ref, *rest):
    o_ref = rest[-1]
    tw = tw_ref[...].T
    moe = tw[:, 0:1] * _unpack_halves(ya_ref[0], yb_ref[0])
    for k in range(1, TOP_K):
        moe = moe + tw[:, k:k + 1] * _unpack_halves(ya_ref[k], yb_ref[k])
    x3 = x2_ref[...] + gt2_ref[0] * moe
    ms = jnp.mean(x3 * x3, axis=-1, keepdims=True)
    o_ref[...] = (x3 * lax.rsqrt(ms + EPS)) * fg_ref[...]


def _final(x2, yga, ygb, tw_tok, gt2, final_g, seq, batch0, n_total, out_prev):
    n, d = x2.shape
    tiles_per_seq = seq // FINAL_TM
    tile0 = batch0 * tiles_per_seq
    half = pl.BlockSpec((TOP_K, FINAL_TM, PACK_W), lambda i: (0, i, 0))
    in_specs = [
        pl.BlockSpec((FINAL_TM, d), lambda i: (i, 0)),
        half,
        half,
        pl.BlockSpec((TOP_K, FINAL_TM), lambda i: (0, i)),
        pl.BlockSpec((1, 1, d), lambda i: (batch0 + i // tiles_per_seq, 0, 0)),
        pl.BlockSpec((1, d), lambda i: (0, 0)),
    ]
    args = [x2, yga, ygb, tw_tok, gt2, final_g]
    aliases = {}
    if out_prev is not None:
        in_specs.append(pl.BlockSpec(memory_space=pl.ANY))
        aliases = {len(args): 0}
        args.append(out_prev)
    return pl.pallas_call(
        _final_kernel,
        grid=(n // FINAL_TM,),
        in_specs=in_specs,
        out_specs=pl.BlockSpec((FINAL_TM, d), lambda i: (tile0 + i, 0)),
        out_shape=jax.ShapeDtypeStruct((n_total, d), F32),
        input_output_aliases=aliases,
        compiler_params=pltpu.CompilerParams(
            dimension_semantics=("arbitrary",), vmem_limit_bytes=VMEM_LIMIT),
        name="final",
    )(*args)


def _block_diag(w):
    per = GATE_BLOCK // RNN_HEAD_DIM
    w4 = w.reshape(N_GATE_BLOCKS, per, RNN_HEAD_DIM, RNN_HEAD_DIM)
    eye = jnp.eye(per, dtype=w.dtype)
    bd = jnp.einsum("gpij,pq->gpiqj", w4, eye)
    return bd.reshape(N_GATE_BLOCKS, GATE_BLOCK, GATE_BLOCK)


def _layer(x2d, c, batch, seq, ada_w, ada_b, norm1_g, w_in, conv_w, conv_b, lru_wa, lru_ba,
           lru_wx, lru_bx, lru_lam, sg_ln_g, sg_ln_b, sg_ws, sg_bs, w_br_rnn, w_br_sg, w_out,
           norm2_g, w_router, b_router, w_gu, b_gu, w_down, b_down, final_g):
    n_total, d = x2d.shape
    mod = _ada(c, ada_w, ada_b)
    sh1, sc1, gt1, sh2, sc2, gt2 = [
        mod[:, i * d:(i + 1) * d].reshape(batch, 1, d) for i in range(N_MOD)]
    row = lambda v: v.reshape(1, -1)

    bs_tile = jnp.repeat(sg_bs.T, SG_GROUP_DIM, axis=1)
    wr_t = w_router.T
    wr_hi = wr_t.astype(BF16)
    wr_lo = (wr_t - wr_hi.astype(F32)).astype(BF16)
    t_out = lax.broadcasted_iota(I32, (MIXER_TS, MIXER_TS), 0)
    t_in = lax.broadcasted_iota(I32, (MIXER_TS, MIXER_TS), 1)
    shifts = jnp.stack([(t_out - t_in == s) for s in range(1, CONV_WIDTH)]).astype(BF16)
    before = (t_out < t_in).astype(BF16)
    causal = jnp.tril(jnp.ones((SG_CHUNK, SG_CHUNK), dtype=bool))
    ws_causal = jnp.where(causal[None], sg_ws, 0.0).astype(BF16)
    mixer_weights = (
        conv_w, row(conv_b), _block_diag(0.5 * lru_wa).astype(BF16), row(0.5 * lru_ba),
        _block_diag(0.5 * lru_wx).astype(BF16), row(0.5 * lru_bx), row(lru_lam), row(sg_ln_g),
        row(sg_ln_b), ws_causal, bs_tile, (0.25 * w_br_rnn).astype(BF16), (0.5 * w_br_sg).astype(BF16),
        (0.5 * w_out).astype(BF16), row(norm2_g), jnp.concatenate([wr_hi, wr_lo], axis=0),
        b_router.reshape(N_EXPERTS, 1), shifts, before)
    gate_cols = 2 * D_RNN + 2 * D_SG
    col_scale = jnp.where(jnp.arange(D_IN) >= gate_cols, 0.5, 1.0).astype(F32)
    w_in_bf = (w_in * col_scale[None, :]).astype(BF16)

    groups = TOKEN_GROUPS if batch % TOKEN_GROUPS == 0 and N_EXPERTS % TOKEN_GROUPS == 0 else 1
    batches = batch // groups
    n_cast = N_EXPERTS // groups
    n = batches * seq
    mixed, cast = [], None
    for grp in range(groups):
        *outs, w_gu_bf, w_down_bf = _mixer(
            x2d, row(norm1_g), sc1, sh1, w_in_bf, gt1, sc2, sh2, *mixer_weights, w_gu, w_down,
            grp * batches, batches, seq, grp * n_cast, n_cast, cast)
        cast = (w_gu_bf, w_down_bf)
        mixed.append(outs)

    out = None
    for grp in range(groups):
        batch0 = grp * batches
        x2, h2a, h2b, top_idx, top_w, rank, counts = mixed[grp]

        cap = n * TOP_K + N_EXPERTS * MOE_BM
        nb = cap // MOE_BM
        pos, block_e, n_valid, block_i = _slot_plan(counts[:, 0].astype(I32), top_idx, rank, nb)

        xa = _sc_scatter_rows(h2a, pos, cap)
        xb = _sc_scatter_rows(h2b, pos, cap)
        ya, yb = _experts(block_e, n_valid, block_i, xa, xb, cast[0].reshape(w_gu.shape),
                          b_gu.reshape(N_EXPERTS, 1, -1), cast[1].reshape(w_down.shape),
                          b_down.reshape(N_EXPERTS, 1, -1))
        flat_pos = pos.reshape(1, -1)
        yga = _sc_gather_rows(ya, flat_pos).reshape(TOP_K, n, PACK_W)
        ygb = _sc_gather_rows(yb, flat_pos).reshape(TOP_K, n, PACK_W)
        out = _final(x2, yga, ygb, top_w, gt2, final_g.reshape(1, d), seq, batch0, n_total, out)
    return out


def kernel(x, c, ada_w, ada_b, norm1_g, w_in, conv_w, conv_b, lru_wa, lru_ba, lru_wx, lru_bx,
           lru_lam, sg_ln_g, sg_ln_b, sg_ws, sg_bs, w_br_rnn, w_br_sg, w_out, norm2_g,
           w_router, b_router, w_gu, b_gu, w_down, b_down, final_g):
    batch, seq, d = x.shape
    depth = ada_w.shape[0]
    assert depth == 1, "the combine is fused with the final norm, which follows the only layer"
    x2d = x.reshape(batch * seq, d)
    l = 0
    out = _layer(
        x2d, c, batch, seq, ada_w[l], ada_b[l], norm1_g[l], w_in[l], conv_w[l], conv_b[l],
        lru_wa[l], lru_ba[l], lru_wx[l], lru_bx[l], lru_lam[l], sg_ln_g[l], sg_ln_b[l],
        sg_ws[l], sg_bs[l], w_br_rnn[l], w_br_sg[l], w_out[l], norm2_g[l], w_router[l],
        b_router[l], w_gu[l], b_gu[l], w_down[l], b_down[l], final_g)
    return out.reshape(batch, seq, d)
```
